```python
import math
import jax
import jax.numpy as jnp
from jax import lax
import numpy as np

D_MODEL = 1024
BATCH = 16
SEQ = 4096
DEPTH = 1

D_MIX = D_MODEL
SSD_WIDTH = D_MIX // 2
SSD_HEADDIM = 64
SSD_HEADS = SSD_WIDTH // SSD_HEADDIM
SSD_NGROUPS = 2
SSD_HPG = SSD_HEADS // SSD_NGROUPS
SSD_DSTATE = 128
SSD_CONV = 5
SSD_CHUNK = 128
SSD_CONV_CH = SSD_WIDTH + 2 * SSD_NGROUPS * SSD_DSTATE
DA_WIDTH = D_MIX - SSD_WIDTH
DA_HEADDIM = 64
DA_VDIM = 2 * DA_HEADDIM
DA_HEADS = DA_WIDTH // DA_VDIM
Q_BLOCK = 128
COL_Z = SSD_WIDTH
COL_XBC = SSD_CONV_CH
COL_DT = 2 * SSD_HEADS
COL_Q = DA_HEADS * 2 * DA_HEADDIM
COL_K = DA_HEADS * 2 * DA_HEADDIM
COL_V = DA_HEADS * DA_VDIM
D_IN_PROJ = COL_Z + COL_XBC + COL_DT + COL_Q + COL_K + COL_V
N_EXPERT_GROUPS = 4
EXPERTS_PER_GROUP = 8
N_EXPERTS = N_EXPERT_GROUPS * EXPERTS_PER_GROUP
TOP_K = 2
D_EXPERT = 512
MOE_BLOCK = 128
EPS = 1e-6

kernel_name = 'hymba_ssd_diffattn_hmoe_encoder'


def rmsnorm(x, w):
    xf = x.astype(jnp.float32)
    y = xf * lax.rsqrt(jnp.mean(xf * xf, axis=-1, keepdims=True) + EPS)
    return (y * w.astype(jnp.float32)).astype(x.dtype)


def centred_depthwise_conv(u, w, b):
    pad = (SSD_CONV - 1) // 2
    out = lax.conv_general_dilated(
        u, w[:, None, :].astype(u.dtype), window_strides=(1,), padding=[(pad, pad)],
        dimension_numbers=('NWC', 'WIO', 'NWC'), feature_group_count=u.shape[-1])
    return out + b.astype(u.dtype)


def ssd_scan(x, dt, a, bm, cm):
    bsz, L, G, R, P = x.shape
    N = bm.shape[-1]
    Q = SSD_CHUNK
    nc = L // Q
    xdt = (x * dt[..., None]).reshape(bsz, nc, Q, G, R, P)
    a_dt = (dt * a).reshape(bsz, nc, Q, G, R).transpose(0, 3, 4, 1, 2)
    bm = bm.reshape(bsz, nc, Q, G, N)
    cm = cm.reshape(bsz, nc, Q, G, N)
    a_cs = jnp.cumsum(a_dt, axis=-1)
    lower = jnp.tril(jnp.ones((Q, Q), dtype=bool))
    seg = a_cs[..., :, None] - a_cs[..., None, :]
    decay_in = jnp.exp(jnp.where(lower, seg, -jnp.inf))
    cb = jnp.einsum('bclgn,bcsgn->bcgls', cm, bm)
    y_diag = jnp.einsum('bcgls,bgrcls,bcsgrp->bclgrp', cb, decay_in, xdt)
    decay_to_end = jnp.exp(a_cs[..., -1:] - a_cs)
    chunk_states = jnp.einsum('bcsgn,bgrcs,bcsgrp->bcgrpn', bm, decay_to_end, xdt)
    chunk_decay = jnp.exp(a_cs[..., -1])

    def step(h, inp):
        s, d = inp
        return d[..., None, None] * h + s, h

    h0 = jnp.zeros((bsz, G, R, P, N), dtype=xdt.dtype)
    _, h_prev = lax.scan(step, h0, (jnp.moveaxis(chunk_states, 1, 0),
                                    jnp.moveaxis(chunk_decay, -1, 0)))
    h_prev = jnp.moveaxis(h_prev, 0, 1)
    y_off = jnp.einsum('bclgn,bgrcl,bcgrpn->bclgrp', cm, jnp.exp(a_cs), h_prev)
    return (y_diag + y_off).reshape(bsz, L, G, R, P)


def ssd_mixer(z, xbc, dt_raw, conv_w, conv_b, dt_bias_f, dt_bias_b, a_log_f, a_log_b,
              d_skip, norm_w):
    bsz, L = z.shape[:2]
    G, R, P, N, H = SSD_NGROUPS, SSD_HPG, SSD_HEADDIM, SSD_DSTATE, SSD_HEADS
    xbc = jax.nn.silu(centred_depthwise_conv(xbc, conv_w, conv_b)).astype(jnp.float32)
    xs = xbc[..., :SSD_WIDTH].reshape(bsz, L, G, R, P)
    bm = xbc[..., SSD_WIDTH:SSD_WIDTH + G * N].reshape(bsz, L, G, N)
    cm = xbc[..., SSD_WIDTH + G * N:].reshape(bsz, L, G, N)
    dt_raw = dt_raw.astype(jnp.float32)
    dt_f = jax.nn.softplus(dt_raw[..., :H] + dt_bias_f).reshape(bsz, L, G, R)
    dt_b = jax.nn.softplus(dt_raw[..., H:] + dt_bias_b).reshape(bsz, L, G, R)
    a_f = -jnp.exp(a_log_f.astype(jnp.float32)).reshape(G, R)
    a_b = -jnp.exp(a_log_b.astype(jnp.float32)).reshape(G, R)
    flip = lambda t: jnp.flip(t, axis=1)
    y_f = ssd_scan(xs, dt_f, a_f, bm, cm)
    y_b = flip(ssd_scan(flip(xs), flip(dt_b), a_b, flip(bm), flip(cm)))
    y = y_f + y_b + d_skip.astype(jnp.float32).reshape(G, R)[:, :, None] * xs
    y = y.reshape(bsz, L, SSD_WIDTH)
    gated = y * jax.nn.silu(z.astype(jnp.float32))
    return rmsnorm(gated, norm_w).astype(z.dtype)


def diff_attention(q, k, v, lq1, lk1, lq2, lk2, subln_w, lambda_init):
    bsz, L = q.shape[:2]
    H, D, V = DA_HEADS, DA_HEADDIM, DA_VDIM
    out_dtype = q.dtype
    q = q.reshape(bsz, L, H, 2, D).transpose(0, 2, 3, 1, 4).astype(jnp.float32) * (D ** -0.5)
    k = k.reshape(bsz, L, H, 2, D).transpose(0, 2, 3, 1, 4).astype(jnp.float32)
    v = v.reshape(bsz, L, H, V).transpose(0, 2, 1, 3).astype(jnp.float32)
    lam = (jnp.exp(jnp.sum(lq1.astype(jnp.float32) * lk1.astype(jnp.float32)))
           - jnp.exp(jnp.sum(lq2.astype(jnp.float32) * lk2.astype(jnp.float32)))
           + lambda_init)
    slopes = jnp.power(2.0, -8.0 * jnp.arange(1, H + 1, dtype=jnp.float32) / H)
    nq = L // Q_BLOCK
    q_blocks = jnp.moveaxis(q.reshape(bsz, H, 2, nq, Q_BLOCK, D), 3, 0)
    k_pos = jnp.arange(L, dtype=jnp.float32)

    def attend(args):
        qb, blk = args
        q_pos = (blk * Q_BLOCK).astype(jnp.float32) + jnp.arange(Q_BLOCK, dtype=jnp.float32)
        dist = jnp.abs(q_pos[:, None] - k_pos[None, :])
        s = jnp.einsum('bhcqd,bhckd->bhcqk', qb, k) - slopes[:, None, None, None] * dist
        p = jax.nn.softmax(s, axis=-1)
        w = p[:, :, 0] - lam * p[:, :, 1]
        return jnp.einsum('bhqk,bhkv->bhqv', w, v)

    o = lax.map(attend, (q_blocks, jnp.arange(nq, dtype=jnp.int32)))
    o = o.transpose(1, 0, 3, 2, 4).reshape(bsz, L, H, V)
    o = rmsnorm(o, subln_w) * (1.0 - lambda_init)
    return o.reshape(bsz, L, H * V).astype(out_dtype)


def hier_moe(u, w_rg, b_rg, w_re, b_re, w_gate, w_up, w_down):
    bsz, L, d = u.shape
    T = bsz * L
    xt = u.reshape(T, d)
    g_logits = jnp.einsum('td,dg->tg', xt, w_rg).astype(jnp.float32) + b_rg
    p_group = jax.nn.softmax(g_logits, axis=-1)
    g_idx = jnp.argmax(g_logits, axis=-1).astype(jnp.int32)
    p_g = jnp.take_along_axis(p_group, g_idx[:, None], axis=1)
    e_logits = (jnp.einsum('td,de->te', xt, w_re).astype(jnp.float32) + b_re)
    e_logits = e_logits.reshape(T, N_EXPERT_GROUPS, EXPERTS_PER_GROUP)
    sel = jnp.take_along_axis(e_logits, g_idx[:, None, None], axis=1)[:, 0]
    top_p, top_i = lax.top_k(jax.nn.softmax(sel, axis=-1), TOP_K)
    top_p = top_p / jnp.sum(top_p, axis=-1, keepdims=True)
    gates = (p_g * top_p).reshape(-1)
    eid = (g_idx[:, None] * EXPERTS_PER_GROUP + top_i).reshape(-1).astype(jnp.int32)
    tok = jnp.repeat(jnp.arange(T, dtype=jnp.int32), TOP_K)
    n_assign = T * TOP_K
    order = jnp.argsort(eid)
    e_s, tok_s, gate_s = eid[order], tok[order], gates[order]
    counts = jnp.bincount(eid, length=N_EXPERTS).astype(jnp.int32)
    starts = jnp.cumsum(counts) - counts
    padded = (counts + MOE_BLOCK - 1) // MOE_BLOCK * MOE_BLOCK
    p_ends = jnp.cumsum(padded)
    p_starts = p_ends - padded
    slot = p_starts[e_s] + jnp.arange(n_assign, dtype=jnp.int32) - starts[e_s]
    n_blocks = (n_assign + MOE_BLOCK - 1) // MOE_BLOCK + N_EXPERTS
    n_slots = n_blocks * MOE_BLOCK
    slot_tok = jnp.zeros((n_slots,), jnp.int32).at[slot].set(tok_s)
    slot_gate = jnp.zeros((n_slots,), jnp.float32).at[slot].set(gate_s)
    block_start = jnp.arange(n_blocks, dtype=jnp.int32) * MOE_BLOCK
    block_e = jnp.minimum(jnp.searchsorted(p_ends, block_start, side='right'),
                          N_EXPERTS - 1).astype(jnp.int32)

    def run_block(args):
        toks, g, e = args
        xb = xt[toks]
        hdn = jax.nn.silu(xb @ w_gate[e]) * (xb @ w_up[e])
        return (hdn @ w_down[e]) * g[:, None].astype(xb.dtype)

    y = lax.map(run_block, (slot_tok.reshape(n_blocks, MOE_BLOCK),
                            slot_gate.reshape(n_blocks, MOE_BLOCK), block_e))
    y = jax.ops.segment_sum(y.reshape(n_slots, d), slot_tok, num_segments=T)
    return y.reshape(bsz, L, d).astype(u.dtype)


def setup_inputs(seed: int = 0) -> dict:
    key = jax.random.key(seed)
    ks = jax.random.split(key, 32)
    f32 = jnp.float32

    def nrm(k, shape, scale):
        return jax.random.normal(k, shape, f32) * scale

    def gain(k, shape):
        return 1.0 + 0.02 * jax.random.normal(k, shape, f32)

    def dt_bias(k):
        u = jax.random.uniform(k, (DEPTH, SSD_HEADS), f32)
        dt = jnp.exp(u * (math.log(0.1) - math.log(0.001)) + math.log(0.001))
        return dt + jnp.log(-jnp.expm1(-dt))

    def a_log(k):
        return jnp.log(jax.random.uniform(k, (DEPTH, SSD_HEADS), f32, minval=1.0, maxval=16.0))

    return {
        'x': jax.random.normal(ks[0], (BATCH, SEQ, D_MODEL), f32),
        'norm_mix_w': gain(ks[1], (DEPTH, D_MODEL)),
        'w_in': nrm(ks[2], (DEPTH, D_MODEL, D_IN_PROJ), D_MODEL ** -0.5),
        'conv_w': nrm(ks[3], (DEPTH, SSD_CONV, SSD_CONV_CH), SSD_CONV ** -0.5),
        'conv_b': nrm(ks[4], (DEPTH, SSD_CONV_CH), 0.02),
        'dt_bias_fwd': dt_bias(ks[5]),
        'dt_bias_bwd': dt_bias(ks[6]),
        'a_log_fwd': a_log(ks[7]),
        'a_log_bwd': a_log(ks[8]),
        'ssd_d': 1.0 + 0.1 * jax.random.normal(ks[9], (DEPTH, SSD_HEADS), f32),
        'ssd_norm_w': gain(ks[10], (DEPTH, SSD_WIDTH)),
        'lambda_q1': nrm(ks[11], (DEPTH, DA_HEADDIM), 0.1),
        'lambda_k1': nrm(ks[12], (DEPTH, DA_HEADDIM), 0.1),
        'lambda_q2': nrm(ks[13], (DEPTH, DA_HEADDIM), 0.1),
        'lambda_k2': nrm(ks[14], (DEPTH, DA_HEADDIM), 0.1),
        'subln_w': gain(ks[15], (DEPTH, DA_VDIM)),
        'w_out': nrm(ks[16], (DEPTH, D_MIX, D_MODEL), D_MIX ** -0.5),
        'norm_ffn_w': gain(ks[17], (DEPTH, D_MODEL)),
        'w_router_group': nrm(ks[18], (DEPTH, D_MODEL, N_EXPERT_GROUPS), D_MODEL ** -0.5),
        'b_router_group': nrm(ks[19], (DEPTH, N_EXPERT_GROUPS), 0.01),
        'w_router_exp': nrm(ks[20], (DEPTH, D_MODEL, N_EXPERTS), D_MODEL ** -0.5),
        'b_router_exp': nrm(ks[21], (DEPTH, N_EXPERTS), 0.01),
        'w_exp_gate': nrm(ks[22], (DEPTH, N_EXPERTS, D_MODEL, D_EXPERT), D_MODEL ** -0.5),
        'w_exp_up': nrm(ks[23], (DEPTH, N_EXPERTS, D_MODEL, D_EXPERT), D_MODEL ** -0.5),
        'w_exp_down': nrm(ks[24], (DEPTH, N_EXPERTS, D_EXPERT, D_MODEL), D_EXPERT ** -0.5),
        'norm_final_w': gain(ks[25], (D_MODEL,)),
    }


def reference(x, norm_mix_w, w_in, conv_w, conv_b, dt_bias_fwd, dt_bias_bwd, a_log_fwd,
              a_log_bwd, ssd_d, ssd_norm_w, lambda_q1, lambda_k1, lambda_q2, lambda_k2,
              subln_w, w_out, norm_ffn_w, w_router_group, b_router_group, w_router_exp,
              b_router_exp, w_exp_gate, w_exp_up, w_exp_down, norm_final_w):
    splits = [int(s) for s in np.cumsum([COL_Z, COL_XBC, COL_DT, COL_Q, COL_K])]
    h = x
    for l in range(DEPTH):
        lambda_init = 0.8 - 0.6 * math.exp(-0.3 * l)
        u = rmsnorm(h, norm_mix_w[l])
        proj = jnp.einsum('bsd,de->bse', u, w_in[l])
        z, xbc, dt_raw, q, k, v = jnp.split(proj, splits, axis=-1)
        y_ssd = ssd_mixer(z, xbc, dt_raw, conv_w[l], conv_b[l], dt_bias_fwd[l], dt_bias_bwd[l],
                          a_log_fwd[l], a_log_bwd[l], ssd_d[l], ssd_norm_w[l])
        y_da = diff_attention(q, k, v, lambda_q1[l], lambda_k1[l], lambda_q2[l], lambda_k2[l],
                              subln_w[l], lambda_init)
        mix = jnp.concatenate([y_ssd.astype(h.dtype), y_da.astype(h.dtype)], axis=-1)
        h = h + jnp.einsum('bse,ed->bsd', mix, w_out[l])
        h = h + hier_moe(rmsnorm(h, norm_ffn_w[l]), w_router_group[l], b_router_group[l],
                         w_router_exp[l], b_router_exp[l], w_exp_gate[l], w_exp_up[l],
                         w_exp_down[l])
    return rmsnorm(h, norm_final_w)
```

```python
import functools
import math

import jax
import jax.numpy as jnp
from jax import lax
from jax.experimental import pallas as pl
from jax.experimental.pallas import tpu as pltpu

F32 = jnp.float32
BF16 = jnp.bfloat16

EPS = 1e-6
LANES = 128
SUBLANES = 8
VMEM_LIMIT = 48 * 1024 * 1024

SSD_WIDTH = 512
SSD_HEADDIM = 64
SSD_HEADS = 8
SSD_NGROUPS = 2
SSD_HPG = 4
SSD_DSTATE = 128
SSD_CONV = 5
SSD_CHUNK = 128
SSD_CONV_CH = 1024
DA_HEADDIM = 64
DA_VDIM = 128
DA_HEADS = 4
N_GROUPS = 4
EPG = 8
N_EXPERTS = 32
D_EXPERT = 512
LAMBDA_INIT = 0.8 - 0.6 * math.exp(-0.3 * 0)

TM_PROJ = 512
TQ = 256
TK = 512
TM_ROUTE = 512
TM_DISP = 512
MOE_ROWS = 256
TM_COMB = 256
EXP_LANE0 = 4


def _cparams(*sem):
    return pltpu.CompilerParams(dimension_semantics=sem, vmem_limit_bytes=VMEM_LIMIT)


def _dot(a, b):
    return jnp.dot(a, b, preferred_element_type=F32)


def _dot_nt(a, b):
    return lax.dot_general(a, b, (((1,), (1,)), ((), ())), preferred_element_type=F32)


def _split3(v):
    hi = v.astype(BF16)
    r1 = v - hi.astype(F32)
    mid = r1.astype(BF16)
    lo = (r1 - mid.astype(F32)).astype(BF16)
    return hi, mid, lo


def _dot_exact(a01, v):
    hi, mid, lo = _split3(v)
    return _dot(a01, hi) + _dot(a01, mid) + _dot(a01, lo)


def _dot_exact_r(v, b01):
    hi, mid, lo = _split3(v)
    return _dot(hi, b01) + _dot(mid, b01) + _dot(lo, b01)


def _silu(x):
    return x * (1.0 / (1.0 + jnp.exp(-x)))


def _inproj_kernel(x_ref, nw_ref, w_ref, wdt_ref, z_ref, xbc_ref, dt_ref, q_ref, k_ref, v_ref):
    x = x_ref[...]
    ms = jnp.mean(x * x, axis=-1, keepdims=True)
    u = (x * lax.rsqrt(ms + EPS) * nw_ref[...]).astype(BF16)
    z_ref[...] = _dot(u, w_ref[:, 0:512])
    xbc_ref[...] = _dot(u, w_ref[:, 512:1536])
    q_ref[...] = _dot(u, w_ref[:, 1536:2048]).astype(BF16)
    k_ref[...] = _dot(u, w_ref[:, 2048:2560]).astype(BF16)
    v_ref[...] = _dot(u, w_ref[:, 2560:3072]).astype(BF16)
    dt_ref[...] = _dot(u, wdt_ref[...])


def _in_proj(x2, norm_w, w_main, w_dt):
    T, D = x2.shape
    tm = TM_PROJ
    row = lambda n: pl.BlockSpec((tm, n), lambda i: (i, 0))
    full = lambda a: pl.BlockSpec(a.shape, lambda i: (0,) * a.ndim)
    return pl.pallas_call(
        _inproj_kernel,
        grid=(T // tm,),
        in_specs=[row(D), full(norm_w), full(w_main), full(w_dt)],
        out_specs=[row(512), row(1024), row(LANES), row(512), row(512), row(512)],
        out_shape=[
            jax.ShapeDtypeStruct((T, 512), F32),
            jax.ShapeDtypeStruct((T, 1024), F32),
            jax.ShapeDtypeStruct((T, LANES), F32),
            jax.ShapeDtypeStruct((T, 512), BF16),
            jax.ShapeDtypeStruct((T, 512), BF16),
            jax.ShapeDtypeStruct((T, 512), BF16),
        ],
        compiler_params=_cparams("arbitrary"),
        name="in_proj",
    )(x2, norm_w, w_main, w_dt)


def _ssd_kernel(*refs, reverse):
    if reverse:
        (xp_ref, xc_ref, xn_ref, dt_ref, cw_ref, cb_ref, dtb_ref, alog_ref, tri_ref, e_ref,
         z_ref, yf_ref, dsk_ref, nw_ref, out_ref, ext_ref, ht_ref) = refs
    else:
        (xp_ref, xc_ref, xn_ref, dt_ref, cw_ref, cb_ref, dtb_ref, alog_ref, tri_ref, e_ref,
         out_ref, ext_ref, ht_ref) = refs
    Q = SSD_CHUNK
    c = pl.program_id(1)
    nc = pl.num_programs(1)
    pos = (nc - 1 - c) if reverse else c

    @pl.when(c == 0)
    def _():
        ht_ref[...] = jnp.zeros_like(ht_ref)

    ext_ref[0:8, :] = jnp.where(pos > 0, xp_ref[0], 0.0)
    ext_ref[8:8 + Q, :] = xc_ref[0]
    ext_ref[8 + Q:16 + Q, :] = jnp.where(pos < nc - 1, xn_ref[0], 0.0)
    pad = (SSD_CONV - 1) // 2
    conv = cb_ref[...]
    for k in range(SSD_CONV):
        conv = conv + cw_ref[k:k + 1, :] * ext_ref[8 - pad + k:8 - pad + k + Q, :]
    act = _silu(conv)
    xs = act[:, :SSD_WIDTH]

    dtraw = dt_ref[0] + dtb_ref[...]
    dt = jnp.maximum(dtraw, 0.0) + jnp.log(1.0 + jnp.exp(-jnp.abs(dtraw)))
    a = -jnp.exp(alog_ref[...])
    adt = dt * a
    tri = tri_ref[...]
    cvec = _dot_exact(tri, adt)
    cvt = cvec.T
    e01 = e_ref[...]
    dt_e = _dot_exact_r(dt, e01)
    cv_e = _dot_exact_r(cvec, e01)
    edge = 0 if reverse else Q - 1
    tot_e = cv_e[edge:edge + 1, :]

    li = lax.broadcasted_iota(jnp.int32, (Q, Q), 0)
    si = lax.broadcasted_iota(jnp.int32, (Q, Q), 1)
    mask = (li <= si) if reverse else (li >= si)
    lane = lax.broadcasted_iota(jnp.int32, (Q, LANES), 1)
    head0 = SSD_HEADS if reverse else 0

    xdt = xs * dt_e
    ys = []
    for g in range(SSD_NGROUPS):
        bg = act[:, SSD_WIDTH + g * SSD_DSTATE:SSD_WIDTH + (g + 1) * SSD_DSTATE]
        cg = act[:, SSD_WIDTH + (SSD_NGROUPS + g) * SSD_DSTATE:SSD_WIDTH + (SSD_NGROUPS + g + 1) * SSD_DSTATE]
        bgb = bg.astype(BF16)
        cgb = cg.astype(BF16)
        gm = _dot_nt(cgb, bgb)
        gs = slice(g * 256, (g + 1) * 256)
        xdt_g = xdt[:, gs]
        xdtb = xdt_g.astype(BF16)
        ydiag = []
        for pr in range(2):
            xpair = xdtb[:, pr * LANES:(pr + 1) * LANES]
            yh = []
            for j in range(2):
                hl = head0 + g * SSD_HPG + pr * 2 + j
                seg = cvec[:, hl:hl + 1] - cvt[hl:hl + 1, :]
                decay = jnp.exp(jnp.where(mask, seg, -jnp.inf))
                yh.append(_dot((gm * decay).astype(BF16), xpair))
            ydiag.append(jnp.where(lane < SSD_HEADDIM, yh[0], yh[1]))
        ydiag = jnp.concatenate(ydiag, axis=1)
        ht = ht_ref[g]
        yoff = _dot(cgb, ht.astype(BF16)) * jnp.exp(cv_e[:, gs])
        ys.append(ydiag + yoff)
        wdec = jnp.exp(tot_e[:, gs] - cv_e[:, gs])
        xw = (xdt_g * wdec).astype(BF16)
        ht_ref[g] = jnp.exp(tot_e[:, gs]) * ht + _dot(bg.T.astype(BF16), xw)
    y = jnp.concatenate(ys, axis=1)

    if not reverse:
        out_ref[0] = y
    else:
        y = y + yf_ref[0] + dsk_ref[...] * xs
        gated = y * _silu(z_ref[0])
        ms = jnp.mean(gated * gated, axis=-1, keepdims=True)
        out_ref[0] = (gated * lax.rsqrt(ms + EPS) * nw_ref[...]).astype(out_ref.dtype)


def _ssd(xbc, dt, z, conv_w, conv_b, dtb, alog, dskip_e, norm_w, B, L):
    Q = SSD_CHUNK
    nc = L // Q
    hb = Q // SUBLANES
    nhb = L // SUBLANES
    r = jnp.arange(Q)
    tril = (r[:, None] >= r[None, :]).astype(BF16)
    triu = (r[:, None] <= r[None, :]).astype(BF16)
    lane = jnp.arange(LANES)[:, None]
    ch = jnp.arange(SSD_WIDTH)[None, :] // SSD_HEADDIM
    e_f = (lane == ch).astype(BF16)
    e_b = (lane == ch + SSD_HEADS).astype(BF16)

    def call(reverse, extra_in, extra_specs, out_dtype):
        pos = (lambda c: nc - 1 - c) if reverse else (lambda c: c)
        full = lambda a: pl.BlockSpec(a.shape, lambda b, c: (0,) * a.ndim)
        chunk = lambda n: pl.BlockSpec((1, Q, n), lambda b, c: (b, pos(c), 0))
        in_specs = [
            pl.BlockSpec((1, SUBLANES, 1024), lambda b, c: (b, jnp.maximum(pos(c) * hb - 1, 0), 0)),
            chunk(1024),
            pl.BlockSpec((1, SUBLANES, 1024), lambda b, c: (b, jnp.minimum((pos(c) + 1) * hb, nhb - 1), 0)),
            chunk(LANES),
        ]
        consts = [conv_w, conv_b, dtb, alog, triu if reverse else tril, e_b if reverse else e_f]
        in_specs += [full(a) for a in consts]
        in_specs += [chunk(512) if s == "chunk" else full(a) for a, s in zip(extra_in, extra_specs)]
        return pl.pallas_call(
            functools.partial(_ssd_kernel, reverse=reverse),
            grid=(B, nc),
            in_specs=in_specs,
            out_specs=chunk(512),
            out_shape=jax.ShapeDtypeStruct((B, L, 512), out_dtype),
            scratch_shapes=[pltpu.VMEM((Q + 16, 1024), F32),
                            pltpu.VMEM((SSD_NGROUPS, SSD_DSTATE, 256), F32)],
            compiler_params=_cparams("arbitrary", "arbitrary"),
            name="ssd_bwd" if reverse else "ssd_fwd",
        )(xbc, xbc, xbc, dt, *consts, *extra_in)

    yf = call(False, [], [], F32)
    return call(True, [z, yf, dskip_e, norm_w], ["chunk", "chunk", "full", "full"], BF16)


def _attn_kernel(slope_ref, lamv_ref, q_ref, k_ref, v_ref, sw_ref, o_ref, *, tq, tk, nk):
    h = pl.program_id(1)
    qi = pl.program_id(2)
    slope = slope_ref[h]
    lv = lamv_ref[...]
    lam = (jnp.exp(jnp.sum(lv[0:1] * lv[1:2], axis=-1, keepdims=True))
           - jnp.exp(jnp.sum(lv[2:3] * lv[3:4], axis=-1, keepdims=True)) + LAMBDA_INIT)

    q = q_ref[0]
    lane = lax.broadcasted_iota(jnp.int32, q.shape, 1)
    zero = jnp.zeros_like(q)
    qm = (jnp.where(lane < DA_HEADDIM, q, zero), jnp.where(lane >= DA_HEADDIM, q, zero))
    qk0 = (qi * tq + lax.broadcasted_iota(jnp.int32, (tq, tk), 0)
           - lax.broadcasted_iota(jnp.int32, (tq, tk), 1)).astype(F32)

    def body(j, carry):
        kb = k_ref[0, pl.ds(pl.multiple_of(j * tk, tk), tk), :]
        vb = v_ref[0, pl.ds(pl.multiple_of(j * tk, tk), tk), :]
        bias = slope * jnp.abs(qk0 - lax.convert_element_type(j * tk, F32))
        out = []
        for c in range(2):
            m, l, acc = carry[c]
            s = _dot_nt(qm[c], kb) - bias
            m_new = jnp.maximum(m, jnp.max(s, axis=-1, keepdims=True))
            p = jnp.exp(s - m_new)
            alpha = jnp.exp(m - m_new)
            l_new = alpha * l + jnp.sum(p, axis=-1, keepdims=True)
            acc_new = alpha * acc + _dot(p.astype(BF16), vb)
            out.append((m_new, l_new, acc_new))
        return tuple(out)

    init = tuple((jnp.full((tq, 1), -jnp.inf, F32), jnp.zeros((tq, 1), F32),
                  jnp.zeros((tq, DA_VDIM), F32)) for _ in range(2))
    (m1, l1, a1), (m2, l2, a2) = lax.fori_loop(0, nk, body, init)
    o = a1 * (1.0 / l1) - lam * (a2 * (1.0 / l2))
    ms = jnp.mean(o * o, axis=-1, keepdims=True)
    o = o * lax.rsqrt(ms + EPS) * sw_ref[...] * (1.0 - LAMBDA_INIT)
    o_ref[0] = o.astype(o_ref.dtype)


def _diff_attention(q, k, v, slopes, lamv, subln_w, B, L):
    tq, tk = min(TQ, L), min(TK, L)
    nk = L // tk
    return pl.pallas_call(
        functools.partial(_attn_kernel, tq=tq, tk=tk, nk=nk),
        grid=(B, DA_HEADS, L // tq),
        in_specs=[
            pl.BlockSpec(memory_space=pltpu.SMEM),
            pl.BlockSpec(lamv.shape, lambda b, h, i: (0, 0)),
            pl.BlockSpec((1, tq, LANES), lambda b, h, i: (b, i, h)),
            pl.BlockSpec((1, L, LANES), lambda b, h, i: (b, 0, h)),
            pl.BlockSpec((1, L, LANES), lambda b, h, i: (b, 0, h)),
            pl.BlockSpec(subln_w.shape, lambda b, h, i: (0, 0)),
        ],
        out_specs=pl.BlockSpec((1, tq, LANES), lambda b, h, i: (b, i, h)),
        out_shape=jax.ShapeDtypeStruct((B, L, 512), BF16),
        compiler_params=_cparams("arbitrary", "arbitrary", "arbitrary"),
        name="diff_attn",
    )(slopes, lamv, q, k, v, subln_w)


def _router_kernel(x_ref, ys_ref, ya_ref, wo_ref, nw_ref, wr_ref, br_ref, tri_ref,
                   h_ref, u_ref, meta_ref, cnt_ref, carry_ref):
    i = pl.program_id(0)

    @pl.when(i == 0)
    def _():
        carry_ref[...] = jnp.zeros_like(carry_ref)

    h = x_ref[...] + _dot(ys_ref[...], wo_ref[0:512, :]) + _dot(ya_ref[...], wo_ref[512:1024, :])
    h_ref[...] = h
    ms = jnp.mean(h * h, axis=-1, keepdims=True)
    u = h * lax.rsqrt(ms + EPS) * nw_ref[...]
    u_ref[...] = u
    logits = _dot(u.astype(BF16), wr_ref[...]) + br_ref[...]

    tm = logits.shape[0]
    li = lax.broadcasted_iota(jnp.int32, (tm, LANES), 1)
    lf = li.astype(F32)
    ninf = jnp.float32(-jnp.inf)
    big = jnp.float32(1e9)
    gl = jnp.where(li < N_GROUPS, logits, ninf)
    gmax = jnp.max(gl, axis=-1, keepdims=True)
    gidx = jnp.min(jnp.where(gl == gmax, lf, big), axis=-1, keepdims=True)
    pg = 1.0 / jnp.sum(jnp.exp(gl - gmax), axis=-1, keepdims=True)
    lane_grp = ((li - EXP_LANE0) >> 3).astype(F32)
    in_grp = (li >= EXP_LANE0) & (li < EXP_LANE0 + N_EXPERTS) & (lane_grp == gidx)
    sl = jnp.where(in_grp, logits, ninf)
    m1 = jnp.max(sl, axis=-1, keepdims=True)
    i1 = jnp.min(jnp.where(sl == m1, lf, big), axis=-1, keepdims=True)
    sl2 = jnp.where(lf == i1, ninf, sl)
    m2 = jnp.max(sl2, axis=-1, keepdims=True)
    i2 = jnp.min(jnp.where(sl2 == m2, lf, big), axis=-1, keepdims=True)
    t = jnp.exp(m2 - m1)
    w1 = 1.0 / (1.0 + t)
    g1 = pg * w1
    g2 = pg * (t * w1)
    oh1 = lf == i1
    oh2 = lf == i2
    oh = jnp.where(oh1 | oh2, 1.0, 0.0)
    prefix = _dot(tri_ref[...], oh.astype(BF16)) + carry_ref[0:1, :]
    r1 = jnp.sum(jnp.where(oh1, prefix, 0.0), axis=-1, keepdims=True)
    r2 = jnp.sum(jnp.where(oh2, prefix, 0.0), axis=-1, keepdims=True)
    new_carry = carry_ref[0:1, :] + jnp.sum(oh, axis=0, keepdims=True)
    carry_ref[...] = jnp.broadcast_to(new_carry, carry_ref.shape)
    cnt_ref[...] = jnp.broadcast_to(new_carry, cnt_ref.shape)

    meta = jnp.where(li == 0, i1 - EXP_LANE0,
           jnp.where(li == 1, i2 - EXP_LANE0,
           jnp.where(li == 2, r1,
           jnp.where(li == 3, r2,
           jnp.where(li == 4, g1,
           jnp.where(li == 5, g2, 0.0))))))
    meta_ref[...] = meta


def _out_router(x2, y_ssd, y_da, w_out, norm_w, w_r, b_r):
    T, D = x2.shape
    tm = min(TM_ROUTE, T)
    r = jnp.arange(tm)
    tri = (r[:, None] > r[None, :]).astype(BF16)
    row = lambda n: pl.BlockSpec((tm, n), lambda i: (i, 0))
    full = lambda a: pl.BlockSpec(a.shape, lambda i: (0,) * a.ndim)
    return pl.pallas_call(
        _router_kernel,
        grid=(T // tm,),
        in_specs=[row(D), row(512), row(512), full(w_out), full(norm_w), full(w_r), full(b_r), full(tri)],
        out_specs=[row(D), row(D), row(LANES), pl.BlockSpec((SUBLANES, LANES), lambda i: (0, 0))],
        out_shape=[
            jax.ShapeDtypeStruct((T, D), F32),
            jax.ShapeDtypeStruct((T, D), F32),
            jax.ShapeDtypeStruct((T, LANES), F32),
            jax.ShapeDtypeStruct((SUBLANES, LANES), F32),
        ],
        scratch_shapes=[pltpu.VMEM((SUBLANES, LANES), F32)],
        compiler_params=_cparams("arbitrary"),
        name="out_router",
    )(x2, y_ssd, y_da, w_out, norm_w, w_r, b_r, tri)


def _dispatch_kernel(cnt_ref, idx_ref, u_hbm, xs_hbm, slots_ref, be_ref, nused_ref,
                     pstart_ref, zero_ref, sem, *, tm, n_blocks):
    i = pl.program_id(0)

    @pl.when(i == 0)
    def _():
        zero_ref[...] = jnp.zeros_like(zero_ref)
        acc = jnp.int32(0)
        for e in range(N_EXPERTS):
            padded = ((cnt_ref[e] + (MOE_ROWS - 1)) // MOE_ROWS) * MOE_ROWS
            pstart_ref[e] = acc
            end = acc + padded

            def fill(j, _, e=e):
                be_ref[j] = jnp.int32(e)
                return 0

            lax.fori_loop(acc // MOE_ROWS, end // MOE_ROWS, fill, 0)

            @pl.when(padded > 0)
            def _(end=end):
                dst = xs_hbm.at[pl.ds(pl.multiple_of(end - MOE_ROWS, MOE_ROWS), MOE_ROWS)]
                cp = pltpu.make_async_copy(zero_ref, dst, sem)
                cp.start()
                cp.wait()

            acc = end
        nused = acc // MOE_ROWS
        nused_ref[0] = nused

        def fill_tail(j, _):
            be_ref[j] = jnp.int32(0)
            dst = xs_hbm.at[pl.ds(pl.multiple_of(j * MOE_ROWS, MOE_ROWS), MOE_ROWS)]
            cp = pltpu.make_async_copy(zero_ref, dst, sem)
            cp.start()
            cp.wait()
            return 0

        lax.fori_loop(nused, n_blocks, fill_tail, 0)

    def row_copy(tok, slot):
        return pltpu.make_async_copy(u_hbm.at[pl.ds(tok, 1)], xs_hbm.at[pl.ds(slot, 1)], sem)

    def start(t, _):
        tok = i * tm + t
        for kk in range(2):
            slot = pstart_ref[idx_ref[kk, t]] + idx_ref[2 + kk, t]
            slots_ref[kk, t] = slot
            row_copy(tok, slot).start()
        return 0

    lax.fori_loop(0, tm, start, 0)

    def wait(t, _):
        row_copy(0, 0).wait()
        row_copy(0, 0).wait()
        return 0

    lax.fori_loop(0, tm, wait, 0)


def _dispatch(counts, idx, u, n_blocks):
    T, D = u.shape
    tm = min(TM_DISP, T)
    n_slots = n_blocks * MOE_ROWS
    grid_spec = pltpu.PrefetchScalarGridSpec(
        num_scalar_prefetch=1,
        grid=(T // tm,),
        in_specs=[
            pl.BlockSpec((4, tm), lambda i, cnt: (0, i), memory_space=pltpu.SMEM),
            pl.BlockSpec(memory_space=pl.ANY),
        ],
        out_specs=[
            pl.BlockSpec(memory_space=pl.ANY),
            pl.BlockSpec((2, tm), lambda i, cnt: (0, i), memory_space=pltpu.SMEM),
            pl.BlockSpec(memory_space=pltpu.SMEM),
            pl.BlockSpec(memory_space=pltpu.SMEM),
        ],
        scratch_shapes=[pltpu.SMEM((N_EXPERTS,), jnp.int32),
                        pltpu.VMEM((MOE_ROWS, D), F32),
                        pltpu.SemaphoreType.DMA(())],
    )
    return pl.pallas_call(
        functools.partial(_dispatch_kernel, tm=tm, n_blocks=n_blocks),
        grid_spec=grid_spec,
        out_shape=[
            jax.ShapeDtypeStruct((n_slots, D), F32),
            jax.ShapeDtypeStruct((2, T), jnp.int32),
            jax.ShapeDtypeStruct((n_blocks,), jnp.int32),
            jax.ShapeDtypeStruct((1,), jnp.int32),
        ],
        compiler_params=_cparams("arbitrary"),
        name="dispatch",
    )(counts, idx, u)


def _moe_kernel(be_ref, nused_ref, xs_ref, wg_ref, wu_ref, wd_ref, o_ref):
    j = pl.program_id(0)

    @pl.when(j < nused_ref[0])
    def _():
        x = xs_ref[...].astype(BF16)
        hid = _silu(_dot(x, wg_ref[0])) * _dot(x, wu_ref[0])
        o_ref[...] = _dot(hid.astype(BF16), wd_ref[0])

    @pl.when(j >= nused_ref[0])
    def _():
        o_ref[...] = jnp.zeros_like(o_ref)


def _moe(block_e, nused, xs, wg, wu, wd, n_blocks):
    D = xs.shape[1]
    used = lambda j, be, nu: jnp.minimum(j, nu[0] - 1)
    wspec = lambda a: pl.BlockSpec((1,) + a.shape[1:], lambda j, be, nu: (be[used(j, be, nu)], 0, 0))
    grid_spec = pltpu.PrefetchScalarGridSpec(
        num_scalar_prefetch=2,
        grid=(n_blocks,),
        in_specs=[
            pl.BlockSpec((MOE_ROWS, D), lambda j, be, nu: (used(j, be, nu), 0)),
            wspec(wg), wspec(wu), wspec(wd),
        ],
        out_specs=pl.BlockSpec((MOE_ROWS, D), lambda j, be, nu: (j, 0)),
    )
    return pl.pallas_call(
        _moe_kernel,
        grid_spec=grid_spec,
        out_shape=jax.ShapeDtypeStruct((n_blocks * MOE_ROWS, D), F32),
        compiler_params=_cparams("arbitrary"),
        name="moe_mlp",
    )(block_e, nused, xs, wg, wu, wd)


def _combine_kernel(slots_ref, h_ref, meta_ref, nw_ref, eo_hbm, o_ref, g0_ref, g1_ref, sem, *, tm):
    def row_copy(slot, dst_ref, t):
        return pltpu.make_async_copy(eo_hbm.at[pl.ds(slot, 1)], dst_ref.at[pl.ds(t, 1)], sem)

    def start(t, _):
        row_copy(slots_ref[0, t], g0_ref, t).start()
        row_copy(slots_ref[1, t], g1_ref, t).start()
        return 0

    lax.fori_loop(0, tm, start, 0)

    def wait(t, _):
        row_copy(0, g0_ref, 0).wait()
        row_copy(0, g1_ref, 0).wait()
        return 0

    lax.fori_loop(0, tm, wait, 0)

    meta = meta_ref[...]
    y = h_ref[...] + g0_ref[...] * meta[:, 4:5] + g1_ref[...] * meta[:, 5:6]
    ms = jnp.mean(y * y, axis=-1, keepdims=True)
    o_ref[...] = y * lax.rsqrt(ms + EPS) * nw_ref[...]


def _combine(slots, h, meta, norm_w, eo):
    T, D = h.shape
    tm = min(TM_COMB, T)
    return pl.pallas_call(
        functools.partial(_combine_kernel, tm=tm),
        grid=(T // tm,),
        in_specs=[
            pl.BlockSpec((2, tm), lambda i: (0, i), memory_space=pltpu.SMEM),
            pl.BlockSpec((tm, D), lambda i: (i, 0)),
            pl.BlockSpec((tm, LANES), lambda i: (i, 0)),
            pl.BlockSpec(norm_w.shape, lambda i: (0, 0)),
            pl.BlockSpec(memory_space=pl.ANY),
        ],
        out_specs=pl.BlockSpec((tm, D), lambda i: (i, 0)),
        out_shape=jax.ShapeDtypeStruct((T, D), F32),
        scratch_shapes=[pltpu.VMEM((tm, D), F32), pltpu.VMEM((tm, D), F32), pltpu.SemaphoreType.DMA(())],
        compiler_params=_cparams("arbitrary"),
        name="combine",
    )(slots, h, meta, norm_w, eo)


def _pad_lanes(v, offset=0):
    v = v.astype(F32).reshape(-1)
    return jnp.zeros((1, LANES), F32).at[0, offset:offset + v.shape[0]].set(v)


def kernel(x, norm_mix_w, w_in, conv_w, conv_b, dt_bias_fwd, dt_bias_bwd, a_log_fwd, a_log_bwd, ssd_d,
           ssd_norm_w, lambda_q1, lambda_k1, lambda_q2, lambda_k2, subln_w, w_out, norm_ffn_w,
           w_router_group, b_router_group, w_router_exp, b_router_exp, w_exp_gate, w_exp_up, w_exp_down,
           norm_final_w):
    B, L, D = x.shape
    T = B * L
    x2 = x.reshape(T, D)
    l = 0

    w = w_in[l]
    o_z, o_xbc, o_dt = 512, 512 + 1024, 512 + 1024 + 16
    o_q, o_k = o_dt + 512, o_dt + 1024
    w_main = jnp.concatenate(
        [w[:, :o_xbc], w[:, o_dt:o_q] * (DA_HEADDIM ** -0.5), w[:, o_q:]], axis=1).astype(BF16)
    w_dt = jnp.zeros((D, LANES), F32).at[:, :16].set(w[:, o_xbc:o_dt]).astype(BF16)

    z, xbc, dt, q, k, v = _in_proj(x2, norm_mix_w[l].reshape(1, D), w_main, w_dt)

    dtb = _pad_lanes(jnp.concatenate([dt_bias_fwd[l], dt_bias_bwd[l]]))
    alog = _pad_lanes(jnp.concatenate([a_log_fwd[l], a_log_bwd[l]]))
    dskip_e = jnp.repeat(ssd_d[l].astype(F32), SSD_HEADDIM).reshape(1, SSD_WIDTH)
    y_ssd = _ssd(xbc.reshape(B, L, 1024), dt.reshape(B, L, LANES), z.reshape(B, L, 512),
                 conv_w[l].astype(F32), conv_b[l].reshape(1, -1).astype(F32), dtb, alog, dskip_e,
                 ssd_norm_w[l].reshape(1, -1).astype(F32), B, L)

    slopes = jnp.power(2.0, -8.0 * jnp.arange(1, DA_HEADS + 1, dtype=F32) / DA_HEADS)
    lamv = jnp.concatenate([_pad_lanes(lambda_q1[l]), _pad_lanes(lambda_k1[l]),
                            _pad_lanes(lambda_q2[l]), _pad_lanes(lambda_k2[l]),
                            jnp.zeros((4, LANES), F32)], axis=0)
    y_da = _diff_attention(q.reshape(B, L, 512), k.reshape(B, L, 512), v.reshape(B, L, 512),
                           slopes, lamv, subln_w[l].reshape(1, -1).astype(F32), B, L)

    w_r = (jnp.zeros((D, LANES), F32).at[:, :N_GROUPS].set(w_router_group[l])
           .at[:, EXP_LANE0:EXP_LANE0 + N_EXPERTS].set(w_router_exp[l])).astype(BF16)
    b_r = _pad_lanes(jnp.concatenate([b_router_group[l], b_router_exp[l]]))
    h, u, meta, cnt = _out_router(x2, y_ssd.reshape(T, 512), y_da.reshape(T, 512), w_out[l].astype(BF16),
                                  norm_ffn_w[l].reshape(1, D), w_r, b_r)

    n_blocks = (T * 2) // MOE_ROWS + N_EXPERTS
    counts = cnt[0, EXP_LANE0:EXP_LANE0 + N_EXPERTS].astype(jnp.int32)
    idx = meta[:, :4].astype(jnp.int32).T
    xs, slots, block_e, nused = _dispatch(counts, idx, u, n_blocks)
    eo = _moe(block_e, nused, xs, w_exp_gate[l].astype(BF16), w_exp_up[l].astype(BF16),
              w_exp_down[l].astype(BF16), n_blocks)
    out = _combine(slots, h, meta, norm_final_w.reshape(1, D), eo)
    return out.reshape(B, L, D)
```

```python
import functools
import math

import jax
import jax.numpy as jnp
from jax import lax
from jax.experimental import pallas as pl
from jax.experimental.pallas import tpu as pltpu

F32 = jnp.float32
BF16 = jnp.bfloat16

EPS = 1e-6
LOG2E = math.log2(math.e)
LANES = 128
SUBLANES = 8
VMEM_LIMIT = 48 * 1024 * 1024

SSD_WIDTH = 512
SSD_HEADDIM = 64
SSD_HEADS = 8
SSD_NGROUPS = 2
SSD_HPG = 4
SSD_DSTATE = 128
SSD_CONV = 5
SSD_CHUNK = 128
SSD_CONV_CH = 1024
DA_HEADDIM = 64
DA_VDIM = 128
DA_HEADS = 4
N_GROUPS = 4
EPG = 8
N_EXPERTS = 32
D_EXPERT = 512
LAMBDA_INIT = 0.8 - 0.6 * math.exp(-0.3 * 0)

TM_PROJ = 512
TQ = 512
TK = 512
TM_ROUTE = 512
TM_DISP = 512
MOE_ROWS = 256
TM_COMB = 256
EXP_LANE0 = 4


def _cparams(*sem):
    return pltpu.CompilerParams(dimension_semantics=sem, vmem_limit_bytes=VMEM_LIMIT)


def _dot(a, b):
    return jnp.dot(a, b, preferred_element_type=F32)


def _dot_nt(a, b):
    return lax.dot_general(a, b, (((1,), (1,)), ((), ())), preferred_element_type=F32)


def _split3(v):
    hi = v.astype(BF16)
    r1 = v - hi.astype(F32)
    mid = r1.astype(BF16)
    lo = (r1 - mid.astype(F32)).astype(BF16)
    return hi, mid, lo


def _dot_exact(a01, v):
    hi, mid, lo = _split3(v)
    return _dot(a01, hi) + _dot(a01, mid) + _dot(a01, lo)


def _dot_exact_r(v, b01):
    hi, mid, lo = _split3(v)
    return _dot(hi, b01) + _dot(mid, b01) + _dot(lo, b01)


def _silu(x):
    return x * (1.0 / (1.0 + jnp.exp(-x)))


def _inproj_kernel(x_ref, nw_ref, w_ref, wdt_ref, z_ref, xbc_ref, dt_ref, q_ref, k_ref, v_ref):
    x = x_ref[...]
    ms = jnp.mean(x * x, axis=-1, keepdims=True)
    u = (x * lax.rsqrt(ms + EPS) * nw_ref[...]).astype(BF16)
    z_ref[...] = _dot(u, w_ref[:, 0:512])
    xbc_ref[...] = _dot(u, w_ref[:, 512:1536])
    q_ref[...] = _dot(u, w_ref[:, 1536:2048]).astype(BF16)
    k_ref[...] = _dot(u, w_ref[:, 2048:2560]).astype(BF16)
    v_ref[...] = _dot(u, w_ref[:, 2560:3072]).astype(BF16)
    dt_ref[...] = _dot(u, wdt_ref[...])


def _in_proj(x2, norm_w, w_main, w_dt):
    T, D = x2.shape
    tm = TM_PROJ
    row = lambda n: pl.BlockSpec((tm, n), lambda i: (i, 0))
    full = lambda a: pl.BlockSpec(a.shape, lambda i: (0,) * a.ndim)
    return pl.pallas_call(
        _inproj_kernel,
        grid=(T // tm,),
        in_specs=[row(D), full(norm_w), full(w_main), full(w_dt)],
        out_specs=[row(512), row(1024), row(LANES), row(512), row(512), row(512)],
        out_shape=[
            jax.ShapeDtypeStruct((T, 512), F32),
            jax.ShapeDtypeStruct((T, 1024), F32),
            jax.ShapeDtypeStruct((T, LANES), F32),
            jax.ShapeDtypeStruct((T, 512), BF16),
            jax.ShapeDtypeStruct((T, 512), BF16),
            jax.ShapeDtypeStruct((T, 512), BF16),
        ],
        compiler_params=_cparams("arbitrary"),
        name="in_proj",
    )(x2, norm_w, w_main, w_dt)


def _ssd_kernel(*refs, reverse):
    if reverse:
        (xp_ref, xc_ref, xn_ref, dt_ref, cw_ref, cb_ref, dtb_ref, alog_ref, tri_ref, e_ref,
         z_ref, yf_ref, dsk_ref, nw_ref, out_ref, ext_ref, ht_ref) = refs
    else:
        (xp_ref, xc_ref, xn_ref, dt_ref, cw_ref, cb_ref, dtb_ref, alog_ref, tri_ref, e_ref,
         out_ref, ext_ref, ht_ref) = refs
    Q = SSD_CHUNK
    c = pl.program_id(1)
    nc = pl.num_programs(1)
    pos = (nc - 1 - c) if reverse else c

    @pl.when(c == 0)
    def _():
        ht_ref[...] = jnp.zeros_like(ht_ref)

    ext_ref[0:8, :] = jnp.where(pos > 0, xp_ref[0], 0.0)
    ext_ref[8:8 + Q, :] = xc_ref[0]
    ext_ref[8 + Q:16 + Q, :] = jnp.where(pos < nc - 1, xn_ref[0], 0.0)
    pad = (SSD_CONV - 1) // 2
    conv = cb_ref[...]
    for k in range(SSD_CONV):
        conv = conv + cw_ref[k:k + 1, :] * ext_ref[8 - pad + k:8 - pad + k + Q, :]
    act = _silu(conv)
    xs = act[:, :SSD_WIDTH]

    dtraw = dt_ref[0] + dtb_ref[...]
    dt = jnp.maximum(dtraw, 0.0) + jnp.log(1.0 + jnp.exp(-jnp.abs(dtraw)))
    a = -jnp.exp(alog_ref[...])
    adt = dt * a
    tri = tri_ref[...]
    cvec = _dot_exact(tri, adt)
    cvt = cvec.T
    e01 = e_ref[...]
    dt_e = _dot_exact_r(dt, e01)
    cv_e = _dot_exact_r(cvec, e01)
    edge = 0 if reverse else Q - 1
    tot_e = cv_e[edge:edge + 1, :]

    li = lax.broadcasted_iota(jnp.int32, (Q, Q), 0)
    si = lax.broadcasted_iota(jnp.int32, (Q, Q), 1)
    mask = (li <= si) if reverse else (li >= si)
    lane = lax.broadcasted_iota(jnp.int32, (Q, LANES), 1)
    head0 = SSD_HEADS if reverse else 0

    xdt = xs * dt_e
    ys = []
    for g in range(SSD_NGROUPS):
        bg = act[:, SSD_WIDTH + g * SSD_DSTATE:SSD_WIDTH + (g + 1) * SSD_DSTATE]
        cg = act[:, SSD_WIDTH + (SSD_NGROUPS + g) * SSD_DSTATE:SSD_WIDTH + (SSD_NGROUPS + g + 1) * SSD_DSTATE]
        bgb = bg.astype(BF16)
        cgb = cg.astype(BF16)
        gm = _dot_nt(cgb, bgb)
        gs = slice(g * 256, (g + 1) * 256)
        xdt_g = xdt[:, gs]
        xdtb = xdt_g.astype(BF16)
        ydiag = []
        for pr in range(2):
            xpair = xdtb[:, pr * LANES:(pr + 1) * LANES]
            yh = []
            for j in range(2):
                hl = head0 + g * SSD_HPG + pr * 2 + j
                seg = cvec[:, hl:hl + 1] - cvt[hl:hl + 1, :]
                decay = jnp.exp(jnp.where(mask, seg, -jnp.inf))
                yh.append(_dot((gm * decay).astype(BF16), xpair))
            ydiag.append(jnp.where(lane < SSD_HEADDIM, yh[0], yh[1]))
        ydiag = jnp.concatenate(ydiag, axis=1)
        ht = ht_ref[g]
        yoff = _dot(cgb, ht.astype(BF16)) * jnp.exp(cv_e[:, gs])
        ys.append(ydiag + yoff)
        wdec = jnp.exp(tot_e[:, gs] - cv_e[:, gs])
        xw = (xdt_g * wdec).astype(BF16)
        ht_ref[g] = jnp.exp(tot_e[:, gs]) * ht + _dot(bg.T.astype(BF16), xw)
    y = jnp.concatenate(ys, axis=1)

    if not reverse:
        out_ref[0] = y
    else:
        y = y + yf_ref[0] + dsk_ref[...] * xs
        gated = y * _silu(z_ref[0])
        ms = jnp.mean(gated * gated, axis=-1, keepdims=True)
        out_ref[0] = (gated * lax.rsqrt(ms + EPS) * nw_ref[...]).astype(out_ref.dtype)


def _ssd(xbc, dt, z, conv_w, conv_b, dtb, alog, dskip_e, norm_w, B, L):
    Q = SSD_CHUNK
    nc = L // Q
    hb = Q // SUBLANES
    nhb = L // SUBLANES
    r = jnp.arange(Q)
    tril = (r[:, None] >= r[None, :]).astype(BF16)
    triu = (r[:, None] <= r[None, :]).astype(BF16)
    lane = jnp.arange(LANES)[:, None]
    ch = jnp.arange(SSD_WIDTH)[None, :] // SSD_HEADDIM
    e_f = (lane == ch).astype(BF16)
    e_b = (lane == ch + SSD_HEADS).astype(BF16)

    def call(reverse, extra_in, extra_specs, out_dtype):
        pos = (lambda c: nc - 1 - c) if reverse else (lambda c: c)
        full = lambda a: pl.BlockSpec(a.shape, lambda b, c: (0,) * a.ndim)
        chunk = lambda n: pl.BlockSpec((1, Q, n), lambda b, c: (b, pos(c), 0))
        in_specs = [
            pl.BlockSpec((1, SUBLANES, 1024), lambda b, c: (b, jnp.maximum(pos(c) * hb - 1, 0), 0)),
            chunk(1024),
            pl.BlockSpec((1, SUBLANES, 1024), lambda b, c: (b, jnp.minimum((pos(c) + 1) * hb, nhb - 1), 0)),
            chunk(LANES),
        ]
        consts = [conv_w, conv_b, dtb, alog, triu if reverse else tril, e_b if reverse else e_f]
        in_specs += [full(a) for a in consts]
        in_specs += [chunk(512) if s == "chunk" else full(a) for a, s in zip(extra_in, extra_specs)]
        return pl.pallas_call(
            functools.partial(_ssd_kernel, reverse=reverse),
            grid=(B, nc),
            in_specs=in_specs,
            out_specs=chunk(512),
            out_shape=jax.ShapeDtypeStruct((B, L, 512), out_dtype),
            scratch_shapes=[pltpu.VMEM((Q + 16, 1024), F32),
                            pltpu.VMEM((SSD_NGROUPS, SSD_DSTATE, 256), F32)],
            compiler_params=_cparams("arbitrary", "arbitrary"),
            name="ssd_bwd" if reverse else "ssd_fwd",
        )(xbc, xbc, xbc, dt, *consts, *extra_in)

    yf = call(False, [], [], F32)
    return call(True, [z, yf, dskip_e, norm_w], ["chunk", "chunk", "full", "full"], BF16)


N_POS = 3
N_FEAT = 2 * N_POS + 1
MAX_SHIFT_LOG2 = 50.0
VT_ROWS = DA_VDIM + 16


def _attn_kernel(slope_ref, lamv_ref, q_ref, k_ref, v_ref, sw_ref, o_ref,
                 kaug_ref, vt_ref, pt_ref, kmax_ref, *, tq, tk, L):
    h = pl.program_id(1)
    qi = pl.program_id(2)
    nb = L // tq
    sl2 = slope_ref[h] * LOG2E
    lv = lamv_ref[...]
    lam = (jnp.exp(jnp.sum(lv[0:1] * lv[1:2], axis=-1, keepdims=True))
           - jnp.exp(jnp.sum(lv[2:3] * lv[3:4], axis=-1, keepdims=True)) + LAMBDA_INIT)

    def feat0(c):
        return DA_HEADDIM if c == 0 else 0

    def own_lanes(lane, c):
        return (lane < DA_HEADDIM) if c == 0 else (lane >= DA_HEADDIM)

    ha = lax.broadcasted_iota(jnp.int32, (LANES, LANES), 0) < DA_HEADDIM
    hb = lax.broadcasted_iota(jnp.int32, (LANES, LANES), 1) < DA_HEADDIM
    other_half = jnp.where(ha != hb, 1.0, 0.0).astype(BF16)

    @pl.when(qi == 0)
    def _prep():
        kf = k_ref[0].astype(F32)
        lane = lax.broadcasted_iota(jnp.int32, kf.shape, 1)
        kpos = lax.broadcasted_iota(jnp.int32, kf.shape, 0).astype(F32) * sl2
        pieces = [t.astype(F32) for t in _split3(kpos)]
        for c in range(2):
            rel = lane - feat0(c)
            feat = jnp.where((rel >= 0) & (rel <= N_POS), 1.0, 0.0)
            for n in range(N_POS):
                feat = jnp.where(rel == N_POS + 1 + n, pieces[n], feat)
            kaug_ref[c] = jnp.where(own_lanes(lane, c), kf, feat).astype(BF16)
        kn2 = _dot((kf * kf).astype(BF16), other_half)
        kmax_ref[...] = jnp.broadcast_to(jnp.sqrt(jnp.max(kn2, axis=0, keepdims=True)), kmax_ref.shape)
        vt_ref[0:DA_VDIM, :] = v_ref[0].astype(F32).T.astype(BF16)
        ones_row = lax.broadcasted_iota(jnp.int32, (VT_ROWS - DA_VDIM, L), 0) == 0
        vt_ref[DA_VDIM:, :] = jnp.where(ones_row, 1.0, 0.0).astype(BF16)

    qf = q_ref[0].astype(F32)
    lane = lax.broadcasted_iota(jnp.int32, qf.shape, 1)
    qpos = (qi * tq + lax.broadcasted_iota(jnp.int32, qf.shape, 0)).astype(F32) * sl2
    qpieces = [t.astype(F32) for t in _split3(qpos)]

    qn2 = _dot((qf * qf).astype(BF16), other_half)
    shift = (jnp.sqrt(qn2) * kmax_ref[0:1, :] * 1.02).astype(BF16).astype(F32)
    worst = jnp.max(shift)

    def finish(o1, o2):
        o = o1 - lam * o2
        ms = jnp.mean(o * o, axis=-1, keepdims=True)
        o = o * lax.rsqrt(ms + EPS) * sw_ref[...] * (1.0 - LAMBDA_INIT)
        o_ref[0] = o.astype(o_ref.dtype)

    @pl.when(worst <= MAX_SHIFT_LOG2)
    def _fast():
        kr = lax.broadcasted_iota(jnp.int32, (tq, tq), 0)
        qc = lax.broadcasted_iota(jnp.int32, (tq, tq), 1)
        corr = jnp.maximum(qc - kr, 0).astype(F32) * (2.0 * sl2)
        outs = []
        for c in range(2):
            rel = lane - feat0(c)
            base = jnp.where(own_lanes(lane, c), qf, jnp.where(rel == N_POS, -shift, 0.0))
            posf = jnp.where((rel > N_POS) & (rel < N_FEAT), -1.0, 0.0)
            for n in range(N_POS):
                posf = jnp.where(rel == n, qpieces[n], posf)
            q_after = (base + posf).astype(BF16)
            q_before = (base - posf).astype(BF16)
            for d in range(nb):
                wrapped = qi + d >= nb
                jb = jnp.where(wrapped, qi + d - nb, qi + d)
                dk = pl.ds(pl.multiple_of(jb * tq, tq), tq)
                qsel = q_after if d == 0 else jnp.where(wrapped, q_before, q_after)
                st = _dot_nt(kaug_ref[c, dk, :], qsel)
                if d == 0:
                    st = st - corr
                pt_ref[c, dk, :] = jnp.exp2(st).astype(BF16)
            acc = _dot(vt_ref[...], pt_ref[c])
            outs.append(acc[:DA_VDIM] * (1.0 / acc[DA_VDIM:DA_VDIM + 1]))
        finish(outs[0].T, outs[1].T)

    @pl.when(worst > MAX_SHIFT_LOG2)
    def _online():
        zero = jnp.zeros_like(qf)
        qm = [jnp.where(own_lanes(lane, c), qf, zero).astype(BF16) for c in range(2)]
        qk0 =(qi * tq + lax.broadcasted_iota(jnp.int32, (tq, tk), 0)
               - lax.broadcasted_iota(jnp.int32, (tq, tk), 1)).astype(F32)

        def body(j, carry):
            kb = k_ref[0, pl.ds(pl.multiple_of(j * tk, tk), tk), :]
            vb = v_ref[0, pl.ds(pl.multiple_of(j * tk, tk), tk), :]
            bias = sl2 * jnp.abs(qk0 - lax.convert_element_type(j * tk, F32))
            out = []
            for c in range(2):
                m, l, acc = carry[c]
                s = _dot_nt(qm[c], kb) - bias
                m_new = jnp.maximum(m, jnp.max(s, axis=-1, keepdims=True))
                p = jnp.exp2(s - m_new)
                alpha = jnp.exp2(m - m_new)
                l_new = alpha * l + jnp.sum(p, axis=-1, keepdims=True)
                acc_new = alpha * acc + _dot(p.astype(BF16), vb)
                out.append((m_new, l_new, acc_new))
            return tuple(out)

        init = tuple((jnp.full((tq, 1), -jnp.inf, F32), jnp.zeros((tq, 1), F32),
                      jnp.zeros((tq, DA_VDIM), F32)) for _ in range(2))
        (m1, l1, a1), (m2, l2, a2) = lax.fori_loop(0, L // tk, body, init)
        finish(a1 * (1.0 / l1), a2 * (1.0 / l2))


def _diff_attention(q, k, v, slopes, lamv, subln_w, B, L):
    tq, tk = min(TQ, L), min(TK, L)
    return pl.pallas_call(
        functools.partial(_attn_kernel, tq=tq, tk=tk, L=L),
        grid=(B, DA_HEADS, L // tq),
        in_specs=[
            pl.BlockSpec(memory_space=pltpu.SMEM),
            pl.BlockSpec(lamv.shape, lambda b, h, i: (0, 0)),
            pl.BlockSpec((1, tq, LANES), lambda b, h, i: (b, i, h)),
            pl.BlockSpec((1, L, LANES), lambda b, h, i: (b, 0, h)),
            pl.BlockSpec((1, L, LANES), lambda b, h, i: (b, 0, h)),
            pl.BlockSpec(subln_w.shape, lambda b, h, i: (0, 0)),
        ],
        out_specs=pl.BlockSpec((1, tq, LANES), lambda b, h, i: (b, i, h)),
        out_shape=jax.ShapeDtypeStruct((B, L, 512), BF16),
        scratch_shapes=[pltpu.VMEM((2, L, LANES), BF16),
                        pltpu.VMEM((VT_ROWS, L), BF16),
                        pltpu.VMEM((2, L, tq), BF16),
                        pltpu.VMEM((SUBLANES, LANES), F32)],
        compiler_params=_cparams("arbitrary", "arbitrary", "arbitrary"),
        name="diff_attn",
    )(slopes, lamv, q, k, v, subln_w)


def _router_kernel(x_ref, ys_ref, ya_ref, wo_ref, nw_ref, wr_ref, br_ref, tri_ref,
                   h_ref, u_ref, meta_ref, cnt_ref, carry_ref):
    i = pl.program_id(0)

    @pl.when(i == 0)
    def _():
        carry_ref[...] = jnp.zeros_like(carry_ref)

    h = x_ref[...] + _dot(ys_ref[...], wo_ref[0:512, :]) + _dot(ya_ref[...], wo_ref[512:1024, :])
    h_ref[...] = h
    ms = jnp.mean(h * h, axis=-1, keepdims=True)
    u = h * lax.rsqrt(ms + EPS) * nw_ref[...]
    u_ref[...] = u
    logits = _dot(u.astype(BF16), wr_ref[...]) + br_ref[...]

    tm = logits.shape[0]
    li = lax.broadcasted_iota(jnp.int32, (tm, LANES), 1)
    lf = li.astype(F32)
    ninf = jnp.float32(-jnp.inf)
    big = jnp.float32(1e9)
    gl = jnp.where(li < N_GROUPS, logits, ninf)
    gmax = jnp.max(gl, axis=-1, keepdims=True)
    gidx = jnp.min(jnp.where(gl == gmax, lf, big), axis=-1, keepdims=True)
    pg = 1.0 / jnp.sum(jnp.exp(gl - gmax), axis=-1, keepdims=True)
    lane_grp = ((li - EXP_LANE0) >> 3).astype(F32)
    in_grp = (li >= EXP_LANE0) & (li < EXP_LANE0 + N_EXPERTS) & (lane_grp == gidx)
    sl = jnp.where(in_grp, logits, ninf)
    m1 = jnp.max(sl, axis=-1, keepdims=True)
    i1 = jnp.min(jnp.where(sl == m1, lf, big), axis=-1, keepdims=True)
    sl2 = jnp.where(lf == i1, ninf, sl)
    m2 = jnp.max(sl2, axis=-1, keepdims=True)
    i2 = jnp.min(jnp.where(sl2 == m2, lf, big), axis=-1, keepdims=True)
    t = jnp.exp(m2 - m1)
    w1 = 1.0 / (1.0 + t)
    g1 = pg * w1
    g2 = pg * (t * w1)
    oh1 = lf == i1
    oh2 = lf == i2
    oh = jnp.where(oh1 | oh2, 1.0, 0.0)
    prefix = _dot(tri_ref[...], oh.astype(BF16)) + carry_ref[0:1, :]
    r1 = jnp.sum(jnp.where(oh1, prefix, 0.0), axis=-1, keepdims=True)
    r2 = jnp.sum(jnp.where(oh2, prefix, 0.0), axis=-1, keepdims=True)
    new_carry = carry_ref[0:1, :] + jnp.sum(oh, axis=0, keepdims=True)
    carry_ref[...] = jnp.broadcast_to(new_carry, carry_ref.shape)
    cnt_ref[...] = jnp.broadcast_to(new_carry, cnt_ref.shape)

    meta = jnp.where(li == 0, i1 - EXP_LANE0,
           jnp.where(li == 1, i2 - EXP_LANE0,
           jnp.where(li == 2, r1,
           jnp.where(li == 3, r2,
           jnp.where(li == 4, g1,
           jnp.where(li == 5, g2, 0.0))))))
    meta_ref[...] = meta


def _out_router(x2, y_ssd, y_da, w_out, norm_w, w_r, b_r):
    T, D = x2.shape
    tm = min(TM_ROUTE, T)
    r = jnp.arange(tm)
    tri = (r[:, None] > r[None, :]).astype(BF16)
    row = lambda n: pl.BlockSpec((tm, n), lambda i: (i, 0))
    full = lambda a: pl.BlockSpec(a.shape, lambda i: (0,) * a.ndim)
    return pl.pallas_call(
        _router_kernel,
        grid=(T // tm,),
        in_specs=[row(D), row(512), row(512), full(w_out), full(norm_w), full(w_r), full(b_r), full(tri)],
        out_specs=[row(D), row(D), row(LANES), pl.BlockSpec((SUBLANES, LANES), lambda i: (0, 0))],
        out_shape=[
            jax.ShapeDtypeStruct((T, D), F32),
            jax.ShapeDtypeStruct((T, D), F32),
            jax.ShapeDtypeStruct((T, LANES), F32),
            jax.ShapeDtypeStruct((SUBLANES, LANES), F32),
        ],
        scratch_shapes=[pltpu.VMEM((SUBLANES, LANES), F32)],
        compiler_params=_cparams("arbitrary"),
        name="out_router",
    )(x2, y_ssd, y_da, w_out, norm_w, w_r, b_r, tri)


def _dispatch_kernel(cnt_ref, idx_ref, u_ref, xs_hbm, slots_ref, be_ref, nused_ref,
                     pstart_ref, zero_ref, sem, *, tm, n_blocks):
    i = pl.program_id(0)

    @pl.when(i == 0)
    def _():
        zero_ref[...] = jnp.zeros_like(zero_ref)
        acc = jnp.int32(0)
        for e in range(N_EXPERTS):
            padded = ((cnt_ref[e] + (MOE_ROWS - 1)) // MOE_ROWS) * MOE_ROWS
            pstart_ref[e] = acc
            end = acc + padded

            def fill(j, _, e=e):
                be_ref[j] = jnp.int32(e)
                return 0

            lax.fori_loop(acc // MOE_ROWS, end // MOE_ROWS, fill, 0)

            @pl.when(padded > 0)
            def _(end=end):
                dst = xs_hbm.at[pl.ds(pl.multiple_of(end - MOE_ROWS, MOE_ROWS), MOE_ROWS)]
                cp = pltpu.make_async_copy(zero_ref, dst, sem)
                cp.start()
                cp.wait()

            acc = end
        nused = acc // MOE_ROWS
        nused_ref[0] = nused

        def fill_tail(j, _):
            be_ref[j] = jnp.int32(0)
            dst = xs_hbm.at[pl.ds(pl.multiple_of(j * MOE_ROWS, MOE_ROWS), MOE_ROWS)]
            cp = pltpu.make_async_copy(zero_ref, dst, sem)
            cp.start()
            cp.wait()
            return 0

        lax.fori_loop(nused, n_blocks, fill_tail, 0)

    def row_copy(t, slot):
        return pltpu.make_async_copy(u_ref.at[pl.ds(t, 1)], xs_hbm.at[pl.ds(slot, 1)], sem)

    def start(t, _):
        for kk in range(2):
            slot = pstart_ref[idx_ref[kk, t]] + idx_ref[2 + kk, t]
            slots_ref[kk, t] = slot
            row_copy(t, slot).start()
        return 0

    lax.fori_loop(0, tm, start, 0)

    def wait(t, _):
        row_copy(0, 0).wait()
        row_copy(0, 0).wait()
        return 0

    lax.fori_loop(0, tm, wait, 0)


def _dispatch(counts, idx, u, n_blocks):
    T, D = u.shape
    tm = min(TM_DISP, T)
    n_slots = n_blocks * MOE_ROWS
    grid_spec = pltpu.PrefetchScalarGridSpec(
        num_scalar_prefetch=1,
        grid=(T // tm,),
        in_specs=[
            pl.BlockSpec((4, tm), lambda i, cnt: (0, i), memory_space=pltpu.SMEM),
            pl.BlockSpec((tm, D), lambda i, cnt: (i, 0)),
        ],
        out_specs=[
            pl.BlockSpec(memory_space=pl.ANY),
            pl.BlockSpec((2, tm), lambda i, cnt: (0, i), memory_space=pltpu.SMEM),
            pl.BlockSpec(memory_space=pltpu.SMEM),
            pl.BlockSpec(memory_space=pltpu.SMEM),
        ],
        scratch_shapes=[pltpu.SMEM((N_EXPERTS,), jnp.int32),
                        pltpu.VMEM((MOE_ROWS, D), F32),
                        pltpu.SemaphoreType.DMA(())],
    )
    return pl.pallas_call(
        functools.partial(_dispatch_kernel, tm=tm, n_blocks=n_blocks),
        grid_spec=grid_spec,
        out_shape=[
            jax.ShapeDtypeStruct((n_slots, D), F32),
            jax.ShapeDtypeStruct((2, T), jnp.int32),
            jax.ShapeDtypeStruct((n_blocks,), jnp.int32),
            jax.ShapeDtypeStruct((1,), jnp.int32),
        ],
        compiler_params=_cparams("arbitrary"),
        name="dispatch",
    )(counts, idx, u)


def _moe_kernel(be_ref, nused_ref, xs_ref, wg_ref, wu_ref, wd_ref, o_ref):
    j = pl.program_id(0)

    @pl.when(j < nused_ref[0])
    def _():
        x = xs_ref[...].astype(BF16)
        hid = _silu(_dot(x, wg_ref[0])) * _dot(x, wu_ref[0])
        o_ref[...] = _dot(hid.astype(BF16), wd_ref[0])

    @pl.when(j >= nused_ref[0])
    def _():
        o_ref[...] = jnp.zeros_like(o_ref)


def _moe(block_e, nused, xs, wg, wu, wd, n_blocks):
    D = xs.shape[1]
    used = lambda j, be, nu: jnp.minimum(j, nu[0] - 1)
    wspec = lambda a: pl.BlockSpec((1,) + a.shape[1:], lambda j, be, nu: (be[used(j, be, nu)], 0, 0))
    grid_spec = pltpu.PrefetchScalarGridSpec(
        num_scalar_prefetch=2,
        grid=(n_blocks,),
        in_specs=[
            pl.BlockSpec((MOE_ROWS, D), lambda j, be, nu: (used(j, be, nu), 0)),
            wspec(wg), wspec(wu), wspec(wd),
        ],
        out_specs=pl.BlockSpec((MOE_ROWS, D), lambda j, be, nu: (j, 0)),
    )
    return pl.pallas_call(
        _moe_kernel,
        grid_spec=grid_spec,
        out_shape=jax.ShapeDtypeStruct((n_blocks * MOE_ROWS, D), F32),
        compiler_params=_cparams("arbitrary"),
        name="moe_mlp",
    )(block_e, nused, xs, wg, wu, wd)


def _combine_kernel(slots_ref, h_ref, meta_ref, nw_ref, eo_hbm, o_ref, g0_ref, g1_ref, sem, *, tm):
    def row_copy(slot, dst_ref, t):
        return pltpu.make_async_copy(eo_hbm.at[pl.ds(slot, 1)], dst_ref.at[pl.ds(t, 1)], sem)

    def start(t, _):
        row_copy(slots_ref[0, t], g0_ref, t).start()
        row_copy(slots_ref[1, t], g1_ref, t).start()
        return 0

    lax.fori_loop(0, tm, start, 0)

    def wait(t, _):
        row_copy(0, g0_ref, 0).wait()
        row_copy(0, g1_ref, 0).wait()
        return 0

    lax.fori_loop(0, tm, wait, 0)

    meta = meta_ref[...]
    y = h_ref[...] + g0_ref[...] * meta[:, 4:5] + g1_ref[...] * meta[:, 5:6]
    ms = jnp.mean(y * y, axis=-1, keepdims=True)
    o_ref[...] = y * lax.rsqrt(ms + EPS) * nw_ref[...]


def _combine(slots, h, meta, norm_w, eo):
    T, D = h.shape
    tm = min(TM_COMB, T)
    return pl.pallas_call(
        functools.partial(_combine_kernel, tm=tm),
        grid=(T // tm,),
        in_specs=[
            pl.BlockSpec((2, tm), lambda i: (0, i), memory_space=pltpu.SMEM),
            pl.BlockSpec((tm, D), lambda i: (i, 0)),
            pl.BlockSpec((tm, LANES), lambda i: (i, 0)),
            pl.BlockSpec(norm_w.shape, lambda i: (0, 0)),
            pl.BlockSpec(memory_space=pl.ANY),
        ],
        out_specs=pl.BlockSpec((tm, D), lambda i: (i, 0)),
        out_shape=jax.ShapeDtypeStruct((T, D), F32),
        scratch_shapes=[pltpu.VMEM((tm, D), F32), pltpu.VMEM((tm, D), F32), pltpu.SemaphoreType.DMA(())],
        compiler_params=_cparams("arbitrary"),
        name="combine",
    )(slots, h, meta, norm_w, eo)


def _pad_lanes(v, offset=0):
    v = v.astype(F32).reshape(-1)
    return jnp.zeros((1, LANES), F32).at[0, offset:offset + v.shape[0]].set(v)


def kernel(x, norm_mix_w, w_in, conv_w, conv_b, dt_bias_fwd, dt_bias_bwd, a_log_fwd, a_log_bwd, ssd_d,
           ssd_norm_w, lambda_q1, lambda_k1, lambda_q2, lambda_k2, subln_w, w_out, norm_ffn_w,
           w_router_group, b_router_group, w_router_exp, b_router_exp, w_exp_gate, w_exp_up, w_exp_down,
           norm_final_w):
    B, L, D = x.shape
    T = B * L
    x2 = x.reshape(T, D)
    l = 0

    w = w_in[l]
    o_z, o_xbc, o_dt = 512, 512 + 1024, 512 + 1024 + 16
    o_q, o_k = o_dt + 512, o_dt + 1024
    w_main = jnp.concatenate(
        [w[:, :o_xbc], w[:, o_dt:o_q] * (DA_HEADDIM ** -0.5 * LOG2E), w[:, o_q:]], axis=1).astype(BF16)
    w_dt = jnp.zeros((D, LANES), F32).at[:, :16].set(w[:, o_xbc:o_dt]).astype(BF16)

    z, xbc, dt, q, k, v = _in_proj(x2, norm_mix_w[l].reshape(1, D), w_main, w_dt)

    dtb = _pad_lanes(jnp.concatenate([dt_bias_fwd[l], dt_bias_bwd[l]]))
    alog = _pad_lanes(jnp.concatenate([a_log_fwd[l], a_log_bwd[l]]))
    dskip_e = jnp.repeat(ssd_d[l].astype(F32), SSD_HEADDIM).reshape(1, SSD_WIDTH)
    y_ssd = _ssd(xbc.reshape(B, L, 1024), dt.reshape(B, L, LANES), z.reshape(B, L, 512),
                 conv_w[l].astype(F32), conv_b[l].reshape(1, -1).astype(F32), dtb, alog, dskip_e,
                 ssd_norm_w[l].reshape(1, -1).astype(F32), B, L)

    slopes = jnp.power(2.0, -8.0 * jnp.arange(1, DA_HEADS + 1, dtype=F32) / DA_HEADS)
    lamv = jnp.concatenate([_pad_lanes(lambda_q1[l]), _pad_lanes(lambda_k1[l]),
                            _pad_lanes(lambda_q2[l]), _pad_lanes(lambda_k2[l]),
                            jnp.zeros((4, LANES), F32)], axis=0)
    y_da = _diff_attention(q.reshape(B, L, 512), k.reshape(B, L, 512), v.reshape(B, L, 512),
                           slopes, lamv, subln_w[l].reshape(1, -1).astype(F32), B, L)

    w_r = (jnp.zeros((D, LANES), F32).at[:, :N_GROUPS].set(w_router_group[l])
           .at[:, EXP_LANE0:EXP_LANE0 + N_EXPERTS].set(w_router_exp[l])).astype(BF16)
    b_r = _pad_lanes(jnp.concatenate([b_router_group[l], b_router_exp[l]]))
    h, u, meta, cnt = _out_router(x2, y_ssd.reshape(T, 512), y_da.reshape(T, 512), w_out[l].astype(BF16),
                                  norm_ffn_w[l].reshape(1, D), w_r, b_r)

    n_blocks = (T * 2) // MOE_ROWS + N_EXPERTS
    counts = cnt[0, EXP_LANE0:EXP_LANE0 + N_EXPERTS].astype(jnp.int32)
    idx = meta[:, :4].astype(jnp.int32).T
    xs, slots, block_e, nused = _dispatch(counts, idx, u, n_blocks)
    eo = _moe(block_e, nused, xs, w_exp_gate[l].astype(BF16), w_exp_up[l].astype(BF16),
              w_exp_down[l].astype(BF16), n_blocks)
    out = _combine(slots, h, meta, norm_final_w.reshape(1, D), eo)
    return out.reshape(B, L, D)
```

```python
import functools
import math

import jax
import jax.numpy as jnp
from jax import lax
from jax.experimental import pallas as pl
from jax.experimental.pallas import tpu as pltpu

F32 = jnp.float32
BF16 = jnp.bfloat16

EPS = 1e-6
LOG2E = math.log2(math.e)
LANES = 128
SUBLANES = 8
HALO = 16
VMEM_LIMIT = 48 * 1024 * 1024

SSD_WIDTH = 512
SSD_HEADDIM = 64
SSD_HEADS = 8
SSD_NGROUPS = 2
SSD_HPG = 4
SSD_DSTATE = 128
SSD_CONV = 5
SSD_CHUNK = 128
SSD_CONV_CH = 1024
DA_HEADDIM = 64
DA_VDIM = 128
DA_HEADS = 4
N_GROUPS = 4
EPG = 8
N_EXPERTS = 32
D_EXPERT = 512
LAMBDA_INIT = 0.8 - 0.6 * math.exp(-0.3 * 0)

TM_PROJ = 512
SSD_SEQS = 2
TQ = 512
TK = 512
TM_ROUTE = 512
TM_DISP = 1024
MOE_ROWS = 256
TM_COMB = 256
EXP_LANE0 = 4


def _cparams(*sem):
    return pltpu.CompilerParams(dimension_semantics=sem, vmem_limit_bytes=VMEM_LIMIT)


def _dot(a, b):
    return jnp.dot(a, b, preferred_element_type=F32)


def _dot_nt(a, b):
    return lax.dot_general(a, b, (((1,), (1,)), ((), ())), preferred_element_type=F32)


def _split3(v):
    hi = v.astype(BF16)
    r1 = v - hi.astype(F32)
    mid = r1.astype(BF16)
    lo = (r1 - mid.astype(F32)).astype(BF16)
    return hi, mid, lo


def _dot_exact(a01, v):
    hi, mid, lo = _split3(v)
    return _dot(a01, hi) + _dot(a01, mid) + _dot(a01, lo)


def _dot_exact_r(v, b01):
    hi, mid, lo = _split3(v)
    return _dot(hi, b01) + _dot(mid, b01) + _dot(lo, b01)


def _silu(x):
    return x * (1.0 / (1.0 + jnp.exp(-x)))


def _inproj_kernel(x_ref, nw_ref, w_ref, wdt_ref, z_ref, xbc_ref, dt_ref, q_ref, k_ref, v_ref):
    x = x_ref[...]
    ms = jnp.mean(x * x, axis=-1, keepdims=True)
    u = (x * lax.rsqrt(ms + EPS) * nw_ref[...]).astype(BF16)
    z_ref[...] = _dot(u, w_ref[:, 0:512]).astype(BF16)
    xbc_ref[...] = _dot(u, w_ref[:, 512:1536]).astype(BF16)
    q_ref[...] = _dot(u, w_ref[:, 1536:2048]).astype(BF16)
    k_ref[...] = _dot(u, w_ref[:, 2048:2560]).astype(BF16)
    v_ref[...] = _dot(u, w_ref[:, 2560:3072]).astype(BF16)
    dt_ref[...] = _dot(u, wdt_ref[...])


def _in_proj(x2, norm_w, w_main, w_dt):
    T, D = x2.shape
    tm = TM_PROJ
    row = lambda n: pl.BlockSpec((tm, n), lambda i: (i, 0))
    full = lambda a: pl.BlockSpec(a.shape, lambda i: (0,) * a.ndim)
    return pl.pallas_call(
        _inproj_kernel,
        grid=(T // tm,),
        in_specs=[row(D), full(norm_w), full(w_main), full(w_dt)],
        out_specs=[row(512), row(1024), row(LANES), row(512), row(512), row(512)],
        out_shape=[
            jax.ShapeDtypeStruct((T, 512), BF16),
            jax.ShapeDtypeStruct((T, 1024), BF16),
            jax.ShapeDtypeStruct((T, LANES), F32),
            jax.ShapeDtypeStruct((T, 512), BF16),
            jax.ShapeDtypeStruct((T, 512), BF16),
            jax.ShapeDtypeStruct((T, 512), BF16),
        ],
        compiler_params=_cparams("arbitrary"),
        name="in_proj",
    )(x2, norm_w, w_main, w_dt)


def _ssd_kernel(*refs, reverse):
    c = pl.program_id(1)
    nc = pl.num_programs(1)
    pos = (nc - 1 - c) if reverse else c
    ht_ref = refs[-1]

    @pl.when(c == 0)
    def _():
        ht_ref[...] = jnp.zeros_like(ht_ref)

    for s in range(ht_ref.shape[0]):
        _ssd_sequence(s, pos, nc, refs, reverse)


def _ssd_sequence(s, pos, nc, refs, reverse):
    if reverse:
        (xp_ref, xc_ref, xn_ref, dt_ref, cw_ref, cb_ref, dtb_ref, alog_ref, tri_ref, e_ref, sh_ref,
         z_ref, yf_ref, dsk_ref, nw_ref, out_ref, ht_ref) = refs
    else:
        (xp_ref, xc_ref, xn_ref, dt_ref, cw_ref, cb_ref, dtb_ref, alog_ref, tri_ref, e_ref, sh_ref,
         out_ref, ht_ref) = refs
    Q = SSD_CHUNK

    halo_zero = jnp.zeros((HALO, SSD_CONV_CH), BF16)
    ext = jnp.concatenate([jnp.where(pos > 0, xp_ref[s], halo_zero), xc_ref[s],
                           jnp.where(pos < nc - 1, xn_ref[s], halo_zero)], axis=0)
    shifted = _dot(sh_ref[...], ext)
    conv = cb_ref[...]
    for k in range(SSD_CONV):
        conv = conv + cw_ref[k:k + 1, :] * shifted[k * Q:(k + 1) * Q, :]
    act = _silu(conv)
    xs = act[:, :SSD_WIDTH]

    dtraw = dt_ref[s] + dtb_ref[...]
    dt = jnp.maximum(dtraw, 0.0) + jnp.log(1.0 + jnp.exp(-jnp.abs(dtraw)))
    a = -jnp.exp(alog_ref[...])
    adt = dt * a
    tri = tri_ref[...]
    cvec = _dot_exact(tri, adt)
    cvt = cvec.T
    e01 = e_ref[...]
    dt_e = _dot_exact_r(dt, e01)
    cv_e = _dot_exact_r(cvec, e01)
    edge = 0 if reverse else Q - 1
    tot_e = cv_e[edge:edge + 1, :]

    li = lax.broadcasted_iota(jnp.int32, (Q, Q), 0)
    si = lax.broadcasted_iota(jnp.int32, (Q, Q), 1)
    mask = (li <= si) if reverse else (li >= si)
    lane = lax.broadcasted_iota(jnp.int32, (Q, LANES), 1)
    head0 = SSD_HEADS if reverse else 0

    xdt = xs * dt_e
    ys = []
    for g in range(SSD_NGROUPS):
        bg = act[:, SSD_WIDTH + g * SSD_DSTATE:SSD_WIDTH + (g + 1) * SSD_DSTATE]
        cg = act[:, SSD_WIDTH + (SSD_NGROUPS + g) * SSD_DSTATE:SSD_WIDTH + (SSD_NGROUPS + g + 1) * SSD_DSTATE]
        bgb = bg.astype(BF16)
        cgb = cg.astype(BF16)
        gm = _dot_nt(cgb, bgb)
        gs = slice(g * 256, (g + 1) * 256)
        xdt_g = xdt[:, gs]
        xdtb = xdt_g.astype(BF16)
        ydiag = []
        for pr in range(2):
            xpair = xdtb[:, pr * LANES:(pr + 1) * LANES]
            yh = []
            for j in range(2):
                hl = head0 + g * SSD_HPG + pr * 2 + j
                seg = cvec[:, hl:hl + 1] - cvt[hl:hl + 1, :]
                decay = jnp.exp(jnp.where(mask, seg, -jnp.inf))
                yh.append(_dot((gm * decay).astype(BF16), xpair))
            ydiag.append(jnp.where(lane < SSD_HEADDIM, yh[0], yh[1]))
        ydiag = jnp.concatenate(ydiag, axis=1)
        ht = ht_ref[s, g]
        yoff = _dot(cgb, ht.astype(BF16)) * jnp.exp(cv_e[:, gs])
        ys.append(ydiag + yoff)
        wdec = jnp.exp(tot_e[:, gs] - cv_e[:, gs])
        xw = (xdt_g * wdec).astype(BF16)
        ht_ref[s, g] = jnp.exp(tot_e[:, gs]) * ht + _dot(bg.T.astype(BF16), xw)
    y = jnp.concatenate(ys, axis=1)

    if not reverse:
        out_ref[s] = y
    else:
        y = y + yf_ref[s] + dsk_ref[...] * xs
        gated = y * _silu(z_ref[s].astype(F32))
        ms = jnp.mean(gated * gated, axis=-1, keepdims=True)
        out_ref[s] = (gated * lax.rsqrt(ms + EPS) * nw_ref[...]).astype(out_ref.dtype)


def _ssd(xbc, dt, z, conv_w, conv_b, dtb, alog, dskip_e, norm_w, B, L):
    Q = SSD_CHUNK
    nc = L // Q
    ns = SSD_SEQS if B % SSD_SEQS == 0 else 1
    hb = Q // HALO
    nhb = L // HALO
    r = jnp.arange(Q)
    pad = (SSD_CONV - 1) // 2
    col = jnp.arange(Q + 2 * HALO)[None, :]
    shift = jnp.concatenate([(col == HALO + r[:, None] + k - pad) for k in range(SSD_CONV)], axis=0).astype(BF16)
    tril = (r[:, None] >= r[None, :]).astype(BF16)
    triu = (r[:, None] <= r[None, :]).astype(BF16)
    lane = jnp.arange(LANES)[:, None]
    ch = jnp.arange(SSD_WIDTH)[None, :] // SSD_HEADDIM
    e_f = (lane == ch).astype(BF16)
    e_b = (lane == ch + SSD_HEADS).astype(BF16)

    def call(reverse, extra_in, extra_specs, out_dtype):
        pos = (lambda c: nc - 1 - c) if reverse else (lambda c: c)
        full = lambda a: pl.BlockSpec(a.shape, lambda b, c: (0,) * a.ndim)
        chunk = lambda n: pl.BlockSpec((ns, Q, n), lambda b, c: (b, pos(c), 0))
        in_specs = [
            pl.BlockSpec((ns, HALO, 1024), lambda b, c: (b, jnp.maximum(pos(c) * hb - 1, 0), 0)),
            chunk(1024),
            pl.BlockSpec((ns, HALO, 1024), lambda b, c: (b, jnp.minimum((pos(c) + 1) * hb, nhb - 1), 0)),
            chunk(LANES),
        ]
        consts = [conv_w, conv_b, dtb, alog, triu if reverse else tril, e_b if reverse else e_f, shift]
        in_specs += [full(a) for a in consts]
        in_specs += [chunk(512) if s == "chunk" else full(a) for a, s in zip(extra_in, extra_specs)]
        return pl.pallas_call(
            functools.partial(_ssd_kernel, reverse=reverse),
            grid=(B // ns, nc),
            in_specs=in_specs,
            out_specs=chunk(512),
            out_shape=jax.ShapeDtypeStruct((B, L, 512), out_dtype),
            scratch_shapes=[pltpu.VMEM((ns, SSD_NGROUPS, SSD_DSTATE, 256), F32)],
            compiler_params=_cparams("arbitrary", "arbitrary"),
            name="ssd_bwd" if reverse else "ssd_fwd",
        )(xbc, xbc, xbc, dt, *consts, *extra_in)

    yf = call(False, [], [], F32)
    return call(True, [z, yf, dskip_e, norm_w], ["chunk", "chunk", "full", "full"], BF16)


N_POS = 3
N_FEAT = 2 * N_POS + 1
MAX_SHIFT_LOG2 = 50.0
VT_ROWS = DA_VDIM + 16
ZERO_PROB_LOG2 = 160.0
DEAD_SHIFT = 1.0e4


def _max_block_dist(head, tq, nb):
    slope_log2 = 2.0 ** (-8.0 * (head + 1) / DA_HEADS) * LOG2E
    d = 0
    while d < nb - 1 and slope_log2 * (d * tq + 1) < ZERO_PROB_LOG2:
        d += 1
    return d


def _attn_kernel(slope_ref, lamv_ref, q_ref, qall_ref, k_ref, v_ref, sw_ref, o_ref,
                 kaug_ref, vt_ref, vtb_ref, pt_ref, shift_ref, worst_ref, *, tq, tk, L):
    h = pl.program_id(1)
    qi = pl.program_id(2)
    nb = L // tq
    sl2 = slope_ref[h] * LOG2E
    lv = lamv_ref[...]
    lam = (jnp.exp(jnp.sum(lv[0:1] * lv[1:2], axis=-1, keepdims=True))
           - jnp.exp(jnp.sum(lv[2:3] * lv[3:4], axis=-1, keepdims=True)) + LAMBDA_INIT)

    def feat0(c):
        return DA_HEADDIM if c == 0 else 0

    def own_lanes(lane, c):
        return (lane < DA_HEADDIM) if c == 0 else (lane >= DA_HEADDIM)

    ha = lax.broadcasted_iota(jnp.int32, (LANES, LANES), 0) < DA_HEADDIM
    hb = lax.broadcasted_iota(jnp.int32, (LANES, LANES), 1) < DA_HEADDIM
    other_half = jnp.where(ha != hb, 1.0, 0.0).astype(BF16)

    @pl.when(qi == 0)
    def _prep():
        kf = k_ref[0].astype(F32)
        lane = lax.broadcasted_iota(jnp.int32, kf.shape, 1)
        kpos = lax.broadcasted_iota(jnp.int32, kf.shape, 0).astype(F32) * sl2
        pieces = [t.astype(F32) for t in _split3(kpos)]
        for c in range(2):
            rel = lane - feat0(c)
            feat = jnp.where((rel >= 0) & (rel <= N_POS), 1.0, 0.0)
            for n in range(N_POS):
                feat = jnp.where(rel == N_POS + 1 + n, pieces[n], feat)
            kaug_ref[c] = jnp.where(own_lanes(lane, c), kf, feat).astype(BF16)
        kn2 = _dot((kf * kf).astype(BF16), other_half)
        kmax2 = jnp.max(kn2, axis=0, keepdims=True)
        qa = qall_ref[0].astype(F32)
        prod2 = _dot((qa * qa).astype(BF16), other_half) * kmax2
        bound = prod2 * lax.rsqrt(jnp.maximum(prod2, 1e-30))
        shift_all = (bound * 1.02).astype(BF16).astype(F32)
        shift_ref[...] = shift_all
        worst_ref[0] = jnp.max(shift_all)
        ones_row = lax.broadcasted_iota(jnp.int32, (VT_ROWS - DA_VDIM, tq), 0) == 0
        ones_rows = jnp.where(ones_row, 1.0, 0.0).astype(BF16)
        for jb in range(nb):
            vt = v_ref[0, jb * tq:(jb + 1) * tq, :].astype(F32).T.astype(BF16)
            vt_ref[0:DA_VDIM, jb * tq:(jb + 1) * tq] = vt
            vt_ref[DA_VDIM:, jb * tq:(jb + 1) * tq] = ones_rows
            vtb_ref[jb, 0:DA_VDIM, :] = vt
            vtb_ref[jb, DA_VDIM:, :] = ones_rows

    worst = worst_ref[0]
    lane = lax.broadcasted_iota(jnp.int32, (tq, LANES), 1)

    def load_q():
        return q_ref[0].astype(F32)

    def finish(o1, o2):
        o = o1 - lam * o2
        ms = jnp.mean(o * o, axis=-1, keepdims=True)
        o = o * lax.rsqrt(ms + EPS) * sw_ref[...] * (1.0 - LAMBDA_INIT)
        o_ref[0] = o.astype(o_ref.dtype)

    def fast_path(max_dist):
        qf = load_q()
        qpos = (qi * tq + lax.broadcasted_iota(jnp.int32, qf.shape, 0)).astype(F32) * sl2
        qpieces = [t.astype(F32) for t in _split3(qpos)]
        shift = shift_ref[pl.ds(pl.multiple_of(qi * tq, tq), tq), :]
        kr = lax.broadcasted_iota(jnp.int32, (tq, tq), 0)
        qc = lax.broadcasted_iota(jnp.int32, (tq, tq), 1)
        corr = jnp.maximum(qc - kr, 0).astype(F32) * (2.0 * sl2)
        outs = []
        for c in range(2):
            rel = lane - feat0(c)
            own = own_lanes(lane, c)
            base = jnp.where(own, qf, jnp.where(rel == N_POS, -shift, 0.0))
            posf = jnp.where((rel > N_POS) & (rel < N_FEAT), -1.0, 0.0)
            for n in range(N_POS):
                posf = jnp.where(rel == n, qpieces[n], posf)
            q_after = (base + posf).astype(BF16)
            q_before = (base - posf).astype(BF16)
            if max_dist is None:
                blocks = []
                for d in range(nb):
                    wrapped = qi + d >= nb
                    jb = jnp.where(wrapped, qi + d - nb, qi + d)
                    blocks.append((jb, q_after if d == 0 else jnp.where(wrapped, q_before, q_after), d == 0))
            else:
                q_dead = jnp.where(own, qf, jnp.where(rel == N_POS, -DEAD_SHIFT, 0.0)).astype(BF16)
                blocks = []
                for r in range(-max_dist, max_dist + 1):
                    jb = qi + r
                    side = q_before if r < 0 else q_after
                    inside = (jb >= 0) & (jb < nb)
                    blocks.append((jnp.clip(jb, 0, nb - 1), side if r == 0 else jnp.where(inside, side, q_dead),
                                   r == 0))
            acc = None
            for jb, qsel, diagonal in blocks:
                dk = pl.ds(pl.multiple_of(jb * tq, tq), tq)
                st = _dot_nt(kaug_ref[c, dk, :], qsel)
                if diagonal:
                    st = st - corr
                p = jnp.exp2(st).astype(BF16)
                if max_dist is None:
                    pt_ref[c, dk, :] = p
                else:
                    part = _dot(vtb_ref[jb], p)
                    acc = part if acc is None else acc + part
            if max_dist is None:
                acc = _dot(vt_ref[...], pt_ref[c])
            outs.append(acc[:DA_VDIM] * (1.0 / acc[DA_VDIM:DA_VDIM + 1]))
        finish(outs[0].T, outs[1].T)

    fast = worst <= MAX_SHIFT_LOG2
    dists = [_max_block_dist(hh, tq, nb) for hh in range(DA_HEADS)]
    dists = [d if 2 * d + 1 < nb else None for d in dists]
    for dist in set(dists):
        heads = [hh for hh in range(DA_HEADS) if dists[hh] == dist]
        in_heads = functools.reduce(lambda a, b: a | b, [h == hh for hh in heads])
        pl.when(fast & in_heads)(functools.partial(fast_path, dist))

    @pl.when(jnp.logical_not(fast))
    def _online():
        qf = load_q()
        zero = jnp.zeros_like(qf)
        qm = [jnp.where(own_lanes(lane, c), qf, zero).astype(BF16) for c in range(2)]
        qk0 =(qi * tq + lax.broadcasted_iota(jnp.int32, (tq, tk), 0)
               - lax.broadcasted_iota(jnp.int32, (tq, tk), 1)).astype(F32)

        def body(j, carry):
            kb = k_ref[0, pl.ds(pl.multiple_of(j * tk, tk), tk), :]
            vb = v_ref[0, pl.ds(pl.multiple_of(j * tk, tk), tk), :]
            bias = sl2 * jnp.abs(qk0 - lax.convert_element_type(j * tk, F32))
            out = []
            for c in range(2):
                m, l, acc = carry[c]
                s = _dot_nt(qm[c], kb) - bias
                m_new = jnp.maximum(m, jnp.max(s, axis=-1, keepdims=True))
                p = jnp.exp2(s - m_new)
                alpha = jnp.exp2(m - m_new)
                l_new = alpha * l + jnp.sum(p, axis=-1, keepdims=True)
                acc_new = alpha * acc + _dot(p.astype(BF16), vb)
                out.append((m_new, l_new, acc_new))
            return tuple(out)

        init = tuple((jnp.full((tq, 1), -jnp.inf, F32), jnp.zeros((tq, 1), F32),
                      jnp.zeros((tq, DA_VDIM), F32)) for _ in range(2))
        (m1, l1, a1), (m2, l2, a2) = lax.fori_loop(0, L // tk, body, init)
        finish(a1 * (1.0 / l1), a2 * (1.0 / l2))


def _diff_attention(q, k, v, slopes, lamv, subln_w, B, L):
    tq, tk = min(TQ, L), min(TK, L)
    return pl.pallas_call(
        functools.partial(_attn_kernel, tq=tq, tk=tk, L=L),
        grid=(B, DA_HEADS, L // tq),
        in_specs=[
            pl.BlockSpec(memory_space=pltpu.SMEM),
            pl.BlockSpec(lamv.shape, lambda b, h, i: (0, 0)),
            pl.BlockSpec((1, tq, LANES), lambda b, h, i: (b, i, h)),
            pl.BlockSpec((1, L, LANES), lambda b, h, i: (b, 0, h)),
            pl.BlockSpec((1, L, LANES), lambda b, h, i: (b, 0, h)),
            pl.BlockSpec((1, L, LANES), lambda b, h, i: (b, 0, h)),
            pl.BlockSpec(subln_w.shape, lambda b, h, i: (0, 0)),
        ],
        out_specs=pl.BlockSpec((1, tq, LANES), lambda b, h, i: (b, i, h)),
        out_shape=jax.ShapeDtypeStruct((B, L, 512), BF16),
        scratch_shapes=[pltpu.VMEM((2, L, LANES), BF16),
                        pltpu.VMEM((VT_ROWS, L), BF16),
                        pltpu.VMEM((L // tq, VT_ROWS, tq), BF16),
                        pltpu.VMEM((2, L, tq), BF16),
                        pltpu.VMEM((L, LANES), F32),
                        pltpu.SMEM((1,), F32)],
        compiler_params=_cparams("arbitrary", "arbitrary", "arbitrary"),
        name="diff_attn",
    )(slopes, lamv, q, q, k, v, subln_w)


def _router_kernel(x_ref, ys_ref, ya_ref, wo_ref, nw_ref, wr_ref, br_ref, tri_ref,
                   h_ref, u_ref, meta_ref, cnt_ref, carry_ref):
    i = pl.program_id(0)

    @pl.when(i == 0)
    def _():
        carry_ref[...] = jnp.zeros_like(carry_ref)

    h = x_ref[...] + _dot(ys_ref[...], wo_ref[0:512, :]) + _dot(ya_ref[...], wo_ref[512:1024, :])
    h_ref[...] = h
    ms = jnp.mean(h * h, axis=-1, keepdims=True)
    u = h * lax.rsqrt(ms + EPS) * nw_ref[...]
    u_ref[...] = u
    logits = _dot(u.astype(BF16), wr_ref[...]) + br_ref[...]

    tm = logits.shape[0]
    li = lax.broadcasted_iota(jnp.int32, (tm, LANES), 1)
    lf = li.astype(F32)
    ninf = jnp.float32(-jnp.inf)
    big = jnp.float32(1e9)
    gl = jnp.where(li < N_GROUPS, logits, ninf)
    gmax = jnp.max(gl, axis=-1, keepdims=True)
    gidx = jnp.min(jnp.where(gl == gmax, lf, big), axis=-1, keepdims=True)
    pg = 1.0 / jnp.sum(jnp.exp(gl - gmax), axis=-1, keepdims=True)
    lane_grp = ((li - EXP_LANE0) >> 3).astype(F32)
    in_grp = (li >= EXP_LANE0) & (li < EXP_LANE0 + N_EXPERTS) & (lane_grp == gidx)
    sl = jnp.where(in_grp, logits, ninf)
    m1 = jnp.max(sl, axis=-1, keepdims=True)
    i1 = jnp.min(jnp.where(sl == m1, lf, big), axis=-1, keepdims=True)
    sl2 = jnp.where(lf == i1, ninf, sl)
    m2 = jnp.max(sl2, axis=-1, keepdims=True)
    i2 = jnp.min(jnp.where(sl2 == m2, lf, big), axis=-1, keepdims=True)
    t = jnp.exp(m2 - m1)
    w1 = 1.0 / (1.0 + t)
    g1 = pg * w1
    g2 = pg * (t * w1)
    oh1 = lf == i1
    oh2 = lf == i2
    oh = jnp.where(oh1 | oh2, 1.0, 0.0)
    prefix = _dot(tri_ref[...], oh.astype(BF16)) + carry_ref[0:1, :]
    r1 = jnp.sum(jnp.where(oh1, prefix, 0.0), axis=-1, keepdims=True)
    r2 = jnp.sum(jnp.where(oh2, prefix, 0.0), axis=-1, keepdims=True)
    new_carry = carry_ref[0:1, :] + jnp.sum(oh, axis=0, keepdims=True)
    carry_ref[...] = jnp.broadcast_to(new_carry, carry_ref.shape)
    cnt_ref[...] = jnp.broadcast_to(new_carry, cnt_ref.shape)

    meta = jnp.where(li == 0, i1 - EXP_LANE0,
           jnp.where(li == 1, i2 - EXP_LANE0,
           jnp.where(li == 2, r1,
           jnp.where(li == 3, r2,
           jnp.where(li == 4, g1,
           jnp.where(li == 5, g2, 0.0))))))
    meta_ref[...] = meta


def _out_router(x2, y_ssd, y_da, w_out, norm_w, w_r, b_r):
    T, D = x2.shape
    tm = min(TM_ROUTE, T)
    r = jnp.arange(tm)
    tri = (r[:, None] > r[None, :]).astype(BF16)
    row = lambda n: pl.BlockSpec((tm, n), lambda i: (i, 0))
    full = lambda a: pl.BlockSpec(a.shape, lambda i: (0,) * a.ndim)
    return pl.pallas_call(
        _router_kernel,
        grid=(T // tm,),
        in_specs=[row(D), row(512), row(512), full(w_out), full(norm_w), full(w_r), full(b_r), full(tri)],
        out_specs=[row(D), row(D), row(LANES), pl.BlockSpec((SUBLANES, LANES), lambda i: (0, 0))],
        out_shape=[
            jax.ShapeDtypeStruct((T, D), F32),
            jax.ShapeDtypeStruct((T, D), F32),
            jax.ShapeDtypeStruct((T, LANES), F32),
            jax.ShapeDtypeStruct((SUBLANES, LANES), F32),
        ],
        scratch_shapes=[pltpu.VMEM((SUBLANES, LANES), F32)],
        compiler_params=_cparams("arbitrary"),
        name="out_router",
    )(x2, y_ssd, y_da, w_out, norm_w, w_r, b_r, tri)


def _dispatch_kernel(cnt_ref, idx_ref, u_ref, xs_hbm, slots_ref, be_ref, nused_ref,
                     pstart_ref, zero_ref, sem, *, tm, n_blocks):
    i = pl.program_id(0)

    @pl.when(i == 0)
    def _():
        zero_ref[...] = jnp.zeros_like(zero_ref)
        acc = jnp.int32(0)
        for e in range(N_EXPERTS):
            padded = ((cnt_ref[e] + (MOE_ROWS - 1)) // MOE_ROWS) * MOE_ROWS
            pstart_ref[e] = acc
            end = acc + padded

            def fill(j, _, e=e):
                be_ref[j] = jnp.int32(e)
                return 0

            lax.fori_loop(acc // MOE_ROWS, end // MOE_ROWS, fill, 0)

            @pl.when(padded > 0)
            def _(end=end):
                dst = xs_hbm.at[pl.ds(pl.multiple_of(end - MOE_ROWS, MOE_ROWS), MOE_ROWS)]
                cp = pltpu.make_async_copy(zero_ref, dst, sem)
                cp.start()
                cp.wait()

            acc = end
        nused = acc // MOE_ROWS
        nused_ref[0] = nused

        def fill_tail(j, _):
            be_ref[j] = jnp.int32(0)
            dst = xs_hbm.at[pl.ds(pl.multiple_of(j * MOE_ROWS, MOE_ROWS), MOE_ROWS)]
            cp = pltpu.make_async_copy(zero_ref, dst, sem)
            cp.start()
            cp.wait()
            return 0

        lax.fori_loop(nused, n_blocks, fill_tail, 0)

    def row_copy(t, slot):
        return pltpu.make_async_copy(u_ref.at[pl.ds(t, 1)], xs_hbm.at[pl.ds(slot, 1)], sem)

    def start(t, _):
        for kk in range(2):
            slot = pstart_ref[idx_ref[kk, t]] + idx_ref[2 + kk, t]
            slots_ref[kk, t] = slot
            row_copy(t, slot).start()
        return 0

    lax.fori_loop(0, tm, start, 0)
    for _ in range(2):
        pltpu.make_async_copy(u_ref, u_ref, sem).wait()


def _dispatch(counts, idx, u, n_blocks):
    T, D = u.shape
    tm = min(TM_DISP, T)
    n_slots = n_blocks * MOE_ROWS
    grid_spec = pltpu.PrefetchScalarGridSpec(
        num_scalar_prefetch=1,
        grid=(T // tm,),
        in_specs=[
            pl.BlockSpec((4, tm), lambda i, cnt: (0, i), memory_space=pltpu.SMEM),
            pl.BlockSpec((tm, D), lambda i, cnt: (i, 0)),
        ],
        out_specs=[
            pl.BlockSpec(memory_space=pl.ANY),
            pl.BlockSpec((2, tm), lambda i, cnt: (0, i), memory_space=pltpu.SMEM),
            pl.BlockSpec(memory_space=pltpu.SMEM),
            pl.BlockSpec(memory_space=pltpu.SMEM),
        ],
        scratch_shapes=[pltpu.SMEM((N_EXPERTS,), jnp.int32),
                        pltpu.VMEM((MOE_ROWS, D), F32),
                        pltpu.SemaphoreType.DMA(())],
    )
    return pl.pallas_call(
        functools.partial(_dispatch_kernel, tm=tm, n_blocks=n_blocks),
        grid_spec=grid_spec,
        out_shape=[
            jax.ShapeDtypeStruct((n_slots, D), F32),
            jax.ShapeDtypeStruct((2, T), jnp.int32),
            jax.ShapeDtypeStruct((n_blocks,), jnp.int32),
            jax.ShapeDtypeStruct((1,), jnp.int32),
        ],
        compiler_params=_cparams("arbitrary"),
        name="dispatch",
    )(counts, idx, u)


def _moe_kernel(be_ref, nused_ref, xs_ref, wg_ref, wu_ref, wd_ref, o_ref, wgb_ref, wub_ref, wdb_ref):
    j = pl.program_id(0)
    last = nused_ref[0] - 1
    expert = be_ref[jnp.minimum(j, last)]
    prev = be_ref[jnp.minimum(jnp.maximum(j - 1, 0), last)]

    @pl.when((j == 0) | (expert != prev))
    def _():
        wgb_ref[...] = wg_ref[0].astype(BF16)
        wub_ref[...] = wu_ref[0].astype(BF16)
        wdb_ref[...] = wd_ref[0].astype(BF16)

    @pl.when(j < nused_ref[0])
    def _():
        x = xs_ref[...].astype(BF16)
        hid = _silu(_dot(x, wgb_ref[...])) * _dot(x, wub_ref[...])
        o_ref[...] = _dot(hid.astype(BF16), wdb_ref[...])

    @pl.when(j >= nused_ref[0])
    def _():
        o_ref[...] = jnp.zeros_like(o_ref)


def _moe(block_e, nused, xs, wg, wu, wd, n_blocks):
    D = xs.shape[1]
    used = lambda j, be, nu: jnp.minimum(j, nu[0] - 1)
    wspec = lambda a: pl.BlockSpec((1,) + a.shape[1:], lambda j, be, nu: (be[used(j, be, nu)], 0, 0))
    grid_spec = pltpu.PrefetchScalarGridSpec(
        num_scalar_prefetch=2,
        grid=(n_blocks,),
        in_specs=[
            pl.BlockSpec((MOE_ROWS, D), lambda j, be, nu: (used(j, be, nu), 0)),
            wspec(wg), wspec(wu), wspec(wd),
        ],
        out_specs=pl.BlockSpec((MOE_ROWS, D), lambda j, be, nu: (j, 0)),
        scratch_shapes=[pltpu.VMEM(wg.shape[1:], BF16), pltpu.VMEM(wu.shape[1:], BF16),
                        pltpu.VMEM(wd.shape[1:], BF16)],
    )
    return pl.pallas_call(
        _moe_kernel,
        grid_spec=grid_spec,
        out_shape=jax.ShapeDtypeStruct((n_blocks * MOE_ROWS, D), F32),
        compiler_params=_cparams("arbitrary"),
        name="moe_mlp",
    )(block_e, nused, xs, wg, wu, wd)


def _combine_kernel(slots_ref, next_slots_ref, h_ref, meta_ref, nw_ref, eo_hbm, o_ref, g_ref, sems, *, tm):
    i = pl.program_id(0)
    n = pl.num_programs(0)
    cur = i % 2

    def gather_tile(sref, buf):
        def start(t, _):
            for kk in range(2):
                pltpu.make_async_copy(eo_hbm.at[pl.ds(sref[kk, t], 1)],
                                      g_ref.at[buf, kk, pl.ds(t, 1)], sems.at[buf]).start()
            return 0

        lax.fori_loop(0, tm, start, 0)

    @pl.when(i == 0)
    def _():
        gather_tile(slots_ref, 0)

    @pl.when(i + 1 < n)
    def _():
        gather_tile(next_slots_ref, 1 - cur)

    pltpu.make_async_copy(g_ref.at[cur], g_ref.at[cur], sems.at[cur]).wait()

    meta = meta_ref[...]
    y = h_ref[...] + g_ref[cur, 0] * meta[:, 4:5] + g_ref[cur, 1] * meta[:, 5:6]
    ms = jnp.mean(y * y, axis=-1, keepdims=True)
    o_ref[...] = y * lax.rsqrt(ms + EPS) * nw_ref[...]


def _combine(slots, h, meta, norm_w, eo):
    T, D = h.shape
    tm = min(TM_COMB, T)
    last = T // tm - 1
    return pl.pallas_call(
        functools.partial(_combine_kernel, tm=tm),
        grid=(T // tm,),
        in_specs=[
            pl.BlockSpec((2, tm), lambda i: (0, i), memory_space=pltpu.SMEM),
            pl.BlockSpec((2, tm), lambda i: (0, jnp.minimum(i + 1, last)), memory_space=pltpu.SMEM),
            pl.BlockSpec((tm, D), lambda i: (i, 0)),
            pl.BlockSpec((tm, LANES), lambda i: (i, 0)),
            pl.BlockSpec(norm_w.shape, lambda i: (0, 0)),
            pl.BlockSpec(memory_space=pl.ANY),
        ],
        out_specs=pl.BlockSpec((tm, D), lambda i: (i, 0)),
        out_shape=jax.ShapeDtypeStruct((T, D), F32),
        scratch_shapes=[pltpu.VMEM((2, 2, tm, D), F32),
                        pltpu.SemaphoreType.DMA((2,))],
        compiler_params=_cparams("arbitrary"),
        name="combine",
    )(slots, slots, h, meta, norm_w, eo)


def _pad_lanes(v, offset=0):
    v = v.astype(F32).reshape(-1)
    return jnp.zeros((1, LANES), F32).at[0, offset:offset + v.shape[0]].set(v)


def kernel(x, norm_mix_w, w_in, conv_w, conv_b, dt_bias_fwd, dt_bias_bwd, a_log_fwd, a_log_bwd, ssd_d,
           ssd_norm_w, lambda_q1, lambda_k1, lambda_q2, lambda_k2, subln_w, w_out, norm_ffn_w,
           w_router_group, b_router_group, w_router_exp, b_router_exp, w_exp_gate, w_exp_up, w_exp_down,
           norm_final_w):
    B, L, D = x.shape
    T = B * L
    x2 = x.reshape(T, D)
    l = 0

    w = w_in[l]
    o_z, o_xbc, o_dt = 512, 512 + 1024, 512 + 1024 + 16
    o_q, o_k = o_dt + 512, o_dt + 1024
    w_main = jnp.concatenate(
        [w[:, :o_xbc], w[:, o_dt:o_q] * (DA_HEADDIM ** -0.5 * LOG2E), w[:, o_q:]], axis=1).astype(BF16)
    w_dt = jnp.zeros((D, LANES), F32).at[:, :16].set(w[:, o_xbc:o_dt]).astype(BF16)

    z, xbc, dt, q, k, v = _in_proj(x2, norm_mix_w[l].reshape(1, D), w_main, w_dt)

    dtb = _pad_lanes(jnp.concatenate([dt_bias_fwd[l], dt_bias_bwd[l]]))
    alog = _pad_lanes(jnp.concatenate([a_log_fwd[l], a_log_bwd[l]]))
    dskip_e = jnp.repeat(ssd_d[l].astype(F32), SSD_HEADDIM).reshape(1, SSD_WIDTH)
    y_ssd = _ssd(xbc.reshape(B, L, 1024), dt.reshape(B, L, LANES), z.reshape(B, L, 512),
                 conv_w[l].astype(F32), conv_b[l].reshape(1, -1).astype(F32), dtb, alog, dskip_e,
                 ssd_norm_w[l].reshape(1, -1).astype(F32), B, L)

    slopes = jnp.power(2.0, -8.0 * jnp.arange(1, DA_HEADS + 1, dtype=F32) / DA_HEADS)
    lamv = jnp.concatenate([_pad_lanes(lambda_q1[l]), _pad_lanes(lambda_k1[l]),
                            _pad_lanes(lambda_q2[l]), _pad_lanes(lambda_k2[l]),
                            jnp.zeros((4, LANES), F32)], axis=0)
    y_da = _diff_attention(q.reshape(B, L, 512), k.reshape(B, L, 512), v.reshape(B, L, 512),
                           slopes, lamv, subln_w[l].reshape(1, -1).astype(F32), B, L)

    w_r = (jnp.zeros((D, LANES), F32).at[:, :N_GROUPS].set(w_router_group[l])
           .at[:, EXP_LANE0:EXP_LANE0 + N_EXPERTS].set(w_router_exp[l])).astype(BF16)
    b_r = _pad_lanes(jnp.concatenate([b_router_group[l], b_router_exp[l]]))
    h, u, meta, cnt = _out_router(x2, y_ssd.reshape(T, 512), y_da.reshape(T, 512), w_out[l].astype(BF16),
                                  norm_ffn_w[l].reshape(1, D), w_r, b_r)

    n_blocks = (T * 2) // MOE_ROWS + N_EXPERTS
    counts = cnt[0, EXP_LANE0:EXP_LANE0 + N_EXPERTS].astype(jnp.int32)
    idx = meta[:, :4].astype(jnp.int32).T
    xs, slots, block_e, nused = _dispatch(counts, idx, u, n_blocks)
    eo = _moe(block_e, nused, xs, w_exp_gate[l], w_exp_up[l], w_exp_down[l], n_blocks)
    out = _combine(slots, h, meta, norm_final_w.reshape(1, D), eo)
    return out.reshape(B, L, D)
```

```python
import functools
import math

import jax
import jax.numpy as jnp
from jax import lax
from jax.experimental import pallas as pl
from jax.experimental.pallas import tpu as pltpu

F32 = jnp.float32
BF16 = jnp.bfloat16

EPS = 1e-6
LOG2E = math.log2(math.e)
LANES = 128
SUBLANES = 8
HALO = 16
VMEM_LIMIT = 48 * 1024 * 1024
MOE_VMEM_LIMIT = 56 * 1024 * 1024

SSD_WIDTH = 512
SSD_HEADDIM = 64
SSD_HEADS = 8
SSD_NGROUPS = 2
SSD_HPG = 4
SSD_DSTATE = 128
SSD_CONV = 5
SSD_CHUNK = 128
SSD_CONV_CH = 1024
DA_HEADDIM = 64
DA_VDIM = 128
DA_HEADS = 4
N_GROUPS = 4
EPG = 8
N_EXPERTS = 32
D_EXPERT = 512
LAMBDA_INIT = 0.8 - 0.6 * math.exp(-0.3 * 0)

TM_PROJ = 512
SSD_SEQS = 2
TQ = 512
TK = 512
TM_ROUTE = 512
TM_DISP = 1024
MOE_ROWS = 256
TM_COMB = 256
EXP_LANE0 = 4


def _cparams(*sem):
    return pltpu.CompilerParams(dimension_semantics=sem, vmem_limit_bytes=VMEM_LIMIT)


def _dot(a, b):
    return jnp.dot(a, b, preferred_element_type=F32)


def _dot_nt(a, b):
    return lax.dot_general(a, b, (((1,), (1,)), ((), ())), preferred_element_type=F32)


def _split3(v):
    hi = v.astype(BF16)
    r1 = v - hi.astype(F32)
    mid = r1.astype(BF16)
    lo = (r1 - mid.astype(F32)).astype(BF16)
    return hi, mid, lo


def _dot_exact(a01, v):
    hi, mid, lo = _split3(v)
    return _dot(a01, hi) + _dot(a01, mid) + _dot(a01, lo)


def _dot_exact_r(v, b01):
    hi, mid, lo = _split3(v)
    return _dot(hi, b01) + _dot(mid, b01) + _dot(lo, b01)


def _silu(x):
    return x * (1.0 / (1.0 + jnp.exp(-x)))


def _inproj_kernel(x_ref, nw_ref, w_ref, wdt_ref, z_ref, xbc_ref, dt_ref, q_ref, k_ref, v_ref):
    x = x_ref[...]
    ms = jnp.mean(x * x, axis=-1, keepdims=True)
    u = (x * lax.rsqrt(ms + EPS) * nw_ref[...]).astype(BF16)
    z_ref[...] = _dot(u, w_ref[:, 0:512]).astype(BF16)
    xbc_ref[...] = _dot(u, w_ref[:, 512:1536]).astype(BF16)
    q_ref[...] = _dot(u, w_ref[:, 1536:2048]).astype(BF16)
    k_ref[...] = _dot(u, w_ref[:, 2048:2560]).astype(BF16)
    v_ref[...] = _dot(u, w_ref[:, 2560:3072]).astype(BF16)
    dt_ref[...] = _dot(u, wdt_ref[...])


def _in_proj(x2, norm_w, w_main, w_dt):
    T, D = x2.shape
    tm = TM_PROJ
    row = lambda n: pl.BlockSpec((tm, n), lambda i: (i, 0))
    full = lambda a: pl.BlockSpec(a.shape, lambda i: (0,) * a.ndim)
    return pl.pallas_call(
        _inproj_kernel,
        grid=(T // tm,),
        in_specs=[row(D), full(norm_w), full(w_main), full(w_dt)],
        out_specs=[row(512), row(1024), row(LANES), row(512), row(512), row(512)],
        out_shape=[
            jax.ShapeDtypeStruct((T, 512), BF16),
            jax.ShapeDtypeStruct((T, 1024), BF16),
            jax.ShapeDtypeStruct((T, LANES), F32),
            jax.ShapeDtypeStruct((T, 512), BF16),
            jax.ShapeDtypeStruct((T, 512), BF16),
            jax.ShapeDtypeStruct((T, 512), BF16),
        ],
        compiler_params=_cparams("arbitrary"),
        name="in_proj",
    )(x2, norm_w, w_main, w_dt)


def _ssd_kernel(*refs, reverse):
    c = pl.program_id(1)
    nc = pl.num_programs(1)
    pos = (nc - 1 - c) if reverse else c
    ht_ref = refs[-1]

    @pl.when(c == 0)
    def _():
        ht_ref[...] = jnp.zeros_like(ht_ref)

    for s in range(ht_ref.shape[0]):
        _ssd_sequence(s, pos, nc, refs, reverse)


def _ssd_sequence(s, pos, nc, refs, reverse):
    if reverse:
        (xp_ref, xc_ref, xn_ref, dt_ref, cw_ref, cb_ref, dtb_ref, alog_ref, tri_ref, e_ref, sh_ref,
         z_ref, yf_ref, dsk_ref, nw_ref, out_ref, ht_ref) = refs
    else:
        (xp_ref, xc_ref, xn_ref, dt_ref, cw_ref, cb_ref, dtb_ref, alog_ref, tri_ref, e_ref, sh_ref,
         out_ref, ht_ref) = refs
    Q = SSD_CHUNK

    halo_zero = jnp.zeros((HALO, SSD_CONV_CH), BF16)
    ext = jnp.concatenate([jnp.where(pos > 0, xp_ref[s], halo_zero), xc_ref[s],
                           jnp.where(pos < nc - 1, xn_ref[s], halo_zero)], axis=0)
    shifted = _dot(sh_ref[...], ext)
    conv = cb_ref[...]
    for k in range(SSD_CONV):
        conv = conv + cw_ref[k:k + 1, :] * shifted[k * Q:(k + 1) * Q, :]
    act = _silu(conv)
    xs = act[:, :SSD_WIDTH]

    dtraw = dt_ref[s] + dtb_ref[...]
    dt = jnp.maximum(dtraw, 0.0) + jnp.log(1.0 + jnp.exp(-jnp.abs(dtraw)))
    a = -jnp.exp(alog_ref[...])
    adt = dt * a
    tri = tri_ref[...]
    cvec = _dot_exact(tri, adt)
    cvt = cvec.T
    e01 = e_ref[...]
    dt_e = _dot_exact_r(dt, e01)
    cv_e = _dot_exact_r(cvec, e01)
    edge = 0 if reverse else Q - 1
    tot_e = cv_e[edge:edge + 1, :]

    li = lax.broadcasted_iota(jnp.int32, (Q, Q), 0)
    si = lax.broadcasted_iota(jnp.int32, (Q, Q), 1)
    mask = (li <= si) if reverse else (li >= si)
    lane = lax.broadcasted_iota(jnp.int32, (Q, LANES), 1)
    head0 = SSD_HEADS if reverse else 0

    xdt = xs * dt_e
    ys = []
    for g in range(SSD_NGROUPS):
        bg = act[:, SSD_WIDTH + g * SSD_DSTATE:SSD_WIDTH + (g + 1) * SSD_DSTATE]
        cg = act[:, SSD_WIDTH + (SSD_NGROUPS + g) * SSD_DSTATE:SSD_WIDTH + (SSD_NGROUPS + g + 1) * SSD_DSTATE]
        bgb = bg.astype(BF16)
        cgb = cg.astype(BF16)
        gm = _dot_nt(cgb, bgb)
        gs = slice(g * 256, (g + 1) * 256)
        xdt_g = xdt[:, gs]
        xdtb = xdt_g.astype(BF16)
        ydiag = []
        for pr in range(2):
            xpair = xdtb[:, pr * LANES:(pr + 1) * LANES]
            yh = []
            for j in range(2):
                hl = head0 + g * SSD_HPG + pr * 2 + j
                seg = cvec[:, hl:hl + 1] - cvt[hl:hl + 1, :]
                decay = jnp.exp(jnp.where(mask, seg, -jnp.inf))
                yh.append(_dot((gm * decay).astype(BF16), xpair))
            ydiag.append(jnp.where(lane < SSD_HEADDIM, yh[0], yh[1]))
        ydiag = jnp.concatenate(ydiag, axis=1)
        ht = ht_ref[s, g]
        yoff = _dot(cgb, ht.astype(BF16)) * jnp.exp(cv_e[:, gs])
        ys.append(ydiag + yoff)
        wdec = jnp.exp(tot_e[:, gs] - cv_e[:, gs])
        xw = (xdt_g * wdec).astype(BF16)
        ht_ref[s, g] = jnp.exp(tot_e[:, gs]) * ht + _dot(bg.T.astype(BF16), xw)
    y = jnp.concatenate(ys, axis=1)

    if not reverse:
        out_ref[s] = y
    else:
        y = y + yf_ref[s] + dsk_ref[...] * xs
        gated = y * _silu(z_ref[s].astype(F32))
        ms = jnp.mean(gated * gated, axis=-1, keepdims=True)
        out_ref[s] = (gated * lax.rsqrt(ms + EPS) * nw_ref[...]).astype(out_ref.dtype)


def _ssd(xbc, dt, z, conv_w, conv_b, dtb, alog, dskip_e, norm_w, B, L):
    Q = SSD_CHUNK
    nc = L // Q
    ns = SSD_SEQS if B % SSD_SEQS == 0 else 1
    hb = Q // HALO
    nhb = L // HALO
    r = jnp.arange(Q)
    pad = (SSD_CONV - 1) // 2
    col = jnp.arange(Q + 2 * HALO)[None, :]
    shift = jnp.concatenate([(col == HALO + r[:, None] + k - pad) for k in range(SSD_CONV)], axis=0).astype(BF16)
    tril = (r[:, None] >= r[None, :]).astype(BF16)
    triu = (r[:, None] <= r[None, :]).astype(BF16)
    lane = jnp.arange(LANES)[:, None]
    ch = jnp.arange(SSD_WIDTH)[None, :] // SSD_HEADDIM
    e_f = (lane == ch).astype(BF16)
    e_b = (lane == ch + SSD_HEADS).astype(BF16)

    def call(reverse, extra_in, extra_specs, out_dtype):
        pos = (lambda c: nc - 1 - c) if reverse else (lambda c: c)
        full = lambda a: pl.BlockSpec(a.shape, lambda b, c: (0,) * a.ndim)
        chunk = lambda n: pl.BlockSpec((ns, Q, n), lambda b, c: (b, pos(c), 0))
        in_specs = [
            pl.BlockSpec((ns, HALO, 1024), lambda b, c: (b, jnp.maximum(pos(c) * hb - 1, 0), 0)),
            chunk(1024),
            pl.BlockSpec((ns, HALO, 1024), lambda b, c: (b, jnp.minimum((pos(c) + 1) * hb, nhb - 1), 0)),
            chunk(LANES),
        ]
        consts = [conv_w, conv_b, dtb, alog, triu if reverse else tril, e_b if reverse else e_f, shift]
        in_specs += [full(a) for a in consts]
        in_specs += [chunk(512) if s == "chunk" else full(a) for a, s in zip(extra_in, extra_specs)]
        return pl.pallas_call(
            functools.partial(_ssd_kernel, reverse=reverse),
            grid=(B // ns, nc),
            in_specs=in_specs,
            out_specs=chunk(512),
            out_shape=jax.ShapeDtypeStruct((B, L, 512), out_dtype),
            scratch_shapes=[pltpu.VMEM((ns, SSD_NGROUPS, SSD_DSTATE, 256), F32)],
            compiler_params=_cparams("arbitrary", "arbitrary"),
            name="ssd_bwd" if reverse else "ssd_fwd",
        )(xbc, xbc, xbc, dt, *consts, *extra_in)

    yf = call(False, [], [], F32)
    return call(True, [z, yf, dskip_e, norm_w], ["chunk", "chunk", "full", "full"], BF16)


N_POS = 3
N_FEAT = 2 * N_POS + 1
MAX_SHIFT_LOG2 = 50.0
VT_ROWS = DA_VDIM + 16
ZERO_PROB_LOG2 = 160.0
DEAD_SHIFT = 1.0e4


def _max_block_dist(head, tq, nb):
    slope_log2 = 2.0 ** (-8.0 * (head + 1) / DA_HEADS) * LOG2E
    d = 0
    while d < nb - 1 and slope_log2 * (d * tq + 1) < ZERO_PROB_LOG2:
        d += 1
    return d


def _attn_kernel(slope_ref, lamv_ref, q_ref, qall_ref, k_ref, v_ref, sw_ref, o_ref,
                 kaug_ref, vt_ref, vtb_ref, pt_ref, shift_ref, worst_ref, *, tq, tk, L):
    h = pl.program_id(1)
    qi = pl.program_id(2)
    nb = L // tq
    sl2 = slope_ref[h] * LOG2E
    lv = lamv_ref[...]
    lam = (jnp.exp(jnp.sum(lv[0:1] * lv[1:2], axis=-1, keepdims=True))
           - jnp.exp(jnp.sum(lv[2:3] * lv[3:4], axis=-1, keepdims=True)) + LAMBDA_INIT)

    def feat0(c):
        return DA_HEADDIM if c == 0 else 0

    def own_lanes(lane, c):
        return (lane < DA_HEADDIM) if c == 0 else (lane >= DA_HEADDIM)

    ha = lax.broadcasted_iota(jnp.int32, (LANES, LANES), 0) < DA_HEADDIM
    hb = lax.broadcasted_iota(jnp.int32, (LANES, LANES), 1) < DA_HEADDIM
    other_half = jnp.where(ha != hb, 1.0, 0.0).astype(BF16)

    @pl.when(qi == 0)
    def _prep():
        kf = k_ref[0].astype(F32)
        lane = lax.broadcasted_iota(jnp.int32, kf.shape, 1)
        kpos = lax.broadcasted_iota(jnp.int32, kf.shape, 0).astype(F32) * sl2
        pieces = [t.astype(F32) for t in _split3(kpos)]
        for c in range(2):
            rel = lane - feat0(c)
            feat = jnp.where((rel >= 0) & (rel <= N_POS), 1.0, 0.0)
            for n in range(N_POS):
                feat = jnp.where(rel == N_POS + 1 + n, pieces[n], feat)
            kaug_ref[c] = jnp.where(own_lanes(lane, c), kf, feat).astype(BF16)
        kn2 = _dot((kf * kf).astype(BF16), other_half)
        kmax2 = jnp.max(kn2, axis=0, keepdims=True)
        qa = qall_ref[0].astype(F32)
        prod2 = _dot((qa * qa).astype(BF16), other_half) * kmax2
        bound = prod2 * lax.rsqrt(jnp.maximum(prod2, 1e-30))
        shift_all = (bound * 1.02).astype(BF16).astype(F32)
        shift_ref[...] = shift_all
        worst_ref[0] = jnp.max(shift_all)
        ones_row = lax.broadcasted_iota(jnp.int32, (VT_ROWS - DA_VDIM, tq), 0) == 0
        ones_rows = jnp.where(ones_row, 1.0, 0.0).astype(BF16)
        for jb in range(nb):
            vt = v_ref[0, jb * tq:(jb + 1) * tq, :].astype(F32).T.astype(BF16)
            vt_ref[0:DA_VDIM, jb * tq:(jb + 1) * tq] = vt
            vt_ref[DA_VDIM:, jb * tq:(jb + 1) * tq] = ones_rows
            vtb_ref[jb, 0:DA_VDIM, :] = vt
            vtb_ref[jb, DA_VDIM:, :] = ones_rows

    worst = worst_ref[0]
    lane = lax.broadcasted_iota(jnp.int32, (tq, LANES), 1)

    def load_q():
        return q_ref[0].astype(F32)

    def finish(o1, o2):
        o = o1 - lam * o2
        ms = jnp.mean(o * o, axis=-1, keepdims=True)
        o = o * lax.rsqrt(ms + EPS) * sw_ref[...] * (1.0 - LAMBDA_INIT)
        o_ref[0] = o.astype(o_ref.dtype)

    def fast_path(max_dist):
        qf = load_q()
        qpos = (qi * tq + lax.broadcasted_iota(jnp.int32, qf.shape, 0)).astype(F32) * sl2
        qpieces = [t.astype(F32) for t in _split3(qpos)]
        shift = shift_ref[pl.ds(pl.multiple_of(qi * tq, tq), tq), :]
        kr = lax.broadcasted_iota(jnp.int32, (tq, tq), 0)
        qc = lax.broadcasted_iota(jnp.int32, (tq, tq), 1)
        corr = jnp.maximum(qc - kr, 0).astype(F32) * (2.0 * sl2)
        outs = []
        for c in range(2):
            rel = lane - feat0(c)
            own = own_lanes(lane, c)
            base = jnp.where(own, qf, jnp.where(rel == N_POS, -shift, 0.0))
            posf = jnp.where((rel > N_POS) & (rel < N_FEAT), -1.0, 0.0)
            for n in range(N_POS):
                posf = jnp.where(rel == n, qpieces[n], posf)
            q_after = (base + posf).astype(BF16)
            q_before = (base - posf).astype(BF16)
            if max_dist is None:
                blocks = []
                for d in range(nb):
                    wrapped = qi + d >= nb
                    jb = jnp.where(wrapped, qi + d - nb, qi + d)
                    blocks.append((jb, q_after if d == 0 else jnp.where(wrapped, q_before, q_after), d == 0))
            else:
                q_dead = jnp.where(own, qf, jnp.where(rel == N_POS, -DEAD_SHIFT, 0.0)).astype(BF16)
                blocks = []
                for r in range(-max_dist, max_dist + 1):
                    jb = qi + r
                    side = q_before if r < 0 else q_after
                    inside = (jb >= 0) & (jb < nb)
                    blocks.append((jnp.clip(jb, 0, nb - 1), side if r == 0 else jnp.where(inside, side, q_dead),
                                   r == 0))
            acc = None
            for jb, qsel, diagonal in blocks:
                dk = pl.ds(pl.multiple_of(jb * tq, tq), tq)
                st = _dot_nt(kaug_ref[c, dk, :], qsel)
                if diagonal:
                    st = st - corr
                p = jnp.exp2(st).astype(BF16)
                if max_dist is None:
                    pt_ref[c, dk, :] = p
                else:
                    part = _dot(vtb_ref[jb], p)
                    acc = part if acc is None else acc + part
            if max_dist is None:
                acc = _dot(vt_ref[...], pt_ref[c])
            outs.append(acc[:DA_VDIM] * (1.0 / acc[DA_VDIM:DA_VDIM + 1]))
        finish(outs[0].T, outs[1].T)

    fast = worst <= MAX_SHIFT_LOG2
    dists = [_max_block_dist(hh, tq, nb) for hh in range(DA_HEADS)]
    dists = [d if 2 * d + 1 < nb else None for d in dists]
    for dist in set(dists):
        heads = [hh for hh in range(DA_HEADS) if dists[hh] == dist]
        in_heads = functools.reduce(lambda a, b: a | b, [h == hh for hh in heads])
        pl.when(fast & in_heads)(functools.partial(fast_path, dist))

    @pl.when(jnp.logical_not(fast))
    def _online():
        qf = load_q()
        zero = jnp.zeros_like(qf)
        qm = [jnp.where(own_lanes(lane, c), qf, zero).astype(BF16) for c in range(2)]
        qk0 =(qi * tq + lax.broadcasted_iota(jnp.int32, (tq, tk), 0)
               - lax.broadcasted_iota(jnp.int32, (tq, tk), 1)).astype(F32)

        def body(j, carry):
            kb = k_ref[0, pl.ds(pl.multiple_of(j * tk, tk), tk), :]
            vb = v_ref[0, pl.ds(pl.multiple_of(j * tk, tk), tk), :]
            bias = sl2 * jnp.abs(qk0 - lax.convert_element_type(j * tk, F32))
            out = []
            for c in range(2):
                m, l, acc = carry[c]
                s = _dot_nt(qm[c], kb) - bias
                m_new = jnp.maximum(m, jnp.max(s, axis=-1, keepdims=True))
                p = jnp.exp2(s - m_new)
                alpha = jnp.exp2(m - m_new)
                l_new = alpha * l + jnp.sum(p, axis=-1, keepdims=True)
                acc_new = alpha * acc + _dot(p.astype(BF16), vb)
                out.append((m_new, l_new, acc_new))
            return tuple(out)

        init = tuple((jnp.full((tq, 1), -jnp.inf, F32), jnp.zeros((tq, 1), F32),
                      jnp.zeros((tq, DA_VDIM), F32)) for _ in range(2))
        (m1, l1, a1), (m2, l2, a2) = lax.fori_loop(0, L // tk, body, init)
        finish(a1 * (1.0 / l1), a2 * (1.0 / l2))


def _diff_attention(q, k, v, slopes, lamv, subln_w, B, L):
    tq, tk = min(TQ, L), min(TK, L)
    return pl.pallas_call(
        functools.partial(_attn_kernel, tq=tq, tk=tk, L=L),
        grid=(B, DA_HEADS, L // tq),
        in_specs=[
            pl.BlockSpec(memory_space=pltpu.SMEM),
            pl.BlockSpec(lamv.shape, lambda b, h, i: (0, 0)),
            pl.BlockSpec((1, tq, LANES), lambda b, h, i: (b, i, h)),
            pl.BlockSpec((1, L, LANES), lambda b, h, i: (b, 0, h)),
            pl.BlockSpec((1, L, LANES), lambda b, h, i: (b, 0, h)),
            pl.BlockSpec((1, L, LANES), lambda b, h, i: (b, 0, h)),
            pl.BlockSpec(subln_w.shape, lambda b, h, i: (0, 0)),
        ],
        out_specs=pl.BlockSpec((1, tq, LANES), lambda b, h, i: (b, i, h)),
        out_shape=jax.ShapeDtypeStruct((B, L, 512), BF16),
        scratch_shapes=[pltpu.VMEM((2, L, LANES), BF16),
                        pltpu.VMEM((VT_ROWS, L), BF16),
                        pltpu.VMEM((L // tq, VT_ROWS, tq), BF16),
                        pltpu.VMEM((2, L, tq), BF16),
                        pltpu.VMEM((L, LANES), F32),
                        pltpu.SMEM((1,), F32)],
        compiler_params=_cparams("arbitrary", "arbitrary", "arbitrary"),
        name="diff_attn",
    )(slopes, lamv, q, q, k, v, subln_w)


def _pair_tables():
    lo, hi = [], []
    for g in range(N_GROUPS):
        for a in range(EPG):
            for b in range(a + 1, EPG):
                lo.append(g * EPG + a)
                hi.append(g * EPG + b)
    return lo, hi


PAIRS_PER_GROUP = EPG * (EPG - 1) // 2
N_BUCKETS = N_GROUPS * PAIRS_PER_GROUP
GATE_LANE = 0
ROW_TAIL = LANES


def _router_kernel(x_ref, ys_ref, ya_ref, wo_ref, nw_ref, wr_ref, br_ref, tri_ref,
                   h_ref, u_ref, meta_ref, cnt_ref, carry_ref):
    i = pl.program_id(0)

    @pl.when(i == 0)
    def _():
        carry_ref[...] = jnp.zeros_like(carry_ref)

    h = x_ref[...] + _dot(ys_ref[...], wo_ref[0:512, :]) + _dot(ya_ref[...], wo_ref[512:1024, :])
    h_ref[...] = h
    D = h.shape[1]
    ms = jnp.mean(h * h, axis=-1, keepdims=True)
    u = h * lax.rsqrt(ms + EPS) * nw_ref[...]
    u_ref[:, 0:D] = u
    logits = _dot(u.astype(BF16), wr_ref[...]) + br_ref[...]

    tm = logits.shape[0]
    li = lax.broadcasted_iota(jnp.int32, (tm, LANES), 1)
    lf = li.astype(F32)
    ninf = jnp.float32(-jnp.inf)
    big = jnp.float32(1e9)
    gl = jnp.where(li < N_GROUPS, logits, ninf)
    gmax = jnp.max(gl, axis=-1, keepdims=True)
    gidx = jnp.min(jnp.where(gl == gmax, lf, big), axis=-1, keepdims=True)
    pg = 1.0 / jnp.sum(jnp.exp(gl - gmax), axis=-1, keepdims=True)
    lane_grp = ((li - EXP_LANE0) >> 3).astype(F32)
    in_grp = (li >= EXP_LANE0) & (li < EXP_LANE0 + N_EXPERTS) & (lane_grp == gidx)
    sl = jnp.where(in_grp, logits, ninf)
    m1 = jnp.max(sl, axis=-1, keepdims=True)
    i1 = jnp.min(jnp.where(sl == m1, lf, big), axis=-1, keepdims=True)
    sl2 = jnp.where(lf == i1, ninf, sl)
    m2 = jnp.max(sl2, axis=-1, keepdims=True)
    i2 = jnp.min(jnp.where(sl2 == m2, lf, big), axis=-1, keepdims=True)
    t = jnp.exp(m2 - m1)
    w1 = 1.0 / (1.0 + t)
    g1 = pg * w1
    g2 = pg * (t * w1)
    a1 = i1 - EXP_LANE0 - gidx * EPG
    a2 = i2 - EXP_LANE0 - gidx * EPG
    lo = jnp.minimum(a1, a2)
    hi = jnp.maximum(a1, a2)
    bucket = gidx * PAIRS_PER_GROUP + (lo * (EPG - 1) - lo * (lo - 1.0) * 0.5) + (hi - lo - 1.0)
    first_is_lo = a1 < a2
    g_lo = jnp.where(first_is_lo, g1, g2)
    g_hi = jnp.where(first_is_lo, g2, g1)
    ohb = lf == bucket
    oh = jnp.where(ohb, 1.0, 0.0)
    prefix = _dot(tri_ref[...], oh.astype(BF16)) + carry_ref[0:1, :]
    rank = jnp.sum(jnp.where(ohb, prefix, 0.0), axis=-1, keepdims=True)
    new_carry = carry_ref[0:1, :] + jnp.sum(oh, axis=0, keepdims=True)
    carry_ref[...] = jnp.broadcast_to(new_carry, carry_ref.shape)
    cnt_ref[...] = jnp.broadcast_to(new_carry, cnt_ref.shape)

    meta_ref[...] = jnp.where(li == 0, bucket, jnp.where(li == 1, rank, 0.0))
    u_ref[:, D:] = jnp.where(li == GATE_LANE, g_lo, jnp.where(li == GATE_LANE + 1, g_hi, 0.0))


def _out_router(x2, y_ssd, y_da, w_out, norm_w, w_r, b_r):
    T, D = x2.shape
    tm = min(TM_ROUTE, T)
    r = jnp.arange(tm)
    tri = (r[:, None] > r[None, :]).astype(BF16)
    row = lambda n: pl.BlockSpec((tm, n), lambda i: (i, 0))
    full = lambda a: pl.BlockSpec(a.shape, lambda i: (0,) * a.ndim)
    return pl.pallas_call(
        _router_kernel,
        grid=(T // tm,),
        in_specs=[row(D), row(512), row(512), full(w_out), full(norm_w), full(w_r), full(b_r), full(tri)],
        out_specs=[row(D), row(D + ROW_TAIL), row(LANES), pl.BlockSpec((SUBLANES, LANES), lambda i: (0, 0))],
        out_shape=[
            jax.ShapeDtypeStruct((T, D), F32),
            jax.ShapeDtypeStruct((T, D + ROW_TAIL), F32),
            jax.ShapeDtypeStruct((T, LANES), F32),
            jax.ShapeDtypeStruct((SUBLANES, LANES), F32),
        ],
        scratch_shapes=[pltpu.VMEM((SUBLANES, LANES), F32)],
        compiler_params=_cparams("arbitrary"),
        name="out_router",
    )(x2, y_ssd, y_da, w_out, norm_w, w_r, b_r, tri)


def _dispatch_kernel(cnt_ref, idx_ref, u_ref, xs_hbm, slots_ref, ea_ref, eb_ref, nused_ref,
                     pstart_ref, zero_ref, sem, *, tm, n_blocks):
    i = pl.program_id(0)

    @pl.when(i == 0)
    def _():
        zero_ref[...] = jnp.zeros_like(zero_ref)
        pair_lo, pair_hi = _pair_tables()
        acc = jnp.int32(0)
        for q in range(N_BUCKETS):
            padded = ((cnt_ref[q] + (MOE_ROWS - 1)) // MOE_ROWS) * MOE_ROWS
            pstart_ref[q] = acc
            end = acc + padded

            def fill(j, _, q=q):
                ea_ref[j] = jnp.int32(pair_lo[q])
                eb_ref[j] = jnp.int32(pair_hi[q])
                return 0

            lax.fori_loop(acc // MOE_ROWS, end // MOE_ROWS, fill, 0)

            @pl.when(padded > 0)
            def _(end=end):
                dst = xs_hbm.at[pl.ds(pl.multiple_of(end - MOE_ROWS, MOE_ROWS), MOE_ROWS)]
                cp = pltpu.make_async_copy(zero_ref, dst, sem)
                cp.start()
                cp.wait()

            acc = end
        nused = acc // MOE_ROWS
        nused_ref[0] = nused

        def fill_tail(j, _):
            ea_ref[j] = jnp.int32(0)
            eb_ref[j] = jnp.int32(0)
            dst = xs_hbm.at[pl.ds(pl.multiple_of(j * MOE_ROWS, MOE_ROWS), MOE_ROWS)]
            cp = pltpu.make_async_copy(zero_ref, dst, sem)
            cp.start()
            cp.wait()
            return 0

        lax.fori_loop(nused, n_blocks, fill_tail, 0)

    def start(t, _):
        slot = pstart_ref[idx_ref[0, t]] + idx_ref[1, t]
        slots_ref[0, t] = slot
        pltpu.make_async_copy(u_ref.at[pl.ds(t, 1)], xs_hbm.at[pl.ds(slot, 1)], sem).start()
        return 0

    lax.fori_loop(0, tm, start, 0)
    pltpu.make_async_copy(u_ref, u_ref, sem).wait()


def _dispatch(counts, idx, u, n_blocks):
    T, W = u.shape
    tm = min(TM_DISP, T)
    n_slots = n_blocks * MOE_ROWS
    grid_spec = pltpu.PrefetchScalarGridSpec(
        num_scalar_prefetch=1,
        grid=(T // tm,),
        in_specs=[
            pl.BlockSpec((2, tm), lambda i, cnt: (0, i), memory_space=pltpu.SMEM),
            pl.BlockSpec((tm, W), lambda i, cnt: (i, 0)),
        ],
        out_specs=[
            pl.BlockSpec(memory_space=pl.ANY),
            pl.BlockSpec((1, tm), lambda i, cnt: (0, i), memory_space=pltpu.SMEM),
            pl.BlockSpec(memory_space=pltpu.SMEM),
            pl.BlockSpec(memory_space=pltpu.SMEM),
            pl.BlockSpec(memory_space=pltpu.SMEM),
        ],
        scratch_shapes=[pltpu.SMEM((LANES,), jnp.int32),
                        pltpu.VMEM((MOE_ROWS, W), F32),
                        pltpu.SemaphoreType.DMA(())],
    )
    return pl.pallas_call(
        functools.partial(_dispatch_kernel, tm=tm, n_blocks=n_blocks),
        grid_spec=grid_spec,
        out_shape=[
            jax.ShapeDtypeStruct((n_slots, W), F32),
            jax.ShapeDtypeStruct((1, T), jnp.int32),
            jax.ShapeDtypeStruct((n_blocks,), jnp.int32),
            jax.ShapeDtypeStruct((n_blocks,), jnp.int32),
            jax.ShapeDtypeStruct((1,), jnp.int32),
        ],
        compiler_params=_cparams("arbitrary"),
        name="dispatch",
    )(counts, idx, u)


def _moe_kernel(ea_ref, eb_ref, nused_ref, xs_ref, wga_ref, wua_ref, wda_ref, wgb_ref, wub_ref, wdb_ref,
                o_ref, wa_g, wa_u, wa_d, wb_g, wb_u, wb_d):
    j = pl.program_id(0)
    last = nused_ref[0] - 1
    cur = jnp.minimum(j, last)
    prev = jnp.minimum(jnp.maximum(j - 1, 0), last)
    D = o_ref.shape[1]

    @pl.when((j == 0) | (ea_ref[cur] != ea_ref[prev]))
    def _():
        wa_g[...] = wga_ref[0].astype(BF16)
        wa_u[...] = wua_ref[0].astype(BF16)
        wa_d[...] = wda_ref[0].astype(BF16)

    @pl.when((j == 0) | (eb_ref[cur] != eb_ref[prev]))
    def _():
        wb_g[...] = wgb_ref[0].astype(BF16)
        wb_u[...] = wub_ref[0].astype(BF16)
        wb_d[...] = wdb_ref[0].astype(BF16)

    @pl.when(j < nused_ref[0])
    def _():
        x = xs_ref[:, 0:D].astype(BF16)
        tail = xs_ref[:, D:]
        out = None
        for (wg, wu, wd), lane in (((wa_g, wa_u, wa_d), GATE_LANE), ((wb_g, wb_u, wb_d), GATE_LANE + 1)):
            hid = _silu(_dot(x, wg[...])) * _dot(x, wu[...])
            y = _dot(hid.astype(BF16), wd[...]) * tail[:, lane:lane + 1]
            out = y if out is None else out + y
        o_ref[...] = out

    @pl.when(j >= nused_ref[0])
    def _():
        o_ref[...] = jnp.zeros_like(o_ref)


def _moe(block_ea, block_eb, nused, xs, wg, wu, wd, n_blocks):
    W = xs.shape[1]
    D = W - ROW_TAIL
    used = lambda j, nu: jnp.minimum(j, nu[0] - 1)
    wspec_a = lambda a: pl.BlockSpec((1,) + a.shape[1:], lambda j, ea, eb, nu: (ea[used(j, nu)], 0, 0))
    wspec_b = lambda a: pl.BlockSpec((1,) + a.shape[1:], lambda j, ea, eb, nu: (eb[used(j, nu)], 0, 0))
    wscratch = [pltpu.VMEM(a.shape[1:], BF16) for a in (wg, wu, wd)]
    grid_spec = pltpu.PrefetchScalarGridSpec(
        num_scalar_prefetch=3,
        grid=(n_blocks,),
        in_specs=[
            pl.BlockSpec((MOE_ROWS, W), lambda j, ea, eb, nu: (used(j, nu), 0)),
            wspec_a(wg), wspec_a(wu), wspec_a(wd), wspec_b(wg), wspec_b(wu), wspec_b(wd),
        ],
        out_specs=pl.BlockSpec((MOE_ROWS, D), lambda j, ea, eb, nu: (j, 0)),
        scratch_shapes=wscratch + wscratch,
    )
    return pl.pallas_call(
        _moe_kernel,
        grid_spec=grid_spec,
        out_shape=jax.ShapeDtypeStruct((n_blocks * MOE_ROWS, D), F32),
        compiler_params=pltpu.CompilerParams(dimension_semantics=("arbitrary",),
                                             vmem_limit_bytes=MOE_VMEM_LIMIT),
        name="moe_mlp",
    )(block_ea, block_eb, nused, xs, wg, wu, wd, wg, wu, wd)


def _combine_kernel(slots_ref, next_slots_ref, h_ref, nw_ref, eo_hbm, o_ref, g_ref, sems, *, tm):
    i = pl.program_id(0)
    n = pl.num_programs(0)
    cur = i % 2

    def gather_tile(sref, buf):
        def start(t, _):
            pltpu.make_async_copy(eo_hbm.at[pl.ds(sref[0, t], 1)],
                                  g_ref.at[buf, pl.ds(t, 1)], sems.at[buf]).start()
            return 0

        lax.fori_loop(0, tm, start, 0)

    @pl.when(i == 0)
    def _():
        gather_tile(slots_ref, 0)

    @pl.when(i + 1 < n)
    def _():
        gather_tile(next_slots_ref, 1 - cur)

    pltpu.make_async_copy(g_ref.at[cur], g_ref.at[cur], sems.at[cur]).wait()

    y = h_ref[...] + g_ref[cur]
    ms = jnp.mean(y * y, axis=-1, keepdims=True)
    o_ref[...] = y * lax.rsqrt(ms + EPS) * nw_ref[...]


def _combine(slots, h, norm_w, eo):
    T, D = h.shape
    tm = min(TM_COMB, T)
    last = T // tm - 1
    return pl.pallas_call(
        functools.partial(_combine_kernel, tm=tm),
        grid=(T // tm,),
        in_specs=[
            pl.BlockSpec((1, tm), lambda i: (0, i), memory_space=pltpu.SMEM),
            pl.BlockSpec((1, tm), lambda i: (0, jnp.minimum(i + 1, last)), memory_space=pltpu.SMEM),
            pl.BlockSpec((tm, D), lambda i: (i, 0)),
            pl.BlockSpec(norm_w.shape, lambda i: (0, 0)),
            pl.BlockSpec(memory_space=pl.ANY),
        ],
        out_specs=pl.BlockSpec((tm, D), lambda i: (i, 0)),
        out_shape=jax.ShapeDtypeStruct((T, D), F32),
        scratch_shapes=[pltpu.VMEM((2, tm, D), F32),
                        pltpu.SemaphoreType.DMA((2,))],
        compiler_params=_cparams("arbitrary"),
        name="combine",
    )(slots, slots, h, norm_w, eo)


def _pad_lanes(v, offset=0):
    v = v.astype(F32).reshape(-1)
    return jnp.zeros((1, LANES), F32).at[0, offset:offset + v.shape[0]].set(v)


def kernel(x, norm_mix_w, w_in, conv_w, conv_b, dt_bias_fwd, dt_bias_bwd, a_log_fwd, a_log_bwd, ssd_d,
           ssd_norm_w, lambda_q1, lambda_k1, lambda_q2, lambda_k2, subln_w, w_out, norm_ffn_w,
           w_router_group, b_router_group, w_router_exp, b_router_exp, w_exp_gate, w_exp_up, w_exp_down,
           norm_final_w):
    B, L, D = x.shape
    T = B * L
    x2 = x.reshape(T, D)
    l = 0

    w = w_in[l]
    o_z, o_xbc, o_dt = 512, 512 + 1024, 512 + 1024 + 16
    o_q, o_k = o_dt + 512, o_dt + 1024
    w_main = jnp.concatenate(
        [w[:, :o_xbc], w[:, o_dt:o_q] * (DA_HEADDIM ** -0.5 * LOG2E), w[:, o_q:]], axis=1).astype(BF16)
    w_dt = jnp.zeros((D, LANES), F32).at[:, :16].set(w[:, o_xbc:o_dt]).astype(BF16)

    z, xbc, dt, q, k, v = _in_proj(x2, norm_mix_w[l].reshape(1, D), w_main, w_dt)

    dtb = _pad_lanes(jnp.concatenate([dt_bias_fwd[l], dt_bias_bwd[l]]))
    alog = _pad_lanes(jnp.concatenate([a_log_fwd[l], a_log_bwd[l]]))
    dskip_e = jnp.repeat(ssd_d[l].astype(F32), SSD_HEADDIM).reshape(1, SSD_WIDTH)
    y_ssd = _ssd(xbc.reshape(B, L, 1024), dt.reshape(B, L, LANES), z.reshape(B, L, 512),
                 conv_w[l].astype(F32), conv_b[l].reshape(1, -1).astype(F32), dtb, alog, dskip_e,
                 ssd_norm_w[l].reshape(1, -1).astype(F32), B, L)

    slopes = jnp.power(2.0, -8.0 * jnp.arange(1, DA_HEADS + 1, dtype=F32) / DA_HEADS)
    lamv = jnp.concatenate([_pad_lanes(lambda_q1[l]), _pad_lanes(lambda_k1[l]),
                            _pad_lanes(lambda_q2[l]), _pad_lanes(lambda_k2[l]),
                            jnp.zeros((4, LANES), F32)], axis=0)
    y_da = _diff_attention(q.reshape(B, L, 512), k.reshape(B, L, 512), v.reshape(B, L, 512),
                           slopes, lamv, subln_w[l].reshape(1, -1).astype(F32), B, L)

    w_r = (jnp.zeros((D, LANES), F32).at[:, :N_GROUPS].set(w_router_group[l])
           .at[:, EXP_LANE0:EXP_LANE0 + N_EXPERTS].set(w_router_exp[l])).astype(BF16)
    b_r = _pad_lanes(jnp.concatenate([b_router_group[l], b_router_exp[l]]))
    h, u, meta, cnt = _out_router(x2, y_ssd.reshape(T, 512), y_da.reshape(T, 512), w_out[l].astype(BF16),
                                  norm_ffn_w[l].reshape(1, D), w_r, b_r)

    n_blocks = T // MOE_ROWS + N_BUCKETS
    counts = cnt[0].astype(jnp.int32)
    idx = meta[:, :2].astype(jnp.int32).T
    xs, slots, block_ea, block_eb, nused = _dispatch(counts, idx, u, n_blocks)
    eo = _moe(block_ea, block_eb, nused, xs, w_exp_gate[l], w_exp_up[l], w_exp_down[l], n_blocks)
    out = _combine(slots, h, norm_final_w.reshape(1, D), eo)
    return out.reshape(B, L, D)
```

```python
import functools
import math

import jax
import jax.numpy as jnp
from jax import lax
from jax.experimental import pallas as pl
from jax.experimental.pallas import tpu as pltpu

F32 = jnp.float32
BF16 = jnp.bfloat16

EPS = 1e-6
LOG2E = math.log2(math.e)
LANES = 128
SUBLANES = 8
HALO = 16
VMEM_LIMIT = 48 * 1024 * 1024
MOE_VMEM_LIMIT = 56 * 1024 * 1024

SSD_WIDTH = 512
SSD_HEADDIM = 64
SSD_HEADS = 8
SSD_NGROUPS = 2
SSD_HPG = 4
SSD_DSTATE = 128
SSD_CONV = 5
SSD_CHUNK = 128
SSD_CONV_CH = 1024
DA_HEADDIM = 64
DA_VDIM = 128
DA_HEADS = 4
N_GROUPS = 4
EPG = 8
N_EXPERTS = 32
D_EXPERT = 512
LAMBDA_INIT = 0.8 - 0.6 * math.exp(-0.3 * 0)

TM_PROJ = 512
SSD_SEQS = 2
TQ = 512
TK = 512
TM_ROUTE = 1024
ROUTE_SUB = 512
TM_DISP = 1024
MOE_ROWS = 256
TM_COMB = 256
EXP_LANE0 = 4


def _cparams(*sem):
    return pltpu.CompilerParams(dimension_semantics=sem, vmem_limit_bytes=VMEM_LIMIT)


def _dot(a, b):
    return jnp.dot(a, b, preferred_element_type=F32)


def _dot_nt(a, b):
    return lax.dot_general(a, b, (((1,), (1,)), ((), ())), preferred_element_type=F32)


def _split3(v):
    hi = v.astype(BF16)
    r1 = v - hi.astype(F32)
    mid = r1.astype(BF16)
    lo = (r1 - mid.astype(F32)).astype(BF16)
    return hi, mid, lo


def _dot_exact(a01, v):
    hi, mid, lo = _split3(v)
    return _dot(a01, hi) + _dot(a01, mid) + _dot(a01, lo)


def _dot_exact_r(v, b01):
    hi, mid, lo = _split3(v)
    return _dot(hi, b01) + _dot(mid, b01) + _dot(lo, b01)


def _silu(x):
    return x * (1.0 / (1.0 + jnp.exp(-x)))


def _inproj_kernel(x_ref, nw_ref, w_ref, wdt_ref, z_ref, xbc_ref, dt_ref, q_ref, k_ref, v_ref):
    x = x_ref[...]
    ms = jnp.mean(x * x, axis=-1, keepdims=True)
    u = (x * lax.rsqrt(ms + EPS) * nw_ref[...]).astype(BF16)
    z_ref[...] = _dot(u, w_ref[:, 0:512]).astype(BF16)
    xbc_ref[...] = _dot(u, w_ref[:, 512:1536]).astype(BF16)
    q_ref[...] = _dot(u, w_ref[:, 1536:2048]).astype(BF16)
    k_ref[...] = _dot(u, w_ref[:, 2048:2560]).astype(BF16)
    v_ref[...] = _dot(u, w_ref[:, 2560:3072]).astype(BF16)
    dt_ref[...] = _dot(u, wdt_ref[...])


def _in_proj(x2, norm_w, w_main, w_dt):
    T, D = x2.shape
    tm = TM_PROJ
    row = lambda n: pl.BlockSpec((tm, n), lambda i: (i, 0))
    full = lambda a: pl.BlockSpec(a.shape, lambda i: (0,) * a.ndim)
    return pl.pallas_call(
        _inproj_kernel,
        grid=(T // tm,),
        in_specs=[row(D), full(norm_w), full(w_main), full(w_dt)],
        out_specs=[row(512), row(1024), row(LANES), row(512), row(512), row(512)],
        out_shape=[
            jax.ShapeDtypeStruct((T, 512), BF16),
            jax.ShapeDtypeStruct((T, 1024), BF16),
            jax.ShapeDtypeStruct((T, LANES), F32),
            jax.ShapeDtypeStruct((T, 512), BF16),
            jax.ShapeDtypeStruct((T, 512), BF16),
            jax.ShapeDtypeStruct((T, 512), BF16),
        ],
        compiler_params=_cparams("arbitrary"),
        name="in_proj",
    )(x2, norm_w, w_main, w_dt)


def _ssd_kernel(*refs, reverse):
    c = pl.program_id(1)
    nc = pl.num_programs(1)
    pos = (nc - 1 - c) if reverse else c
    ht_ref = refs[-1]

    @pl.when(c == 0)
    def _():
        ht_ref[...] = jnp.zeros_like(ht_ref)

    for s in range(ht_ref.shape[0]):
        _ssd_sequence(s, pos, nc, refs, reverse)


def _ssd_sequence(s, pos, nc, refs, reverse):
    if reverse:
        (xp_ref, xc_ref, xn_ref, dt_ref, cw_ref, cb_ref, dtb_ref, alog_ref, tri_ref, e_ref, sh_ref,
         z_ref, yf_ref, dsk_ref, nw_ref, out_ref, ht_ref) = refs
    else:
        (xp_ref, xc_ref, xn_ref, dt_ref, cw_ref, cb_ref, dtb_ref, alog_ref, tri_ref, e_ref, sh_ref,
         out_ref, ht_ref) = refs
    Q = SSD_CHUNK

    halo_zero = jnp.zeros((HALO, SSD_CONV_CH), BF16)
    ext = jnp.concatenate([jnp.where(pos > 0, xp_ref[s], halo_zero), xc_ref[s],
                           jnp.where(pos < nc - 1, xn_ref[s], halo_zero)], axis=0)
    shifted = _dot(sh_ref[...], ext)
    conv = cb_ref[...]
    for k in range(SSD_CONV):
        conv = conv + cw_ref[k:k + 1, :] * shifted[k * Q:(k + 1) * Q, :]
    act = _silu(conv)
    xs = act[:, :SSD_WIDTH]

    dtraw = dt_ref[s] + dtb_ref[...]
    dt = jnp.maximum(dtraw, 0.0) + jnp.log(1.0 + jnp.exp(-jnp.abs(dtraw)))
    a = -jnp.exp(alog_ref[...])
    adt = dt * a
    tri = tri_ref[...]
    cvec = _dot_exact(tri, adt)
    cvt = cvec.T
    e01 = e_ref[...]
    dt_e = _dot_exact_r(dt, e01)
    cv_e = _dot_exact_r(cvec, e01)
    edge = 0 if reverse else Q - 1
    tot_e = cv_e[edge:edge + 1, :]

    li = lax.broadcasted_iota(jnp.int32, (Q, Q), 0)
    si = lax.broadcasted_iota(jnp.int32, (Q, Q), 1)
    mask = (li <= si) if reverse else (li >= si)
    lane = lax.broadcasted_iota(jnp.int32, (Q, LANES), 1)
    head0 = SSD_HEADS if reverse else 0

    xdt = xs * dt_e
    ys = []
    for g in range(SSD_NGROUPS):
        bg = act[:, SSD_WIDTH + g * SSD_DSTATE:SSD_WIDTH + (g + 1) * SSD_DSTATE]
        cg = act[:, SSD_WIDTH + (SSD_NGROUPS + g) * SSD_DSTATE:SSD_WIDTH + (SSD_NGROUPS + g + 1) * SSD_DSTATE]
        bgb = bg.astype(BF16)
        cgb = cg.astype(BF16)
        gm = _dot_nt(cgb, bgb)
        gs = slice(g * 256, (g + 1) * 256)
        xdt_g = xdt[:, gs]
        xdtb = xdt_g.astype(BF16)
        ydiag = []
        for pr in range(2):
            xpair = xdtb[:, pr * LANES:(pr + 1) * LANES]
            yh = []
            for j in range(2):
                hl = head0 + g * SSD_HPG + pr * 2 + j
                seg = cvec[:, hl:hl + 1] - cvt[hl:hl + 1, :]
                decay = jnp.exp(jnp.where(mask, seg, -jnp.inf))
                yh.append(_dot((gm * decay).astype(BF16), xpair))
            ydiag.append(jnp.where(lane < SSD_HEADDIM, yh[0], yh[1]))
        ydiag = jnp.concatenate(ydiag, axis=1)
        ht = ht_ref[s, g]
        yoff = _dot(cgb, ht.astype(BF16)) * jnp.exp(cv_e[:, gs])
        ys.append(ydiag + yoff)
        wdec = jnp.exp(tot_e[:, gs] - cv_e[:, gs])
        xw = (xdt_g * wdec).astype(BF16)
        ht_ref[s, g] = jnp.exp(tot_e[:, gs]) * ht + _dot(bg.T.astype(BF16), xw)
    y = jnp.concatenate(ys, axis=1)

    if not reverse:
        out_ref[s] = y
    else:
        y = y + yf_ref[s] + dsk_ref[...] * xs
        gated = y * _silu(z_ref[s].astype(F32))
        ms = jnp.mean(gated * gated, axis=-1, keepdims=True)
        out_ref[s] = (gated * lax.rsqrt(ms + EPS) * nw_ref[...]).astype(out_ref.dtype)


def _ssd(xbc, dt, z, conv_w, conv_b, dtb, alog, dskip_e, norm_w, B, L):
    Q = SSD_CHUNK
    nc = L // Q
    ns = SSD_SEQS if B % SSD_SEQS == 0 else 1
    hb = Q // HALO
    nhb = L // HALO
    r = jnp.arange(Q)
    pad = (SSD_CONV - 1) // 2
    col = jnp.arange(Q + 2 * HALO)[None, :]
    shift = jnp.concatenate([(col == HALO + r[:, None] + k - pad) for k in range(SSD_CONV)], axis=0).astype(BF16)
    tril = (r[:, None] >= r[None, :]).astype(BF16)
    triu = (r[:, None] <= r[None, :]).astype(BF16)
    lane = jnp.arange(LANES)[:, None]
    ch = jnp.arange(SSD_WIDTH)[None, :] // SSD_HEADDIM
    e_f = (lane == ch).astype(BF16)
    e_b = (lane == ch + SSD_HEADS).astype(BF16)

    def call(reverse, extra_in, extra_specs, out_dtype):
        pos = (lambda c: nc - 1 - c) if reverse else (lambda c: c)
        full = lambda a: pl.BlockSpec(a.shape, lambda b, c: (0,) * a.ndim)
        chunk = lambda n: pl.BlockSpec((ns, Q, n), lambda b, c: (b, pos(c), 0))
        in_specs = [
            pl.BlockSpec((ns, HALO, 1024), lambda b, c: (b, jnp.maximum(pos(c) * hb - 1, 0), 0)),
            chunk(1024),
            pl.BlockSpec((ns, HALO, 1024), lambda b, c: (b, jnp.minimum((pos(c) + 1) * hb, nhb - 1), 0)),
            chunk(LANES),
        ]
        consts = [conv_w, conv_b, dtb, alog, triu if reverse else tril, e_b if reverse else e_f, shift]
        in_specs += [full(a) for a in consts]
        in_specs += [chunk(512) if s == "chunk" else full(a) for a, s in zip(extra_in, extra_specs)]
        return pl.pallas_call(
            functools.partial(_ssd_kernel, reverse=reverse),
            grid=(B // ns, nc),
            in_specs=in_specs,
            out_specs=chunk(512),
            out_shape=jax.ShapeDtypeStruct((B, L, 512), out_dtype),
            scratch_shapes=[pltpu.VMEM((ns, SSD_NGROUPS, SSD_DSTATE, 256), F32)],
            compiler_params=_cparams("arbitrary", "arbitrary"),
            name="ssd_bwd" if reverse else "ssd_fwd",
        )(xbc, xbc, xbc, dt, *consts, *extra_in)

    yf = call(False, [], [], F32)
    return call(True, [z, yf, dskip_e, norm_w], ["chunk", "chunk", "full", "full"], BF16)


N_POS = 3
N_FEAT = 2 * N_POS + 1
MAX_SHIFT_LOG2 = 50.0
VT_ROWS = DA_VDIM + 16
ZERO_PROB_LOG2 = 160.0
DEAD_SHIFT = 1.0e4


def _max_block_dist(head, tq, nb):
    slope_log2 = 2.0 ** (-8.0 * (head + 1) / DA_HEADS) * LOG2E
    d = 0
    while d < nb - 1 and slope_log2 * (d * tq + 1) < ZERO_PROB_LOG2:
        d += 1
    return d


def _attn_kernel(slope_ref, lamv_ref, q_ref, qall_ref, k_ref, v_ref, sw_ref, o_ref,
                 kaug_ref, kfeat_ref, vt_ref, vtb_ref, pt_ref, shift_ref, worst_ref, *, tq, tk, L):
    h = pl.program_id(0)
    b = pl.program_id(1)
    qi = pl.program_id(2)
    nb = L // tq
    sl2 = slope_ref[h] * LOG2E
    lv = lamv_ref[...]
    lam = (jnp.exp(jnp.sum(lv[0:1] * lv[1:2], axis=-1, keepdims=True))
           - jnp.exp(jnp.sum(lv[2:3] * lv[3:4], axis=-1, keepdims=True)) + LAMBDA_INIT)

    def feat0(c):
        return DA_HEADDIM if c == 0 else 0

    def own_lanes(lane, c):
        return (lane < DA_HEADDIM) if c == 0 else (lane >= DA_HEADDIM)

    ha = lax.broadcasted_iota(jnp.int32, (LANES, LANES), 0) < DA_HEADDIM
    hb = lax.broadcasted_iota(jnp.int32, (LANES, LANES), 1) < DA_HEADDIM
    other_half = jnp.where(ha != hb, 1.0, 0.0).astype(BF16)

    @pl.when((b == 0) & (qi == 0))
    def _tables():
        lane = lax.broadcasted_iota(jnp.int32, (L, LANES), 1)
        kpos = lax.broadcasted_iota(jnp.int32, (L, LANES), 0).astype(F32) * sl2
        pieces = [t.astype(F32) for t in _split3(kpos)]
        for c in range(2):
            rel = lane - feat0(c)
            feat = jnp.where((rel >= 0) & (rel <= N_POS), 1.0, 0.0)
            for n in range(N_POS):
                feat = jnp.where(rel == N_POS + 1 + n, pieces[n], feat)
            kfeat_ref[c] = feat.astype(BF16)

    @pl.when(qi == 0)
    def _prep():
        kraw = k_ref[0]
        lane = lax.broadcasted_iota(jnp.int32, kraw.shape, 1)
        for c in range(2):
            kaug_ref[c] = jnp.where(own_lanes(lane, c), kraw, kfeat_ref[c])
        kf = kraw.astype(F32)
        kn2 =_dot((kf * kf).astype(BF16), other_half)
        kmax2 = jnp.max(kn2, axis=0, keepdims=True)
        qa = qall_ref[0].astype(F32)
        prod2 = _dot((qa * qa).astype(BF16), other_half) * kmax2
        bound = prod2 * lax.rsqrt(jnp.maximum(prod2, 1e-30))
        shift_all = (bound * 1.02).astype(BF16).astype(F32)
        shift_ref[...] = shift_all
        worst_ref[0] = jnp.max(shift_all)
        ones_row = lax.broadcasted_iota(jnp.int32, (VT_ROWS - DA_VDIM, tq), 0) == 0
        ones_rows = jnp.where(ones_row, 1.0, 0.0).astype(BF16)
        for jb in range(nb):
            vt = v_ref[0, jb * tq:(jb + 1) * tq, :].astype(F32).T.astype(BF16)
            vt_ref[0:DA_VDIM, jb * tq:(jb + 1) * tq] = vt
            vt_ref[DA_VDIM:, jb * tq:(jb + 1) * tq] = ones_rows
            vtb_ref[jb, 0:DA_VDIM, :] = vt
            vtb_ref[jb, DA_VDIM:, :] = ones_rows

    worst = worst_ref[0]
    lane = lax.broadcasted_iota(jnp.int32, (tq, LANES), 1)

    def load_q():
        return q_ref[0].astype(F32)

    def finish(o1, o2):
        o = o1 - lam * o2
        ms = jnp.mean(o * o, axis=-1, keepdims=True)
        o = o * lax.rsqrt(ms + EPS) * sw_ref[...] * (1.0 - LAMBDA_INIT)
        o_ref[0] = o.astype(o_ref.dtype)

    def fast_path(max_dist):
        qf = load_q()
        qpos = (qi * tq + lax.broadcasted_iota(jnp.int32, qf.shape, 0)).astype(F32) * sl2
        qpieces = [t.astype(F32) for t in _split3(qpos)]
        shift = shift_ref[pl.ds(pl.multiple_of(qi * tq, tq), tq), :]
        kr = lax.broadcasted_iota(jnp.int32, (tq, tq), 0)
        qc = lax.broadcasted_iota(jnp.int32, (tq, tq), 1)
        corr = jnp.maximum(qc - kr, 0).astype(F32) * (2.0 * sl2)
        outs = []
        for c in range(2):
            rel = lane - feat0(c)
            own = own_lanes(lane, c)
            base = jnp.where(own, qf, jnp.where(rel == N_POS, -shift, 0.0))
            posf = jnp.where((rel > N_POS) & (rel < N_FEAT), -1.0, 0.0)
            for n in range(N_POS):
                posf = jnp.where(rel == n, qpieces[n], posf)
            q_after = (base + posf).astype(BF16)
            q_before = (base - posf).astype(BF16)
            if max_dist is None:
                blocks = []
                for d in range(nb):
                    wrapped = qi + d >= nb
                    jb = jnp.where(wrapped, qi + d - nb, qi + d)
                    blocks.append((jb, q_after if d == 0 else jnp.where(wrapped, q_before, q_after), d == 0))
            else:
                q_dead = jnp.where(own, qf, jnp.where(rel == N_POS, -DEAD_SHIFT, 0.0)).astype(BF16)
                blocks = []
                for r in range(-max_dist, max_dist + 1):
                    jb = qi + r
                    side = q_before if r < 0 else q_after
                    inside = (jb >= 0) & (jb < nb)
                    blocks.append((jnp.clip(jb, 0, nb - 1), side if r == 0 else jnp.where(inside, side, q_dead),
                                   r == 0))
            acc = None
            for jb, qsel, diagonal in blocks:
                dk = pl.ds(pl.multiple_of(jb * tq, tq), tq)
                st = _dot_nt(kaug_ref[c, dk, :], qsel)
                if diagonal:
                    st = st - corr
                p = jnp.exp2(st).astype(BF16)
                if max_dist is None:
                    pt_ref[c, dk, :] = p
                else:
                    part = _dot(vtb_ref[jb], p)
                    acc = part if acc is None else acc + part
            if max_dist is None:
                acc = _dot(vt_ref[...], pt_ref[c])
            outs.append(acc[:DA_VDIM] * (1.0 / acc[DA_VDIM:DA_VDIM + 1]))
        finish(outs[0].T, outs[1].T)

    fast = worst <= MAX_SHIFT_LOG2
    dists = [_max_block_dist(hh, tq, nb) for hh in range(DA_HEADS)]
    dists = [d if 2 * d + 1 < nb else None for d in dists]
    for dist in set(dists):
        heads = [hh for hh in range(DA_HEADS) if dists[hh] == dist]
        in_heads = functools.reduce(lambda a, b: a | b, [h == hh for hh in heads])
        pl.when(fast & in_heads)(functools.partial(fast_path, dist))

    @pl.when(jnp.logical_not(fast))
    def _online():
        qf = load_q()
        zero = jnp.zeros_like(qf)
        qm = [jnp.where(own_lanes(lane, c), qf, zero).astype(BF16) for c in range(2)]
        qk0 =(qi * tq + lax.broadcasted_iota(jnp.int32, (tq, tk), 0)
               - lax.broadcasted_iota(jnp.int32, (tq, tk), 1)).astype(F32)

        def body(j, carry):
            kb = k_ref[0, pl.ds(pl.multiple_of(j * tk, tk), tk), :]
            vb = v_ref[0, pl.ds(pl.multiple_of(j * tk, tk), tk), :]
            bias = sl2 * jnp.abs(qk0 - lax.convert_element_type(j * tk, F32))
            out = []
            for c in range(2):
                m, l, acc = carry[c]
                s = _dot_nt(qm[c], kb) - bias
                m_new = jnp.maximum(m, jnp.max(s, axis=-1, keepdims=True))
                p = jnp.exp2(s - m_new)
                alpha = jnp.exp2(m - m_new)
                l_new = alpha * l + jnp.sum(p, axis=-1, keepdims=True)
                acc_new = alpha * acc + _dot(p.astype(BF16), vb)
                out.append((m_new, l_new, acc_new))
            return tuple(out)

        init = tuple((jnp.full((tq, 1), -jnp.inf, F32), jnp.zeros((tq, 1), F32),
                      jnp.zeros((tq, DA_VDIM), F32)) for _ in range(2))
        (m1, l1, a1), (m2, l2, a2) = lax.fori_loop(0, L // tk, body, init)
        finish(a1 * (1.0 / l1), a2 * (1.0 / l2))


def _diff_attention(q, k, v, slopes, lamv, subln_w, B, L):
    tq, tk = min(TQ, L), min(TK, L)
    return pl.pallas_call(
        functools.partial(_attn_kernel, tq=tq, tk=tk, L=L),
        grid=(DA_HEADS, B, L // tq),
        in_specs=[
            pl.BlockSpec(memory_space=pltpu.SMEM),
            pl.BlockSpec(lamv.shape, lambda h, b, i: (0, 0)),
            pl.BlockSpec((1, tq, LANES), lambda h, b, i: (b, i, h)),
            pl.BlockSpec((1, L, LANES), lambda h, b, i: (b, 0, h)),
            pl.BlockSpec((1, L, LANES), lambda h, b, i: (b, 0, h)),
            pl.BlockSpec((1, L, LANES), lambda h, b, i: (b, 0, h)),
            pl.BlockSpec(subln_w.shape, lambda h, b, i: (0, 0)),
        ],
        out_specs=pl.BlockSpec((1, tq, LANES), lambda h, b, i: (b, i, h)),
        out_shape=jax.ShapeDtypeStruct((B, L, 512), BF16),
        scratch_shapes=[pltpu.VMEM((2, L, LANES), BF16),
                        pltpu.VMEM((2, L, LANES), BF16),
                        pltpu.VMEM((VT_ROWS, L), BF16),
                        pltpu.VMEM((L // tq, VT_ROWS, tq), BF16),
                        pltpu.VMEM((2, L, tq), BF16),
                        pltpu.VMEM((L, LANES), F32),
                        pltpu.SMEM((1,), F32)],
        compiler_params=_cparams("arbitrary", "arbitrary", "arbitrary"),
        name="diff_attn",
    )(slopes, lamv, q, q, k, v, subln_w)


def _pair_tables():
    lo, hi = [], []
    for g in range(N_GROUPS):
        for a in range(EPG):
            for b in range(a + 1, EPG):
                lo.append(g * EPG + a)
                hi.append(g * EPG + b)
    return lo, hi


PAIRS_PER_GROUP = EPG * (EPG - 1) // 2
N_BUCKETS = N_GROUPS * PAIRS_PER_GROUP
GATE_LANE = 0
ROW_TAIL = LANES


def _router_kernel(x_ref, ys_ref, ya_ref, wo_ref, nw_ref, wr_ref, br_ref, tri_ref,
                   h_ref, u_ref, meta_ref, cnt_ref, carry_ref):
    i = pl.program_id(0)

    @pl.when(i == 0)
    def _():
        carry_ref[...] = jnp.zeros_like(carry_ref)

    sub = tri_ref.shape[0]
    for s in range(x_ref.shape[0] // sub):
        rows = pl.ds(s * sub, sub)
        _route_tile(x_ref.at[rows], ys_ref.at[rows], ya_ref.at[rows], wo_ref, nw_ref, wr_ref, br_ref, tri_ref,
                    h_ref.at[rows], u_ref.at[rows], meta_ref.at[rows], cnt_ref, carry_ref)


def _route_tile(x_ref, ys_ref, ya_ref, wo_ref, nw_ref, wr_ref, br_ref, tri_ref,
                h_ref, u_ref, meta_ref, cnt_ref, carry_ref):
    h = x_ref[...] + _dot(ys_ref[...], wo_ref[0:512, :]) + _dot(ya_ref[...], wo_ref[512:1024, :])
    h_ref[...] = h
    D = h.shape[1]
    ms = jnp.mean(h * h, axis=-1, keepdims=True)
    u = h * lax.rsqrt(ms + EPS) * nw_ref[...]
    u_ref[:, 0:D] = u
    logits = _dot(u.astype(BF16), wr_ref[...]) + br_ref[...]

    tm = logits.shape[0]
    li = lax.broadcasted_iota(jnp.int32, (tm, LANES), 1)
    lf = li.astype(F32)
    ninf = jnp.float32(-jnp.inf)
    big = jnp.float32(1e9)
    gl = jnp.where(li < N_GROUPS, logits, ninf)
    gmax = jnp.max(gl, axis=-1, keepdims=True)
    gidx = jnp.min(jnp.where(gl == gmax, lf, big), axis=-1, keepdims=True)
    pg = 1.0 / jnp.sum(jnp.exp(gl - gmax), axis=-1, keepdims=True)
    lane_grp = ((li - EXP_LANE0) >> 3).astype(F32)
    in_grp = (li >= EXP_LANE0) & (li < EXP_LANE0 + N_EXPERTS) & (lane_grp == gidx)
    sl = jnp.where(in_grp, logits, ninf)
    m1 = jnp.max(sl, axis=-1, keepdims=True)
    i1 = jnp.min(jnp.where(sl == m1, lf, big), axis=-1, keepdims=True)
    sl2 = jnp.where(lf == i1, ninf, sl)
    m2 = jnp.max(sl2, axis=-1, keepdims=True)
    i2 = jnp.min(jnp.where(sl2 == m2, lf, big), axis=-1, keepdims=True)
    t = jnp.exp(m2 - m1)
    w1 = 1.0 / (1.0 + t)
    g1 = pg * w1
    g2 = pg * (t * w1)
    a1 = i1 - EXP_LANE0 - gidx * EPG
    a2 = i2 - EXP_LANE0 - gidx * EPG
    lo = jnp.minimum(a1, a2)
    hi = jnp.maximum(a1, a2)
    bucket = gidx * PAIRS_PER_GROUP + (lo * (EPG - 1) - lo * (lo - 1.0) * 0.5) + (hi - lo - 1.0)
    first_is_lo = a1 < a2
    g_lo = jnp.where(first_is_lo, g1, g2)
    g_hi = jnp.where(first_is_lo, g2, g1)
    ohb = lf == bucket
    oh = jnp.where(ohb, 1.0, 0.0)
    prefix = _dot(tri_ref[...], oh.astype(BF16)) + carry_ref[0:1, :]
    rank = jnp.sum(jnp.where(ohb, prefix, 0.0), axis=-1, keepdims=True)
    new_carry = carry_ref[0:1, :] + jnp.sum(oh, axis=0, keepdims=True)
    carry_ref[...] = jnp.broadcast_to(new_carry, carry_ref.shape)
    cnt_ref[...] = jnp.broadcast_to(new_carry, cnt_ref.shape)

    meta_ref[...] = jnp.where(li == 0, bucket, jnp.where(li == 1, rank, 0.0))
    u_ref[:, D:] = jnp.where(li == GATE_LANE, g_lo, jnp.where(li == GATE_LANE + 1, g_hi, 0.0))


def _out_router(x2, y_ssd, y_da, w_out, norm_w, w_r, b_r):
    T, D = x2.shape
    tm = min(TM_ROUTE, T)
    r = jnp.arange(min(ROUTE_SUB, tm))
    tri = (r[:, None] > r[None, :]).astype(BF16)
    row = lambda n: pl.BlockSpec((tm, n), lambda i: (i, 0))
    full = lambda a: pl.BlockSpec(a.shape, lambda i: (0,) * a.ndim)
    return pl.pallas_call(
        _router_kernel,
        grid=(T // tm,),
        in_specs=[row(D), row(512), row(512), full(w_out), full(norm_w), full(w_r), full(b_r), full(tri)],
        out_specs=[row(D), row(D + ROW_TAIL), row(LANES), pl.BlockSpec((SUBLANES, LANES), lambda i: (0, 0))],
        out_shape=[
            jax.ShapeDtypeStruct((T, D), F32),
            jax.ShapeDtypeStruct((T, D + ROW_TAIL), F32),
            jax.ShapeDtypeStruct((T, LANES), F32),
            jax.ShapeDtypeStruct((SUBLANES, LANES), F32),
        ],
        scratch_shapes=[pltpu.VMEM((SUBLANES, LANES), F32)],
        compiler_params=_cparams("arbitrary"),
        name="out_router",
    )(x2, y_ssd, y_da, w_out, norm_w, w_r, b_r, tri)


def _dispatch_kernel(cnt_ref, idx_ref, u_ref, xs_hbm, slots_ref, ea_ref, eb_ref, nused_ref,
                     pstart_ref, zero_ref, sem, zsem, *, tm, n_blocks):
    i = pl.program_id(0)

    @pl.when(i == 0)
    def _():
        zero_ref[...] = jnp.zeros_like(zero_ref)
        pair_lo, pair_hi = _pair_tables()

        def zero_block(first_row):
            return pltpu.make_async_copy(
                zero_ref, xs_hbm.at[pl.ds(pl.multiple_of(first_row, MOE_ROWS), MOE_ROWS)], zsem)

        acc = jnp.int32(0)
        nfill = jnp.int32(0)
        for q in range(N_BUCKETS):
            padded = ((cnt_ref[q] + (MOE_ROWS - 1)) // MOE_ROWS) * MOE_ROWS
            pstart_ref[q] = acc
            end = acc + padded

            def fill(j, _, q=q):
                ea_ref[j] = jnp.int32(pair_lo[q])
                eb_ref[j] = jnp.int32(pair_hi[q])
                return 0

            lax.fori_loop(acc // MOE_ROWS, end // MOE_ROWS, fill, 0)

            @pl.when(padded > 0)
            def _(end=end):
                zero_block(end - MOE_ROWS).start()

            nfill = nfill + jnp.where(padded > 0, 1, 0)
            acc = end
        nused = acc // MOE_ROWS
        nused_ref[0] = nused

        def fill_tail(j, _):
            ea_ref[j] = jnp.int32(0)
            eb_ref[j] = jnp.int32(0)
            zero_block(j * MOE_ROWS).start()
            return 0

        lax.fori_loop(nused, n_blocks, fill_tail, 0)

        def wait_fill(_, carry):
            zero_block(0).wait()
            return carry

        lax.fori_loop(0, nfill + (n_blocks - nused), wait_fill, 0)

    def start(t, _):
        slot = pstart_ref[idx_ref[0, t]] + idx_ref[1, t]
        slots_ref[0, t] = slot
        pltpu.make_async_copy(u_ref.at[pl.ds(t, 1)], xs_hbm.at[pl.ds(slot, 1)], sem).start()
        return 0

    lax.fori_loop(0, tm, start, 0)
    pltpu.make_async_copy(u_ref, u_ref, sem).wait()


def _dispatch(counts, idx, u, n_blocks):
    T, W = u.shape
    tm = min(TM_DISP, T)
    n_slots = n_blocks * MOE_ROWS
    grid_spec = pltpu.PrefetchScalarGridSpec(
        num_scalar_prefetch=1,
        grid=(T // tm,),
        in_specs=[
            pl.BlockSpec((2, tm), lambda i, cnt: (0, i), memory_space=pltpu.SMEM),
            pl.BlockSpec((tm, W), lambda i, cnt: (i, 0)),
        ],
        out_specs=[
            pl.BlockSpec(memory_space=pl.ANY),
            pl.BlockSpec((1, tm), lambda i, cnt: (0, i), memory_space=pltpu.SMEM),
            pl.BlockSpec(memory_space=pltpu.SMEM),
            pl.BlockSpec(memory_space=pltpu.SMEM),
            pl.BlockSpec(memory_space=pltpu.SMEM),
        ],
        scratch_shapes=[pltpu.SMEM((LANES,), jnp.int32),
                        pltpu.VMEM((MOE_ROWS, W), F32),
                        pltpu.SemaphoreType.DMA(()),
                        pltpu.SemaphoreType.DMA(())],
    )
    return pl.pallas_call(
        functools.partial(_dispatch_kernel, tm=tm, n_blocks=n_blocks),
        grid_spec=grid_spec,
        out_shape=[
            jax.ShapeDtypeStruct((n_slots, W), F32),
            jax.ShapeDtypeStruct((1, T), jnp.int32),
            jax.ShapeDtypeStruct((n_blocks,), jnp.int32),
            jax.ShapeDtypeStruct((n_blocks,), jnp.int32),
            jax.ShapeDtypeStruct((1,), jnp.int32),
        ],
        compiler_params=_cparams("arbitrary"),
        name="dispatch",
    )(counts, idx, u)


def _moe_kernel(ea_ref, eb_ref, nused_ref, xs_ref, wga_ref, wua_ref, wda_ref, wgb_ref, wub_ref, wdb_ref,
                o_ref, wa_g, wa_u, wa_d, wb_g, wb_u, wb_d):
    j = pl.program_id(0)
    last = nused_ref[0] - 1
    cur = jnp.minimum(j, last)
    prev = jnp.minimum(jnp.maximum(j - 1, 0), last)
    D = o_ref.shape[1]

    @pl.when((j == 0) | (ea_ref[cur] != ea_ref[prev]))
    def _():
        wa_g[...] = wga_ref[0].astype(BF16)
        wa_u[...] = wua_ref[0].astype(BF16)
        wa_d[...] = wda_ref[0].astype(BF16)

    @pl.when((j == 0) | (eb_ref[cur] != eb_ref[prev]))
    def _():
        wb_g[...] = wgb_ref[0].astype(BF16)
        wb_u[...] = wub_ref[0].astype(BF16)
        wb_d[...] = wdb_ref[0].astype(BF16)

    @pl.when(j < nused_ref[0])
    def _():
        x = xs_ref[:, 0:D].astype(BF16)
        tail = xs_ref[:, D:]
        out = None
        for (wg, wu, wd), lane in (((wa_g, wa_u, wa_d), GATE_LANE), ((wb_g, wb_u, wb_d), GATE_LANE + 1)):
            hid = _silu(_dot(x, wg[...])) * _dot(x, wu[...])
            y = _dot(hid.astype(BF16), wd[...]) * tail[:, lane:lane + 1]
            out = y if out is None else out + y
        o_ref[...] = out

    @pl.when(j >= nused_ref[0])
    def _():
        o_ref[...] = jnp.zeros_like(o_ref)


def _moe(block_ea, block_eb, nused, xs, wg, wu, wd, n_blocks):
    W = xs.shape[1]
    D = W - ROW_TAIL
    used = lambda j, nu: jnp.minimum(j, nu[0] - 1)
    wspec_a = lambda a: pl.BlockSpec((1,) + a.shape[1:], lambda j, ea, eb, nu: (ea[used(j, nu)], 0, 0))
    wspec_b = lambda a: pl.BlockSpec((1,) + a.shape[1:], lambda j, ea, eb, nu: (eb[used(j, nu)], 0, 0))
    wscratch = [pltpu.VMEM(a.shape[1:], BF16) for a in (wg, wu, wd)]
    grid_spec = pltpu.PrefetchScalarGridSpec(
        num_scalar_prefetch=3,
        grid=(n_blocks,),
        in_specs=[
            pl.BlockSpec((MOE_ROWS, W), lambda j, ea, eb, nu: (used(j, nu), 0)),
            wspec_a(wg), wspec_a(wu), wspec_a(wd), wspec_b(wg), wspec_b(wu), wspec_b(wd),
        ],
        out_specs=pl.BlockSpec((MOE_ROWS, D), lambda j, ea, eb, nu: (j, 0)),
        scratch_shapes=wscratch + wscratch,
    )
    return pl.pallas_call(
        _moe_kernel,
        grid_spec=grid_spec,
        out_shape=jax.ShapeDtypeStruct((n_blocks * MOE_ROWS, D), F32),
        compiler_params=pltpu.CompilerParams(dimension_semantics=("arbitrary",),
                                             vmem_limit_bytes=MOE_VMEM_LIMIT),
        name="moe_mlp",
    )(block_ea, block_eb, nused, xs, wg, wu, wd, wg, wu, wd)


def _combine_kernel(slots_ref, next_slots_ref, h_ref, nw_ref, eo_hbm, o_ref, g_ref, sems, *, tm):
    i = pl.program_id(0)
    n = pl.num_programs(0)
    cur = i % 2

    def gather_tile(sref, buf):
        def start(t, _):
            pltpu.make_async_copy(eo_hbm.at[pl.ds(sref[0, t], 1)],
                                  g_ref.at[buf, pl.ds(t, 1)], sems.at[buf]).start()
            return 0

        lax.fori_loop(0, tm, start, 0)

    @pl.when(i == 0)
    def _():
        gather_tile(slots_ref, 0)

    @pl.when(i + 1 < n)
    def _():
        gather_tile(next_slots_ref, 1 - cur)

    pltpu.make_async_copy(g_ref.at[cur], g_ref.at[cur], sems.at[cur]).wait()

    y = h_ref[...] + g_ref[cur]
    ms = jnp.mean(y * y, axis=-1, keepdims=True)
    o_ref[...] = y * lax.rsqrt(ms + EPS) * nw_ref[...]


def _combine(slots, h, norm_w, eo):
    T, D = h.shape
    tm = min(TM_COMB, T)
    last = T // tm - 1
    return pl.pallas_call(
        functools.partial(_combine_kernel, tm=tm),
        grid=(T // tm,),
        in_specs=[
            pl.BlockSpec((1, tm), lambda i: (0, i), memory_space=pltpu.SMEM),
            pl.BlockSpec((1, tm), lambda i: (0, jnp.minimum(i + 1, last)), memory_space=pltpu.SMEM),
            pl.BlockSpec((tm, D), lambda i: (i, 0)),
            pl.BlockSpec(norm_w.shape, lambda i: (0, 0)),
            pl.BlockSpec(memory_space=pl.ANY),
        ],
        out_specs=pl.BlockSpec((tm, D), lambda i: (i, 0)),
        out_shape=jax.ShapeDtypeStruct((T, D), F32),
        scratch_shapes=[pltpu.VMEM((2, tm, D), F32),
                        pltpu.SemaphoreType.DMA((2,))],
        compiler_params=_cparams("arbitrary"),
        name="combine",
    )(slots, slots, h, norm_w, eo)


def _pad_lanes(v, offset=0):
    v = v.astype(F32).reshape(-1)
    return jnp.zeros((1, LANES), F32).at[0, offset:offset + v.shape[0]].set(v)


def kernel(x, norm_mix_w, w_in, conv_w, conv_b, dt_bias_fwd, dt_bias_bwd, a_log_fwd, a_log_bwd, ssd_d,
           ssd_norm_w, lambda_q1, lambda_k1, lambda_q2, lambda_k2, subln_w, w_out, norm_ffn_w,
           w_router_group, b_router_group, w_router_exp, b_router_exp, w_exp_gate, w_exp_up, w_exp_down,
           norm_final_w):
    B, L, D = x.shape
    T = B * L
    x2 = x.reshape(T, D)
    l = 0

    w = w_in[l]
    o_z, o_xbc, o_dt = 512, 512 + 1024, 512 + 1024 + 16
    o_q, o_k = o_dt + 512, o_dt + 1024
    w_main = jnp.concatenate(
        [w[:, :o_xbc], w[:, o_dt:o_q] * (DA_HEADDIM ** -0.5 * LOG2E), w[:, o_q:]], axis=1).astype(BF16)
    w_dt = jnp.zeros((D, LANES), F32).at[:, :16].set(w[:, o_xbc:o_dt]).astype(BF16)

    z, xbc, dt, q, k, v = _in_proj(x2, norm_mix_w[l].reshape(1, D), w_main, w_dt)

    dtb = _pad_lanes(jnp.concatenate([dt_bias_fwd[l], dt_bias_bwd[l]]))
    alog = _pad_lanes(jnp.concatenate([a_log_fwd[l], a_log_bwd[l]]))
    dskip_e = jnp.repeat(ssd_d[l].astype(F32), SSD_HEADDIM).reshape(1, SSD_WIDTH)
    y_ssd = _ssd(xbc.reshape(B, L, 1024), dt.reshape(B, L, LANES), z.reshape(B, L, 512),
                 conv_w[l].astype(F32), conv_b[l].reshape(1, -1).astype(F32), dtb, alog, dskip_e,
                 ssd_norm_w[l].reshape(1, -1).astype(F32), B, L)

    slopes = jnp.power(2.0, -8.0 * jnp.arange(1, DA_HEADS + 1, dtype=F32) / DA_HEADS)
    lamv = jnp.concatenate([_pad_lanes(lambda_q1[l]), _pad_lanes(lambda_k1[l]),
                            _pad_lanes(lambda_q2[l]), _pad_lanes(lambda_k2[l]),
                            jnp.zeros((4, LANES), F32)], axis=0)
    y_da = _diff_attention(q.reshape(B, L, 512), k.reshape(B, L, 512), v.reshape(B, L, 512),
                           slopes, lamv, subln_w[l].reshape(1, -1).astype(F32), B, L)

    w_r = (jnp.zeros((D, LANES), F32).at[:, :N_GROUPS].set(w_router_group[l])
           .at[:, EXP_LANE0:EXP_LANE0 + N_EXPERTS].set(w_router_exp[l])).astype(BF16)
    b_r = _pad_lanes(jnp.concatenate([b_router_group[l], b_router_exp[l]]))
    h, u, meta, cnt = _out_router(x2, y_ssd.reshape(T, 512), y_da.reshape(T, 512), w_out[l].astype(BF16),
                                  norm_ffn_w[l].reshape(1, D), w_r, b_r)

    n_blocks = T // MOE_ROWS + N_BUCKETS
    counts = cnt[0].astype(jnp.int32)
    idx = meta[:, :2].astype(jnp.int32).T
    xs, slots, block_ea, block_eb, nused = _dispatch(counts, idx, u, n_blocks)
    eo = _moe(block_ea, block_eb, nused, xs, w_exp_gate[l], w_exp_up[l], w_exp_down[l], n_blocks)
    out = _combine(slots, h, norm_final_w.reshape(1, D), eo)
    return out.reshape(B, L, D)
```

```python
import functools
import math

import jax
import jax.numpy as jnp
from jax import lax
from jax.experimental import pallas as pl
from jax.experimental.pallas import tpu as pltpu

F32 = jnp.float32
BF16 = jnp.bfloat16

EPS = 1e-6
LOG2E = math.log2(math.e)
LANES = 128
SUBLANES = 8
HALO = 16
VMEM_LIMIT = 48 * 1024 * 1024
MOE_VMEM_LIMIT = 56 * 1024 * 1024

SSD_WIDTH = 512
SSD_HEADDIM = 64
SSD_HEADS = 8
SSD_NGROUPS = 2
SSD_HPG = 4
SSD_DSTATE = 128
SSD_CONV = 5
SSD_CHUNK = 128
SSD_CONV_CH = 1024
DA_HEADDIM = 64
DA_VDIM = 128
DA_HEADS = 4
N_GROUPS = 4
EPG = 8
N_EXPERTS = 32
D_EXPERT = 512
LAMBDA_INIT = 0.8 - 0.6 * math.exp(-0.3 * 0)

TM_PROJ = 512
SSD_SEQS = 2
TQ = 512
TK = 512
TM_ROUTE = 1024
ROUTE_SUB = 512
TM_DISP = 1024
MOE_ROWS = 256
TM_COMB = 256
EXP_LANE0 = 4


def _cparams(*sem):
    return pltpu.CompilerParams(dimension_semantics=sem, vmem_limit_bytes=VMEM_LIMIT)


def _dot(a, b):
    return jnp.dot(a, b, preferred_element_type=F32)


def _dot_nt(a, b):
    return lax.dot_general(a, b, (((1,), (1,)), ((), ())), preferred_element_type=F32)


def _split3(v):
    hi = v.astype(BF16)
    r1 = v - hi.astype(F32)
    mid = r1.astype(BF16)
    lo = (r1 - mid.astype(F32)).astype(BF16)
    return hi, mid, lo


def _dot_exact(a01, v):
    hi, mid, lo = _split3(v)
    return _dot(a01, hi) + _dot(a01, mid) + _dot(a01, lo)


def _dot_exact_r(v, b01):
    hi, mid, lo = _split3(v)
    return _dot(hi, b01) + _dot(mid, b01) + _dot(lo, b01)


def _silu(x):
    return x * (1.0 / (1.0 + jnp.exp(-x)))


def _pack_bf16_pairs(x):
    n = x.shape[1] // 2
    bits = pltpu.bitcast(x.astype(BF16).astype(F32), jnp.uint32)
    return (bits[:, :n] >> 16) | bits[:, n:]


def _unpack_bf16_pairs(w):
    lo = pltpu.bitcast(w << 16, F32)
    hi = pltpu.bitcast(w & jnp.uint32(0xFFFF0000), F32)
    return jnp.concatenate([lo, hi], axis=1)


def _inproj_kernel(x_ref, nw_ref, w_ref, wdt_ref, z_ref, xbc_ref, dt_ref, q_ref, k_ref, v_ref):
    x = x_ref[...]
    ms = jnp.mean(x * x, axis=-1, keepdims=True)
    u = (x * lax.rsqrt(ms + EPS) * nw_ref[...]).astype(BF16)
    z_ref[...] = _dot(u, w_ref[:, 0:512]).astype(BF16)
    xbc_ref[...] = _dot(u, w_ref[:, 512:1536]).astype(BF16)
    q_ref[...] = _dot(u, w_ref[:, 1536:2048]).astype(BF16)
    k_ref[...] = _dot(u, w_ref[:, 2048:2560]).astype(BF16)
    v_ref[...] = _dot(u, w_ref[:, 2560:3072]).astype(BF16)
    dt_ref[...] = _dot(u, wdt_ref[...])


def _in_proj(x2, norm_w, w_main, w_dt):
    T, D = x2.shape
    tm = TM_PROJ
    row = lambda n: pl.BlockSpec((tm, n), lambda i: (i, 0))
    full = lambda a: pl.BlockSpec(a.shape, lambda i: (0,) * a.ndim)
    return pl.pallas_call(
        _inproj_kernel,
        grid=(T // tm,),
        in_specs=[row(D), full(norm_w), full(w_main), full(w_dt)],
        out_specs=[row(512), row(1024), row(LANES), row(512), row(512), row(512)],
        out_shape=[
            jax.ShapeDtypeStruct((T, 512), BF16),
            jax.ShapeDtypeStruct((T, 1024), BF16),
            jax.ShapeDtypeStruct((T, LANES), F32),
            jax.ShapeDtypeStruct((T, 512), BF16),
            jax.ShapeDtypeStruct((T, 512), BF16),
            jax.ShapeDtypeStruct((T, 512), BF16),
        ],
        compiler_params=_cparams("arbitrary"),
        name="in_proj",
    )(x2, norm_w, w_main, w_dt)


def _ssd_kernel(*refs, reverse):
    c = pl.program_id(1)
    nc = pl.num_programs(1)
    pos = (nc - 1 - c) if reverse else c
    ht_ref = refs[-1]

    @pl.when(c == 0)
    def _():
        ht_ref[...] = jnp.zeros_like(ht_ref)

    for s in range(ht_ref.shape[0]):
        _ssd_sequence(s, pos, nc, refs, reverse)


def _ssd_sequence(s, pos, nc, refs, reverse):
    if reverse:
        (xp_ref, xc_ref, xn_ref, dt_ref, cw_ref, cb_ref, dtb_ref, alog_ref, tri_ref, e_ref, sh_ref,
         z_ref, yf_ref, dsk_ref, nw_ref, out_ref, ht_ref) = refs
    else:
        (xp_ref, xc_ref, xn_ref, dt_ref, cw_ref, cb_ref, dtb_ref, alog_ref, tri_ref, e_ref, sh_ref,
         out_ref, ht_ref) = refs
    Q = SSD_CHUNK

    halo_zero = jnp.zeros((HALO, SSD_CONV_CH), BF16)
    ext = jnp.concatenate([jnp.where(pos > 0, xp_ref[s], halo_zero), xc_ref[s],
                           jnp.where(pos < nc - 1, xn_ref[s], halo_zero)], axis=0)
    shifted = _dot(sh_ref[...], ext)
    conv = cb_ref[...]
    for k in range(SSD_CONV):
        conv = conv + cw_ref[k:k + 1, :] * shifted[k * Q:(k + 1) * Q, :]
    act = _silu(conv)
    xs = act[:, :SSD_WIDTH]

    dtraw = dt_ref[s] + dtb_ref[...]
    dt = jnp.maximum(dtraw, 0.0) + jnp.log(1.0 + jnp.exp(-jnp.abs(dtraw)))
    a = -jnp.exp(alog_ref[...])
    adt = dt * a
    tri = tri_ref[...]
    cvec = _dot_exact(tri, adt)
    cvt = cvec.T
    e01 = e_ref[...]
    dt_e = _dot_exact_r(dt, e01)
    cv_e = _dot_exact_r(cvec, e01)
    edge = 0 if reverse else Q - 1
    tot_e = cv_e[edge:edge + 1, :]

    li = lax.broadcasted_iota(jnp.int32, (Q, Q), 0)
    si = lax.broadcasted_iota(jnp.int32, (Q, Q), 1)
    mask = (li <= si) if reverse else (li >= si)
    lane = lax.broadcasted_iota(jnp.int32, (Q, LANES), 1)
    head0 = SSD_HEADS if reverse else 0

    xdt = xs * dt_e
    ys = []
    for g in range(SSD_NGROUPS):
        bg = act[:, SSD_WIDTH + g * SSD_DSTATE:SSD_WIDTH + (g + 1) * SSD_DSTATE]
        cg = act[:, SSD_WIDTH + (SSD_NGROUPS + g) * SSD_DSTATE:SSD_WIDTH + (SSD_NGROUPS + g + 1) * SSD_DSTATE]
        bgb = bg.astype(BF16)
        cgb = cg.astype(BF16)
        gm = _dot_nt(cgb, bgb)
        gs = slice(g * 256, (g + 1) * 256)
        xdt_g = xdt[:, gs]
        xdtb = xdt_g.astype(BF16)
        ydiag = []
        for pr in range(2):
            xpair = xdtb[:, pr * LANES:(pr + 1) * LANES]
            yh = []
            for j in range(2):
                hl = head0 + g * SSD_HPG + pr * 2 + j
                seg = cvec[:, hl:hl + 1] - cvt[hl:hl + 1, :]
                decay = jnp.exp(jnp.where(mask, seg, -jnp.inf))
                yh.append(_dot((gm * decay).astype(BF16), xpair))
            ydiag.append(jnp.where(lane < SSD_HEADDIM, yh[0], yh[1]))
        ydiag = jnp.concatenate(ydiag, axis=1)
        ht = ht_ref[s, g]
        yoff = _dot(cgb, ht.astype(BF16)) * jnp.exp(cv_e[:, gs])
        ys.append(ydiag + yoff)
        wdec = jnp.exp(tot_e[:, gs] - cv_e[:, gs])
        xw = (xdt_g * wdec).astype(BF16)
        ht_ref[s, g] = jnp.exp(tot_e[:, gs]) * ht + _dot(bg.T.astype(BF16), xw)
    y = jnp.concatenate(ys, axis=1)

    if not reverse:
        out_ref[s] = y
    else:
        y = y + yf_ref[s] + dsk_ref[...] * xs
        gated = y * _silu(z_ref[s].astype(F32))
        ms = jnp.mean(gated * gated, axis=-1, keepdims=True)
        out_ref[s] = (gated * lax.rsqrt(ms + EPS) * nw_ref[...]).astype(out_ref.dtype)


def _ssd(xbc, dt, z, conv_w, conv_b, dtb, alog, dskip_e, norm_w, B, L):
    Q = SSD_CHUNK
    nc = L // Q
    ns = SSD_SEQS if B % SSD_SEQS == 0 else 1
    hb = Q // HALO
    nhb = L // HALO
    r = jnp.arange(Q)
    pad = (SSD_CONV - 1) // 2
    col = jnp.arange(Q + 2 * HALO)[None, :]
    shift = jnp.concatenate([(col == HALO + r[:, None] + k - pad) for k in range(SSD_CONV)], axis=0).astype(BF16)
    tril = (r[:, None] >= r[None, :]).astype(BF16)
    triu = (r[:, None] <= r[None, :]).astype(BF16)
    lane = jnp.arange(LANES)[:, None]
    ch = jnp.arange(SSD_WIDTH)[None, :] // SSD_HEADDIM
    e_f = (lane == ch).astype(BF16)
    e_b = (lane == ch + SSD_HEADS).astype(BF16)

    def call(reverse, extra_in, extra_specs, out_dtype):
        pos = (lambda c: nc - 1 - c) if reverse else (lambda c: c)
        full = lambda a: pl.BlockSpec(a.shape, lambda b, c: (0,) * a.ndim)
        chunk = lambda n: pl.BlockSpec((ns, Q, n), lambda b, c: (b, pos(c), 0))
        in_specs = [
            pl.BlockSpec((ns, HALO, 1024), lambda b, c: (b, jnp.maximum(pos(c) * hb - 1, 0), 0)),
            chunk(1024),
            pl.BlockSpec((ns, HALO, 1024), lambda b, c: (b, jnp.minimum((pos(c) + 1) * hb, nhb - 1), 0)),
            chunk(LANES),
        ]
        consts = [conv_w, conv_b, dtb, alog, triu if reverse else tril, e_b if reverse else e_f, shift]
        in_specs += [full(a) for a in consts]
        in_specs += [chunk(512) if s == "chunk" else full(a) for a, s in zip(extra_in, extra_specs)]
        return pl.pallas_call(
            functools.partial(_ssd_kernel, reverse=reverse),
            grid=(B // ns, nc),
            in_specs=in_specs,
            out_specs=chunk(512),
            out_shape=jax.ShapeDtypeStruct((B, L, 512), out_dtype),
            scratch_shapes=[pltpu.VMEM((ns, SSD_NGROUPS, SSD_DSTATE, 256), F32)],
            compiler_params=_cparams("arbitrary", "arbitrary"),
            name="ssd_bwd" if reverse else "ssd_fwd",
        )(xbc, xbc, xbc, dt, *consts, *extra_in)

    yf = call(False, [], [], F32)
    return call(True, [z, yf, dskip_e, norm_w], ["chunk", "chunk", "full", "full"], BF16)


N_POS = 3
N_FEAT = 2 * N_POS + 1
MAX_SHIFT_LOG2 = 50.0
VT_ROWS = DA_VDIM + 16
ZERO_PROB_LOG2 = 160.0
DEAD_SHIFT = 1.0e4


def _max_block_dist(head, tq, nb):
    slope_log2 = 2.0 ** (-8.0 * (head + 1) / DA_HEADS) * LOG2E
    d = 0
    while d < nb - 1 and slope_log2 * (d * tq + 1) < ZERO_PROB_LOG2:
        d += 1
    return d


def _attn_kernel(slope_ref, lamv_ref, q_ref, qall_ref, k_ref, v_ref, sw_ref, o_ref,
                 kaug_ref, kfeat_ref, vt_ref, vtb_ref, pt_ref, shift_ref, worst_ref, *, tq, tk, L):
    h = pl.program_id(0)
    b = pl.program_id(1)
    qi = pl.program_id(2)
    nb = L // tq
    sl2 = slope_ref[h] * LOG2E
    lv = lamv_ref[...]
    lam = (jnp.exp(jnp.sum(lv[0:1] * lv[1:2], axis=-1, keepdims=True))
           - jnp.exp(jnp.sum(lv[2:3] * lv[3:4], axis=-1, keepdims=True)) + LAMBDA_INIT)

    def feat0(c):
        return DA_HEADDIM if c == 0 else 0

    def own_lanes(lane, c):
        return (lane < DA_HEADDIM) if c == 0 else (lane >= DA_HEADDIM)

    ha = lax.broadcasted_iota(jnp.int32, (LANES, LANES), 0) < DA_HEADDIM
    hb = lax.broadcasted_iota(jnp.int32, (LANES, LANES), 1) < DA_HEADDIM
    other_half = jnp.where(ha != hb, 1.0, 0.0).astype(BF16)

    @pl.when((b == 0) & (qi == 0))
    def _tables():
        lane = lax.broadcasted_iota(jnp.int32, (L, LANES), 1)
        kpos = lax.broadcasted_iota(jnp.int32, (L, LANES), 0).astype(F32) * sl2
        pieces = [t.astype(F32) for t in _split3(kpos)]
        for c in range(2):
            rel = lane - feat0(c)
            feat = jnp.where((rel >= 0) & (rel <= N_POS), 1.0, 0.0)
            for n in range(N_POS):
                feat = jnp.where(rel == N_POS + 1 + n, pieces[n], feat)
            kfeat_ref[c] = feat.astype(BF16)

    @pl.when(qi == 0)
    def _prep():
        kraw = k_ref[0]
        lane = lax.broadcasted_iota(jnp.int32, kraw.shape, 1)
        for c in range(2):
            kaug_ref[c] = jnp.where(own_lanes(lane, c), kraw, kfeat_ref[c])
        kf = kraw.astype(F32)
        kn2 =_dot((kf * kf).astype(BF16), other_half)
        kmax2 = jnp.max(kn2, axis=0, keepdims=True)
        qa = qall_ref[0].astype(F32)
        prod2 = _dot((qa * qa).astype(BF16), other_half) * kmax2
        bound = prod2 * lax.rsqrt(jnp.maximum(prod2, 1e-30))
        shift_all = (bound * 1.02).astype(BF16).astype(F32)
        shift_ref[...] = shift_all
        worst_ref[0] = jnp.max(shift_all)
        ones_row = lax.broadcasted_iota(jnp.int32, (VT_ROWS - DA_VDIM, tq), 0) == 0
        ones_rows = jnp.where(ones_row, 1.0, 0.0).astype(BF16)
        for jb in range(nb):
            vt = v_ref[0, jb * tq:(jb + 1) * tq, :].astype(F32).T.astype(BF16)
            vt_ref[0:DA_VDIM, jb * tq:(jb + 1) * tq] = vt
            vt_ref[DA_VDIM:, jb * tq:(jb + 1) * tq] = ones_rows
            vtb_ref[jb, 0:DA_VDIM, :] = vt
            vtb_ref[jb, DA_VDIM:, :] = ones_rows

    worst = worst_ref[0]
    lane = lax.broadcasted_iota(jnp.int32, (tq, LANES), 1)

    def load_q():
        return q_ref[0].astype(F32)

    def finish(o1, o2):
        o = o1 - lam * o2
        ms = jnp.mean(o * o, axis=-1, keepdims=True)
        o = o * lax.rsqrt(ms + EPS) * sw_ref[...] * (1.0 - LAMBDA_INIT)
        o_ref[0] = o.astype(o_ref.dtype)

    def fast_path(max_dist):
        qf = load_q()
        qpos = (qi * tq + lax.broadcasted_iota(jnp.int32, qf.shape, 0)).astype(F32) * sl2
        qpieces = [t.astype(F32) for t in _split3(qpos)]
        shift = shift_ref[pl.ds(pl.multiple_of(qi * tq, tq), tq), :]
        kr = lax.broadcasted_iota(jnp.int32, (tq, tq), 0)
        qc = lax.broadcasted_iota(jnp.int32, (tq, tq), 1)
        corr = jnp.maximum(qc - kr, 0).astype(F32) * (2.0 * sl2)
        outs = []
        for c in range(2):
            rel = lane - feat0(c)
            own = own_lanes(lane, c)
            base = jnp.where(own, qf, jnp.where(rel == N_POS, -shift, 0.0))
            posf = jnp.where((rel > N_POS) & (rel < N_FEAT), -1.0, 0.0)
            for n in range(N_POS):
                posf = jnp.where(rel == n, qpieces[n], posf)
            q_after = (base + posf).astype(BF16)
            q_before = (base - posf).astype(BF16)
            if max_dist is None:
                blocks = []
                for d in range(nb):
                    wrapped = qi + d >= nb
                    jb = jnp.where(wrapped, qi + d - nb, qi + d)
                    blocks.append((jb, q_after if d == 0 else jnp.where(wrapped, q_before, q_after), d == 0))
            else:
                q_dead = jnp.where(own, qf, jnp.where(rel == N_POS, -DEAD_SHIFT, 0.0)).astype(BF16)
                blocks = []
                for r in range(-max_dist, max_dist + 1):
                    jb = qi + r
                    side = q_before if r < 0 else q_after
                    inside = (jb >= 0) & (jb < nb)
                    blocks.append((jnp.clip(jb, 0, nb - 1), side if r == 0 else jnp.where(inside, side, q_dead),
                                   r == 0))
            acc = None
            for jb, qsel, diagonal in blocks:
                dk = pl.ds(pl.multiple_of(jb * tq, tq), tq)
                st = _dot_nt(kaug_ref[c, dk, :], qsel)
                if diagonal:
                    st = st - corr
                p = jnp.exp2(st).astype(BF16)
                if max_dist is None:
                    pt_ref[c, dk, :] = p
                else:
                    part = _dot(vtb_ref[jb], p)
                    acc = part if acc is None else acc + part
            if max_dist is None:
                acc = _dot(vt_ref[...], pt_ref[c])
            outs.append(acc[:DA_VDIM] * (1.0 / acc[DA_VDIM:DA_VDIM + 1]))
        finish(outs[0].T, outs[1].T)

    fast = worst <= MAX_SHIFT_LOG2
    dists = [_max_block_dist(hh, tq, nb) for hh in range(DA_HEADS)]
    dists = [d if 2 * d + 1 < nb else None for d in dists]
    for dist in set(dists):
        heads = [hh for hh in range(DA_HEADS) if dists[hh] == dist]
        in_heads = functools.reduce(lambda a, b: a | b, [h == hh for hh in heads])
        pl.when(fast & in_heads)(functools.partial(fast_path, dist))

    @pl.when(jnp.logical_not(fast))
    def _online():
        qf = load_q()
        zero = jnp.zeros_like(qf)
        qm = [jnp.where(own_lanes(lane, c), qf, zero).astype(BF16) for c in range(2)]
        qk0 =(qi * tq + lax.broadcasted_iota(jnp.int32, (tq, tk), 0)
               - lax.broadcasted_iota(jnp.int32, (tq, tk), 1)).astype(F32)

        def body(j, carry):
            kb = k_ref[0, pl.ds(pl.multiple_of(j * tk, tk), tk), :]
            vb = v_ref[0, pl.ds(pl.multiple_of(j * tk, tk), tk), :]
            bias = sl2 * jnp.abs(qk0 - lax.convert_element_type(j * tk, F32))
            out = []
            for c in range(2):
                m, l, acc = carry[c]
                s = _dot_nt(qm[c], kb) - bias
                m_new = jnp.maximum(m, jnp.max(s, axis=-1, keepdims=True))
                p = jnp.exp2(s - m_new)
                alpha = jnp.exp2(m - m_new)
                l_new = alpha * l + jnp.sum(p, axis=-1, keepdims=True)
                acc_new = alpha * acc + _dot(p.astype(BF16), vb)
                out.append((m_new, l_new, acc_new))
            return tuple(out)

        init = tuple((jnp.full((tq, 1), -jnp.inf, F32), jnp.zeros((tq, 1), F32),
                      jnp.zeros((tq, DA_VDIM), F32)) for _ in range(2))
        (m1, l1, a1), (m2, l2, a2) = lax.fori_loop(0, L // tk, body, init)
        finish(a1 * (1.0 / l1), a2 * (1.0 / l2))


def _diff_attention(q, k, v, slopes, lamv, subln_w, B, L):
    tq, tk = min(TQ, L), min(TK, L)
    return pl.pallas_call(
        functools.partial(_attn_kernel, tq=tq, tk=tk, L=L),
        grid=(DA_HEADS, B, L // tq),
        in_specs=[
            pl.BlockSpec(memory_space=pltpu.SMEM),
            pl.BlockSpec(lamv.shape, lambda h, b, i: (0, 0)),
            pl.BlockSpec((1, tq, LANES), lambda h, b, i: (b, i, h)),
            pl.BlockSpec((1, L, LANES), lambda h, b, i: (b, 0, h)),
            pl.BlockSpec((1, L, LANES), lambda h, b, i: (b, 0, h)),
            pl.BlockSpec((1, L, LANES), lambda h, b, i: (b, 0, h)),
            pl.BlockSpec(subln_w.shape, lambda h, b, i: (0, 0)),
        ],
        out_specs=pl.BlockSpec((1, tq, LANES), lambda h, b, i: (b, i, h)),
        out_shape=jax.ShapeDtypeStruct((B, L, 512), BF16),
        scratch_shapes=[pltpu.VMEM((2, L, LANES), BF16),
                        pltpu.VMEM((2, L, LANES), BF16),
                        pltpu.VMEM((VT_ROWS, L), BF16),
                        pltpu.VMEM((L // tq, VT_ROWS, tq), BF16),
                        pltpu.VMEM((2, L, tq), BF16),
                        pltpu.VMEM((L, LANES), F32),
                        pltpu.SMEM((1,), F32)],
        compiler_params=_cparams("arbitrary", "arbitrary", "arbitrary"),
        name="diff_attn",
    )(slopes, lamv, q, q, k, v, subln_w)


def _pair_tables():
    lo, hi = [], []
    for g in range(N_GROUPS):
        for a in range(EPG):
            for b in range(a + 1, EPG):
                lo.append(g * EPG + a)
                hi.append(g * EPG + b)
    return lo, hi


PAIRS_PER_GROUP = EPG * (EPG - 1) // 2
N_BUCKETS = N_GROUPS * PAIRS_PER_GROUP
GATE_LANE = 0
ROW_TAIL = LANES


def _router_kernel(x_ref, ys_ref, ya_ref, wo_ref, nw_ref, wr_ref, br_ref, tri_ref,
                   h_ref, u_ref, meta_ref, cnt_ref, carry_ref):
    i = pl.program_id(0)

    @pl.when(i == 0)
    def _():
        carry_ref[...] = jnp.zeros_like(carry_ref)

    sub = tri_ref.shape[0]
    for s in range(x_ref.shape[0] // sub):
        rows = pl.ds(s * sub, sub)
        _route_tile(x_ref.at[rows], ys_ref.at[rows], ya_ref.at[rows], wo_ref, nw_ref, wr_ref, br_ref, tri_ref,
                    h_ref.at[rows], u_ref.at[rows], meta_ref.at[rows], cnt_ref, carry_ref)


def _route_tile(x_ref, ys_ref, ya_ref, wo_ref, nw_ref, wr_ref, br_ref, tri_ref,
                h_ref, u_ref, meta_ref, cnt_ref, carry_ref):
    h = x_ref[...] + _dot(ys_ref[...], wo_ref[0:512, :]) + _dot(ya_ref[...], wo_ref[512:1024, :])
    h_ref[...] = h
    D = h.shape[1]
    ms = jnp.mean(h * h, axis=-1, keepdims=True)
    u = h * lax.rsqrt(ms + EPS) * nw_ref[...]
    u_ref[:, 0:D // 2] = _pack_bf16_pairs(u)
    logits = _dot(u.astype(BF16), wr_ref[...]) + br_ref[...]

    tm = logits.shape[0]
    li = lax.broadcasted_iota(jnp.int32, (tm, LANES), 1)
    lf = li.astype(F32)
    ninf = jnp.float32(-jnp.inf)
    big = jnp.float32(1e9)
    gl = jnp.where(li < N_GROUPS, logits, ninf)
    gmax = jnp.max(gl, axis=-1, keepdims=True)
    gidx = jnp.min(jnp.where(gl == gmax, lf, big), axis=-1, keepdims=True)
    pg = 1.0 / jnp.sum(jnp.exp(gl - gmax), axis=-1, keepdims=True)
    lane_grp = ((li - EXP_LANE0) >> 3).astype(F32)
    in_grp = (li >= EXP_LANE0) & (li < EXP_LANE0 + N_EXPERTS) & (lane_grp == gidx)
    sl = jnp.where(in_grp, logits, ninf)
    m1 = jnp.max(sl, axis=-1, keepdims=True)
    i1 = jnp.min(jnp.where(sl == m1, lf, big), axis=-1, keepdims=True)
    sl2 = jnp.where(lf == i1, ninf, sl)
    m2 = jnp.max(sl2, axis=-1, keepdims=True)
    i2 = jnp.min(jnp.where(sl2 == m2, lf, big), axis=-1, keepdims=True)
    t = jnp.exp(m2 - m1)
    w1 = 1.0 / (1.0 + t)
    g1 = pg * w1
    g2 = pg * (t * w1)
    a1 = i1 - EXP_LANE0 - gidx * EPG
    a2 = i2 - EXP_LANE0 - gidx * EPG
    lo = jnp.minimum(a1, a2)
    hi = jnp.maximum(a1, a2)
    bucket = gidx * PAIRS_PER_GROUP + (lo * (EPG - 1) - lo * (lo - 1.0) * 0.5) + (hi - lo - 1.0)
    first_is_lo = a1 < a2
    g_lo = jnp.where(first_is_lo, g1, g2)
    g_hi = jnp.where(first_is_lo, g2, g1)
    ohb = lf == bucket
    oh = jnp.where(ohb, 1.0, 0.0)
    prefix = _dot(tri_ref[...], oh.astype(BF16)) + carry_ref[0:1, :]
    rank = jnp.sum(jnp.where(ohb, prefix, 0.0), axis=-1, keepdims=True)
    new_carry = carry_ref[0:1, :] + jnp.sum(oh, axis=0, keepdims=True)
    carry_ref[...] = jnp.broadcast_to(new_carry, carry_ref.shape)
    cnt_ref[...] = jnp.broadcast_to(new_carry, cnt_ref.shape)

    meta_ref[...] = jnp.where(li == 0, bucket, jnp.where(li == 1, rank, 0.0))
    gates = jnp.where(li == GATE_LANE, g_lo, jnp.where(li == GATE_LANE + 1, g_hi, 0.0))
    u_ref[:, D // 2:] = pltpu.bitcast(gates, jnp.uint32)


def _out_router(x2, y_ssd, y_da, w_out, norm_w, w_r, b_r):
    T, D = x2.shape
    tm = min(TM_ROUTE, T)
    r = jnp.arange(min(ROUTE_SUB, tm))
    tri = (r[:, None] > r[None, :]).astype(BF16)
    row = lambda n: pl.BlockSpec((tm, n), lambda i: (i, 0))
    full = lambda a: pl.BlockSpec(a.shape, lambda i: (0,) * a.ndim)
    return pl.pallas_call(
        _router_kernel,
        grid=(T // tm,),
        in_specs=[row(D), row(512), row(512), full(w_out), full(norm_w), full(w_r), full(b_r), full(tri)],
        out_specs=[row(D), row(D // 2 + ROW_TAIL), row(LANES), pl.BlockSpec((SUBLANES, LANES), lambda i: (0, 0))],
        out_shape=[
            jax.ShapeDtypeStruct((T, D), F32),
            jax.ShapeDtypeStruct((T, D // 2 + ROW_TAIL), jnp.uint32),
            jax.ShapeDtypeStruct((T, LANES), F32),
            jax.ShapeDtypeStruct((SUBLANES, LANES), F32),
        ],
        scratch_shapes=[pltpu.VMEM((SUBLANES, LANES), F32)],
        compiler_params=_cparams("arbitrary"),
        name="out_router",
    )(x2, y_ssd, y_da, w_out, norm_w, w_r, b_r, tri)


def _dispatch_kernel(cnt_ref, idx_ref, u_ref, xs_hbm, slots_ref, ea_ref, eb_ref, nused_ref,
                     pstart_ref, zero_ref, sem, zsem, *, tm, n_blocks):
    i = pl.program_id(0)

    @pl.when(i == 0)
    def _():
        zero_ref[...] = jnp.zeros_like(zero_ref)
        pair_lo, pair_hi = _pair_tables()

        def zero_block(first_row):
            return pltpu.make_async_copy(
                zero_ref, xs_hbm.at[pl.ds(pl.multiple_of(first_row, MOE_ROWS), MOE_ROWS)], zsem)

        acc = jnp.int32(0)
        nfill = jnp.int32(0)
        for q in range(N_BUCKETS):
            padded = ((cnt_ref[q] + (MOE_ROWS - 1)) // MOE_ROWS) * MOE_ROWS
            pstart_ref[q] = acc
            end = acc + padded

            def fill(j, _, q=q):
                ea_ref[j] = jnp.int32(pair_lo[q])
                eb_ref[j] = jnp.int32(pair_hi[q])
                return 0

            lax.fori_loop(acc // MOE_ROWS, end // MOE_ROWS, fill, 0)

            @pl.when(padded > 0)
            def _(end=end):
                zero_block(end - MOE_ROWS).start()

            nfill = nfill + jnp.where(padded > 0, 1, 0)
            acc = end
        nused = acc // MOE_ROWS
        nused_ref[0] = nused

        def fill_tail(j, _):
            ea_ref[j] = jnp.int32(0)
            eb_ref[j] = jnp.int32(0)
            zero_block(j * MOE_ROWS).start()
            return 0

        lax.fori_loop(nused, n_blocks, fill_tail, 0)

        def wait_fill(_, carry):
            zero_block(0).wait()
            return carry

        lax.fori_loop(0, nfill + (n_blocks - nused), wait_fill, 0)

    def start(t, _):
        slot = pstart_ref[idx_ref[0, t]] + idx_ref[1, t]
        slots_ref[0, t] = slot
        pltpu.make_async_copy(u_ref.at[pl.ds(t, 1)], xs_hbm.at[pl.ds(slot, 1)], sem).start()
        return 0

    lax.fori_loop(0, tm, start, 0)
    pltpu.make_async_copy(u_ref, u_ref, sem).wait()


def _dispatch(counts, idx, u, n_blocks):
    T, W = u.shape
    tm = min(TM_DISP, T)
    n_slots = n_blocks * MOE_ROWS
    grid_spec = pltpu.PrefetchScalarGridSpec(
        num_scalar_prefetch=1,
        grid=(T // tm,),
        in_specs=[
            pl.BlockSpec((2, tm), lambda i, cnt: (0, i), memory_space=pltpu.SMEM),
            pl.BlockSpec((tm, W), lambda i, cnt: (i, 0)),
        ],
        out_specs=[
            pl.BlockSpec(memory_space=pl.ANY),
            pl.BlockSpec((1, tm), lambda i, cnt: (0, i), memory_space=pltpu.SMEM),
            pl.BlockSpec(memory_space=pltpu.SMEM),
            pl.BlockSpec(memory_space=pltpu.SMEM),
            pl.BlockSpec(memory_space=pltpu.SMEM),
        ],
        scratch_shapes=[pltpu.SMEM((LANES,), jnp.int32),
                        pltpu.VMEM((MOE_ROWS, W), u.dtype),
                        pltpu.SemaphoreType.DMA(()),
                        pltpu.SemaphoreType.DMA(())],
    )
    return pl.pallas_call(
        functools.partial(_dispatch_kernel, tm=tm, n_blocks=n_blocks),
        grid_spec=grid_spec,
        out_shape=[
            jax.ShapeDtypeStruct((n_slots, W), u.dtype),
            jax.ShapeDtypeStruct((1, T), jnp.int32),
            jax.ShapeDtypeStruct((n_blocks,), jnp.int32),
            jax.ShapeDtypeStruct((n_blocks,), jnp.int32),
            jax.ShapeDtypeStruct((1,), jnp.int32),
        ],
        compiler_params=_cparams("arbitrary"),
        name="dispatch",
    )(counts, idx, u)


def _moe_kernel(ea_ref, eb_ref, nused_ref, xs_ref, wga_ref, wua_ref, wda_ref, wgb_ref, wub_ref, wdb_ref,
                o_ref, wa_g, wa_u, wa_d, wb_g, wb_u, wb_d):
    j = pl.program_id(0)
    last = nused_ref[0] - 1
    cur = jnp.minimum(j, last)
    prev = jnp.minimum(jnp.maximum(j - 1, 0), last)
    H = o_ref.shape[1]

    @pl.when((j == 0) | (ea_ref[cur] != ea_ref[prev]))
    def _():
        wa_g[...] = wga_ref[0].astype(BF16)
        wa_u[...] = wua_ref[0].astype(BF16)
        wa_d[...] = wda_ref[0].astype(BF16)

    @pl.when((j == 0) | (eb_ref[cur] != eb_ref[prev]))
    def _():
        wb_g[...] = wgb_ref[0].astype(BF16)
        wb_u[...] = wub_ref[0].astype(BF16)
        wb_d[...] = wdb_ref[0].astype(BF16)

    @pl.when(j < nused_ref[0])
    def _():
        x = _unpack_bf16_pairs(xs_ref[:, 0:H]).astype(BF16)
        tail = pltpu.bitcast(xs_ref[:, H:], F32)
        out = None
        for (wg, wu, wd), lane in (((wa_g, wa_u, wa_d), GATE_LANE), ((wb_g, wb_u, wb_d), GATE_LANE + 1)):
            hid = _silu(_dot(x, wg[...])) * _dot(x, wu[...])
            y = _dot(hid.astype(BF16), wd[...]) * tail[:, lane:lane + 1]
            out = y if out is None else out + y
        o_ref[...] = _pack_bf16_pairs(out)

    @pl.when(j >= nused_ref[0])
    def _():
        o_ref[...] = jnp.zeros_like(o_ref)


def _moe(block_ea, block_eb, nused, xs, wg, wu, wd, n_blocks):
    W = xs.shape[1]
    H = W - ROW_TAIL
    used = lambda j, nu: jnp.minimum(j, nu[0] - 1)
    wspec_a = lambda a: pl.BlockSpec((1,) + a.shape[1:], lambda j, ea, eb, nu: (ea[used(j, nu)], 0, 0))
    wspec_b = lambda a: pl.BlockSpec((1,) + a.shape[1:], lambda j, ea, eb, nu: (eb[used(j, nu)], 0, 0))
    wscratch = [pltpu.VMEM(a.shape[1:], BF16) for a in (wg, wu, wd)]
    grid_spec = pltpu.PrefetchScalarGridSpec(
        num_scalar_prefetch=3,
        grid=(n_blocks,),
        in_specs=[
            pl.BlockSpec((MOE_ROWS, W), lambda j, ea, eb, nu: (used(j, nu), 0)),
            wspec_a(wg), wspec_a(wu), wspec_a(wd), wspec_b(wg), wspec_b(wu), wspec_b(wd),
        ],
        out_specs=pl.BlockSpec((MOE_ROWS, H), lambda j, ea, eb, nu: (j, 0)),
        scratch_shapes=wscratch + wscratch,
    )
    return pl.pallas_call(
        _moe_kernel,
        grid_spec=grid_spec,
        out_shape=jax.ShapeDtypeStruct((n_blocks * MOE_ROWS, H), jnp.uint32),
        compiler_params=pltpu.CompilerParams(dimension_semantics=("arbitrary",),
                                             vmem_limit_bytes=MOE_VMEM_LIMIT),
        name="moe_mlp",
    )(block_ea, block_eb, nused, xs, wg, wu, wd, wg, wu, wd)


def _combine_kernel(slots_ref, next_slots_ref, h_ref, nw_ref, eo_hbm, o_ref, g_ref, sems, *, tm):
    i = pl.program_id(0)
    n = pl.num_programs(0)
    cur = i % 2

    def gather_tile(sref, buf):
        def start(t, _):
            pltpu.make_async_copy(eo_hbm.at[pl.ds(sref[0, t], 1)],
                                  g_ref.at[buf, pl.ds(t, 1)], sems.at[buf]).start()
            return 0

        lax.fori_loop(0, tm, start, 0)

    @pl.when(i == 0)
    def _():
        gather_tile(slots_ref, 0)

    @pl.when(i + 1 < n)
    def _():
        gather_tile(next_slots_ref, 1 - cur)

    pltpu.make_async_copy(g_ref.at[cur], g_ref.at[cur], sems.at[cur]).wait()

    y = h_ref[...] + _unpack_bf16_pairs(g_ref[cur])
    ms = jnp.mean(y * y, axis=-1, keepdims=True)
    o_ref[...] = y * lax.rsqrt(ms + EPS) * nw_ref[...]


def _combine(slots, h, norm_w, eo):
    T, D = h.shape
    tm = min(TM_COMB, T)
    last = T // tm - 1
    return pl.pallas_call(
        functools.partial(_combine_kernel, tm=tm),
        grid=(T // tm,),
        in_specs=[
            pl.BlockSpec((1, tm), lambda i: (0, i), memory_space=pltpu.SMEM),
            pl.BlockSpec((1, tm), lambda i: (0, jnp.minimum(i + 1, last)), memory_space=pltpu.SMEM),
            pl.BlockSpec((tm, D), lambda i: (i, 0)),
            pl.BlockSpec(norm_w.shape, lambda i: (0, 0)),
            pl.BlockSpec(memory_space=pl.ANY),
        ],
        out_specs=pl.BlockSpec((tm, D), lambda i: (i, 0)),
        out_shape=jax.ShapeDtypeStruct((T, D), F32),
        scratch_shapes=[pltpu.VMEM((2, tm, D // 2), jnp.uint32),
                        pltpu.SemaphoreType.DMA((2,))],
        compiler_params=_cparams("arbitrary"),
        name="combine",
    )(slots, slots, h, norm_w, eo)


def _pad_lanes(v, offset=0):
    v = v.astype(F32).reshape(-1)
    return jnp.zeros((1, LANES), F32).at[0, offset:offset + v.shape[0]].set(v)


def kernel(x, norm_mix_w, w_in, conv_w, conv_b, dt_bias_fwd, dt_bias_bwd, a_log_fwd, a_log_bwd, ssd_d,
           ssd_norm_w, lambda_q1, lambda_k1, lambda_q2, lambda_k2, subln_w, w_out, norm_ffn_w,
           w_router_group, b_router_group, w_router_exp, b_router_exp, w_exp_gate, w_exp_up, w_exp_down,
           norm_final_w):
    B, L, D = x.shape
    T = B * L
    x2 = x.reshape(T, D)
    l = 0

    w = w_in[l]
    o_z, o_xbc, o_dt = 512, 512 + 1024, 512 + 1024 + 16
    o_q, o_k = o_dt + 512, o_dt + 1024
    w_main = jnp.concatenate(
        [w[:, :o_xbc], w[:, o_dt:o_q] * (DA_HEADDIM ** -0.5 * LOG2E), w[:, o_q:]], axis=1).astype(BF16)
    w_dt = jnp.zeros((D, LANES), F32).at[:, :16].set(w[:, o_xbc:o_dt]).astype(BF16)

    z, xbc, dt, q, k, v = _in_proj(x2, norm_mix_w[l].reshape(1, D), w_main, w_dt)

    dtb = _pad_lanes(jnp.concatenate([dt_bias_fwd[l], dt_bias_bwd[l]]))
    alog = _pad_lanes(jnp.concatenate([a_log_fwd[l], a_log_bwd[l]]))
    dskip_e = jnp.repeat(ssd_d[l].astype(F32), SSD_HEADDIM).reshape(1, SSD_WIDTH)
    y_ssd = _ssd(xbc.reshape(B, L, 1024), dt.reshape(B, L, LANES), z.reshape(B, L, 512),
                 conv_w[l].astype(F32), conv_b[l].reshape(1, -1).astype(F32), dtb, alog, dskip_e,
                 ssd_norm_w[l].reshape(1, -1).astype(F32), B, L)

    slopes = jnp.power(2.0, -8.0 * jnp.arange(1, DA_HEADS + 1, dtype=F32) / DA_HEADS)
    lamv = jnp.concatenate([_pad_lanes(lambda_q1[l]), _pad_lanes(lambda_k1[l]),
                            _pad_lanes(lambda_q2[l]), _pad_lanes(lambda_k2[l]),
                            jnp.zeros((4, LANES), F32)], axis=0)
    y_da = _diff_attention(q.reshape(B, L, 512), k.reshape(B, L, 512), v.reshape(B, L, 512),
                           slopes, lamv, subln_w[l].reshape(1, -1).astype(F32), B, L)

    w_r = (jnp.zeros((D, LANES), F32).at[:, :N_GROUPS].set(w_router_group[l])
           .at[:, EXP_LANE0:EXP_LANE0 + N_EXPERTS].set(w_router_exp[l])).astype(BF16)
    b_r = _pad_lanes(jnp.concatenate([b_router_group[l], b_router_exp[l]]))
    h, u, meta, cnt = _out_router(x2, y_ssd.reshape(T, 512), y_da.reshape(T, 512), w_out[l].astype(BF16),
                                  norm_ffn_w[l].reshape(1, D), w_r, b_r)

    n_blocks = T // MOE_ROWS + N_BUCKETS
    counts = cnt[0].astype(jnp.int32)
    idx = meta[:, :2].astype(jnp.int32).T
    xs, slots, block_ea, block_eb, nused = _dispatch(counts, idx, u, n_blocks)
    eo = _moe(block_ea, block_eb, nused, xs, w_exp_gate[l], w_exp_up[l], w_exp_down[l], n_blocks)
    out = _combine(slots, h, norm_final_w.reshape(1, D), eo)
    return out.reshape(B, L, D)
```

```python
import functools
import math

import jax
import jax.numpy as jnp
from jax import lax
from jax.experimental import pallas as pl
from jax.experimental.pallas import tpu as pltpu

F32 = jnp.float32
BF16 = jnp.bfloat16

EPS = 1e-6
LOG2E = math.log2(math.e)
LANES = 128
SUBLANES = 8
HALO = 16
VMEM_LIMIT = 48 * 1024 * 1024
MOE_VMEM_LIMIT = 56 * 1024 * 1024

SSD_WIDTH = 512
SSD_HEADDIM = 64
SSD_HEADS = 8
SSD_NGROUPS = 2
SSD_HPG = 4
SSD_DSTATE = 128
SSD_CONV = 5
SSD_CHUNK = 128
SSD_CONV_CH = 1024
DA_HEADDIM = 64
DA_VDIM = 128
DA_HEADS = 4
N_GROUPS = 4
EPG = 8
N_EXPERTS = 32
D_EXPERT = 512
LAMBDA_INIT = 0.8 - 0.6 * math.exp(-0.3 * 0)

TM_PROJ = 512
SSD_SEQS = 2
TQ = 512
TK = 512
TM_ROUTE = 1024
ROUTE_SUB = 512
TM_DISP = 1024
MOE_ROWS = 256
TM_COMB = 512
DMA_UNROLL = 8
EXP_LANE0 = 4


def _cparams(*sem):
    return pltpu.CompilerParams(dimension_semantics=sem, vmem_limit_bytes=VMEM_LIMIT)


def _dot(a, b):
    return jnp.dot(a, b, preferred_element_type=F32)


def _dot_nt(a, b):
    return lax.dot_general(a, b, (((1,), (1,)), ((), ())), preferred_element_type=F32)


def _split3(v):
    hi = v.astype(BF16)
    r1 = v - hi.astype(F32)
    mid = r1.astype(BF16)
    lo = (r1 - mid.astype(F32)).astype(BF16)
    return hi, mid, lo


def _dot_exact(a01, v):
    hi, mid, lo = _split3(v)
    return _dot(a01, hi) + _dot(a01, mid) + _dot(a01, lo)


def _dot_exact_r(v, b01):
    hi, mid, lo = _split3(v)
    return _dot(hi, b01) + _dot(mid, b01) + _dot(lo, b01)


def _silu(x):
    return x * (1.0 / (1.0 + jnp.exp(-x)))


def _pack_bf16_pairs(x):
    n = x.shape[1] // 2
    bits = pltpu.bitcast(x.astype(BF16).astype(F32), jnp.uint32)
    return (bits[:, :n] >> 16) | bits[:, n:]


def _unpack_bf16_pairs(w):
    lo = pltpu.bitcast(w << 16, F32)
    hi = pltpu.bitcast(w & jnp.uint32(0xFFFF0000), F32)
    return jnp.concatenate([lo, hi], axis=1)


def _inproj_kernel(x_ref, nw_ref, w_ref, wdt_ref, z_ref, xbc_ref, dt_ref, q_ref, k_ref, v_ref):
    x = x_ref[...]
    ms = jnp.mean(x * x, axis=-1, keepdims=True)
    u = (x * lax.rsqrt(ms + EPS) * nw_ref[...]).astype(BF16)
    z_ref[...] = _dot(u, w_ref[:, 0:512]).astype(BF16)
    xbc_ref[...] = _dot(u, w_ref[:, 512:1536]).astype(BF16)
    q_ref[...] = _dot(u, w_ref[:, 1536:2048]).astype(BF16)
    k_ref[...] = _dot(u, w_ref[:, 2048:2560]).astype(BF16)
    v_ref[...] = _dot(u, w_ref[:, 2560:3072]).astype(BF16)
    dt_ref[...] = _dot(u, wdt_ref[...])


def _in_proj(x2, norm_w, w_main, w_dt):
    T, D = x2.shape
    tm = TM_PROJ
    row = lambda n: pl.BlockSpec((tm, n), lambda i: (i, 0))
    full = lambda a: pl.BlockSpec(a.shape, lambda i: (0,) * a.ndim)
    return pl.pallas_call(
        _inproj_kernel,
        grid=(T // tm,),
        in_specs=[row(D), full(norm_w), full(w_main), full(w_dt)],
        out_specs=[row(512), row(1024), row(LANES), row(512), row(512), row(512)],
        out_shape=[
            jax.ShapeDtypeStruct((T, 512), BF16),
            jax.ShapeDtypeStruct((T, 1024), BF16),
            jax.ShapeDtypeStruct((T, LANES), F32),
            jax.ShapeDtypeStruct((T, 512), BF16),
            jax.ShapeDtypeStruct((T, 512), BF16),
            jax.ShapeDtypeStruct((T, 512), BF16),
        ],
        compiler_params=_cparams("arbitrary"),
        name="in_proj",
    )(x2, norm_w, w_main, w_dt)


def _ssd_kernel(*refs, reverse):
    c = pl.program_id(1)
    nc = pl.num_programs(1)
    pos = (nc - 1 - c) if reverse else c
    ht_ref = refs[-1]

    @pl.when(c == 0)
    def _():
        ht_ref[...] = jnp.zeros_like(ht_ref)

    for s in range(ht_ref.shape[0]):
        _ssd_sequence(s, pos, nc, refs, reverse)


def _ssd_sequence(s, pos, nc, refs, reverse):
    if reverse:
        (xp_ref, xc_ref, xn_ref, dt_ref, cw_ref, cb_ref, dtb_ref, alog_ref, tri_ref, e_ref, sh_ref,
         z_ref, yf_ref, dsk_ref, nw_ref, out_ref, ht_ref) = refs
    else:
        (xp_ref, xc_ref, xn_ref, dt_ref, cw_ref, cb_ref, dtb_ref, alog_ref, tri_ref, e_ref, sh_ref,
         out_ref, ht_ref) = refs
    Q = SSD_CHUNK

    halo_zero = jnp.zeros((HALO, SSD_CONV_CH), BF16)
    ext = jnp.concatenate([jnp.where(pos > 0, xp_ref[s], halo_zero), xc_ref[s],
                           jnp.where(pos < nc - 1, xn_ref[s], halo_zero)], axis=0)
    shifted = _dot(sh_ref[...], ext)
    pad = (SSD_CONV - 1) // 2
    conv = cb_ref[...] + cw_ref[pad:pad + 1, :] * xc_ref[s].astype(F32)
    for n, k in enumerate(k for k in range(SSD_CONV) if k != pad):
        conv = conv + cw_ref[k:k + 1, :] * shifted[n * Q:(n + 1) * Q, :]
    act = _silu(conv)
    xs = act[:, :SSD_WIDTH]

    dtraw = dt_ref[s] + dtb_ref[...]
    dt = jnp.maximum(dtraw, 0.0) + jnp.log(1.0 + jnp.exp(-jnp.abs(dtraw)))
    a = -jnp.exp(alog_ref[...])
    adt = dt * a
    tri = tri_ref[...]
    cvec = _dot_exact(tri, adt)
    cvt = cvec.T
    e01 = e_ref[...]
    dt_e = _dot_exact_r(dt, e01)
    cv_e = _dot_exact_r(cvec, e01)
    edge = 0 if reverse else Q - 1
    tot_e = cv_e[edge:edge + 1, :]

    li = lax.broadcasted_iota(jnp.int32, (Q, Q), 0)
    si = lax.broadcasted_iota(jnp.int32, (Q, Q), 1)
    mask = (li <= si) if reverse else (li >= si)
    lane = lax.broadcasted_iota(jnp.int32, (Q, LANES), 1)
    head0 = SSD_HEADS if reverse else 0

    xdt = xs * dt_e
    ys = []
    for g in range(SSD_NGROUPS):
        bg = act[:, SSD_WIDTH + g * SSD_DSTATE:SSD_WIDTH + (g + 1) * SSD_DSTATE]
        cg = act[:, SSD_WIDTH + (SSD_NGROUPS + g) * SSD_DSTATE:SSD_WIDTH + (SSD_NGROUPS + g + 1) * SSD_DSTATE]
        bgb = bg.astype(BF16)
        cgb = cg.astype(BF16)
        gm = _dot_nt(cgb, bgb)
        gs = slice(g * 256, (g + 1) * 256)
        xdt_g = xdt[:, gs]
        xdtb = xdt_g.astype(BF16)
        ydiag = []
        for pr in range(2):
            xpair = xdtb[:, pr * LANES:(pr + 1) * LANES]
            yh = []
            for j in range(2):
                hl = head0 + g * SSD_HPG + pr * 2 + j
                seg = cvec[:, hl:hl + 1] - cvt[hl:hl + 1, :]
                decay = jnp.exp(jnp.where(mask, seg, -jnp.inf))
                yh.append(_dot((gm * decay).astype(BF16), xpair))
            ydiag.append(jnp.where(lane < SSD_HEADDIM, yh[0], yh[1]))
        ydiag = jnp.concatenate(ydiag, axis=1)
        ht = ht_ref[s, g]
        yoff = _dot(cgb, ht.astype(BF16)) * jnp.exp(cv_e[:, gs])
        ys.append(ydiag + yoff)
        wdec = jnp.exp(tot_e[:, gs] - cv_e[:, gs])
        xw = (xdt_g * wdec).astype(BF16)
        ht_ref[s, g] = jnp.exp(tot_e[:, gs]) * ht + _dot(bg.T.astype(BF16), xw)
    y = jnp.concatenate(ys, axis=1)

    if not reverse:
        out_ref[s] = y
    else:
        y = y + yf_ref[s] + dsk_ref[...] * xs
        gated = y * _silu(z_ref[s].astype(F32))
        ms = jnp.mean(gated * gated, axis=-1, keepdims=True)
        out_ref[s] = (gated * lax.rsqrt(ms + EPS) * nw_ref[...]).astype(out_ref.dtype)


def _ssd(xbc, dt, z, conv_w, conv_b, dtb, alog, dskip_e, norm_w, B, L):
    Q = SSD_CHUNK
    nc = L // Q
    ns = SSD_SEQS if B % SSD_SEQS == 0 else 1
    hb = Q // HALO
    nhb = L // HALO
    r = jnp.arange(Q)
    pad = (SSD_CONV - 1) // 2
    col = jnp.arange(Q + 2 * HALO)[None, :]
    shift = jnp.concatenate([(col == HALO + r[:, None] + k - pad) for k in range(SSD_CONV) if k != pad],
                            axis=0).astype(BF16)
    tril = (r[:, None] >= r[None, :]).astype(BF16)
    triu = (r[:, None] <= r[None, :]).astype(BF16)
    lane = jnp.arange(LANES)[:, None]
    ch = jnp.arange(SSD_WIDTH)[None, :] // SSD_HEADDIM
    e_f = (lane == ch).astype(BF16)
    e_b = (lane == ch + SSD_HEADS).astype(BF16)

    def call(reverse, extra_in, extra_specs, out_dtype):
        pos = (lambda c: nc - 1 - c) if reverse else (lambda c: c)
        full = lambda a: pl.BlockSpec(a.shape, lambda b, c: (0,) * a.ndim)
        chunk = lambda n: pl.BlockSpec((ns, Q, n), lambda b, c: (b, pos(c), 0))
        in_specs = [
            pl.BlockSpec((ns, HALO, 1024), lambda b, c: (b, jnp.maximum(pos(c) * hb - 1, 0), 0)),
            chunk(1024),
            pl.BlockSpec((ns, HALO, 1024), lambda b, c: (b, jnp.minimum((pos(c) + 1) * hb, nhb - 1), 0)),
            chunk(LANES),
        ]
        consts = [conv_w, conv_b, dtb, alog, triu if reverse else tril, e_b if reverse else e_f, shift]
        in_specs += [full(a) for a in consts]
        in_specs += [chunk(512) if s == "chunk" else full(a) for a, s in zip(extra_in, extra_specs)]
        return pl.pallas_call(
            functools.partial(_ssd_kernel, reverse=reverse),
            grid=(B // ns, nc),
            in_specs=in_specs,
            out_specs=chunk(512),
            out_shape=jax.ShapeDtypeStruct((B, L, 512), out_dtype),
            scratch_shapes=[pltpu.VMEM((ns, SSD_NGROUPS, SSD_DSTATE, 256), F32)],
            compiler_params=_cparams("arbitrary", "arbitrary"),
            name="ssd_bwd" if reverse else "ssd_fwd",
        )(xbc, xbc, xbc, dt, *consts, *extra_in)

    yf = call(False, [], [], F32)
    return call(True, [z, yf, dskip_e, norm_w], ["chunk", "chunk", "full", "full"], BF16)


N_POS = 3
N_FEAT = 2 * N_POS + 1
MAX_SHIFT_LOG2 = 50.0
VT_ROWS = DA_VDIM + 16
ZERO_PROB_LOG2 = 160.0
DEAD_SHIFT = 1.0e4


def _max_block_dist(head, tq, nb):
    slope_log2 = 2.0 ** (-8.0 * (head + 1) / DA_HEADS) * LOG2E
    d = 0
    while d < nb - 1 and slope_log2 * (d * tq + 1) < ZERO_PROB_LOG2:
        d += 1
    return d


def _attn_kernel(slope_ref, lamv_ref, q_ref, qall_ref, k_ref, v_ref, sw_ref, o_ref,
                 kaug_ref, kfeat_ref, vt_ref, vtb_ref, pt_ref, shift_ref, worst_ref, *, tq, tk, L):
    h = pl.program_id(0)
    b = pl.program_id(1)
    qi = pl.program_id(2)
    nb = L // tq
    sl2 = slope_ref[h] * LOG2E
    lv = lamv_ref[...]
    lam = (jnp.exp(jnp.sum(lv[0:1] * lv[1:2], axis=-1, keepdims=True))
           - jnp.exp(jnp.sum(lv[2:3] * lv[3:4], axis=-1, keepdims=True)) + LAMBDA_INIT)

    def feat0(c):
        return DA_HEADDIM if c == 0 else 0

    def own_lanes(lane, c):
        return (lane < DA_HEADDIM) if c == 0 else (lane >= DA_HEADDIM)

    ha = lax.broadcasted_iota(jnp.int32, (LANES, LANES), 0) < DA_HEADDIM
    hb = lax.broadcasted_iota(jnp.int32, (LANES, LANES), 1) < DA_HEADDIM
    other_half = jnp.where(ha != hb, 1.0, 0.0).astype(BF16)

    @pl.when((b == 0) & (qi == 0))
    def _tables():
        lane = lax.broadcasted_iota(jnp.int32, (L, LANES), 1)
        kpos = lax.broadcasted_iota(jnp.int32, (L, LANES), 0).astype(F32) * sl2
        pieces = [t.astype(F32) for t in _split3(kpos)]
        for c in range(2):
            rel = lane - feat0(c)
            feat = jnp.where((rel >= 0) & (rel <= N_POS), 1.0, 0.0)
            for n in range(N_POS):
                feat = jnp.where(rel == N_POS + 1 + n, pieces[n], feat)
            kfeat_ref[c] = feat.astype(BF16)

    @pl.when(qi == 0)
    def _prep():
        kraw = k_ref[0]
        lane = lax.broadcasted_iota(jnp.int32, kraw.shape, 1)
        for c in range(2):
            kaug_ref[c] = jnp.where(own_lanes(lane, c), kraw, kfeat_ref[c])
        kf = kraw.astype(F32)
        kn2 =_dot((kf * kf).astype(BF16), other_half)
        kmax2 = jnp.max(kn2, axis=0, keepdims=True)
        qa = qall_ref[0].astype(F32)
        prod2 = _dot((qa * qa).astype(BF16), other_half) * kmax2
        bound = prod2 * lax.rsqrt(jnp.maximum(prod2, 1e-30))
        shift_all = (bound * 1.02).astype(BF16).astype(F32)
        shift_ref[...] = shift_all
        worst_ref[0] = jnp.max(shift_all)
        ones_row = lax.broadcasted_iota(jnp.int32, (VT_ROWS - DA_VDIM, tq), 0) == 0
        ones_rows = jnp.where(ones_row, 1.0, 0.0).astype(BF16)
        for jb in range(nb):
            vt = v_ref[0, jb * tq:(jb + 1) * tq, :].astype(F32).T.astype(BF16)
            vt_ref[0:DA_VDIM, jb * tq:(jb + 1) * tq] = vt
            vt_ref[DA_VDIM:, jb * tq:(jb + 1) * tq] = ones_rows
            vtb_ref[jb, 0:DA_VDIM, :] = vt
            vtb_ref[jb, DA_VDIM:, :] = ones_rows

    worst = worst_ref[0]
    lane = lax.broadcasted_iota(jnp.int32, (tq, LANES), 1)

    def load_q():
        return q_ref[0].astype(F32)

    def finish(o1, o2):
        o = o1 - lam * o2
        ms = jnp.mean(o * o, axis=-1, keepdims=True)
        o = o * lax.rsqrt(ms + EPS) * sw_ref[...] * (1.0 - LAMBDA_INIT)
        o_ref[0] = o.astype(o_ref.dtype)

    def fast_path(max_dist):
        qf = load_q()
        qpos = (qi * tq + lax.broadcasted_iota(jnp.int32, qf.shape, 0)).astype(F32) * sl2
        qpieces = [t.astype(F32) for t in _split3(qpos)]
        shift = shift_ref[pl.ds(pl.multiple_of(qi * tq, tq), tq), :]
        kr = lax.broadcasted_iota(jnp.int32, (tq, tq), 0)
        qc = lax.broadcasted_iota(jnp.int32, (tq, tq), 1)
        corr = jnp.maximum(qc - kr, 0).astype(F32) * (2.0 * sl2)
        outs = []
        for c in range(2):
            rel = lane - feat0(c)
            own = own_lanes(lane, c)
            base = jnp.where(own, qf, jnp.where(rel == N_POS, -shift, 0.0))
            posf = jnp.where((rel > N_POS) & (rel < N_FEAT), -1.0, 0.0)
            for n in range(N_POS):
                posf = jnp.where(rel == n, qpieces[n], posf)
            q_after = (base + posf).astype(BF16)
            q_before = (base - posf).astype(BF16)
            if max_dist is None:
                blocks = []
                for d in range(nb):
                    wrapped = qi + d >= nb
                    jb = jnp.where(wrapped, qi + d - nb, qi + d)
                    blocks.append((jb, q_after if d == 0 else jnp.where(wrapped, q_before, q_after), d == 0))
            else:
                q_dead = jnp.where(own, qf, jnp.where(rel == N_POS, -DEAD_SHIFT, 0.0)).astype(BF16)
                blocks = []
                for r in range(-max_dist, max_dist + 1):
                    jb = qi + r
                    side = q_before if r < 0 else q_after
                    inside = (jb >= 0) & (jb < nb)
                    blocks.append((jnp.clip(jb, 0, nb - 1), side if r == 0 else jnp.where(inside, side, q_dead),
                                   r == 0))
            acc = None
            for jb, qsel, diagonal in blocks:
                dk = pl.ds(pl.multiple_of(jb * tq, tq), tq)
                st = _dot_nt(kaug_ref[c, dk, :], qsel)
                if diagonal:
                    st = st - corr
                p = jnp.exp2(st).astype(BF16)
                if max_dist is None:
                    pt_ref[c, dk, :] = p
                else:
                    part = _dot(vtb_ref[jb], p)
                    acc = part if acc is None else acc + part
            if max_dist is None:
                acc = _dot(vt_ref[...], pt_ref[c])
            outs.append(acc[:DA_VDIM] * (1.0 / acc[DA_VDIM:DA_VDIM + 1]))
        finish(outs[0].T, outs[1].T)

    fast = worst <= MAX_SHIFT_LOG2
    dists = [_max_block_dist(hh, tq, nb) for hh in range(DA_HEADS)]
    dists = [d if 2 * d + 1 < nb else None for d in dists]
    for dist in set(dists):
        heads = [hh for hh in range(DA_HEADS) if dists[hh] == dist]
        in_heads = functools.reduce(lambda a, b: a | b, [h == hh for hh in heads])
        pl.when(fast & in_heads)(functools.partial(fast_path, dist))

    @pl.when(jnp.logical_not(fast))
    def _online():
        qf = load_q()
        zero = jnp.zeros_like(qf)
        qm = [jnp.where(own_lanes(lane, c), qf, zero).astype(BF16) for c in range(2)]
        qk0 =(qi * tq + lax.broadcasted_iota(jnp.int32, (tq, tk), 0)
               - lax.broadcasted_iota(jnp.int32, (tq, tk), 1)).astype(F32)

        def body(j, carry):
            kb = k_ref[0, pl.ds(pl.multiple_of(j * tk, tk), tk), :]
            vb = v_ref[0, pl.ds(pl.multiple_of(j * tk, tk), tk), :]
            bias = sl2 * jnp.abs(qk0 - lax.convert_element_type(j * tk, F32))
            out = []
            for c in range(2):
                m, l, acc = carry[c]
                s = _dot_nt(qm[c], kb) - bias
                m_new = jnp.maximum(m, jnp.max(s, axis=-1, keepdims=True))
                p = jnp.exp2(s - m_new)
                alpha = jnp.exp2(m - m_new)
                l_new = alpha * l + jnp.sum(p, axis=-1, keepdims=True)
                acc_new = alpha * acc + _dot(p.astype(BF16), vb)
                out.append((m_new, l_new, acc_new))
            return tuple(out)

        init = tuple((jnp.full((tq, 1), -jnp.inf, F32), jnp.zeros((tq, 1), F32),
                      jnp.zeros((tq, DA_VDIM), F32)) for _ in range(2))
        (m1, l1, a1), (m2, l2, a2) = lax.fori_loop(0, L // tk, body, init)
        finish(a1 * (1.0 / l1), a2 * (1.0 / l2))


def _diff_attention(q, k, v, slopes, lamv, subln_w, B, L):
    tq, tk = min(TQ, L), min(TK, L)
    return pl.pallas_call(
        functools.partial(_attn_kernel, tq=tq, tk=tk, L=L),
        grid=(DA_HEADS, B, L // tq),
        in_specs=[
            pl.BlockSpec(memory_space=pltpu.SMEM),
            pl.BlockSpec(lamv.shape, lambda h, b, i: (0, 0)),
            pl.BlockSpec((1, tq, LANES), lambda h, b, i: (b, i, h)),
            pl.BlockSpec((1, L, LANES), lambda h, b, i: (b, 0, h)),
            pl.BlockSpec((1, L, LANES), lambda h, b, i: (b, 0, h)),
            pl.BlockSpec((1, L, LANES), lambda h, b, i: (b, 0, h)),
            pl.BlockSpec(subln_w.shape, lambda h, b, i: (0, 0)),
        ],
        out_specs=pl.BlockSpec((1, tq, LANES), lambda h, b, i: (b, i, h)),
        out_shape=jax.ShapeDtypeStruct((B, L, 512), BF16),
        scratch_shapes=[pltpu.VMEM((2, L, LANES), BF16),
                        pltpu.VMEM((2, L, LANES), BF16),
                        pltpu.VMEM((VT_ROWS, L), BF16),
                        pltpu.VMEM((L // tq, VT_ROWS, tq), BF16),
                        pltpu.VMEM((2, L, tq), BF16),
                        pltpu.VMEM((L, LANES), F32),
                        pltpu.SMEM((1,), F32)],
        compiler_params=_cparams("arbitrary", "arbitrary", "arbitrary"),
        name="diff_attn",
    )(slopes, lamv, q, q, k, v, subln_w)


def _pair_tables():
    lo, hi = [], []
    for g in range(N_GROUPS):
        for a in range(EPG):
            for b in range(a + 1, EPG):
                lo.append(g * EPG + a)
                hi.append(g * EPG + b)
    return lo, hi


PAIRS_PER_GROUP = EPG * (EPG - 1) // 2
N_BUCKETS = N_GROUPS * PAIRS_PER_GROUP
GATE_LANE = 0
ROW_TAIL = LANES


def _router_kernel(x_ref, ys_ref, ya_ref, wo_ref, nw_ref, wr_ref, br_ref, tri_ref,
                   h_ref, u_ref, meta_ref, cnt_ref, carry_ref):
    i = pl.program_id(0)

    @pl.when(i == 0)
    def _():
        carry_ref[...] = jnp.zeros_like(carry_ref)

    sub = tri_ref.shape[0]
    for s in range(x_ref.shape[0] // sub):
        rows = pl.ds(s * sub, sub)
        _route_tile(x_ref.at[rows], ys_ref.at[rows], ya_ref.at[rows], wo_ref, nw_ref, wr_ref, br_ref, tri_ref,
                    h_ref.at[rows], u_ref.at[rows], meta_ref.at[rows], cnt_ref, carry_ref)


def _route_tile(x_ref, ys_ref, ya_ref, wo_ref, nw_ref, wr_ref, br_ref, tri_ref,
                h_ref, u_ref, meta_ref, cnt_ref, carry_ref):
    h = x_ref[...] + _dot(ys_ref[...], wo_ref[0:512, :]) + _dot(ya_ref[...], wo_ref[512:1024, :])
    h_ref[...] = h
    D = h.shape[1]
    ms = jnp.mean(h * h, axis=-1, keepdims=True)
    u = h * lax.rsqrt(ms + EPS) * nw_ref[...]
    u_ref[:, 0:D // 2] = _pack_bf16_pairs(u)
    logits = _dot(u.astype(BF16), wr_ref[...]) + br_ref[...]

    tm = logits.shape[0]
    li = lax.broadcasted_iota(jnp.int32, (tm, LANES), 1)
    lf = li.astype(F32)
    ninf = jnp.float32(-jnp.inf)
    big = jnp.float32(1e9)
    gl = jnp.where(li < N_GROUPS, logits, ninf)
    gmax = jnp.max(gl, axis=-1, keepdims=True)
    gidx = jnp.min(jnp.where(gl == gmax, lf, big), axis=-1, keepdims=True)
    pg = 1.0 / jnp.sum(jnp.exp(gl - gmax), axis=-1, keepdims=True)
    lane_grp = ((li - EXP_LANE0) >> 3).astype(F32)
    in_grp = (li >= EXP_LANE0) & (li < EXP_LANE0 + N_EXPERTS) & (lane_grp == gidx)
    sl = jnp.where(in_grp, logits, ninf)
    m1 = jnp.max(sl, axis=-1, keepdims=True)
    i1 = jnp.min(jnp.where(sl == m1, lf, big), axis=-1, keepdims=True)
    sl2 = jnp.where(lf == i1, ninf, sl)
    m2 = jnp.max(sl2, axis=-1, keepdims=True)
    i2 = jnp.min(jnp.where(sl2 == m2, lf, big), axis=-1, keepdims=True)
    t = jnp.exp(m2 - m1)
    w1 = 1.0 / (1.0 + t)
    g1 = pg * w1
    g2 = pg * (t * w1)
    a1 = i1 - EXP_LANE0 - gidx * EPG
    a2 = i2 - EXP_LANE0 - gidx * EPG
    lo = jnp.minimum(a1, a2)
    hi = jnp.maximum(a1, a2)
    bucket = gidx * PAIRS_PER_GROUP + (lo * (EPG - 1) - lo * (lo - 1.0) * 0.5) + (hi - lo - 1.0)
    first_is_lo = a1 < a2
    g_lo = jnp.where(first_is_lo, g1, g2)
    g_hi = jnp.where(first_is_lo, g2, g1)
    ohb = lf == bucket
    oh = jnp.where(ohb, 1.0, 0.0)
    prefix = _dot(tri_ref[...], oh.astype(BF16)) + carry_ref[0:1, :]
    rank = jnp.sum(jnp.where(ohb, prefix, 0.0), axis=-1, keepdims=True)
    new_carry = carry_ref[0:1, :] + jnp.sum(oh, axis=0, keepdims=True)
    carry_ref[...] = jnp.broadcast_to(new_carry, carry_ref.shape)
    cnt_ref[...] = jnp.broadcast_to(new_carry, cnt_ref.shape)

    meta_ref[...] = jnp.where(li == 0, bucket, jnp.where(li == 1, rank, 0.0))
    gates = jnp.where(li == GATE_LANE, g_lo, jnp.where(li == GATE_LANE + 1, g_hi, 0.0))
    u_ref[:, D // 2:] = pltpu.bitcast(gates, jnp.uint32)


def _out_router(x2, y_ssd, y_da, w_out, norm_w, w_r, b_r):
    T, D = x2.shape
    tm = min(TM_ROUTE, T)
    r = jnp.arange(min(ROUTE_SUB, tm))
    tri = (r[:, None] > r[None, :]).astype(BF16)
    row = lambda n: pl.BlockSpec((tm, n), lambda i: (i, 0))
    full = lambda a: pl.BlockSpec(a.shape, lambda i: (0,) * a.ndim)
    return pl.pallas_call(
        _router_kernel,
        grid=(T // tm,),
        in_specs=[row(D), row(512), row(512), full(w_out), full(norm_w), full(w_r), full(b_r), full(tri)],
        out_specs=[row(D), row(D // 2 + ROW_TAIL), row(LANES), pl.BlockSpec((SUBLANES, LANES), lambda i: (0, 0))],
        out_shape=[
            jax.ShapeDtypeStruct((T, D), F32),
            jax.ShapeDtypeStruct((T, D // 2 + ROW_TAIL), jnp.uint32),
            jax.ShapeDtypeStruct((T, LANES), F32),
            jax.ShapeDtypeStruct((SUBLANES, LANES), F32),
        ],
        scratch_shapes=[pltpu.VMEM((SUBLANES, LANES), F32)],
        compiler_params=_cparams("arbitrary"),
        name="out_router",
    )(x2, y_ssd, y_da, w_out, norm_w, w_r, b_r, tri)


def _dispatch_kernel(cnt_ref, idx_ref, u_ref, xs_hbm, slots_ref, ea_ref, eb_ref, nused_ref,
                     pstart_ref, zero_ref, sem, zsem, *, tm, n_blocks):
    i = pl.program_id(0)

    @pl.when(i == 0)
    def _():
        zero_ref[...] = jnp.zeros_like(zero_ref)
        pair_lo, pair_hi = _pair_tables()

        def zero_block(first_row):
            return pltpu.make_async_copy(
                zero_ref, xs_hbm.at[pl.ds(pl.multiple_of(first_row, MOE_ROWS), MOE_ROWS)], zsem)

        acc = jnp.int32(0)
        nfill = jnp.int32(0)
        for q in range(N_BUCKETS):
            padded = ((cnt_ref[q] + (MOE_ROWS - 1)) // MOE_ROWS) * MOE_ROWS
            pstart_ref[q] = acc
            end = acc + padded

            def fill(j, _, q=q):
                ea_ref[j] = jnp.int32(pair_lo[q])
                eb_ref[j] = jnp.int32(pair_hi[q])
                return 0

            lax.fori_loop(acc // MOE_ROWS, end // MOE_ROWS, fill, 0)

            @pl.when(padded > 0)
            def _(end=end):
                zero_block(end - MOE_ROWS).start()

            nfill = nfill + jnp.where(padded > 0, 1, 0)
            acc = end
        nused = acc // MOE_ROWS
        nused_ref[0] = nused

        def fill_tail(j, _):
            ea_ref[j] = jnp.int32(0)
            eb_ref[j] = jnp.int32(0)
            zero_block(j * MOE_ROWS).start()
            return 0

        lax.fori_loop(nused, n_blocks, fill_tail, 0)

        def wait_fill(_, carry):
            zero_block(0).wait()
            return carry

        lax.fori_loop(0, nfill + (n_blocks - nused), wait_fill, 0)

    def start(t, _):
        slot = pstart_ref[idx_ref[0, t]] + idx_ref[1, t]
        slots_ref[0, t] = slot
        pltpu.make_async_copy(u_ref.at[pl.ds(t, 1)], xs_hbm.at[pl.ds(slot, 1)], sem).start()
        return 0

    lax.fori_loop(0, tm, start, 0, unroll=DMA_UNROLL)
    pltpu.make_async_copy(u_ref, u_ref, sem).wait()


def _dispatch(counts, idx, u, n_blocks):
    T, W = u.shape
    tm = min(TM_DISP, T)
    n_slots = n_blocks * MOE_ROWS
    grid_spec = pltpu.PrefetchScalarGridSpec(
        num_scalar_prefetch=1,
        grid=(T // tm,),
        in_specs=[
            pl.BlockSpec((2, tm), lambda i, cnt: (0, i), memory_space=pltpu.SMEM),
            pl.BlockSpec((tm, W), lambda i, cnt: (i, 0)),
        ],
        out_specs=[
            pl.BlockSpec(memory_space=pl.ANY),
            pl.BlockSpec((1, tm), lambda i, cnt: (0, i), memory_space=pltpu.SMEM),
            pl.BlockSpec(memory_space=pltpu.SMEM),
            pl.BlockSpec(memory_space=pltpu.SMEM),
            pl.BlockSpec(memory_space=pltpu.SMEM),
        ],
        scratch_shapes=[pltpu.SMEM((LANES,), jnp.int32),
                        pltpu.VMEM((MOE_ROWS, W), u.dtype),
                        pltpu.SemaphoreType.DMA(()),
                        pltpu.SemaphoreType.DMA(())],
    )
    return pl.pallas_call(
        functools.partial(_dispatch_kernel, tm=tm, n_blocks=n_blocks),
        grid_spec=grid_spec,
        out_shape=[
            jax.ShapeDtypeStruct((n_slots, W), u.dtype),
            jax.ShapeDtypeStruct((1, T), jnp.int32),
            jax.ShapeDtypeStruct((n_blocks,), jnp.int32),
            jax.ShapeDtypeStruct((n_blocks,), jnp.int32),
            jax.ShapeDtypeStruct((1,), jnp.int32),
        ],
        compiler_params=_cparams("arbitrary"),
        name="dispatch",
    )(counts, idx, u)


def _moe_kernel(ea_ref, eb_ref, nused_ref, xs_ref, wga_ref, wua_ref, wda_ref, wgb_ref, wub_ref, wdb_ref,
                o_ref, wa_g, wa_u, wa_d, wb_g, wb_u, wb_d):
    j = pl.program_id(0)
    last = nused_ref[0] - 1
    cur = jnp.minimum(j, last)
    prev = jnp.minimum(jnp.maximum(j - 1, 0), last)
    H = o_ref.shape[1]

    @pl.when((j == 0) | (ea_ref[cur] != ea_ref[prev]))
    def _():
        wa_g[...] = wga_ref[0].astype(BF16)
        wa_u[...] = wua_ref[0].astype(BF16)
        wa_d[...] = wda_ref[0].astype(BF16)

    @pl.when((j == 0) | (eb_ref[cur] != eb_ref[prev]))
    def _():
        wb_g[...] = wgb_ref[0].astype(BF16)
        wb_u[...] = wub_ref[0].astype(BF16)
        wb_d[...] = wdb_ref[0].astype(BF16)

    @pl.when(j < nused_ref[0])
    def _():
        x = _unpack_bf16_pairs(xs_ref[:, 0:H]).astype(BF16)
        tail = pltpu.bitcast(xs_ref[:, H:], F32)
        out = None
        for (wg, wu, wd), lane in (((wa_g, wa_u, wa_d), GATE_LANE), ((wb_g, wb_u, wb_d), GATE_LANE + 1)):
            hid = _silu(_dot(x, wg[...])) * _dot(x, wu[...])
            y = _dot(hid.astype(BF16), wd[...]) * tail[:, lane:lane + 1]
            out = y if out is None else out + y
        o_ref[...] = _pack_bf16_pairs(out)

    @pl.when(j >= nused_ref[0])
    def _():
        o_ref[...] = jnp.zeros_like(o_ref)


def _moe(block_ea, block_eb, nused, xs, wg, wu, wd, n_blocks):
    W = xs.shape[1]
    H = W - ROW_TAIL
    used = lambda j, nu: jnp.minimum(j, nu[0] - 1)
    wspec_a = lambda a: pl.BlockSpec((1,) + a.shape[1:], lambda j, ea, eb, nu: (ea[used(j, nu)], 0, 0))
    wspec_b = lambda a: pl.BlockSpec((1,) + a.shape[1:], lambda j, ea, eb, nu: (eb[used(j, nu)], 0, 0))
    wscratch = [pltpu.VMEM(a.shape[1:], BF16) for a in (wg, wu, wd)]
    grid_spec = pltpu.PrefetchScalarGridSpec(
        num_scalar_prefetch=3,
        grid=(n_blocks,),
        in_specs=[
            pl.BlockSpec((MOE_ROWS, W), lambda j, ea, eb, nu: (used(j, nu), 0)),
            wspec_a(wg), wspec_a(wu), wspec_a(wd), wspec_b(wg), wspec_b(wu), wspec_b(wd),
        ],
        out_specs=pl.BlockSpec((MOE_ROWS, H), lambda j, ea, eb, nu: (j, 0)),
        scratch_shapes=wscratch + wscratch,
    )
    return pl.pallas_call(
        _moe_kernel,
        grid_spec=grid_spec,
        out_shape=jax.ShapeDtypeStruct((n_blocks * MOE_ROWS, H), jnp.uint32),
        compiler_params=pltpu.CompilerParams(dimension_semantics=("arbitrary",),
                                             vmem_limit_bytes=MOE_VMEM_LIMIT),
        name="moe_mlp",
    )(block_ea, block_eb, nused, xs, wg, wu, wd, wg, wu, wd)


def _combine_kernel(slots_ref, next_slots_ref, h_ref, nw_ref, eo_hbm, o_ref, g_ref, sems, *, tm):
    i = pl.program_id(0)
    n = pl.num_programs(0)
    cur = i % 2

    def gather_tile(sref, buf):
        def start(t, _):
            pltpu.make_async_copy(eo_hbm.at[pl.ds(sref[0, t], 1)],
                                  g_ref.at[buf, pl.ds(t, 1)], sems.at[buf]).start()
            return 0

        lax.fori_loop(0, tm, start, 0, unroll=DMA_UNROLL)

    @pl.when(i == 0)
    def _():
        gather_tile(slots_ref, 0)

    @pl.when(i + 1 < n)
    def _():
        gather_tile(next_slots_ref, 1 - cur)

    pltpu.make_async_copy(g_ref.at[cur], g_ref.at[cur], sems.at[cur]).wait()

    y = h_ref[...] + _unpack_bf16_pairs(g_ref[cur])
    ms = jnp.mean(y * y, axis=-1, keepdims=True)
    o_ref[...] = y * lax.rsqrt(ms + EPS) * nw_ref[...]


def _combine(slots, h, norm_w, eo):
    T, D = h.shape
    tm = min(TM_COMB, T)
    last = T // tm - 1
    return pl.pallas_call(
        functools.partial(_combine_kernel, tm=tm),
        grid=(T // tm,),
        in_specs=[
            pl.BlockSpec((1, tm), lambda i: (0, i), memory_space=pltpu.SMEM),
            pl.BlockSpec((1, tm), lambda i: (0, jnp.minimum(i + 1, last)), memory_space=pltpu.SMEM),
            pl.BlockSpec((tm, D), lambda i: (i, 0)),
            pl.BlockSpec(norm_w.shape, lambda i: (0, 0)),
            pl.BlockSpec(memory_space=pl.ANY),
        ],
        out_specs=pl.BlockSpec((tm, D), lambda i: (i, 0)),
        out_shape=jax.ShapeDtypeStruct((T, D), F32),
        scratch_shapes=[pltpu.VMEM((2, tm, D // 2), jnp.uint32),
                        pltpu.SemaphoreType.DMA((2,))],
        compiler_params=_cparams("arbitrary"),
        name="combine",
    )(slots, slots, h, norm_w, eo)


def _pad_lanes(v, offset=0):
    v = v.astype(F32).reshape(-1)
    return jnp.zeros((1, LANES), F32).at[0, offset:offset + v.shape[0]].set(v)


def kernel(x, norm_mix_w, w_in, conv_w, conv_b, dt_bias_fwd, dt_bias_bwd, a_log_fwd, a_log_bwd, ssd_d,
           ssd_norm_w, lambda_q1, lambda_k1, lambda_q2, lambda_k2, subln_w, w_out, norm_ffn_w,
           w_router_group, b_router_group, w_router_exp, b_router_exp, w_exp_gate, w_exp_up, w_exp_down,
           norm_final_w):
    B, L, D = x.shape
    T = B * L
    x2 = x.reshape(T, D)
    l = 0

    w = w_in[l]
    o_z, o_xbc, o_dt = 512, 512 + 1024, 512 + 1024 + 16
    o_q, o_k = o_dt + 512, o_dt + 1024
    w_main = jnp.concatenate(
        [w[:, :o_xbc], w[:, o_dt:o_q] * (DA_HEADDIM ** -0.5 * LOG2E), w[:, o_q:]], axis=1).astype(BF16)
    w_dt = jnp.zeros((D, LANES), F32).at[:, :16].set(w[:, o_xbc:o_dt]).astype(BF16)

    z, xbc, dt, q, k, v = _in_proj(x2, norm_mix_w[l].reshape(1, D), w_main, w_dt)

    dtb = _pad_lanes(jnp.concatenate([dt_bias_fwd[l], dt_bias_bwd[l]]))
    alog = _pad_lanes(jnp.concatenate([a_log_fwd[l], a_log_bwd[l]]))
    dskip_e = jnp.repeat(ssd_d[l].astype(F32), SSD_HEADDIM).reshape(1, SSD_WIDTH)
    y_ssd = _ssd(xbc.reshape(B, L, 1024), dt.reshape(B, L, LANES), z.reshape(B, L, 512),
                 conv_w[l].astype(F32), conv_b[l].reshape(1, -1).astype(F32), dtb, alog, dskip_e,
                 ssd_norm_w[l].reshape(1, -1).astype(F32), B, L)

    slopes = jnp.power(2.0, -8.0 * jnp.arange(1, DA_HEADS + 1, dtype=F32) / DA_HEADS)
    lamv = jnp.concatenate([_pad_lanes(lambda_q1[l]), _pad_lanes(lambda_k1[l]),
                            _pad_lanes(lambda_q2[l]), _pad_lanes(lambda_k2[l]),
                            jnp.zeros((4, LANES), F32)], axis=0)
    y_da = _diff_attention(q.reshape(B, L, 512), k.reshape(B, L, 512), v.reshape(B, L, 512),
                           slopes, lamv, subln_w[l].reshape(1, -1).astype(F32), B, L)

    w_r = (jnp.zeros((D, LANES), F32).at[:, :N_GROUPS].set(w_router_group[l])
           .at[:, EXP_LANE0:EXP_LANE0 + N_EXPERTS].set(w_router_exp[l])).astype(BF16)
    b_r = _pad_lanes(jnp.concatenate([b_router_group[l], b_router_exp[l]]))
    h, u, meta, cnt = _out_router(x2, y_ssd.reshape(T, 512), y_da.reshape(T, 512), w_out[l].astype(BF16),
                                  norm_ffn_w[l].reshape(1, D), w_r, b_r)

    n_blocks = T // MOE_ROWS + N_BUCKETS
    counts = cnt[0].astype(jnp.int32)
    idx = meta[:, :2].astype(jnp.int32).T
    xs, slots, block_ea, block_eb, nused = _dispatch(counts, idx, u, n_blocks)
    eo = _moe(block_ea, block_eb, nused, xs, w_exp_gate[l], w_exp_up[l], w_exp_down[l], n_blocks)
    out = _combine(slots, h, norm_final_w.reshape(1, D), eo)
    return out.reshape(B, L, D)
```

```python
import functools
import math

import jax
import jax.numpy as jnp
from jax import lax
from jax.experimental import pallas as pl
from jax.experimental.pallas import tpu as pltpu

F32 = jnp.float32
BF16 = jnp.bfloat16

EPS = 1e-6
LOG2E = math.log2(math.e)
LANES = 128
SUBLANES = 8
HALO = 16
VMEM_LIMIT = 48 * 1024 * 1024

SSD_WIDTH = 512
SSD_HEADDIM = 64
SSD_HEADS = 8
SSD_NGROUPS = 2
SSD_HPG = 4
SSD_DSTATE = 128
SSD_CONV = 5
SSD_CHUNK = 128
SSD_CONV_CH = 1024
DA_HEADDIM = 64
DA_VDIM = 128
DA_HEADS = 4
N_GROUPS = 4
EPG = 8
N_EXPERTS = 32
D_EXPERT = 512
LAMBDA_INIT = 0.8 - 0.6 * math.exp(-0.3 * 0)

TM_PROJ = 512
SSD_SEQS = 2
TQ = 512
TK = 512
TM_ROUTE = 1024
ROUTE_SUB = 512
TM_DISP = 1024
MOE_ROWS = 256
TM_COMB = 512
DMA_UNROLL = 8
EXP_LANE0 = 4


def _cparams(*sem):
    return pltpu.CompilerParams(dimension_semantics=sem, vmem_limit_bytes=VMEM_LIMIT)


def _dot(a, b):
    return jnp.dot(a, b, preferred_element_type=F32)


def _dot_nt(a, b):
    return lax.dot_general(a, b, (((1,), (1,)), ((), ())), preferred_element_type=F32)


def _split3(v):
    hi = v.astype(BF16)
    r1 = v - hi.astype(F32)
    mid = r1.astype(BF16)
    lo = (r1 - mid.astype(F32)).astype(BF16)
    return hi, mid, lo


def _dot_exact(a01, v):
    hi, mid, lo = _split3(v)
    return _dot(a01, hi) + _dot(a01, mid) + _dot(a01, lo)


def _dot_exact_r(v, b01):
    hi, mid, lo = _split3(v)
    return _dot(hi, b01) + _dot(mid, b01) + _dot(lo, b01)


def _silu(x):
    return x * (1.0 / (1.0 + jnp.exp(-x)))


def _pack_bf16_pairs(x):
    n = x.shape[1] // 2
    bits = pltpu.bitcast(x.astype(BF16).astype(F32), jnp.uint32)
    return (bits[:, :n] >> 16) | bits[:, n:]


def _unpack_bf16_pairs(w):
    lo = pltpu.bitcast(w << 16, F32)
    hi = pltpu.bitcast(w & jnp.uint32(0xFFFF0000), F32)
    return jnp.concatenate([lo, hi], axis=1)


def _inproj_kernel(x_ref, nw_ref, w_ref, wdt_ref, z_ref, xbc_ref, dt_ref, q_ref, k_ref, v_ref):
    x = x_ref[...]
    ms = jnp.mean(x * x, axis=-1, keepdims=True)
    u = (x * lax.rsqrt(ms + EPS) * nw_ref[...]).astype(BF16)
    z_ref[...] = _dot(u, w_ref[:, 0:512]).astype(BF16)
    xbc_ref[...] = _dot(u, w_ref[:, 512:1536]).astype(BF16)
    q_ref[...] = _dot(u, w_ref[:, 1536:2048]).astype(BF16)
    k_ref[...] = _dot(u, w_ref[:, 2048:2560]).astype(BF16)
    v_ref[...] = _dot(u, w_ref[:, 2560:3072]).astype(BF16)
    dt_ref[...] = _dot(u, wdt_ref[...])


def _in_proj(x2, norm_w, w_main, w_dt):
    T, D = x2.shape
    tm = TM_PROJ
    row = lambda n: pl.BlockSpec((tm, n), lambda i: (i, 0))
    full = lambda a: pl.BlockSpec(a.shape, lambda i: (0,) * a.ndim)
    return pl.pallas_call(
        _inproj_kernel,
        grid=(T // tm,),
        in_specs=[row(D), full(norm_w), full(w_main), full(w_dt)],
        out_specs=[row(512), row(1024), row(LANES), row(512), row(512), row(512)],
        out_shape=[
            jax.ShapeDtypeStruct((T, 512), BF16),
            jax.ShapeDtypeStruct((T, 1024), BF16),
            jax.ShapeDtypeStruct((T, LANES), F32),
            jax.ShapeDtypeStruct((T, 512), BF16),
            jax.ShapeDtypeStruct((T, 512), BF16),
            jax.ShapeDtypeStruct((T, 512), BF16),
        ],
        compiler_params=_cparams("arbitrary"),
        name="in_proj",
    )(x2, norm_w, w_main, w_dt)


def _ssd_kernel(*refs, reverse):
    c = pl.program_id(1)
    nc = pl.num_programs(1)
    pos = (nc - 1 - c) if reverse else c
    ht_ref = refs[-1]

    @pl.when(c == 0)
    def _():
        ht_ref[...] = jnp.zeros_like(ht_ref)

    for s in range(ht_ref.shape[0]):
        _ssd_sequence(s, pos, nc, refs, reverse)


def _ssd_sequence(s, pos, nc, refs, reverse):
    if reverse:
        (xp_ref, xc_ref, xn_ref, dt_ref, cw_ref, cb_ref, dtb_ref, alog_ref, tri_ref, e_ref, sh_ref,
         z_ref, yf_ref, dsk_ref, nw_ref, out_ref, ht_ref) = refs
    else:
        (xp_ref, xc_ref, xn_ref, dt_ref, cw_ref, cb_ref, dtb_ref, alog_ref, tri_ref, e_ref, sh_ref,
         out_ref, ht_ref) = refs
    Q = SSD_CHUNK

    halo_zero = jnp.zeros((HALO, SSD_CONV_CH), BF16)
    ext = jnp.concatenate([jnp.where(pos > 0, xp_ref[s], halo_zero), xc_ref[s],
                           jnp.where(pos < nc - 1, xn_ref[s], halo_zero)], axis=0)
    shifted = _dot(sh_ref[...], ext)
    pad = (SSD_CONV - 1) // 2
    conv = cb_ref[...] + cw_ref[pad:pad + 1, :] * xc_ref[s].astype(F32)
    for n, k in enumerate(k for k in range(SSD_CONV) if k != pad):
        conv = conv + cw_ref[k:k + 1, :] * shifted[n * Q:(n + 1) * Q, :]
    act = _silu(conv)
    xs = act[:, :SSD_WIDTH]

    dtraw = dt_ref[s] + dtb_ref[...]
    dt = jnp.maximum(dtraw, 0.0) + jnp.log(1.0 + jnp.exp(-jnp.abs(dtraw)))
    a = -jnp.exp(alog_ref[...])
    adt = dt * a
    tri = tri_ref[...]
    cvec = _dot_exact(tri, adt)
    cvt = cvec.T
    e01 = e_ref[...]
    dt_e = _dot_exact_r(dt, e01)
    cv_e = _dot_exact_r(cvec, e01)
    edge = 0 if reverse else Q - 1
    tot_e = cv_e[edge:edge + 1, :]

    li = lax.broadcasted_iota(jnp.int32, (Q, Q), 0)
    si = lax.broadcasted_iota(jnp.int32, (Q, Q), 1)
    mask = (li <= si) if reverse else (li >= si)
    lane = lax.broadcasted_iota(jnp.int32, (Q, LANES), 1)
    head0 = SSD_HEADS if reverse else 0

    xdt = xs * dt_e
    ys = []
    for g in range(SSD_NGROUPS):
        bg = act[:, SSD_WIDTH + g * SSD_DSTATE:SSD_WIDTH + (g + 1) * SSD_DSTATE]
        cg = act[:, SSD_WIDTH + (SSD_NGROUPS + g) * SSD_DSTATE:SSD_WIDTH + (SSD_NGROUPS + g + 1) * SSD_DSTATE]
        bgb = bg.astype(BF16)
        cgb = cg.astype(BF16)
        gm = _dot_nt(cgb, bgb)
        gs = slice(g * 256, (g + 1) * 256)
        xdt_g = xdt[:, gs]
        xdtb = xdt_g.astype(BF16)
        ydiag = []
        for pr in range(2):
            xpair = xdtb[:, pr * LANES:(pr + 1) * LANES]
            yh = []
            for j in range(2):
                hl = head0 + g * SSD_HPG + pr * 2 + j
                seg = cvec[:, hl:hl + 1] - cvt[hl:hl + 1, :]
                decay = jnp.exp(jnp.where(mask, seg, -jnp.inf))
                yh.append(_dot((gm * decay).astype(BF16), xpair))
            ydiag.append(jnp.where(lane < SSD_HEADDIM, yh[0], yh[1]))
        ydiag = jnp.concatenate(ydiag, axis=1)
        ht = ht_ref[s, g]
        yoff = _dot(cgb, ht.astype(BF16)) * jnp.exp(cv_e[:, gs])
        ys.append(ydiag + yoff)
        wdec = jnp.exp(tot_e[:, gs] - cv_e[:, gs])
        xw = (xdt_g * wdec).astype(BF16)
        ht_ref[s, g] = jnp.exp(tot_e[:, gs]) * ht + _dot(bg.T.astype(BF16), xw)
    y = jnp.concatenate(ys, axis=1)

    if not reverse:
        out_ref[s] = y
    else:
        y = y + yf_ref[s] + dsk_ref[...] * xs
        gated = y * _silu(z_ref[s].astype(F32))
        ms = jnp.mean(gated * gated, axis=-1, keepdims=True)
        out_ref[s] = (gated * lax.rsqrt(ms + EPS) * nw_ref[...]).astype(out_ref.dtype)


def _ssd(xbc, dt, z, conv_w, conv_b, dtb, alog, dskip_e, norm_w, B, L):
    Q = SSD_CHUNK
    nc = L // Q
    ns = SSD_SEQS if B % SSD_SEQS == 0 else 1
    hb = Q // HALO
    nhb = L // HALO
    r = jnp.arange(Q)
    pad = (SSD_CONV - 1) // 2
    col = jnp.arange(Q + 2 * HALO)[None, :]
    shift = jnp.concatenate([(col == HALO + r[:, None] + k - pad) for k in range(SSD_CONV) if k != pad],
                            axis=0).astype(BF16)
    tril = (r[:, None] >= r[None, :]).astype(BF16)
    triu = (r[:, None] <= r[None, :]).astype(BF16)
    lane = jnp.arange(LANES)[:, None]
    ch = jnp.arange(SSD_WIDTH)[None, :] // SSD_HEADDIM
    e_f = (lane == ch).astype(BF16)
    e_b = (lane == ch + SSD_HEADS).astype(BF16)

    def call(reverse, extra_in, extra_specs, out_dtype):
        pos = (lambda c: nc - 1 - c) if reverse else (lambda c: c)
        full = lambda a: pl.BlockSpec(a.shape, lambda b, c: (0,) * a.ndim)
        chunk = lambda n: pl.BlockSpec((ns, Q, n), lambda b, c: (b, pos(c), 0))
        in_specs = [
            pl.BlockSpec((ns, HALO, 1024), lambda b, c: (b, jnp.maximum(pos(c) * hb - 1, 0), 0)),
            chunk(1024),
            pl.BlockSpec((ns, HALO, 1024), lambda b, c: (b, jnp.minimum((pos(c) + 1) * hb, nhb - 1), 0)),
            chunk(LANES),
        ]
        consts = [conv_w, conv_b, dtb, alog, triu if reverse else tril, e_b if reverse else e_f, shift]
        in_specs += [full(a) for a in consts]
        in_specs += [chunk(512) if s == "chunk" else full(a) for a, s in zip(extra_in, extra_specs)]
        return pl.pallas_call(
            functools.partial(_ssd_kernel, reverse=reverse),
            grid=(B // ns, nc),
            in_specs=in_specs,
            out_specs=chunk(512),
            out_shape=jax.ShapeDtypeStruct((B, L, 512), out_dtype),
            scratch_shapes=[pltpu.VMEM((ns, SSD_NGROUPS, SSD_DSTATE, 256), F32)],
            compiler_params=_cparams("arbitrary", "arbitrary"),
            name="ssd_bwd" if reverse else "ssd_fwd",
        )(xbc, xbc, xbc, dt, *consts, *extra_in)

    yf = call(False, [], [], F32)
    return call(True, [z, yf, dskip_e, norm_w], ["chunk", "chunk", "full", "full"], BF16)


N_POS = 3
N_FEAT = 2 * N_POS + 1
MAX_SHIFT_LOG2 = 50.0
VT_ROWS = DA_VDIM + 16
ZERO_PROB_LOG2 = 160.0
DEAD_SHIFT = 1.0e4


def _max_block_dist(head, tq, nb):
    slope_log2 = 2.0 ** (-8.0 * (head + 1) / DA_HEADS) * LOG2E
    d = 0
    while d < nb - 1 and slope_log2 * (d * tq + 1) < ZERO_PROB_LOG2:
        d += 1
    return d


def _attn_kernel(slope_ref, lamv_ref, q_ref, qall_ref, k_ref, v_ref, sw_ref, o_ref,
                 kaug_ref, kfeat_ref, qfeat_ref, corr_ref, vt_ref, vtb_ref, pt_ref, shift_ref, worst_ref,
                 *, tq, tk, L):
    h = pl.program_id(0)
    b = pl.program_id(1)
    qi = pl.program_id(2)
    nb = L // tq
    sl2 = slope_ref[h] * LOG2E
    lv = lamv_ref[...]
    lam = (jnp.exp(jnp.sum(lv[0:1] * lv[1:2], axis=-1, keepdims=True))
           - jnp.exp(jnp.sum(lv[2:3] * lv[3:4], axis=-1, keepdims=True)) + LAMBDA_INIT)

    def feat0(c):
        return DA_HEADDIM if c == 0 else 0

    def own_lanes(lane, c):
        return (lane < DA_HEADDIM) if c == 0 else (lane >= DA_HEADDIM)

    ha = lax.broadcasted_iota(jnp.int32, (LANES, LANES), 0) < DA_HEADDIM
    hb = lax.broadcasted_iota(jnp.int32, (LANES, LANES), 1) < DA_HEADDIM
    other_half = jnp.where(ha != hb, 1.0, 0.0).astype(BF16)

    @pl.when((b == 0) & (qi == 0))
    def _tables():
        lane = lax.broadcasted_iota(jnp.int32, (L, LANES), 1)
        kpos = lax.broadcasted_iota(jnp.int32, (L, LANES), 0).astype(F32) * sl2
        pieces = [t.astype(F32) for t in _split3(kpos)]
        for c in range(2):
            rel = lane - feat0(c)
            feat = jnp.where((rel >= 0) & (rel <= N_POS), 1.0, 0.0)
            for n in range(N_POS):
                feat = jnp.where(rel == N_POS + 1 + n, pieces[n], feat)
            kfeat_ref[c] = feat.astype(BF16)
            qfeat = jnp.where((rel > N_POS) & (rel < N_FEAT), -1.0, 0.0)
            for n in range(N_POS):
                qfeat = jnp.where(rel == n, pieces[n], qfeat)
            qfeat_ref[c] = qfeat.astype(BF16)
        kr = lax.broadcasted_iota(jnp.int32, (tq, tq), 0)
        qc = lax.broadcasted_iota(jnp.int32, (tq, tq), 1)
        corr_ref[...] = jnp.maximum(qc - kr, 0).astype(F32) * (2.0 * sl2)

    @pl.when(qi == 0)
    def _prep():
        kraw = k_ref[0]
        lane = lax.broadcasted_iota(jnp.int32, kraw.shape, 1)
        for c in range(2):
            kaug_ref[c] = jnp.where(own_lanes(lane, c), kraw, kfeat_ref[c])
        kf = kraw.astype(F32)
        kn2 =_dot((kf * kf).astype(BF16), other_half)
        kmax2 = jnp.max(kn2, axis=0, keepdims=True)
        qa = qall_ref[0].astype(F32)
        prod2 = _dot((qa * qa).astype(BF16), other_half) * kmax2
        bound = prod2 * lax.rsqrt(jnp.maximum(prod2, 1e-30))
        shift_all = (bound * 1.02).astype(BF16).astype(F32)
        shift_ref[...] = shift_all
        worst_ref[0] = jnp.max(shift_all)
        ones_row = lax.broadcasted_iota(jnp.int32, (VT_ROWS - DA_VDIM, tq), 0) == 0
        ones_rows = jnp.where(ones_row, 1.0, 0.0).astype(BF16)
        for jb in range(nb):
            vt = v_ref[0, jb * tq:(jb + 1) * tq, :].astype(F32).T.astype(BF16)
            vt_ref[0:DA_VDIM, jb * tq:(jb + 1) * tq] = vt
            vt_ref[DA_VDIM:, jb * tq:(jb + 1) * tq] = ones_rows
            vtb_ref[jb, 0:DA_VDIM, :] = vt
            vtb_ref[jb, DA_VDIM:, :] = ones_rows

    worst = worst_ref[0]
    lane = lax.broadcasted_iota(jnp.int32, (tq, LANES), 1)

    def load_q():
        return q_ref[0].astype(F32)

    def finish_t(o1t, o2t):
        finish((o1t - lam * o2t).T, None)

    def finish(o1, o2):
        o = o1 if o2 is None else o1 - lam * o2
        ms = jnp.mean(o * o, axis=-1, keepdims=True)
        o = o * lax.rsqrt(ms + EPS) * sw_ref[...] * (1.0 - LAMBDA_INIT)
        o_ref[0] = o.astype(o_ref.dtype)

    def fast_path(max_dist):
        rows = pl.ds(pl.multiple_of(qi * tq, tq), tq)
        qraw = q_ref[0]
        neg_shift = (-shift_ref[rows, :]).astype(BF16)
        zero = jnp.zeros_like(qraw)
        outs = []
        for c in range(2):
            rel = lane - feat0(c)
            own = own_lanes(lane, c)
            base = jnp.where(own, qraw, jnp.where(rel == N_POS, neg_shift, zero))
            posf = qfeat_ref[c, rows, :]
            q_after = base + posf
            q_before = base - posf
            if max_dist is None:
                blocks = []
                for d in range(nb):
                    wrapped = qi + d >= nb
                    jb = jnp.where(wrapped, qi + d - nb, qi + d)
                    blocks.append((jb, q_after if d == 0 else jnp.where(wrapped, q_before, q_after), d == 0))
            else:
                q_dead = jnp.where(own, qraw, jnp.where(rel == N_POS, -DEAD_SHIFT, 0.0).astype(BF16))
                blocks = []
                for r in range(-max_dist, max_dist + 1):
                    jb = qi + r
                    side = q_before if r < 0 else q_after
                    inside = (jb >= 0) & (jb < nb)
                    blocks.append((jnp.clip(jb, 0, nb - 1), side if r == 0 else jnp.where(inside, side, q_dead),
                                   r == 0))
            acc = None
            for jb, qsel, diagonal in blocks:
                dk = pl.ds(pl.multiple_of(jb * tq, tq), tq)
                st = _dot_nt(kaug_ref[c, dk, :], qsel)
                if diagonal:
                    st = st - corr_ref[...]
                p = jnp.exp2(st).astype(BF16)
                if max_dist is None:
                    pt_ref[c, dk, :] = p
                else:
                    part = _dot(vtb_ref[jb], p)
                    acc = part if acc is None else acc + part
            if max_dist is None:
                acc = _dot(vt_ref[...], pt_ref[c])
            outs.append(acc[:DA_VDIM] * (1.0 / acc[DA_VDIM:DA_VDIM + 1]))
        finish_t(outs[0], outs[1])

    fast = worst <= MAX_SHIFT_LOG2
    dists = [_max_block_dist(hh, tq, nb) for hh in range(DA_HEADS)]
    dists = [d if 2 * d + 1 < nb else None for d in dists]
    for dist in set(dists):
        heads = [hh for hh in range(DA_HEADS) if dists[hh] == dist]
        in_heads = functools.reduce(lambda a, b: a | b, [h == hh for hh in heads])
        pl.when(fast & in_heads)(functools.partial(fast_path, dist))

    @pl.when(jnp.logical_not(fast))
    def _online():
        qf = load_q()
        zero = jnp.zeros_like(qf)
        qm = [jnp.where(own_lanes(lane, c), qf, zero).astype(BF16) for c in range(2)]
        qk0 =(qi * tq + lax.broadcasted_iota(jnp.int32, (tq, tk), 0)
               - lax.broadcasted_iota(jnp.int32, (tq, tk), 1)).astype(F32)

        def body(j, carry):
            kb = k_ref[0, pl.ds(pl.multiple_of(j * tk, tk), tk), :]
            vb = v_ref[0, pl.ds(pl.multiple_of(j * tk, tk), tk), :]
            bias = sl2 * jnp.abs(qk0 - lax.convert_element_type(j * tk, F32))
            out = []
            for c in range(2):
                m, l, acc = carry[c]
                s = _dot_nt(qm[c], kb) - bias
                m_new = jnp.maximum(m, jnp.max(s, axis=-1, keepdims=True))
                p = jnp.exp2(s - m_new)
                alpha = jnp.exp2(m - m_new)
                l_new = alpha * l + jnp.sum(p, axis=-1, keepdims=True)
                acc_new = alpha * acc + _dot(p.astype(BF16), vb)
                out.append((m_new, l_new, acc_new))
            return tuple(out)

        init = tuple((jnp.full((tq, 1), -jnp.inf, F32), jnp.zeros((tq, 1), F32),
                      jnp.zeros((tq, DA_VDIM), F32)) for _ in range(2))
        (m1, l1, a1), (m2, l2, a2) = lax.fori_loop(0, L // tk, body, init)
        finish(a1 * (1.0 / l1), a2 * (1.0 / l2))


def _diff_attention(q, k, v, slopes, lamv, subln_w, B, L):
    tq, tk = min(TQ, L), min(TK, L)
    return pl.pallas_call(
        functools.partial(_attn_kernel, tq=tq, tk=tk, L=L),
        grid=(DA_HEADS, B, L // tq),
        in_specs=[
            pl.BlockSpec(memory_space=pltpu.SMEM),
            pl.BlockSpec(lamv.shape, lambda h, b, i: (0, 0)),
            pl.BlockSpec((1, tq, LANES), lambda h, b, i: (b, i, h)),
            pl.BlockSpec((1, L, LANES), lambda h, b, i: (b, 0, h)),
            pl.BlockSpec((1, L, LANES), lambda h, b, i: (b, 0, h)),
            pl.BlockSpec((1, L, LANES), lambda h, b, i: (b, 0, h)),
            pl.BlockSpec(subln_w.shape, lambda h, b, i: (0, 0)),
        ],
        out_specs=pl.BlockSpec((1, tq, LANES), lambda h, b, i: (b, i, h)),
        out_shape=jax.ShapeDtypeStruct((B, L, 512), BF16),
        scratch_shapes=[pltpu.VMEM((2, L, LANES), BF16),
                        pltpu.VMEM((2, L, LANES), BF16),
                        pltpu.VMEM((2, L, LANES), BF16),
                        pltpu.VMEM((tq, tq), F32),
                        pltpu.VMEM((VT_ROWS, L), BF16),
                        pltpu.VMEM((L // tq, VT_ROWS, tq), BF16),
                        pltpu.VMEM((2, L, tq), BF16),
                        pltpu.VMEM((L, LANES), F32),
                        pltpu.SMEM((1,), F32)],
        compiler_params=_cparams("arbitrary", "arbitrary", "arbitrary"),
        name="diff_attn",
    )(slopes, lamv, q, q, k, v, subln_w)


def _pair_tables():
    lo, hi = [], []
    for g in range(N_GROUPS):
        for a in range(EPG):
            for b in range(a + 1, EPG):
                lo.append(g * EPG + a)
                hi.append(g * EPG + b)
    return lo, hi


PAIRS_PER_GROUP = EPG * (EPG - 1) // 2
N_BUCKETS = N_GROUPS * PAIRS_PER_GROUP
GATE_LANE = 0
ROW_TAIL = LANES


def _router_kernel(x_ref, ys_ref, ya_ref, wo_ref, nw_ref, wr_ref, br_ref, tri_ref,
                   h_ref, u_ref, meta_ref, cnt_ref, carry_ref):
    i = pl.program_id(0)

    @pl.when(i == 0)
    def _():
        carry_ref[...] = jnp.zeros_like(carry_ref)

    sub = tri_ref.shape[0]
    for s in range(x_ref.shape[0] // sub):
        rows = pl.ds(s * sub, sub)
        _route_tile(x_ref.at[rows], ys_ref.at[rows], ya_ref.at[rows], wo_ref, nw_ref, wr_ref, br_ref, tri_ref,
                    h_ref.at[rows], u_ref.at[rows], meta_ref.at[rows], cnt_ref, carry_ref)


def _route_tile(x_ref, ys_ref, ya_ref, wo_ref, nw_ref, wr_ref, br_ref, tri_ref,
                h_ref, u_ref, meta_ref, cnt_ref, carry_ref):
    h = x_ref[...] + _dot(ys_ref[...], wo_ref[0:512, :]) + _dot(ya_ref[...], wo_ref[512:1024, :])
    h_ref[...] = h
    D = h.shape[1]
    ms = jnp.mean(h * h, axis=-1, keepdims=True)
    u = h * lax.rsqrt(ms + EPS) * nw_ref[...]
    u_ref[:, 0:D // 2] = _pack_bf16_pairs(u)
    logits = _dot(u.astype(BF16), wr_ref[...]) + br_ref[...]

    tm = logits.shape[0]
    li = lax.broadcasted_iota(jnp.int32, (tm, LANES), 1)
    lf = li.astype(F32)
    ninf = jnp.float32(-jnp.inf)
    big = jnp.float32(1e9)
    gl = jnp.where(li < N_GROUPS, logits, ninf)
    gmax = jnp.max(gl, axis=-1, keepdims=True)
    gidx = jnp.min(jnp.where(gl == gmax, lf, big), axis=-1, keepdims=True)
    pg = 1.0 / jnp.sum(jnp.exp(gl - gmax), axis=-1, keepdims=True)
    lane_grp = ((li - EXP_LANE0) >> 3).astype(F32)
    in_grp = (li >= EXP_LANE0) & (li < EXP_LANE0 + N_EXPERTS) & (lane_grp == gidx)
    sl = jnp.where(in_grp, logits, ninf)
    m1 = jnp.max(sl, axis=-1, keepdims=True)
    i1 = jnp.min(jnp.where(sl == m1, lf, big), axis=-1, keepdims=True)
    sl2 = jnp.where(lf == i1, ninf, sl)
    m2 = jnp.max(sl2, axis=-1, keepdims=True)
    i2 = jnp.min(jnp.where(sl2 == m2, lf, big), axis=-1, keepdims=True)
    t = jnp.exp(m2 - m1)
    w1 = 1.0 / (1.0 + t)
    g1 = pg * w1
    g2 = pg * (t * w1)
    a1 = i1 - EXP_LANE0 - gidx * EPG
    a2 = i2 - EXP_LANE0 - gidx * EPG
    lo = jnp.minimum(a1, a2)
    hi = jnp.maximum(a1, a2)
    bucket = gidx * PAIRS_PER_GROUP + (lo * (EPG - 1) - lo * (lo - 1.0) * 0.5) + (hi - lo - 1.0)
    first_is_lo = a1 < a2
    g_lo = jnp.where(first_is_lo, g1, g2)
    g_hi = jnp.where(first_is_lo, g2, g1)
    ohb = lf == bucket
    oh = jnp.where(ohb, 1.0, 0.0)
    prefix = _dot(tri_ref[...], oh.astype(BF16)) + carry_ref[0:1, :]
    rank = jnp.sum(jnp.where(ohb, prefix, 0.0), axis=-1, keepdims=True)
    new_carry = carry_ref[0:1, :] + jnp.sum(oh, axis=0, keepdims=True)
    carry_ref[...] = jnp.broadcast_to(new_carry, carry_ref.shape)
    cnt_ref[...] = jnp.broadcast_to(new_carry, cnt_ref.shape)

    meta_ref[...] = jnp.where(li == 0, bucket, jnp.where(li == 1, rank, 0.0))
    gates = jnp.where(li == GATE_LANE, g_lo, jnp.where(li == GATE_LANE + 1, g_hi, 0.0))
    u_ref[:, D // 2:] = pltpu.bitcast(gates, jnp.uint32)


def _out_router(x2, y_ssd, y_da, w_out, norm_w, w_r, b_r):
    T, D = x2.shape
    tm = min(TM_ROUTE, T)
    r = jnp.arange(min(ROUTE_SUB, tm))
    tri = (r[:, None] > r[None, :]).astype(BF16)
    row = lambda n: pl.BlockSpec((tm, n), lambda i: (i, 0))
    full = lambda a: pl.BlockSpec(a.shape, lambda i: (0,) * a.ndim)
    return pl.pallas_call(
        _router_kernel,
        grid=(T // tm,),
        in_specs=[row(D), row(512), row(512), full(w_out), full(norm_w), full(w_r), full(b_r), full(tri)],
        out_specs=[row(D), row(D // 2 + ROW_TAIL), row(LANES), pl.BlockSpec((SUBLANES, LANES), lambda i: (0, 0))],
        out_shape=[
            jax.ShapeDtypeStruct((T, D), F32),
            jax.ShapeDtypeStruct((T, D // 2 + ROW_TAIL), jnp.uint32),
            jax.ShapeDtypeStruct((T, LANES), F32),
            jax.ShapeDtypeStruct((SUBLANES, LANES), F32),
        ],
        scratch_shapes=[pltpu.VMEM((SUBLANES, LANES), F32)],
        compiler_params=_cparams("arbitrary"),
        name="out_router",
    )(x2, y_ssd, y_da, w_out, norm_w, w_r, b_r, tri)


def _dispatch_kernel(cnt_ref, idx_ref, u_ref, xs_hbm, slots_ref, ea_ref, eb_ref, nused_ref,
                     pstart_ref, zero_ref, sem, zsem, *, tm, n_blocks):
    i = pl.program_id(0)

    @pl.when(i == 0)
    def _():
        zero_ref[...] = jnp.zeros_like(zero_ref)
        pair_lo, pair_hi = _pair_tables()

        def zero_block(first_row):
            return pltpu.make_async_copy(
                zero_ref, xs_hbm.at[pl.ds(pl.multiple_of(first_row, MOE_ROWS), MOE_ROWS)], zsem)

        acc = jnp.int32(0)
        nfill = jnp.int32(0)
        for q in range(N_BUCKETS):
            padded = ((cnt_ref[q] + (MOE_ROWS - 1)) // MOE_ROWS) * MOE_ROWS
            pstart_ref[q] = acc
            end = acc + padded

            def fill(j, _, q=q):
                ea_ref[j] = jnp.int32(pair_lo[q])
                eb_ref[j] = jnp.int32(pair_hi[q])
                return 0

            lax.fori_loop(acc // MOE_ROWS, end // MOE_ROWS, fill, 0)

            @pl.when(padded > 0)
            def _(end=end):
                zero_block(end - MOE_ROWS).start()

            nfill = nfill + jnp.where(padded > 0, 1, 0)
            acc = end
        nused = acc // MOE_ROWS
        nused_ref[0] = nused

        def fill_tail(j, _):
            ea_ref[j] = jnp.int32(0)
            eb_ref[j] = jnp.int32(0)
            zero_block(j * MOE_ROWS).start()
            return 0

        lax.fori_loop(nused, n_blocks, fill_tail, 0)

        def wait_fill(_, carry):
            zero_block(0).wait()
            return carry

        lax.fori_loop(0, nfill + (n_blocks - nused), wait_fill, 0)

    def start(t, _):
        slot = pstart_ref[idx_ref[0, t]] + idx_ref[1, t]
        slots_ref[0, t] = slot
        pltpu.make_async_copy(u_ref.at[pl.ds(t, 1)], xs_hbm.at[pl.ds(slot, 1)], sem).start()
        return 0

    lax.fori_loop(0, tm, start, 0, unroll=DMA_UNROLL)
    pltpu.make_async_copy(u_ref, u_ref, sem).wait()


def _dispatch(counts, idx, u, n_blocks):
    T, W = u.shape
    tm = min(TM_DISP, T)
    n_slots = n_blocks * MOE_ROWS
    grid_spec = pltpu.PrefetchScalarGridSpec(
        num_scalar_prefetch=1,
        grid=(T // tm,),
        in_specs=[
            pl.BlockSpec((2, tm), lambda i, cnt: (0, i), memory_space=pltpu.SMEM),
            pl.BlockSpec((tm, W), lambda i, cnt: (i, 0)),
        ],
        out_specs=[
            pl.BlockSpec(memory_space=pl.ANY),
            pl.BlockSpec((1, tm), lambda i, cnt: (0, i), memory_space=pltpu.SMEM),
            pl.BlockSpec(memory_space=pltpu.SMEM),
            pl.BlockSpec(memory_space=pltpu.SMEM),
            pl.BlockSpec(memory_space=pltpu.SMEM),
        ],
        scratch_shapes=[pltpu.SMEM((LANES,), jnp.int32),
                        pltpu.VMEM((MOE_ROWS, W), u.dtype),
                        pltpu.SemaphoreType.DMA(()),
                        pltpu.SemaphoreType.DMA(())],
    )
    return pl.pallas_call(
        functools.partial(_dispatch_kernel, tm=tm, n_blocks=n_blocks),
        grid_spec=grid_spec,
        out_shape=[
            jax.ShapeDtypeStruct((n_slots, W), u.dtype),
            jax.ShapeDtypeStruct((1, T), jnp.int32),
            jax.ShapeDtypeStruct((n_blocks,), jnp.int32),
            jax.ShapeDtypeStruct((n_blocks,), jnp.int32),
            jax.ShapeDtypeStruct((1,), jnp.int32),
        ],
        compiler_params=_cparams("arbitrary"),
        name="dispatch",
    )(counts, idx, u)


def _moe_kernel(ea_ref, eb_ref, nused_ref, xs_ref, wga_ref, wua_ref, wda_ref, wgb_ref, wub_ref, wdb_ref, o_ref):
    j = pl.program_id(0)
    H = o_ref.shape[1]

    @pl.when(j < nused_ref[0])
    def _():
        x = _unpack_bf16_pairs(xs_ref[:, 0:H]).astype(BF16)
        tail = pltpu.bitcast(xs_ref[:, H:], F32)
        out = None
        for (wg, wu, wd), lane in (((wga_ref, wua_ref, wda_ref), GATE_LANE),
                                   ((wgb_ref, wub_ref, wdb_ref), GATE_LANE + 1)):
            hid = _silu(_dot(x, wg[0])) * _dot(x, wu[0])
            y = _dot(hid.astype(BF16), wd[0]) * tail[:, lane:lane + 1]
            out = y if out is None else out + y
        o_ref[...] = _pack_bf16_pairs(out)

    @pl.when(j >= nused_ref[0])
    def _():
        o_ref[...] = jnp.zeros_like(o_ref)


def _moe(block_ea, block_eb, nused, xs, wg, wu, wd, n_blocks):
    W = xs.shape[1]
    H = W - ROW_TAIL
    used = lambda j, nu: jnp.minimum(j, nu[0] - 1)
    wspec_a = lambda a: pl.BlockSpec((1,) + a.shape[1:], lambda j, ea, eb, nu: (ea[used(j, nu)], 0, 0))
    wspec_b = lambda a: pl.BlockSpec((1,) + a.shape[1:], lambda j, ea, eb, nu: (eb[used(j, nu)], 0, 0))
    grid_spec = pltpu.PrefetchScalarGridSpec(
        num_scalar_prefetch=3,
        grid=(n_blocks,),
        in_specs=[
            pl.BlockSpec((MOE_ROWS, W), lambda j, ea, eb, nu: (used(j, nu), 0)),
            wspec_a(wg), wspec_a(wu), wspec_a(wd), wspec_b(wg), wspec_b(wu), wspec_b(wd),
        ],
        out_specs=pl.BlockSpec((MOE_ROWS, H), lambda j, ea, eb, nu: (j, 0)),
    )
    return pl.pallas_call(
        _moe_kernel,
        grid_spec=grid_spec,
        out_shape=jax.ShapeDtypeStruct((n_blocks * MOE_ROWS, H), jnp.uint32),
        compiler_params=_cparams("arbitrary"),
        name="moe_mlp",
    )(block_ea, block_eb, nused, xs, wg, wu, wd, wg, wu, wd)


def _combine_kernel(slots_ref, next_slots_ref, h_ref, nw_ref, eo_hbm, o_ref, g_ref, sems, *, tm):
    i = pl.program_id(0)
    n = pl.num_programs(0)
    cur = i % 2

    def gather_tile(sref, buf):
        def start(t, _):
            pltpu.make_async_copy(eo_hbm.at[pl.ds(sref[0, t], 1)],
                                  g_ref.at[buf, pl.ds(t, 1)], sems.at[buf]).start()
            return 0

        lax.fori_loop(0, tm, start, 0, unroll=DMA_UNROLL)

    @pl.when(i == 0)
    def _():
        gather_tile(slots_ref, 0)

    @pl.when(i + 1 < n)
    def _():
        gather_tile(next_slots_ref, 1 - cur)

    pltpu.make_async_copy(g_ref.at[cur], g_ref.at[cur], sems.at[cur]).wait()

    y = h_ref[...] + _unpack_bf16_pairs(g_ref[cur])
    ms = jnp.mean(y * y, axis=-1, keepdims=True)
    o_ref[...] = y * lax.rsqrt(ms + EPS) * nw_ref[...]


def _combine(slots, h, norm_w, eo):
    T, D = h.shape
    tm = min(TM_COMB, T)
    last = T // tm - 1
    return pl.pallas_call(
        functools.partial(_combine_kernel, tm=tm),
        grid=(T // tm,),
        in_specs=[
            pl.BlockSpec((1, tm), lambda i: (0, i), memory_space=pltpu.SMEM),
            pl.BlockSpec((1, tm), lambda i: (0, jnp.minimum(i + 1, last)), memory_space=pltpu.SMEM),
            pl.BlockSpec((tm, D), lambda i: (i, 0)),
            pl.BlockSpec(norm_w.shape, lambda i: (0, 0)),
            pl.BlockSpec(memory_space=pl.ANY),
        ],
        out_specs=pl.BlockSpec((tm, D), lambda i: (i, 0)),
        out_shape=jax.ShapeDtypeStruct((T, D), F32),
        scratch_shapes=[pltpu.VMEM((2, tm, D // 2), jnp.uint32),
                        pltpu.SemaphoreType.DMA((2,))],
        compiler_params=_cparams("arbitrary"),
        name="combine",
    )(slots, slots, h, norm_w, eo)


def _pad_lanes(v, offset=0):
    v = v.astype(F32).reshape(-1)
    return jnp.zeros((1, LANES), F32).at[0, offset:offset + v.shape[0]].set(v)


def kernel(x, norm_mix_w, w_in, conv_w, conv_b, dt_bias_fwd, dt_bias_bwd, a_log_fwd, a_log_bwd, ssd_d,
           ssd_norm_w, lambda_q1, lambda_k1, lambda_q2, lambda_k2, subln_w, w_out, norm_ffn_w,
           w_router_group, b_router_group, w_router_exp, b_router_exp, w_exp_gate, w_exp_up, w_exp_down,
           norm_final_w):
    B, L, D = x.shape
    T = B * L
    x2 = x.reshape(T, D)
    l = 0

    w = w_in[l]
    o_z, o_xbc, o_dt = 512, 512 + 1024, 512 + 1024 + 16
    o_q, o_k = o_dt + 512, o_dt + 1024
    w_main = jnp.concatenate(
        [w[:, :o_xbc], w[:, o_dt:o_q] * (DA_HEADDIM ** -0.5 * LOG2E), w[:, o_q:]], axis=1).astype(BF16)
    w_dt = jnp.zeros((D, LANES), F32).at[:, :16].set(w[:, o_xbc:o_dt]).astype(BF16)

    z, xbc, dt, q, k, v = _in_proj(x2, norm_mix_w[l].reshape(1, D), w_main, w_dt)

    dtb = _pad_lanes(jnp.concatenate([dt_bias_fwd[l], dt_bias_bwd[l]]))
    alog = _pad_lanes(jnp.concatenate([a_log_fwd[l], a_log_bwd[l]]))
    dskip_e = jnp.repeat(ssd_d[l].astype(F32), SSD_HEADDIM).reshape(1, SSD_WIDTH)
    y_ssd = _ssd(xbc.reshape(B, L, 1024), dt.reshape(B, L, LANES), z.reshape(B, L, 512),
                 conv_w[l].astype(F32), conv_b[l].reshape(1, -1).astype(F32), dtb, alog, dskip_e,
                 ssd_norm_w[l].reshape(1, -1).astype(F32), B, L)

    slopes = jnp.power(2.0, -8.0 * jnp.arange(1, DA_HEADS + 1, dtype=F32) / DA_HEADS)
    lamv = jnp.concatenate([_pad_lanes(lambda_q1[l]), _pad_lanes(lambda_k1[l]),
                            _pad_lanes(lambda_q2[l]), _pad_lanes(lambda_k2[l]),
                            jnp.zeros((4, LANES), F32)], axis=0)
    y_da = _diff_attention(q.reshape(B, L, 512), k.reshape(B, L, 512), v.reshape(B, L, 512),
                           slopes, lamv, subln_w[l].reshape(1, -1).astype(F32), B, L)

    w_r = (jnp.zeros((D, LANES), F32).at[:, :N_GROUPS].set(w_router_group[l])
           .at[:, EXP_LANE0:EXP_LANE0 + N_EXPERTS].set(w_router_exp[l])).astype(BF16)
    b_r = _pad_lanes(jnp.concatenate([b_router_group[l], b_router_exp[l]]))
    h, u, meta, cnt = _out_router(x2, y_ssd.reshape(T, 512), y_da.reshape(T, 512), w_out[l].astype(BF16),
                                  norm_ffn_w[l].reshape(1, D), w_r, b_r)

    n_blocks = T // MOE_ROWS + N_BUCKETS
    counts = cnt[0].astype(jnp.int32)
    idx = meta[:, :2].astype(jnp.int32).T
    xs, slots, block_ea, block_eb, nused = _dispatch(counts, idx, u, n_blocks)
    eo = _moe(block_ea, block_eb, nused, xs, w_exp_gate[l].astype(BF16), w_exp_up[l].astype(BF16),
              w_exp_down[l].astype(BF16), n_blocks)
    out = _combine(slots, h, norm_final_w.reshape(1, D), eo)
    return out.reshape(B, L, D)
```

```python
import functools
import math

import jax
import jax.numpy as jnp
from jax import lax
from jax.experimental import pallas as pl
from jax.experimental.pallas import tpu as pltpu

F32 = jnp.float32
BF16 = jnp.bfloat16

EPS = 1e-6
LOG2E = math.log2(math.e)
LANES = 128
SUBLANES = 8
HALO = 16
VMEM_LIMIT = 48 * 1024 * 1024

SSD_WIDTH = 512
SSD_HEADDIM = 64
SSD_HEADS = 8
SSD_NGROUPS = 2
SSD_HPG = 4
SSD_DSTATE = 128
SSD_CONV = 5
SSD_CHUNK = 128
SSD_CONV_CH = 1024
DA_HEADDIM = 64
DA_VDIM = 128
DA_HEADS = 4
N_GROUPS = 4
EPG = 8
N_EXPERTS = 32
D_EXPERT = 512
LAMBDA_INIT = 0.8 - 0.6 * math.exp(-0.3 * 0)

TM_PROJ = 512
SSD_SEQS = 4
TQ = 512
TQ_SUBS = 2
TK = 512
TM_ROUTE = 1024
ROUTE_SUB = 512
TM_DISP = 1024
MOE_ROWS = 256
TM_COMB = 512
DMA_UNROLL = 8
EXP_LANE0 = 4


def _cparams(*sem):
    return pltpu.CompilerParams(dimension_semantics=sem, vmem_limit_bytes=VMEM_LIMIT)


def _dot(a, b):
    return jnp.dot(a, b, preferred_element_type=F32)


def _dot_nt(a, b):
    return lax.dot_general(a, b, (((1,), (1,)), ((), ())), preferred_element_type=F32)


def _split3(v):
    hi = v.astype(BF16)
    r1 = v - hi.astype(F32)
    mid = r1.astype(BF16)
    lo = (r1 - mid.astype(F32)).astype(BF16)
    return hi, mid, lo


def _split2(v):
    hi = v.astype(BF16)
    return hi, (v - hi.astype(F32)).astype(BF16)


def _select_rows(a01x2, v):
    hi, lo = _split2(v)
    return _dot(a01x2, jnp.concatenate([hi, lo], axis=0))


def _select_lanes(v, b01x2):
    hi, lo = _split2(v)
    return _dot(jnp.concatenate([hi, lo], axis=1), b01x2)


def _silu(x):
    return x * (1.0 / (1.0 + jnp.exp(-x)))


def _pack_bf16_pairs(x):
    n = x.shape[1] // 2
    bits = pltpu.bitcast(x.astype(BF16).astype(F32), jnp.uint32)
    return (bits[:, :n] >> 16) | bits[:, n:]


def _unpack_bf16_pairs(w):
    lo = pltpu.bitcast(w << 16, F32)
    hi = pltpu.bitcast(w & jnp.uint32(0xFFFF0000), F32)
    return jnp.concatenate([lo, hi], axis=1)


def _inproj_kernel(x_ref, nw_ref, w_ref, wdt_ref, z_ref, xbc_ref, dt_ref, q_ref, k_ref, v_ref):
    x = x_ref[...]
    ms = jnp.mean(x * x, axis=-1, keepdims=True)
    u = (x * lax.rsqrt(ms + EPS) * nw_ref[...]).astype(BF16)
    z_ref[...] = _dot(u, w_ref[:, 0:512]).astype(BF16)
    xbc_ref[...] = _dot(u, w_ref[:, 512:1536]).astype(BF16)
    q_ref[...] = _dot(u, w_ref[:, 1536:2048]).astype(BF16)
    k_ref[...] = _dot(u, w_ref[:, 2048:2560]).astype(BF16)
    v_ref[...] = _dot(u, w_ref[:, 2560:3072]).astype(BF16)
    dt_ref[...] = _dot(u, wdt_ref[...])


def _in_proj(x2, norm_w, w_main, w_dt):
    T, D = x2.shape
    tm = TM_PROJ
    row = lambda n: pl.BlockSpec((tm, n), lambda i: (i, 0))
    full = lambda a: pl.BlockSpec(a.shape, lambda i: (0,) * a.ndim)
    return pl.pallas_call(
        _inproj_kernel,
        grid=(T // tm,),
        in_specs=[row(D), full(norm_w), full(w_main), full(w_dt)],
        out_specs=[row(512), row(1024), row(LANES), row(512), row(512), row(512)],
        out_shape=[
            jax.ShapeDtypeStruct((T, 512), BF16),
            jax.ShapeDtypeStruct((T, 1024), BF16),
            jax.ShapeDtypeStruct((T, LANES), F32),
            jax.ShapeDtypeStruct((T, 512), BF16),
            jax.ShapeDtypeStruct((T, 512), BF16),
            jax.ShapeDtypeStruct((T, 512), BF16),
        ],
        compiler_params=_cparams("arbitrary"),
        name="in_proj",
    )(x2, norm_w, w_main, w_dt)


def _ssd_kernel(*refs, reverse):
    c = pl.program_id(1)
    nc = pl.num_programs(1)
    pos = (nc - 1 - c) if reverse else c
    ht_ref = refs[-1]

    @pl.when(c == 0)
    def _():
        ht_ref[...] = jnp.zeros_like(ht_ref)

    for s in range(ht_ref.shape[0]):
        _ssd_sequence(s, pos, nc, refs, reverse)


def _ssd_sequence(s, pos, nc, refs, reverse):
    if reverse:
        (xp_ref, xc_ref, xn_ref, dt_ref, cw_ref, cb_ref, dtb_ref, alog_ref, tri_ref, e_ref, sh_ref,
         z_ref, yf_ref, dsk_ref, nw_ref, out_ref, ht_ref) = refs
    else:
        (xp_ref, xc_ref, xn_ref, dt_ref, cw_ref, cb_ref, dtb_ref, alog_ref, tri_ref, e_ref, sh_ref,
         out_ref, ht_ref) = refs
    Q = SSD_CHUNK

    halo_zero = jnp.zeros((HALO, SSD_CONV_CH), BF16)
    ext = jnp.concatenate([jnp.where(pos > 0, xp_ref[s], halo_zero), xc_ref[s],
                           jnp.where(pos < nc - 1, xn_ref[s], halo_zero)], axis=0)
    shifted = _dot(sh_ref[...], ext)
    pad = (SSD_CONV - 1) // 2
    conv = cb_ref[...] + cw_ref[pad:pad + 1, :] * xc_ref[s].astype(F32)
    for n, k in enumerate(k for k in range(SSD_CONV) if k != pad):
        conv = conv + cw_ref[k:k + 1, :] * shifted[n * Q:(n + 1) * Q, :]
    act = _silu(conv)
    xs = act[:, :SSD_WIDTH]

    dtraw = dt_ref[s] + dtb_ref[...]
    dt = jnp.maximum(dtraw, 0.0) + jnp.log(1.0 + jnp.exp(-jnp.abs(dtraw)))
    a = -jnp.exp(alog_ref[...])
    adt = dt * a
    tri = tri_ref[...]
    cvec = _select_rows(tri, adt)
    cvt = cvec.T
    e01 = e_ref[...]
    dt_e = _select_lanes(dt, e01)
    cv_e = _select_lanes(cvec, e01)
    edge = 0 if reverse else Q - 1
    tot_e = cv_e[edge:edge + 1, :]

    li = lax.broadcasted_iota(jnp.int32, (Q, Q), 0)
    si = lax.broadcasted_iota(jnp.int32, (Q, Q), 1)
    mask = (li <= si) if reverse else (li >= si)
    lane = lax.broadcasted_iota(jnp.int32, (Q, LANES), 1)
    head0 = SSD_HEADS if reverse else 0

    xdt = xs * dt_e
    ys = []
    for g in range(SSD_NGROUPS):
        bg = act[:, SSD_WIDTH + g * SSD_DSTATE:SSD_WIDTH + (g + 1) * SSD_DSTATE]
        cg = act[:, SSD_WIDTH + (SSD_NGROUPS + g) * SSD_DSTATE:SSD_WIDTH + (SSD_NGROUPS + g + 1) * SSD_DSTATE]
        bgb = bg.astype(BF16)
        cgb = cg.astype(BF16)
        gm = _dot_nt(cgb, bgb)
        gs = slice(g * 256, (g + 1) * 256)
        xdt_g = xdt[:, gs]
        xdtb = xdt_g.astype(BF16)
        ydiag = []
        for pr in range(2):
            xpair = xdtb[:, pr * LANES:(pr + 1) * LANES]
            yh = []
            for j in range(2):
                hl = head0 + g * SSD_HPG + pr * 2 + j
                seg = cvec[:, hl:hl + 1] - cvt[hl:hl + 1, :]
                decay = jnp.exp(jnp.where(mask, seg, -jnp.inf))
                yh.append(_dot((gm * decay).astype(BF16), xpair))
            ydiag.append(jnp.where(lane < SSD_HEADDIM, yh[0], yh[1]))
        ydiag = jnp.concatenate(ydiag, axis=1)
        ht = ht_ref[s, g]
        yoff = _dot(cgb, ht.astype(BF16)) * jnp.exp(cv_e[:, gs])
        ys.append(ydiag + yoff)
        wdec = jnp.exp(tot_e[:, gs] - cv_e[:, gs])
        xw = (xdt_g * wdec).astype(BF16)
        ht_ref[s, g] = jnp.exp(tot_e[:, gs]) * ht + _dot(bg.T.astype(BF16), xw)
    y = jnp.concatenate(ys, axis=1)

    if not reverse:
        out_ref[s] = y
    else:
        y = y + yf_ref[s] + dsk_ref[...] * xs
        gated = y * _silu(z_ref[s].astype(F32))
        ms = jnp.mean(gated * gated, axis=-1, keepdims=True)
        out_ref[s] = (gated * lax.rsqrt(ms + EPS) * nw_ref[...]).astype(out_ref.dtype)


def _ssd(xbc, dt, z, conv_w, conv_b, dtb, alog, dskip_e, norm_w, B, L):
    Q = SSD_CHUNK
    nc = L // Q
    ns = SSD_SEQS if B % SSD_SEQS == 0 else 1
    hb = Q // HALO
    nhb = L // HALO
    r = jnp.arange(Q)
    pad = (SSD_CONV - 1) // 2
    col = jnp.arange(Q + 2 * HALO)[None, :]
    shift = jnp.concatenate([(col == HALO + r[:, None] + k - pad) for k in range(SSD_CONV) if k != pad],
                            axis=0).astype(BF16)
    twice = lambda m, axis: jnp.concatenate([m, m], axis=axis).astype(BF16)
    tril = twice(r[:, None] >= r[None, :], 1)
    triu = twice(r[:, None] <= r[None, :], 1)
    lane = jnp.arange(LANES)[:, None]
    ch = jnp.arange(SSD_WIDTH)[None, :] // SSD_HEADDIM
    e_f = twice(lane == ch, 0)
    e_b = twice(lane == ch + SSD_HEADS, 0)

    def call(reverse, extra_in, extra_specs, out_dtype):
        pos = (lambda c: nc - 1 - c) if reverse else (lambda c: c)
        full = lambda a: pl.BlockSpec(a.shape, lambda b, c: (0,) * a.ndim)
        chunk = lambda n: pl.BlockSpec((ns, Q, n), lambda b, c: (b, pos(c), 0))
        in_specs = [
            pl.BlockSpec((ns, HALO, 1024), lambda b, c: (b, jnp.maximum(pos(c) * hb - 1, 0), 0)),
            chunk(1024),
            pl.BlockSpec((ns, HALO, 1024), lambda b, c: (b, jnp.minimum((pos(c) + 1) * hb, nhb - 1), 0)),
            chunk(LANES),
        ]
        consts = [conv_w, conv_b, dtb, alog, triu if reverse else tril, e_b if reverse else e_f, shift]
        in_specs += [full(a) for a in consts]
        in_specs += [chunk(512) if s == "chunk" else full(a) for a, s in zip(extra_in, extra_specs)]
        return pl.pallas_call(
            functools.partial(_ssd_kernel, reverse=reverse),
            grid=(B // ns, nc),
            in_specs=in_specs,
            out_specs=chunk(512),
            out_shape=jax.ShapeDtypeStruct((B, L, 512), out_dtype),
            scratch_shapes=[pltpu.VMEM((ns, SSD_NGROUPS, SSD_DSTATE, 256), F32)],
            compiler_params=_cparams("arbitrary", "arbitrary"),
            name="ssd_bwd" if reverse else "ssd_fwd",
        )(xbc, xbc, xbc, dt, *consts, *extra_in)

    yf = call(False, [], [], F32)
    return call(True, [z, yf, dskip_e, norm_w], ["chunk", "chunk", "full", "full"], BF16)


N_POS = 3
N_FEAT = 2 * N_POS + 1
MAX_SHIFT_LOG2 = 50.0
VT_ROWS = DA_VDIM + 16
ZERO_PROB_LOG2 = 160.0
DEAD_SHIFT = 1.0e4


def _max_block_dist(head, tq, nb):
    slope_log2 = 2.0 ** (-8.0 * (head + 1) / DA_HEADS) * LOG2E
    d = 0
    while d < nb - 1 and slope_log2 * (d * tq + 1) < ZERO_PROB_LOG2:
        d += 1
    return d


def _attn_kernel(slope_ref, lamv_ref, q_ref, qall_ref, k_ref, v_ref, sw_ref, o_ref,
                 kaug_ref, kfeat_ref, qfeat_ref, corr_ref, vt_ref, vtb_ref, pt_ref, shift_ref, worst_ref,
                 *, tq, tk, L, nsub):
    h = pl.program_id(0)
    b = pl.program_id(1)
    gi = pl.program_id(2)
    nb = L // tq
    sl2 = slope_ref[h] * LOG2E
    lv = lamv_ref[...]
    lam = (jnp.exp(jnp.sum(lv[0:1] * lv[1:2], axis=-1, keepdims=True))
           - jnp.exp(jnp.sum(lv[2:3] * lv[3:4], axis=-1, keepdims=True)) + LAMBDA_INIT)

    def feat0(c):
        return DA_HEADDIM if c == 0 else 0

    def own_lanes(lane, c):
        return (lane < DA_HEADDIM) if c == 0 else (lane >= DA_HEADDIM)

    ha = lax.broadcasted_iota(jnp.int32, (LANES, LANES), 0) < DA_HEADDIM
    hb = lax.broadcasted_iota(jnp.int32, (LANES, LANES), 1) < DA_HEADDIM
    other_half = jnp.where(ha != hb, 1.0, 0.0).astype(BF16)

    @pl.when((b == 0) & (gi == 0))
    def _tables():
        lane = lax.broadcasted_iota(jnp.int32, (L, LANES), 1)
        kpos = lax.broadcasted_iota(jnp.int32, (L, LANES), 0).astype(F32) * sl2
        pieces = [t.astype(F32) for t in _split3(kpos)]
        for c in range(2):
            rel = lane - feat0(c)
            feat = jnp.where((rel >= 0) & (rel <= N_POS), 1.0, 0.0)
            for n in range(N_POS):
                feat = jnp.where(rel == N_POS + 1 + n, pieces[n], feat)
            kfeat_ref[c] = feat.astype(BF16)
            qfeat = jnp.where((rel > N_POS) & (rel < N_FEAT), -1.0, 0.0)
            for n in range(N_POS):
                qfeat = jnp.where(rel == n, pieces[n], qfeat)
            qfeat_ref[c] = qfeat.astype(BF16)
        kr = lax.broadcasted_iota(jnp.int32, (tq, tq), 0)
        qc = lax.broadcasted_iota(jnp.int32, (tq, tq), 1)
        corr_ref[...] = jnp.maximum(qc - kr, 0).astype(F32) * (2.0 * sl2)

    @pl.when(gi == 0)
    def _prep():
        kraw = k_ref[0]
        lane = lax.broadcasted_iota(jnp.int32, kraw.shape, 1)
        for c in range(2):
            kaug_ref[c] = jnp.where(own_lanes(lane, c), kraw, kfeat_ref[c])
        kf = kraw.astype(F32)
        kn2 =_dot((kf * kf).astype(BF16), other_half)
        kmax2 = jnp.max(kn2, axis=0, keepdims=True)
        qa = qall_ref[0].astype(F32)
        prod2 = _dot((qa * qa).astype(BF16), other_half) * kmax2
        bound = prod2 * lax.rsqrt(jnp.maximum(prod2, 1e-30))
        shift_all = (bound * 1.02).astype(BF16).astype(F32)
        shift_ref[...] = shift_all
        worst_ref[0] = jnp.max(shift_all)
        ones_row = lax.broadcasted_iota(jnp.int32, (VT_ROWS - DA_VDIM, tq), 0) == 0
        ones_rows = jnp.where(ones_row, 1.0, 0.0).astype(BF16)
        for jb in range(nb):
            vt = v_ref[0, jb * tq:(jb + 1) * tq, :].astype(F32).T.astype(BF16)
            vt_ref[0:DA_VDIM, jb * tq:(jb + 1) * tq] = vt
            vt_ref[DA_VDIM:, jb * tq:(jb + 1) * tq] = ones_rows
            vtb_ref[jb, 0:DA_VDIM, :] = vt
            vtb_ref[jb, DA_VDIM:, :] = ones_rows

    worst = worst_ref[0]
    lane = lax.broadcasted_iota(jnp.int32, (tq, LANES), 1)

    def sub_rows(sub):
        return slice(sub * tq, (sub + 1) * tq)

    def load_q(sub):
        return q_ref[0, sub_rows(sub), :].astype(F32)

    def finish_t(o1t, o2t, sub):
        finish((o1t - lam * o2t).T, None, sub)

    def finish(o1, o2, sub):
        o = o1 if o2 is None else o1 - lam * o2
        ms = jnp.mean(o * o, axis=-1, keepdims=True)
        o = o * lax.rsqrt(ms + EPS) * sw_ref[...] * (1.0 - LAMBDA_INIT)
        o_ref[0, sub_rows(sub), :] = o.astype(o_ref.dtype)

    def fast_path(max_dist, sub):
        qi = gi * nsub + sub
        rows = pl.ds(pl.multiple_of(qi * tq, tq), tq)
        qraw = q_ref[0, sub_rows(sub), :]
        neg_shift = (-shift_ref[rows, :]).astype(BF16)
        zero = jnp.zeros_like(qraw)
        outs = []
        for c in range(2):
            rel = lane - feat0(c)
            own = own_lanes(lane, c)
            base = jnp.where(own, qraw, jnp.where(rel == N_POS, neg_shift, zero))
            posf = qfeat_ref[c, rows, :]
            q_after = base + posf
            q_before = base - posf
            if max_dist is None:
                blocks = []
                for d in range(nb):
                    wrapped = qi + d >= nb
                    jb = jnp.where(wrapped, qi + d - nb, qi + d)
                    blocks.append((jb, q_after if d == 0 else jnp.where(wrapped, q_before, q_after), d == 0))
            else:
                q_dead = jnp.where(own, qraw, jnp.where(rel == N_POS, -DEAD_SHIFT, 0.0).astype(BF16))
                blocks = []
                for r in range(-max_dist, max_dist + 1):
                    jb = qi + r
                    side = q_before if r < 0 else q_after
                    inside = (jb >= 0) & (jb < nb)
                    blocks.append((jnp.clip(jb, 0, nb - 1), side if r == 0 else jnp.where(inside, side, q_dead),
                                   r == 0))
            acc = None
            for jb, qsel, diagonal in blocks:
                dk = pl.ds(pl.multiple_of(jb * tq, tq), tq)
                st = _dot_nt(kaug_ref[c, dk, :], qsel)
                if diagonal:
                    st = st - corr_ref[...]
                p = jnp.exp2(st).astype(BF16)
                if max_dist is None:
                    pt_ref[sub, c, dk, :] = p
                else:
                    part = _dot(vtb_ref[jb], p)
                    acc = part if acc is None else acc + part
            if max_dist is None:
                acc = _dot(vt_ref[...], pt_ref[sub, c])
            outs.append(acc[:DA_VDIM] * (1.0 / acc[DA_VDIM:DA_VDIM + 1]))
        finish_t(outs[0], outs[1], sub)

    fast = worst <= MAX_SHIFT_LOG2
    dists = [_max_block_dist(hh, tq, nb) for hh in range(DA_HEADS)]
    dists = [d if 2 * d + 1 < nb else None for d in dists]
    for dist in set(dists):
        heads = [hh for hh in range(DA_HEADS) if dists[hh] == dist]
        in_heads = functools.reduce(lambda a, b: a | b, [h == hh for hh in heads])

        @pl.when(fast & in_heads)
        def _(dist=dist):
            for sub in range(nsub):
                fast_path(dist, sub)

    def online_path(sub):
        qi = gi * nsub + sub
        qf = load_q(sub)
        zero = jnp.zeros_like(qf)
        qm = [jnp.where(own_lanes(lane, c), qf, zero).astype(BF16) for c in range(2)]
        qk0 =(qi * tq + lax.broadcasted_iota(jnp.int32, (tq, tk), 0)
               - lax.broadcasted_iota(jnp.int32, (tq, tk), 1)).astype(F32)

        def body(j, carry):
            kb = k_ref[0, pl.ds(pl.multiple_of(j * tk, tk), tk), :]
            vb = v_ref[0, pl.ds(pl.multiple_of(j * tk, tk), tk), :]
            bias = sl2 * jnp.abs(qk0 - lax.convert_element_type(j * tk, F32))
            out = []
            for c in range(2):
                m, l, acc = carry[c]
                s = _dot_nt(qm[c], kb) - bias
                m_new = jnp.maximum(m, jnp.max(s, axis=-1, keepdims=True))
                p = jnp.exp2(s - m_new)
                alpha = jnp.exp2(m - m_new)
                l_new = alpha * l + jnp.sum(p, axis=-1, keepdims=True)
                acc_new = alpha * acc + _dot(p.astype(BF16), vb)
                out.append((m_new, l_new, acc_new))
            return tuple(out)

        init = tuple((jnp.full((tq, 1), -jnp.inf, F32), jnp.zeros((tq, 1), F32),
                      jnp.zeros((tq, DA_VDIM), F32)) for _ in range(2))
        (m1, l1, a1), (m2, l2, a2) = lax.fori_loop(0, L // tk, body, init)
        finish(a1 * (1.0 / l1), a2 * (1.0 / l2), sub)

    @pl.when(jnp.logical_not(fast))
    def _():
        for sub in range(nsub):
            online_path(sub)


def _diff_attention(q, k, v, slopes, lamv, subln_w, B, L):
    tq, tk = min(TQ, L), min(TK, L)
    nsub = TQ_SUBS if L % (tq * TQ_SUBS) == 0 else 1
    return pl.pallas_call(
        functools.partial(_attn_kernel, tq=tq, tk=tk, L=L, nsub=nsub),
        grid=(DA_HEADS, B, L // (tq * nsub)),
        in_specs=[
            pl.BlockSpec(memory_space=pltpu.SMEM),
            pl.BlockSpec(lamv.shape, lambda h, b, i: (0, 0)),
            pl.BlockSpec((1, tq * nsub, LANES), lambda h, b, i: (b, i, h)),
            pl.BlockSpec((1, L, LANES), lambda h, b, i: (b, 0, h)),
            pl.BlockSpec((1, L, LANES), lambda h, b, i: (b, 0, h)),
            pl.BlockSpec((1, L, LANES), lambda h, b, i: (b, 0, h)),
            pl.BlockSpec(subln_w.shape, lambda h, b, i: (0, 0)),
        ],
        out_specs=pl.BlockSpec((1, tq * nsub, LANES), lambda h, b, i: (b, i, h)),
        out_shape=jax.ShapeDtypeStruct((B, L, 512), BF16),
        scratch_shapes=[pltpu.VMEM((2, L, LANES), BF16),
                        pltpu.VMEM((2, L, LANES), BF16),
                        pltpu.VMEM((2, L, LANES), BF16),
                        pltpu.VMEM((tq, tq), F32),
                        pltpu.VMEM((VT_ROWS, L), BF16),
                        pltpu.VMEM((L // tq, VT_ROWS, tq), BF16),
                        pltpu.VMEM((nsub, 2, L, tq), BF16),
                        pltpu.VMEM((L, LANES), F32),
                        pltpu.SMEM((1,), F32)],
        compiler_params=_cparams("arbitrary", "arbitrary", "arbitrary"),
        name="diff_attn",
    )(slopes, lamv, q, q, k, v, subln_w)


def _pair_tables():
    lo, hi = [], []
    for g in range(N_GROUPS):
        for a in range(EPG):
            for b in range(a + 1, EPG):
                lo.append(g * EPG + a)
                hi.append(g * EPG + b)
    return lo, hi


PAIRS_PER_GROUP = EPG * (EPG - 1) // 2
N_BUCKETS = N_GROUPS * PAIRS_PER_GROUP
GATE_LANE = 0
ROW_TAIL = LANES


def _router_kernel(x_ref, ys_ref, ya_ref, wo_ref, nw_ref, wr_ref, br_ref, tri_ref,
                   h_ref, u_ref, meta_ref, cnt_ref, carry_ref):
    i = pl.program_id(0)

    @pl.when(i == 0)
    def _():
        carry_ref[...] = jnp.zeros_like(carry_ref)

    sub = tri_ref.shape[0]
    for s in range(x_ref.shape[0] // sub):
        rows = pl.ds(s * sub, sub)
        _route_tile(x_ref.at[rows], ys_ref.at[rows], ya_ref.at[rows], wo_ref, nw_ref, wr_ref, br_ref, tri_ref,
                    h_ref.at[rows], u_ref.at[rows], meta_ref.at[rows], cnt_ref, carry_ref)


def _route_tile(x_ref, ys_ref, ya_ref, wo_ref, nw_ref, wr_ref, br_ref, tri_ref,
                h_ref, u_ref, meta_ref, cnt_ref, carry_ref):
    h = x_ref[...] + _dot(ys_ref[...], wo_ref[0:512, :]) + _dot(ya_ref[...], wo_ref[512:1024, :])
    h_ref[...] = h
    D = h.shape[1]
    ms = jnp.mean(h * h, axis=-1, keepdims=True)
    u = h * lax.rsqrt(ms + EPS) * nw_ref[...]
    u_ref[:, 0:D // 2] = _pack_bf16_pairs(u)
    logits = _dot(u.astype(BF16), wr_ref[...]) + br_ref[...]

    tm = logits.shape[0]
    li = lax.broadcasted_iota(jnp.int32, (tm, LANES), 1)
    lf = li.astype(F32)
    ninf = jnp.float32(-jnp.inf)
    big = jnp.float32(1e9)
    gl = jnp.where(li < N_GROUPS, logits, ninf)
    gmax = jnp.max(gl, axis=-1, keepdims=True)
    gidx = jnp.min(jnp.where(gl == gmax, lf, big), axis=-1, keepdims=True)
    pg = 1.0 / jnp.sum(jnp.exp(gl - gmax), axis=-1, keepdims=True)
    lane_grp = ((li - EXP_LANE0) >> 3).astype(F32)
    in_grp = (li >= EXP_LANE0) & (li < EXP_LANE0 + N_EXPERTS) & (lane_grp == gidx)
    sl = jnp.where(in_grp, logits, ninf)
    m1 = jnp.max(sl, axis=-1, keepdims=True)
    i1 = jnp.min(jnp.where(sl == m1, lf, big), axis=-1, keepdims=True)
    sl2 = jnp.where(lf == i1, ninf, sl)
    m2 = jnp.max(sl2, axis=-1, keepdims=True)
    i2 = jnp.min(jnp.where(sl2 == m2, lf, big), axis=-1, keepdims=True)
    t = jnp.exp(m2 - m1)
    w1 = 1.0 / (1.0 + t)
    g1 = pg * w1
    g2 = pg * (t * w1)
    a1 = i1 - EXP_LANE0 - gidx * EPG
    a2 = i2 - EXP_LANE0 - gidx * EPG
    lo = jnp.minimum(a1, a2)
    hi = jnp.maximum(a1, a2)
    bucket = gidx * PAIRS_PER_GROUP + (lo * (EPG - 1) - lo * (lo - 1.0) * 0.5) + (hi - lo - 1.0)
    first_is_lo = a1 < a2
    g_lo = jnp.where(first_is_lo, g1, g2)
    g_hi = jnp.where(first_is_lo, g2, g1)
    ohb = lf == bucket
    oh = jnp.where(ohb, 1.0, 0.0)
    prefix = _dot(tri_ref[...], oh.astype(BF16)) + carry_ref[0:1, :]
    rank = jnp.sum(jnp.where(ohb, prefix, 0.0), axis=-1, keepdims=True)
    new_carry = carry_ref[0:1, :] + jnp.sum(oh, axis=0, keepdims=True)
    carry_ref[...] = jnp.broadcast_to(new_carry, carry_ref.shape)
    cnt_ref[...] = jnp.broadcast_to(new_carry, cnt_ref.shape)

    meta_ref[...] = jnp.where(li == 0, bucket, jnp.where(li == 1, rank, 0.0))
    gates = jnp.where(li == GATE_LANE, g_lo, jnp.where(li == GATE_LANE + 1, g_hi, 0.0))
    u_ref[:, D // 2:] = pltpu.bitcast(gates, jnp.uint32)


def _out_router(x2, y_ssd, y_da, w_out, norm_w, w_r, b_r):
    T, D = x2.shape
    tm = min(TM_ROUTE, T)
    r = jnp.arange(min(ROUTE_SUB, tm))
    tri = (r[:, None] > r[None, :]).astype(BF16)
    row = lambda n: pl.BlockSpec((tm, n), lambda i: (i, 0))
    full = lambda a: pl.BlockSpec(a.shape, lambda i: (0,) * a.ndim)
    return pl.pallas_call(
        _router_kernel,
        grid=(T // tm,),
        in_specs=[row(D), row(512), row(512), full(w_out), full(norm_w), full(w_r), full(b_r), full(tri)],
        out_specs=[row(D), row(D // 2 + ROW_TAIL), row(LANES), pl.BlockSpec((SUBLANES, LANES), lambda i: (0, 0))],
        out_shape=[
            jax.ShapeDtypeStruct((T, D), F32),
            jax.ShapeDtypeStruct((T, D // 2 + ROW_TAIL), jnp.uint32),
            jax.ShapeDtypeStruct((T, LANES), F32),
            jax.ShapeDtypeStruct((SUBLANES, LANES), F32),
        ],
        scratch_shapes=[pltpu.VMEM((SUBLANES, LANES), F32)],
        compiler_params=_cparams("arbitrary"),
        name="out_router",
    )(x2, y_ssd, y_da, w_out, norm_w, w_r, b_r, tri)


def _dispatch_kernel(cnt_ref, idx_ref, u_ref, xs_hbm, slots_ref, ea_ref, eb_ref, nused_ref,
                     pstart_ref, zero_ref, sem, zsem, *, tm, n_blocks):
    i = pl.program_id(0)

    @pl.when(i == 0)
    def _():
        zero_ref[...] = jnp.zeros_like(zero_ref)
        pair_lo, pair_hi = _pair_tables()

        def zero_block(first_row):
            return pltpu.make_async_copy(
                zero_ref, xs_hbm.at[pl.ds(pl.multiple_of(first_row, MOE_ROWS), MOE_ROWS)], zsem)

        acc = jnp.int32(0)
        nfill = jnp.int32(0)
        for q in range(N_BUCKETS):
            padded = ((cnt_ref[q] + (MOE_ROWS - 1)) // MOE_ROWS) * MOE_ROWS
            pstart_ref[q] = acc
            end = acc + padded

            def fill(j, _, q=q):
                ea_ref[j] = jnp.int32(pair_lo[q])
                eb_ref[j] = jnp.int32(pair_hi[q])
                return 0

            lax.fori_loop(acc // MOE_ROWS, end // MOE_ROWS, fill, 0)

            @pl.when(padded > 0)
            def _(end=end):
                zero_block(end - MOE_ROWS).start()

            nfill = nfill + jnp.where(padded > 0, 1, 0)
            acc = end
        nused = acc // MOE_ROWS
        nused_ref[0] = nused

        def fill_tail(j, _):
            ea_ref[j] = jnp.int32(0)
            eb_ref[j] = jnp.int32(0)
            zero_block(j * MOE_ROWS).start()
            return 0

        lax.fori_loop(nused, n_blocks, fill_tail, 0)

        def wait_fill(_, carry):
            zero_block(0).wait()
            return carry

        lax.fori_loop(0, nfill + (n_blocks - nused), wait_fill, 0)

    def start(t, _):
        slot = pstart_ref[idx_ref[0, t]] + idx_ref[1, t]
        slots_ref[0, t] = slot
        pltpu.make_async_copy(u_ref.at[pl.ds(t, 1)], xs_hbm.at[pl.ds(slot, 1)], sem).start()
        return 0

    lax.fori_loop(0, tm, start, 0, unroll=DMA_UNROLL)
    pltpu.make_async_copy(u_ref, u_ref, sem).wait()


def _dispatch(counts, idx, u, n_blocks):
    T, W = u.shape
    tm = min(TM_DISP, T)
    n_slots = n_blocks * MOE_ROWS
    grid_spec = pltpu.PrefetchScalarGridSpec(
        num_scalar_prefetch=1,
        grid=(T // tm,),
        in_specs=[
            pl.BlockSpec((2, tm), lambda i, cnt: (0, i), memory_space=pltpu.SMEM),
            pl.BlockSpec((tm, W), lambda i, cnt: (i, 0)),
        ],
        out_specs=[
            pl.BlockSpec(memory_space=pl.ANY),
            pl.BlockSpec((1, tm), lambda i, cnt: (0, i), memory_space=pltpu.SMEM),
            pl.BlockSpec(memory_space=pltpu.SMEM),
            pl.BlockSpec(memory_space=pltpu.SMEM),
            pl.BlockSpec(memory_space=pltpu.SMEM),
        ],
        scratch_shapes=[pltpu.SMEM((LANES,), jnp.int32),
                        pltpu.VMEM((MOE_ROWS, W), u.dtype),
                        pltpu.SemaphoreType.DMA(()),
                        pltpu.SemaphoreType.DMA(())],
    )
    return pl.pallas_call(
        functools.partial(_dispatch_kernel, tm=tm, n_blocks=n_blocks),
        grid_spec=grid_spec,
        out_shape=[
            jax.ShapeDtypeStruct((n_slots, W), u.dtype),
            jax.ShapeDtypeStruct((1, T), jnp.int32),
            jax.ShapeDtypeStruct((n_blocks,), jnp.int32),
            jax.ShapeDtypeStruct((n_blocks,), jnp.int32),
            jax.ShapeDtypeStruct((1,), jnp.int32),
        ],
        compiler_params=_cparams("arbitrary"),
        name="dispatch",
    )(counts, idx, u)


def _moe_kernel(ea_ref, eb_ref, nused_ref, xs_ref, wga_ref, wua_ref, wda_ref, wgb_ref, wub_ref, wdb_ref, o_ref):
    j = pl.program_id(0)
    H = o_ref.shape[1]

    @pl.when(j < nused_ref[0])
    def _():
        x = _unpack_bf16_pairs(xs_ref[:, 0:H]).astype(BF16)
        tail = pltpu.bitcast(xs_ref[:, H:], F32)
        out = None
        for (wg, wu, wd), lane in (((wga_ref, wua_ref, wda_ref), GATE_LANE),
                                   ((wgb_ref, wub_ref, wdb_ref), GATE_LANE + 1)):
            hid = _silu(_dot(x, wg[0])) * _dot(x, wu[0])
            y = _dot(hid.astype(BF16), wd[0]) * tail[:, lane:lane + 1]
            out = y if out is None else out + y
        o_ref[...] = _pack_bf16_pairs(out)

    @pl.when(j >= nused_ref[0])
    def _():
        o_ref[...] = jnp.zeros_like(o_ref)


def _moe(block_ea, block_eb, nused, xs, wg, wu, wd, n_blocks):
    W = xs.shape[1]
    H = W - ROW_TAIL
    used = lambda j, nu: jnp.minimum(j, nu[0] - 1)
    wspec_a = lambda a: pl.BlockSpec((1,) + a.shape[1:], lambda j, ea, eb, nu: (ea[used(j, nu)], 0, 0))
    wspec_b = lambda a: pl.BlockSpec((1,) + a.shape[1:], lambda j, ea, eb, nu: (eb[used(j, nu)], 0, 0))
    grid_spec = pltpu.PrefetchScalarGridSpec(
        num_scalar_prefetch=3,
        grid=(n_blocks,),
        in_specs=[
            pl.BlockSpec((MOE_ROWS, W), lambda j, ea, eb, nu: (used(j, nu), 0)),
            wspec_a(wg), wspec_a(wu), wspec_a(wd), wspec_b(wg), wspec_b(wu), wspec_b(wd),
        ],
        out_specs=pl.BlockSpec((MOE_ROWS, H), lambda j, ea, eb, nu: (j, 0)),
    )
    return pl.pallas_call(
        _moe_kernel,
        grid_spec=grid_spec,
        out_shape=jax.ShapeDtypeStruct((n_blocks * MOE_ROWS, H), jnp.uint32),
        compiler_params=_cparams("arbitrary"),
        name="moe_mlp",
    )(block_ea, block_eb, nused, xs, wg, wu, wd, wg, wu, wd)


def _combine_kernel(slots_ref, next_slots_ref, h_ref, nw_ref, eo_hbm, o_ref, g_ref, sems, *, tm):
    i = pl.program_id(0)
    n = pl.num_programs(0)
    cur = i % 2

    def gather_tile(sref, buf):
        def start(t, _):
            pltpu.make_async_copy(eo_hbm.at[pl.ds(sref[0, t], 1)],
                                  g_ref.at[buf, pl.ds(t, 1)], sems.at[buf]).start()
            return 0

        lax.fori_loop(0, tm, start, 0, unroll=DMA_UNROLL)

    @pl.when(i == 0)
    def _():
        gather_tile(slots_ref, 0)

    @pl.when(i + 1 < n)
    def _():
        gather_tile(next_slots_ref, 1 - cur)

    pltpu.make_async_copy(g_ref.at[cur], g_ref.at[cur], sems.at[cur]).wait()

    y = h_ref[...] + _unpack_bf16_pairs(g_ref[cur])
    ms = jnp.mean(y * y, axis=-1, keepdims=True)
    o_ref[...] = y * lax.rsqrt(ms + EPS) * nw_ref[...]


def _combine(slots, h, norm_w, eo):
    T, D = h.shape
    tm = min(TM_COMB, T)
    last = T // tm - 1
    return pl.pallas_call(
        functools.partial(_combine_kernel, tm=tm),
        grid=(T // tm,),
        in_specs=[
            pl.BlockSpec((1, tm), lambda i: (0, i), memory_space=pltpu.SMEM),
            pl.BlockSpec((1, tm), lambda i: (0, jnp.minimum(i + 1, last)), memory_space=pltpu.SMEM),
            pl.BlockSpec((tm, D), lambda i: (i, 0)),
            pl.BlockSpec(norm_w.shape, lambda i: (0, 0)),
            pl.BlockSpec(memory_space=pl.ANY),
        ],
        out_specs=pl.BlockSpec((tm, D), lambda i: (i, 0)),
        out_shape=jax.ShapeDtypeStruct((T, D), F32),
        scratch_shapes=[pltpu.VMEM((2, tm, D // 2), jnp.uint32),
                        pltpu.SemaphoreType.DMA((2,))],
        compiler_params=_cparams("arbitrary"),
        name="combine",
    )(slots, slots, h, norm_w, eo)


def _pad_lanes(v, offset=0):
    v = v.astype(F32).reshape(-1)
    return jnp.zeros((1, LANES), F32).at[0, offset:offset + v.shape[0]].set(v)


def kernel(x, norm_mix_w, w_in, conv_w, conv_b, dt_bias_fwd, dt_bias_bwd, a_log_fwd, a_log_bwd, ssd_d,
           ssd_norm_w, lambda_q1, lambda_k1, lambda_q2, lambda_k2, subln_w, w_out, norm_ffn_w,
           w_router_group, b_router_group, w_router_exp, b_router_exp, w_exp_gate, w_exp_up, w_exp_down,
           norm_final_w):
    B, L, D = x.shape
    T = B * L
    x2 = x.reshape(T, D)
    l = 0

    w = w_in[l]
    o_z, o_xbc, o_dt = 512, 512 + 1024, 512 + 1024 + 16
    o_q, o_k = o_dt + 512, o_dt + 1024
    w_main = jnp.concatenate(
        [w[:, :o_xbc], w[:, o_dt:o_q] * (DA_HEADDIM ** -0.5 * LOG2E), w[:, o_q:]], axis=1).astype(BF16)
    w_dt = jnp.zeros((D, LANES), F32).at[:, :16].set(w[:, o_xbc:o_dt]).astype(BF16)

    z, xbc, dt, q, k, v = _in_proj(x2, norm_mix_w[l].reshape(1, D), w_main, w_dt)

    dtb = _pad_lanes(jnp.concatenate([dt_bias_fwd[l], dt_bias_bwd[l]]))
    alog = _pad_lanes(jnp.concatenate([a_log_fwd[l], a_log_bwd[l]]))
    dskip_e = jnp.repeat(ssd_d[l].astype(F32), SSD_HEADDIM).reshape(1, SSD_WIDTH)
    y_ssd = _ssd(xbc.reshape(B, L, 1024), dt.reshape(B, L, LANES), z.reshape(B, L, 512),
                 conv_w[l].astype(F32), conv_b[l].reshape(1, -1).astype(F32), dtb, alog, dskip_e,
                 ssd_norm_w[l].reshape(1, -1).astype(F32), B, L)

    slopes = jnp.power(2.0, -8.0 * jnp.arange(1, DA_HEADS + 1, dtype=F32) / DA_HEADS)
    lamv = jnp.concatenate([_pad_lanes(lambda_q1[l]), _pad_lanes(lambda_k1[l]),
                            _pad_lanes(lambda_q2[l]), _pad_lanes(lambda_k2[l]),
                            jnp.zeros((4, LANES), F32)], axis=0)
    y_da = _diff_attention(q.reshape(B, L, 512), k.reshape(B, L, 512), v.reshape(B, L, 512),
                           slopes, lamv, subln_w[l].reshape(1, -1).astype(F32), B, L)

    w_r = (jnp.zeros((D, LANES), F32).at[:, :N_GROUPS].set(w_router_group[l])
           .at[:, EXP_LANE0:EXP_LANE0 + N_EXPERTS].set(w_router_exp[l])).astype(BF16)
    b_r = _pad_lanes(jnp.concatenate([b_router_group[l], b_router_exp[l]]))
    h, u, meta, cnt = _out_router(x2, y_ssd.reshape(T, 512), y_da.reshape(T, 512), w_out[l].astype(BF16),
                                  norm_ffn_w[l].reshape(1, D), w_r, b_r)

    n_blocks = T // MOE_ROWS + N_BUCKETS
    counts = cnt[0].astype(jnp.int32)
    idx = meta[:, :2].astype(jnp.int32).T
    xs, slots, block_ea, block_eb, nused = _dispatch(counts, idx, u, n_blocks)
    eo = _moe(block_ea, block_eb, nused, xs, w_exp_gate[l].astype(BF16), w_exp_up[l].astype(BF16),
              w_exp_down[l].astype(BF16), n_blocks)
    out = _combine(slots, h, norm_final_w.reshape(1, D), eo)
    return out.reshape(B, L, D)
```

```python
import functools
import math

import jax
import jax.numpy as jnp
from jax import lax
from jax.experimental import pallas as pl
from jax.experimental.pallas import tpu as pltpu

F32 = jnp.float32
BF16 = jnp.bfloat16

EPS = 1e-6
LOG2E = math.log2(math.e)
LANES = 128
SUBLANES = 8
HALO = 16
VMEM_LIMIT = 48 * 1024 * 1024

SSD_WIDTH = 512
SSD_HEADDIM = 64
SSD_HEADS = 8
SSD_NGROUPS = 2
SSD_HPG = 4
SSD_DSTATE = 128
SSD_CONV = 5
SSD_CHUNK = 128
SSD_CONV_CH = 1024
DA_HEADDIM = 64
DA_VDIM = 128
DA_HEADS = 4
N_GROUPS = 4
EPG = 8
N_EXPERTS = 32
D_EXPERT = 512
LAMBDA_INIT = 0.8 - 0.6 * math.exp(-0.3 * 0)

TM_PROJ = 1024
PROJ_SUB = 512
SSD_SEQS = 4
TQ = 512
TQ_SUBS = 2
TK = 512
TM_ROUTE = 1024
ROUTE_SUB = 512
TM_DISP = 1024
MOE_ROWS = 256
TM_COMB = 512
EXP_LANE0 = 4


def _cparams(*sem):
    return pltpu.CompilerParams(dimension_semantics=sem, vmem_limit_bytes=VMEM_LIMIT)


def _dot(a, b):
    return jnp.dot(a, b, preferred_element_type=F32)


def _dot_nt(a, b):
    return lax.dot_general(a, b, (((1,), (1,)), ((), ())), preferred_element_type=F32)


def _split3(v):
    hi = v.astype(BF16)
    r1 = v - hi.astype(F32)
    mid = r1.astype(BF16)
    lo = (r1 - mid.astype(F32)).astype(BF16)
    return hi, mid, lo


def _split2(v):
    hi = v.astype(BF16)
    return hi, (v - hi.astype(F32)).astype(BF16)


def _select_rows(a01x2, v):
    hi, lo = _split2(v)
    return _dot(a01x2, jnp.concatenate([hi, lo], axis=0))


def _select_lanes(v, b01x2):
    hi, lo = _split2(v)
    return _dot(jnp.concatenate([hi, lo], axis=1), b01x2)


def _silu(x):
    return x * (1.0 / (1.0 + jnp.exp(-x)))


def _pack_bf16_pairs(x):
    n = x.shape[1] // 2
    bits = pltpu.bitcast(x.astype(BF16).astype(F32), jnp.uint32)
    return (bits[:, :n] >> 16) | bits[:, n:]


def _unpack_bf16_pairs(w):
    lo = pltpu.bitcast(w << 16, F32)
    hi = pltpu.bitcast(w & jnp.uint32(0xFFFF0000), F32)
    return jnp.concatenate([lo, hi], axis=1)


def _inproj_kernel(x_ref, nw_ref, w_ref, wdt_ref, *refs, n_cast):
    cast_in, out_refs, cast_out = refs[:n_cast], refs[n_cast:len(refs) - n_cast], refs[len(refs) - n_cast:]
    for s in range(x_ref.shape[0] // PROJ_SUB):
        rows = pl.ds(s * PROJ_SUB, PROJ_SUB)
        _inproj_rows(x_ref.at[rows], nw_ref, w_ref, wdt_ref, *[r.at[rows] for r in out_refs])
    for src, dst in zip(cast_in, cast_out):
        dst[...] = src[...].astype(BF16)


def _inproj_rows(x_ref, nw_ref, w_ref, wdt_ref, z_ref, xbc_ref, dt_ref, q_ref, k_ref, v_ref):
    x = x_ref[...]
    ms = jnp.mean(x * x, axis=-1, keepdims=True)
    u = (x * lax.rsqrt(ms + EPS) * nw_ref[...]).astype(BF16)
    z_ref[...] = _dot(u, w_ref[:, 0:512]).astype(BF16)
    xbc_ref[...] = _dot(u, w_ref[:, 512:1536]).astype(BF16)
    q_ref[...] = _dot(u, w_ref[:, 1536:2048]).astype(BF16)
    k_ref[...] = _dot(u, w_ref[:, 2048:2560]).astype(BF16)
    v_ref[...] = _dot(u, w_ref[:, 2560:3072]).astype(BF16)
    dt_ref[...] = _dot(u, wdt_ref[...])


def _in_proj(x2, norm_w, w_main, w_dt, to_bf16):
    T, D = x2.shape
    tm = min(TM_PROJ, T)
    steps = T // tm
    row = lambda n: pl.BlockSpec((tm, n), lambda i: (i, 0))
    full = lambda a: pl.BlockSpec(a.shape, lambda i: (0,) * a.ndim)
    piece = lambda a: pl.BlockSpec((a.shape[0] // steps, a.shape[1]), lambda i: (i, 0))
    return pl.pallas_call(
        functools.partial(_inproj_kernel, n_cast=len(to_bf16)),
        grid=(steps,),
        in_specs=[row(D), full(norm_w), full(w_main), full(w_dt)] + [piece(a) for a in to_bf16],
        out_specs=[row(512), row(1024), row(LANES), row(512), row(512), row(512)] + [piece(a) for a in to_bf16],
        out_shape=[
            jax.ShapeDtypeStruct((T, 512), BF16),
            jax.ShapeDtypeStruct((T, 1024), BF16),
            jax.ShapeDtypeStruct((T, LANES), F32),
            jax.ShapeDtypeStruct((T, 512), BF16),
            jax.ShapeDtypeStruct((T, 512), BF16),
            jax.ShapeDtypeStruct((T, 512), BF16),
        ] + [jax.ShapeDtypeStruct(a.shape, BF16) for a in to_bf16],
        compiler_params=_cparams("arbitrary"),
        name="in_proj",
    )(x2, norm_w, w_main, w_dt, *to_bf16)


def _ssd_kernel(*refs, reverse):
    c = pl.program_id(1)
    nc = pl.num_programs(1)
    pos = (nc - 1 - c) if reverse else c
    ht_ref = refs[-1]

    @pl.when(c == 0)
    def _():
        ht_ref[...] = jnp.zeros_like(ht_ref)

    for s in range(ht_ref.shape[0]):
        _ssd_sequence(s, pos, nc, refs, reverse)


def _ssd_sequence(s, pos, nc, refs, reverse):
    if reverse:
        (xp_ref, xc_ref, xn_ref, dt_ref, cw_ref, cb_ref, dtb_ref, alog_ref, tri_ref, e_ref, sh_ref,
         z_ref, yf_ref, dsk_ref, nw_ref, out_ref, ht_ref) = refs
    else:
        (xp_ref, xc_ref, xn_ref, dt_ref, cw_ref, cb_ref, dtb_ref, alog_ref, tri_ref, e_ref, sh_ref,
         out_ref, ht_ref) = refs
    Q = SSD_CHUNK

    halo_zero = jnp.zeros((HALO, SSD_CONV_CH), BF16)
    ext = jnp.concatenate([jnp.where(pos > 0, xp_ref[s], halo_zero), xc_ref[s],
                           jnp.where(pos < nc - 1, xn_ref[s], halo_zero)], axis=0)
    shifted = _dot(sh_ref[...], ext)
    pad = (SSD_CONV - 1) // 2
    conv = cb_ref[...] + cw_ref[pad:pad + 1, :] * xc_ref[s].astype(F32)
    for n, k in enumerate(k for k in range(SSD_CONV) if k != pad):
        conv = conv + cw_ref[k:k + 1, :] * shifted[n * Q:(n + 1) * Q, :]
    act = _silu(conv)
    xs = act[:, :SSD_WIDTH]

    dtraw = dt_ref[s] + dtb_ref[...]
    dt = jnp.maximum(dtraw, 0.0) + jnp.log(1.0 + jnp.exp(-jnp.abs(dtraw)))
    a = -jnp.exp(alog_ref[...])
    adt = dt * a
    tri = tri_ref[...]
    cvec = _select_rows(tri, adt)
    cvt = cvec.T
    e01 = e_ref[...]
    dt_e = _select_lanes(dt, e01)
    cv_e = _select_lanes(cvec, e01)
    edge = 0 if reverse else Q - 1
    tot_e = cv_e[edge:edge + 1, :]

    li = lax.broadcasted_iota(jnp.int32, (Q, Q), 0)
    si = lax.broadcasted_iota(jnp.int32, (Q, Q), 1)
    mask = (li <= si) if reverse else (li >= si)
    lane = lax.broadcasted_iota(jnp.int32, (Q, LANES), 1)
    head0 = SSD_HEADS if reverse else 0

    xdt = xs * dt_e
    ys = []
    for g in range(SSD_NGROUPS):
        bg = act[:, SSD_WIDTH + g * SSD_DSTATE:SSD_WIDTH + (g + 1) * SSD_DSTATE]
        cg = act[:, SSD_WIDTH + (SSD_NGROUPS + g) * SSD_DSTATE:SSD_WIDTH + (SSD_NGROUPS + g + 1) * SSD_DSTATE]
        bgb = bg.astype(BF16)
        cgb = cg.astype(BF16)
        gm = _dot_nt(cgb, bgb)
        gs = slice(g * 256, (g + 1) * 256)
        xdt_g = xdt[:, gs]
        xdtb = xdt_g.astype(BF16)
        ydiag = []
        for pr in range(2):
            xpair = xdtb[:, pr * LANES:(pr + 1) * LANES]
            yh = []
            for j in range(2):
                hl = head0 + g * SSD_HPG + pr * 2 + j
                seg = cvec[:, hl:hl + 1] - cvt[hl:hl + 1, :]
                decay = jnp.exp(jnp.where(mask, seg, -jnp.inf))
                yh.append(_dot((gm * decay).astype(BF16), xpair))
            ydiag.append(jnp.where(lane < SSD_HEADDIM, yh[0], yh[1]))
        ydiag = jnp.concatenate(ydiag, axis=1)
        ht = ht_ref[s, g]
        yoff = _dot(cgb, ht.astype(BF16)) * jnp.exp(cv_e[:, gs])
        ys.append(ydiag + yoff)
        wdec = jnp.exp(tot_e[:, gs] - cv_e[:, gs])
        xw = (xdt_g * wdec).astype(BF16)
        ht_ref[s, g] = jnp.exp(tot_e[:, gs]) * ht + _dot(bg.T.astype(BF16), xw)
    y = jnp.concatenate(ys, axis=1)

    if not reverse:
        out_ref[s] = y
    else:
        y = y + yf_ref[s] + dsk_ref[...] * xs
        gated = y * _silu(z_ref[s].astype(F32))
        ms = jnp.mean(gated * gated, axis=-1, keepdims=True)
        out_ref[s] = (gated * lax.rsqrt(ms + EPS) * nw_ref[...]).astype(out_ref.dtype)


def _ssd(xbc, dt, z, conv_w, conv_b, dtb, alog, dskip_e, norm_w, B, L):
    Q = SSD_CHUNK
    nc = L // Q
    ns = SSD_SEQS if B % SSD_SEQS == 0 else 1
    hb = Q // HALO
    nhb = L // HALO
    r = jnp.arange(Q)
    pad = (SSD_CONV - 1) // 2
    col = jnp.arange(Q + 2 * HALO)[None, :]
    shift = jnp.concatenate([(col == HALO + r[:, None] + k - pad) for k in range(SSD_CONV) if k != pad],
                            axis=0).astype(BF16)
    twice = lambda m, axis: jnp.concatenate([m, m], axis=axis).astype(BF16)
    tril = twice(r[:, None] >= r[None, :], 1)
    triu = twice(r[:, None] <= r[None, :], 1)
    lane = jnp.arange(LANES)[:, None]
    ch = jnp.arange(SSD_WIDTH)[None, :] // SSD_HEADDIM
    e_f = twice(lane == ch, 0)
    e_b = twice(lane == ch + SSD_HEADS, 0)

    def call(reverse, extra_in, extra_specs, out_dtype):
        pos = (lambda c: nc - 1 - c) if reverse else (lambda c: c)
        full = lambda a: pl.BlockSpec(a.shape, lambda b, c: (0,) * a.ndim)
        chunk = lambda n: pl.BlockSpec((ns, Q, n), lambda b, c: (b, pos(c), 0))
        in_specs = [
            pl.BlockSpec((ns, HALO, 1024), lambda b, c: (b, jnp.maximum(pos(c) * hb - 1, 0), 0)),
            chunk(1024),
            pl.BlockSpec((ns, HALO, 1024), lambda b, c: (b, jnp.minimum((pos(c) + 1) * hb, nhb - 1), 0)),
            chunk(LANES),
        ]
        consts = [conv_w, conv_b, dtb, alog, triu if reverse else tril, e_b if reverse else e_f, shift]
        in_specs += [full(a) for a in consts]
        in_specs += [chunk(512) if s == "chunk" else full(a) for a, s in zip(extra_in, extra_specs)]
        return pl.pallas_call(
            functools.partial(_ssd_kernel, reverse=reverse),
            grid=(B // ns, nc),
            in_specs=in_specs,
            out_specs=chunk(512),
            out_shape=jax.ShapeDtypeStruct((B, L, 512), out_dtype),
            scratch_shapes=[pltpu.VMEM((ns, SSD_NGROUPS, SSD_DSTATE, 256), F32)],
            compiler_params=_cparams("arbitrary", "arbitrary"),
            name="ssd_bwd" if reverse else "ssd_fwd",
        )(xbc, xbc, xbc, dt, *consts, *extra_in)

    yf = call(False, [], [], F32)
    return call(True, [z, yf, dskip_e, norm_w], ["chunk", "chunk", "full", "full"], BF16)


N_POS = 3
N_FEAT = 2 * N_POS + 1
MAX_SHIFT_LOG2 = 50.0
VT_ROWS = DA_VDIM + 16
ZERO_PROB_LOG2 = 160.0
DEAD_SHIFT = 1.0e4


def _max_block_dist(head, tq, nb):
    slope_log2 = 2.0 ** (-8.0 * (head + 1) / DA_HEADS) * LOG2E
    d = 0
    while d < nb - 1 and slope_log2 * (d * tq + 1) < ZERO_PROB_LOG2:
        d += 1
    return d


def _attn_kernel(slope_ref, lamv_ref, q_ref, qall_ref, k_ref, v_ref, sw_ref, o_ref,
                 kaug_ref, kfeat_ref, qfeat_ref, corr_ref, vt_ref, vtb_ref, pt_ref, shift_ref, worst_ref,
                 *, tq, tk, L, nsub):
    h = pl.program_id(0)
    b = pl.program_id(1)
    gi = pl.program_id(2)
    nb = L // tq
    sl2 = slope_ref[h] * LOG2E
    lv = lamv_ref[...]
    lam = (jnp.exp(jnp.sum(lv[0:1] * lv[1:2], axis=-1, keepdims=True))
           - jnp.exp(jnp.sum(lv[2:3] * lv[3:4], axis=-1, keepdims=True)) + LAMBDA_INIT)

    def feat0(c):
        return DA_HEADDIM if c == 0 else 0

    def own_lanes(lane, c):
        return (lane < DA_HEADDIM) if c == 0 else (lane >= DA_HEADDIM)

    ha = lax.broadcasted_iota(jnp.int32, (LANES, LANES), 0) < DA_HEADDIM
    hb = lax.broadcasted_iota(jnp.int32, (LANES, LANES), 1) < DA_HEADDIM
    other_half = jnp.where(ha != hb, 1.0, 0.0).astype(BF16)

    @pl.when((b == 0) & (gi == 0))
    def _tables():
        lane = lax.broadcasted_iota(jnp.int32, (L, LANES), 1)
        kpos = lax.broadcasted_iota(jnp.int32, (L, LANES), 0).astype(F32) * sl2
        pieces = [t.astype(F32) for t in _split3(kpos)]
        for c in range(2):
            rel = lane - feat0(c)
            feat = jnp.where((rel >= 0) & (rel <= N_POS), 1.0, 0.0)
            for n in range(N_POS):
                feat = jnp.where(rel == N_POS + 1 + n, pieces[n], feat)
            kfeat_ref[c] = feat.astype(BF16)
            qfeat = jnp.where((rel > N_POS) & (rel < N_FEAT), -1.0, 0.0)
            for n in range(N_POS):
                qfeat = jnp.where(rel == n, pieces[n], qfeat)
            qfeat_ref[c] = qfeat.astype(BF16)
        kr = lax.broadcasted_iota(jnp.int32, (tq, tq), 0)
        qc = lax.broadcasted_iota(jnp.int32, (tq, tq), 1)
        corr_ref[...] = jnp.maximum(qc - kr, 0).astype(F32) * (2.0 * sl2)

    @pl.when(gi == 0)
    def _prep():
        kraw = k_ref[0]
        lane = lax.broadcasted_iota(jnp.int32, kraw.shape, 1)
        for c in range(2):
            kaug_ref[c] = jnp.where(own_lanes(lane, c), kraw, kfeat_ref[c])
        kf = kraw.astype(F32)
        kn2 =_dot((kf * kf).astype(BF16), other_half)
        kmax2 = jnp.max(kn2, axis=0, keepdims=True)
        qa = qall_ref[0].astype(F32)
        prod2 = _dot((qa * qa).astype(BF16), other_half) * kmax2
        bound = prod2 * lax.rsqrt(jnp.maximum(prod2, 1e-30))
        shift_all = (bound * 1.02).astype(BF16).astype(F32)
        shift_ref[...] = shift_all
        worst_ref[0] = jnp.max(shift_all)
        ones_row = lax.broadcasted_iota(jnp.int32, (VT_ROWS - DA_VDIM, tq), 0) == 0
        ones_rows = jnp.where(ones_row, 1.0, 0.0).astype(BF16)
        for jb in range(nb):
            vt = v_ref[0, jb * tq:(jb + 1) * tq, :].astype(F32).T.astype(BF16)
            vt_ref[0:DA_VDIM, jb * tq:(jb + 1) * tq] = vt
            vt_ref[DA_VDIM:, jb * tq:(jb + 1) * tq] = ones_rows
            vtb_ref[jb, 0:DA_VDIM, :] = vt
            vtb_ref[jb, DA_VDIM:, :] = ones_rows

    worst = worst_ref[0]
    lane = lax.broadcasted_iota(jnp.int32, (tq, LANES), 1)

    def sub_rows(sub):
        return slice(sub * tq, (sub + 1) * tq)

    def load_q(sub):
        return q_ref[0, sub_rows(sub), :].astype(F32)

    def finish_t(o1t, o2t, sub):
        finish((o1t - lam * o2t).T, None, sub)

    def finish(o1, o2, sub):
        o = o1 if o2 is None else o1 - lam * o2
        ms = jnp.mean(o * o, axis=-1, keepdims=True)
        o = o * lax.rsqrt(ms + EPS) * sw_ref[...] * (1.0 - LAMBDA_INIT)
        o_ref[0, sub_rows(sub), :] = o.astype(o_ref.dtype)

    def fast_path(max_dist, sub):
        qi = gi * nsub + sub
        rows = pl.ds(pl.multiple_of(qi * tq, tq), tq)
        qraw = q_ref[0, sub_rows(sub), :]
        neg_shift = (-shift_ref[rows, :]).astype(BF16)
        zero = jnp.zeros_like(qraw)
        outs = []
        for c in range(2):
            rel = lane - feat0(c)
            own = own_lanes(lane, c)
            base = jnp.where(own, qraw, jnp.where(rel == N_POS, neg_shift, zero))
            posf = qfeat_ref[c, rows, :]
            q_after = base + posf
            q_before = base - posf
            if max_dist is None:
                blocks = []
                for d in range(nb):
                    wrapped = qi + d >= nb
                    jb = jnp.where(wrapped, qi + d - nb, qi + d)
                    blocks.append((jb, q_after if d == 0 else jnp.where(wrapped, q_before, q_after), d == 0))
            else:
                q_dead = jnp.where(own, qraw, jnp.where(rel == N_POS, -DEAD_SHIFT, 0.0).astype(BF16))
                blocks = []
                for r in range(-max_dist, max_dist + 1):
                    jb = qi + r
                    side = q_before if r < 0 else q_after
                    inside = (jb >= 0) & (jb < nb)
                    blocks.append((jnp.clip(jb, 0, nb - 1), side if r == 0 else jnp.where(inside, side, q_dead),
                                   r == 0))
            acc = None
            for jb, qsel, diagonal in blocks:
                dk = pl.ds(pl.multiple_of(jb * tq, tq), tq)
                st = _dot_nt(kaug_ref[c, dk, :], qsel)
                if diagonal:
                    st = st - corr_ref[...]
                p = jnp.exp2(st).astype(BF16)
                if max_dist is None:
                    pt_ref[sub, c, dk, :] = p
                else:
                    part = _dot(vtb_ref[jb], p)
                    acc = part if acc is None else acc + part
            if max_dist is None:
                acc = _dot(vt_ref[...], pt_ref[sub, c])
            outs.append(acc[:DA_VDIM] * (1.0 / acc[DA_VDIM:DA_VDIM + 1]))
        finish_t(outs[0], outs[1], sub)

    fast = worst <= MAX_SHIFT_LOG2
    dists = [_max_block_dist(hh, tq, nb) for hh in range(DA_HEADS)]
    dists = [d if 2 * d + 1 < nb else None for d in dists]
    for dist in set(dists):
        heads = [hh for hh in range(DA_HEADS) if dists[hh] == dist]
        in_heads = functools.reduce(lambda a, b: a | b, [h == hh for hh in heads])

        @pl.when(fast & in_heads)
        def _(dist=dist):
            for sub in range(nsub):
                fast_path(dist, sub)

    def online_path(sub):
        qi = gi * nsub + sub
        qf = load_q(sub)
        zero = jnp.zeros_like(qf)
        qm = [jnp.where(own_lanes(lane, c), qf, zero).astype(BF16) for c in range(2)]
        qk0 =(qi * tq + lax.broadcasted_iota(jnp.int32, (tq, tk), 0)
               - lax.broadcasted_iota(jnp.int32, (tq, tk), 1)).astype(F32)

        def body(j, carry):
            kb = k_ref[0, pl.ds(pl.multiple_of(j * tk, tk), tk), :]
            vb = v_ref[0, pl.ds(pl.multiple_of(j * tk, tk), tk), :]
            bias = sl2 * jnp.abs(qk0 - lax.convert_element_type(j * tk, F32))
            out = []
            for c in range(2):
                m, l, acc = carry[c]
                s = _dot_nt(qm[c], kb) - bias
                m_new = jnp.maximum(m, jnp.max(s, axis=-1, keepdims=True))
                p = jnp.exp2(s - m_new)
                alpha = jnp.exp2(m - m_new)
                l_new = alpha * l + jnp.sum(p, axis=-1, keepdims=True)
                acc_new = alpha * acc + _dot(p.astype(BF16), vb)
                out.append((m_new, l_new, acc_new))
            return tuple(out)

        init = tuple((jnp.full((tq, 1), -jnp.inf, F32), jnp.zeros((tq, 1), F32),
                      jnp.zeros((tq, DA_VDIM), F32)) for _ in range(2))
        (m1, l1, a1), (m2, l2, a2) = lax.fori_loop(0, L // tk, body, init)
        finish(a1 * (1.0 / l1), a2 * (1.0 / l2), sub)

    @pl.when(jnp.logical_not(fast))
    def _():
        for sub in range(nsub):
            online_path(sub)


def _diff_attention(q, k, v, slopes, lamv, subln_w, B, L):
    tq, tk = min(TQ, L), min(TK, L)
    nsub = TQ_SUBS if L % (tq * TQ_SUBS) == 0 else 1
    return pl.pallas_call(
        functools.partial(_attn_kernel, tq=tq, tk=tk, L=L, nsub=nsub),
        grid=(DA_HEADS, B, L // (tq * nsub)),
        in_specs=[
            pl.BlockSpec(memory_space=pltpu.SMEM),
            pl.BlockSpec(lamv.shape, lambda h, b, i: (0, 0)),
            pl.BlockSpec((1, tq * nsub, LANES), lambda h, b, i: (b, i, h)),
            pl.BlockSpec((1, L, LANES), lambda h, b, i: (b, 0, h)),
            pl.BlockSpec((1, L, LANES), lambda h, b, i: (b, 0, h)),
            pl.BlockSpec((1, L, LANES), lambda h, b, i: (b, 0, h)),
            pl.BlockSpec(subln_w.shape, lambda h, b, i: (0, 0)),
        ],
        out_specs=pl.BlockSpec((1, tq * nsub, LANES), lambda h, b, i: (b, i, h)),
        out_shape=jax.ShapeDtypeStruct((B, L, 512), BF16),
        scratch_shapes=[pltpu.VMEM((2, L, LANES), BF16),
                        pltpu.VMEM((2, L, LANES), BF16),
                        pltpu.VMEM((2, L, LANES), BF16),
                        pltpu.VMEM((tq, tq), F32),
                        pltpu.VMEM((VT_ROWS, L), BF16),
                        pltpu.VMEM((L // tq, VT_ROWS, tq), BF16),
                        pltpu.VMEM((nsub, 2, L, tq), BF16),
                        pltpu.VMEM((L, LANES), F32),
                        pltpu.SMEM((1,), F32)],
        compiler_params=_cparams("arbitrary", "arbitrary", "arbitrary"),
        name="diff_attn",
    )(slopes, lamv, q, q, k, v, subln_w)


def _pair_tables():
    lo, hi = [], []
    for g in range(N_GROUPS):
        for a in range(EPG):
            for b in range(a + 1, EPG):
                lo.append(g * EPG + a)
                hi.append(g * EPG + b)
    return lo, hi


PAIRS_PER_GROUP = EPG * (EPG - 1) // 2
N_BUCKETS = N_GROUPS * PAIRS_PER_GROUP
GATE_LANE = 0
ROW_TAIL = LANES


def _router_kernel(x_ref, ys_ref, ya_ref, wo_ref, nw_ref, wr_ref, br_ref, tri_ref,
                   h_ref, u_ref, meta_ref, cnt_ref, carry_ref):
    i = pl.program_id(0)

    @pl.when(i == 0)
    def _():
        carry_ref[...] = jnp.zeros_like(carry_ref)

    sub = tri_ref.shape[0]
    for s in range(x_ref.shape[0] // sub):
        rows = pl.ds(s * sub, sub)
        _route_tile(x_ref.at[rows], ys_ref.at[rows], ya_ref.at[rows], wo_ref, nw_ref, wr_ref, br_ref, tri_ref,
                    h_ref.at[rows], u_ref.at[rows], meta_ref.at[rows], cnt_ref, carry_ref)


def _route_tile(x_ref, ys_ref, ya_ref, wo_ref, nw_ref, wr_ref, br_ref, tri_ref,
                h_ref, u_ref, meta_ref, cnt_ref, carry_ref):
    h = x_ref[...] + _dot(ys_ref[...], wo_ref[0:512, :]) + _dot(ya_ref[...], wo_ref[512:1024, :])
    h_ref[...] = h
    D = h.shape[1]
    ms = jnp.mean(h * h, axis=-1, keepdims=True)
    u = h * lax.rsqrt(ms + EPS) * nw_ref[...]
    u_ref[:, 0:D // 2] = _pack_bf16_pairs(u)
    logits = _dot(u.astype(BF16), wr_ref[...]) + br_ref[...]

    tm = logits.shape[0]
    li = lax.broadcasted_iota(jnp.int32, (tm, LANES), 1)
    lf = li.astype(F32)
    ninf = jnp.float32(-jnp.inf)
    big = jnp.float32(1e9)
    gl = jnp.where(li < N_GROUPS, logits, ninf)
    gmax = jnp.max(gl, axis=-1, keepdims=True)
    gidx = jnp.min(jnp.where(gl == gmax, lf, big), axis=-1, keepdims=True)
    pg = 1.0 / jnp.sum(jnp.exp(gl - gmax), axis=-1, keepdims=True)
    lane_grp = ((li - EXP_LANE0) >> 3).astype(F32)
    in_grp = (li >= EXP_LANE0) & (li < EXP_LANE0 + N_EXPERTS) & (lane_grp == gidx)
    sl = jnp.where(in_grp, logits, ninf)
    m1 = jnp.max(sl, axis=-1, keepdims=True)
    i1 = jnp.min(jnp.where(sl == m1, lf, big), axis=-1, keepdims=True)
    sl2 = jnp.where(lf == i1, ninf, sl)
    m2 = jnp.max(sl2, axis=-1, keepdims=True)
    i2 = jnp.min(jnp.where(sl2 == m2, lf, big), axis=-1, keepdims=True)
    t = jnp.exp(m2 - m1)
    w1 = 1.0 / (1.0 + t)
    g1 = pg * w1
    g2 = pg * (t * w1)
    a1 = i1 - EXP_LANE0 - gidx * EPG
    a2 = i2 - EXP_LANE0 - gidx * EPG
    lo = jnp.minimum(a1, a2)
    hi = jnp.maximum(a1, a2)
    bucket = gidx * PAIRS_PER_GROUP + (lo * (EPG - 1) - lo * (lo - 1.0) * 0.5) + (hi - lo - 1.0)
    first_is_lo = a1 < a2
    g_lo = jnp.where(first_is_lo, g1, g2)
    g_hi = jnp.where(first_is_lo, g2, g1)
    ohb = lf == bucket
    oh = jnp.where(ohb, 1.0, 0.0)
    prefix = _dot(tri_ref[...], oh.astype(BF16)) + carry_ref[0:1, :]
    rank = jnp.sum(jnp.where(ohb, prefix, 0.0), axis=-1, keepdims=True)
    new_carry = carry_ref[0:1, :] + jnp.sum(oh, axis=0, keepdims=True)
    carry_ref[...] = jnp.broadcast_to(new_carry, carry_ref.shape)
    cnt_ref[...] = jnp.broadcast_to(new_carry, cnt_ref.shape)

    meta_ref[...] = jnp.where(li == 0, bucket, jnp.where(li == 1, rank, 0.0))
    gates = jnp.where(li == GATE_LANE, g_lo, jnp.where(li == GATE_LANE + 1, g_hi, 0.0))
    u_ref[:, D // 2:] = pltpu.bitcast(gates, jnp.uint32)


def _out_router(x2, y_ssd, y_da, w_out, norm_w, w_r, b_r):
    T, D = x2.shape
    tm = min(TM_ROUTE, T)
    r = jnp.arange(min(ROUTE_SUB, tm))
    tri = (r[:, None] > r[None, :]).astype(BF16)
    row = lambda n: pl.BlockSpec((tm, n), lambda i: (i, 0))
    full = lambda a: pl.BlockSpec(a.shape, lambda i: (0,) * a.ndim)
    return pl.pallas_call(
        _router_kernel,
        grid=(T // tm,),
        in_specs=[row(D), row(512), row(512), full(w_out), full(norm_w), full(w_r), full(b_r), full(tri)],
        out_specs=[row(D), row(D // 2 + ROW_TAIL), row(LANES), pl.BlockSpec((SUBLANES, LANES), lambda i: (0, 0))],
        out_shape=[
            jax.ShapeDtypeStruct((T, D), F32),
            jax.ShapeDtypeStruct((T, D // 2 + ROW_TAIL), jnp.uint32),
            jax.ShapeDtypeStruct((T, LANES), F32),
            jax.ShapeDtypeStruct((SUBLANES, LANES), F32),
        ],
        scratch_shapes=[pltpu.VMEM((SUBLANES, LANES), F32)],
        compiler_params=_cparams("arbitrary"),
        name="out_router",
    )(x2, y_ssd, y_da, w_out, norm_w, w_r, b_r, tri)


def _dispatch_kernel(cnt_ref, idx_ref, u_ref, xs_hbm, slots_ref, ea_ref, eb_ref, nused_ref,
                     pstart_ref, zero_ref, sem, zsem, *, tm, n_blocks):
    i = pl.program_id(0)

    @pl.when(i == 0)
    def _():
        zero_ref[...] = jnp.zeros_like(zero_ref)
        pair_lo, pair_hi = _pair_tables()

        def zero_block(first_row):
            return pltpu.make_async_copy(
                zero_ref, xs_hbm.at[pl.ds(pl.multiple_of(first_row, MOE_ROWS), MOE_ROWS)], zsem)

        acc = jnp.int32(0)
        nfill = jnp.int32(0)
        for q in range(N_BUCKETS):
            padded = ((cnt_ref[q] + (MOE_ROWS - 1)) // MOE_ROWS) * MOE_ROWS
            pstart_ref[q] = acc
            end = acc + padded

            def fill(j, _, q=q):
                ea_ref[j] = jnp.int32(pair_lo[q])
                eb_ref[j] = jnp.int32(pair_hi[q])
                return 0

            lax.fori_loop(acc // MOE_ROWS, end // MOE_ROWS, fill, 0)

            @pl.when(padded > 0)
            def _(end=end):
                zero_block(end - MOE_ROWS).start()

            nfill = nfill + jnp.where(padded > 0, 1, 0)
            acc = end
        nused = acc // MOE_ROWS
        nused_ref[0] = nused

        def fill_tail(j, _):
            ea_ref[j] = jnp.int32(0)
            eb_ref[j] = jnp.int32(0)
            zero_block(j * MOE_ROWS).start()
            return 0

        lax.fori_loop(nused, n_blocks, fill_tail, 0)

        def wait_fill(_, carry):
            zero_block(0).wait()
            return carry

        lax.fori_loop(0, nfill + (n_blocks - nused), wait_fill, 0)

    def start(i8, _):
        for r in range(SUBLANES):
            t = i8 * SUBLANES + r
            slot = pstart_ref[idx_ref[0, t]] + idx_ref[1, t]
            slots_ref[0, t] = slot
            pltpu.make_async_copy(u_ref.at[i8, pl.ds(r, 1)], xs_hbm.at[pl.ds(slot, 1)], sem).start()
        return 0

    lax.fori_loop(0, tm // SUBLANES, start, 0)
    pltpu.make_async_copy(u_ref, u_ref, sem).wait()


def _dispatch(counts, idx, u, n_blocks):
    T, W = u.shape
    tm = min(TM_DISP, T)
    n_slots = n_blocks * MOE_ROWS
    grid_spec = pltpu.PrefetchScalarGridSpec(
        num_scalar_prefetch=1,
        grid=(T // tm,),
        in_specs=[
            pl.BlockSpec((2, tm), lambda i, cnt: (0, i), memory_space=pltpu.SMEM),
            pl.BlockSpec((tm // SUBLANES, SUBLANES, W), lambda i, cnt: (i, 0, 0)),
        ],
        out_specs=[
            pl.BlockSpec(memory_space=pl.ANY),
            pl.BlockSpec((1, tm), lambda i, cnt: (0, i), memory_space=pltpu.SMEM),
            pl.BlockSpec(memory_space=pltpu.SMEM),
            pl.BlockSpec(memory_space=pltpu.SMEM),
            pl.BlockSpec(memory_space=pltpu.SMEM),
        ],
        scratch_shapes=[pltpu.SMEM((LANES,), jnp.int32),
                        pltpu.VMEM((MOE_ROWS, W), u.dtype),
                        pltpu.SemaphoreType.DMA(()),
                        pltpu.SemaphoreType.DMA(())],
    )
    return pl.pallas_call(
        functools.partial(_dispatch_kernel, tm=tm, n_blocks=n_blocks),
        grid_spec=grid_spec,
        out_shape=[
            jax.ShapeDtypeStruct((n_slots, W), u.dtype),
            jax.ShapeDtypeStruct((1, T), jnp.int32),
            jax.ShapeDtypeStruct((n_blocks,), jnp.int32),
            jax.ShapeDtypeStruct((n_blocks,), jnp.int32),
            jax.ShapeDtypeStruct((1,), jnp.int32),
        ],
        compiler_params=_cparams("arbitrary"),
        name="dispatch",
    )(counts, idx, u.reshape(T // SUBLANES, SUBLANES, W))


def _moe_kernel(ea_ref, eb_ref, nused_ref, xs_ref, wga_ref, wua_ref, wda_ref, wgb_ref, wub_ref, wdb_ref, o_ref):
    j = pl.program_id(0)
    H = o_ref.shape[1]

    @pl.when(j < nused_ref[0])
    def _():
        x = _unpack_bf16_pairs(xs_ref[:, 0:H]).astype(BF16)
        tail = pltpu.bitcast(xs_ref[:, H:], F32)
        out = None
        for (wg, wu, wd), lane in (((wga_ref, wua_ref, wda_ref), GATE_LANE),
                                   ((wgb_ref, wub_ref, wdb_ref), GATE_LANE + 1)):
            hid = _silu(_dot(x, wg[0])) * _dot(x, wu[0])
            y = _dot(hid.astype(BF16), wd[0]) * tail[:, lane:lane + 1]
            out = y if out is None else out + y
        o_ref[...] = _pack_bf16_pairs(out)

    @pl.when(j >= nused_ref[0])
    def _():
        o_ref[...] = jnp.zeros_like(o_ref)


def _moe(block_ea, block_eb, nused, xs, wg, wu, wd, n_blocks):
    W = xs.shape[1]
    H = W - ROW_TAIL
    used = lambda j, nu: jnp.minimum(j, nu[0] - 1)
    wspec_a = lambda a: pl.BlockSpec((1,) + a.shape[1:], lambda j, ea, eb, nu: (ea[used(j, nu)], 0, 0))
    wspec_b = lambda a: pl.BlockSpec((1,) + a.shape[1:], lambda j, ea, eb, nu: (eb[used(j, nu)], 0, 0))
    grid_spec = pltpu.PrefetchScalarGridSpec(
        num_scalar_prefetch=3,
        grid=(n_blocks,),
        in_specs=[
            pl.BlockSpec((MOE_ROWS, W), lambda j, ea, eb, nu: (used(j, nu), 0)),
            wspec_a(wg), wspec_a(wu), wspec_a(wd), wspec_b(wg), wspec_b(wu), wspec_b(wd),
        ],
        out_specs=pl.BlockSpec((MOE_ROWS, H), lambda j, ea, eb, nu: (j, 0)),
    )
    return pl.pallas_call(
        _moe_kernel,
        grid_spec=grid_spec,
        out_shape=jax.ShapeDtypeStruct((n_blocks * MOE_ROWS, H), jnp.uint32),
        compiler_params=_cparams("arbitrary"),
        name="moe_mlp",
    )(block_ea, block_eb, nused, xs, wg, wu, wd, wg, wu, wd)


def _combine_kernel(slots_ref, next_slots_ref, h_ref, nw_ref, eo_hbm, o_ref, g_ref, sems, *, tm):
    i = pl.program_id(0)
    n = pl.num_programs(0)
    cur = i % 2

    def gather_tile(sref, buf):
        def start(i8, _):
            for r in range(SUBLANES):
                pltpu.make_async_copy(eo_hbm.at[pl.ds(sref[0, i8 * SUBLANES + r], 1)],
                                      g_ref.at[buf, i8, pl.ds(r, 1)], sems.at[buf]).start()
            return 0

        lax.fori_loop(0, tm // SUBLANES, start, 0)

    @pl.when(i == 0)
    def _():
        gather_tile(slots_ref, 0)

    @pl.when(i + 1 < n)
    def _():
        gather_tile(next_slots_ref, 1 - cur)

    pltpu.make_async_copy(g_ref.at[cur], g_ref.at[cur], sems.at[cur]).wait()

    y = h_ref[...] + _unpack_bf16_pairs(g_ref[cur].reshape(tm, -1))
    ms = jnp.mean(y * y, axis=-1, keepdims=True)
    o_ref[...] = y * lax.rsqrt(ms + EPS) * nw_ref[...]


def _combine(slots, h, norm_w, eo):
    T, D = h.shape
    tm = min(TM_COMB, T)
    last = T // tm - 1
    return pl.pallas_call(
        functools.partial(_combine_kernel, tm=tm),
        grid=(T // tm,),
        in_specs=[
            pl.BlockSpec((1, tm), lambda i: (0, i), memory_space=pltpu.SMEM),
            pl.BlockSpec((1, tm), lambda i: (0, jnp.minimum(i + 1, last)), memory_space=pltpu.SMEM),
            pl.BlockSpec((tm, D), lambda i: (i, 0)),
            pl.BlockSpec(norm_w.shape, lambda i: (0, 0)),
            pl.BlockSpec(memory_space=pl.ANY),
        ],
        out_specs=pl.BlockSpec((tm, D), lambda i: (i, 0)),
        out_shape=jax.ShapeDtypeStruct((T, D), F32),
        scratch_shapes=[pltpu.VMEM((2, tm // SUBLANES, SUBLANES, D // 2), jnp.uint32),
                        pltpu.SemaphoreType.DMA((2,))],
        compiler_params=_cparams("arbitrary"),
        name="combine",
    )(slots, slots, h, norm_w, eo)


def _pad_lanes(v, offset=0):
    v = v.astype(F32).reshape(-1)
    return jnp.zeros((1, LANES), F32).at[0, offset:offset + v.shape[0]].set(v)


def kernel(x, norm_mix_w, w_in, conv_w, conv_b, dt_bias_fwd, dt_bias_bwd, a_log_fwd, a_log_bwd, ssd_d,
           ssd_norm_w, lambda_q1, lambda_k1, lambda_q2, lambda_k2, subln_w, w_out, norm_ffn_w,
           w_router_group, b_router_group, w_router_exp, b_router_exp, w_exp_gate, w_exp_up, w_exp_down,
           norm_final_w):
    B, L, D = x.shape
    T = B * L
    x2 = x.reshape(T, D)
    l = 0

    w = w_in[l]
    o_z, o_xbc, o_dt = 512, 512 + 1024, 512 + 1024 + 16
    o_q, o_k = o_dt + 512, o_dt + 1024
    w_main = jnp.concatenate(
        [w[:, :o_xbc], w[:, o_dt:o_q] * (DA_HEADDIM ** -0.5 * LOG2E), w[:, o_q:]], axis=1).astype(BF16)
    w_dt = jnp.zeros((D, LANES), F32).at[:, :16].set(w[:, o_xbc:o_dt]).astype(BF16)

    experts = [w_exp_gate[l], w_exp_up[l], w_exp_down[l]]
    z, xbc, dt, q, k, v, *experts_bf16 = _in_proj(
        x2, norm_mix_w[l].reshape(1, D), w_main, w_dt, [a.reshape(-1, a.shape[-1]) for a in experts])
    wg_b, wu_b, wd_b = [c.reshape(a.shape) for c, a in zip(experts_bf16, experts)]

    dtb = _pad_lanes(jnp.concatenate([dt_bias_fwd[l], dt_bias_bwd[l]]))
    alog = _pad_lanes(jnp.concatenate([a_log_fwd[l], a_log_bwd[l]]))
    dskip_e = jnp.repeat(ssd_d[l].astype(F32), SSD_HEADDIM).reshape(1, SSD_WIDTH)
    y_ssd = _ssd(xbc.reshape(B, L, 1024), dt.reshape(B, L, LANES), z.reshape(B, L, 512),
                 conv_w[l].astype(F32), conv_b[l].reshape(1, -1).astype(F32), dtb, alog, dskip_e,
                 ssd_norm_w[l].reshape(1, -1).astype(F32), B, L)

    slopes = jnp.power(2.0, -8.0 * jnp.arange(1, DA_HEADS + 1, dtype=F32) / DA_HEADS)
    lamv = jnp.concatenate([_pad_lanes(lambda_q1[l]), _pad_lanes(lambda_k1[l]),
                            _pad_lanes(lambda_q2[l]), _pad_lanes(lambda_k2[l]),
                            jnp.zeros((4, LANES), F32)], axis=0)
    y_da = _diff_attention(q.reshape(B, L, 512), k.reshape(B, L, 512), v.reshape(B, L, 512),
                           slopes, lamv, subln_w[l].reshape(1, -1).astype(F32), B, L)

    w_r = (jnp.zeros((D, LANES), F32).at[:, :N_GROUPS].set(w_router_group[l])
           .at[:, EXP_LANE0:EXP_LANE0 + N_EXPERTS].set(w_router_exp[l])).astype(BF16)
    b_r = _pad_lanes(jnp.concatenate([b_router_group[l], b_router_exp[l]]))
    h, u, meta, cnt = _out_router(x2, y_ssd.reshape(T, 512), y_da.reshape(T, 512), w_out[l].astype(BF16),
                                  norm_ffn_w[l].reshape(1, D), w_r, b_r)

    n_blocks = T // MOE_ROWS + N_BUCKETS
    counts = cnt[0].astype(jnp.int32)
    idx = meta[:, :2].astype(jnp.int32).T
    xs, slots, block_ea, block_eb, nused = _dispatch(counts, idx, u, n_blocks)
    eo = _moe(block_ea, block_eb, nused, xs, wg_b, wu_b, wd_b, n_blocks)
    out = _combine(slots, h, norm_final_w.reshape(1, D), eo)
    return out.reshape(B, L, D)
```

```python
import functools
import math

import jax
import jax.numpy as jnp
from jax import lax
from jax.experimental import pallas as pl
from jax.experimental.pallas import tpu as pltpu

F32 = jnp.float32
BF16 = jnp.bfloat16

EPS = 1e-6
LOG2E = math.log2(math.e)
LANES = 128
SUBLANES = 8
HALO = 16
VMEM_LIMIT = 48 * 1024 * 1024

SSD_WIDTH = 512
SSD_HEADDIM = 64
SSD_HEADS = 8
SSD_NGROUPS = 2
SSD_HPG = 4
SSD_DSTATE = 128
SSD_CONV = 5
SSD_CHUNK = 128
SSD_CONV_CH = 1024
DA_HEADDIM = 64
DA_VDIM = 128
DA_HEADS = 4
N_GROUPS = 4
EPG = 8
N_EXPERTS = 32
D_EXPERT = 512
LAMBDA_INIT = 0.8 - 0.6 * math.exp(-0.3 * 0)

TM_PROJ = 1024
PROJ_SUB = 512
SSD_SEQS = 4
TQ = 512
TQ_SUBS = 2
TK = 512
TM_ROUTE = 1024
ROUTE_SUB = 512
TM_DISP = 1024
MOE_ROWS = 256
TM_COMB = 512
EXP_LANE0 = 4


def _cparams(*sem):
    return pltpu.CompilerParams(dimension_semantics=sem, vmem_limit_bytes=VMEM_LIMIT)


def _dot(a, b):
    return jnp.dot(a, b, preferred_element_type=F32)


def _dot_nt(a, b):
    return lax.dot_general(a, b, (((1,), (1,)), ((), ())), preferred_element_type=F32)


def _split3(v):
    hi = v.astype(BF16)
    r1 = v - hi.astype(F32)
    mid = r1.astype(BF16)
    lo = (r1 - mid.astype(F32)).astype(BF16)
    return hi, mid, lo


def _split2(v):
    hi = v.astype(BF16)
    return hi, (v - hi.astype(F32)).astype(BF16)


def _select_rows(a01x2, v):
    hi, lo = _split2(v)
    return _dot(a01x2, jnp.concatenate([hi, lo], axis=0))


def _select_lanes(v, b01x2):
    hi, lo = _split2(v)
    return _dot(jnp.concatenate([hi, lo], axis=1), b01x2)


def _silu(x):
    return x * (1.0 / (1.0 + jnp.exp(-x)))


def _pack_bf16_pairs(x):
    n = x.shape[1] // 2
    bits = pltpu.bitcast(x.astype(BF16).astype(F32), jnp.uint32)
    return (bits[:, :n] >> 16) | bits[:, n:]


def _unpack_bf16_pairs(w):
    lo = pltpu.bitcast(w << 16, F32)
    hi = pltpu.bitcast(w & jnp.uint32(0xFFFF0000), F32)
    return jnp.concatenate([lo, hi], axis=1)


def _inproj_kernel(x_ref, nw_ref, w_ref, wdt_ref, *refs, n_cast):
    cast_in, out_refs, cast_out = refs[:n_cast], refs[n_cast:len(refs) - n_cast], refs[len(refs) - n_cast:]
    for s in range(x_ref.shape[0] // PROJ_SUB):
        rows = pl.ds(s * PROJ_SUB, PROJ_SUB)
        _inproj_rows(x_ref.at[rows], nw_ref, w_ref, wdt_ref, *[r.at[rows] for r in out_refs])
    for src, dst in zip(cast_in, cast_out):
        dst[...] = src[...].astype(BF16)


def _inproj_rows(x_ref, nw_ref, w_ref, wdt_ref, z_ref, xbc_ref, dt_ref, q_ref, k_ref, v_ref):
    x = x_ref[...]
    ms = jnp.mean(x * x, axis=-1, keepdims=True)
    u = (x * lax.rsqrt(ms + EPS) * nw_ref[...]).astype(BF16)
    z_ref[...] = _dot(u, w_ref[:, 0:512]).astype(BF16)
    xbc_ref[...] = _dot(u, w_ref[:, 512:1536]).astype(BF16)
    q_ref[...] = _dot(u, w_ref[:, 1536:2048]).astype(BF16)
    k_ref[...] = _dot(u, w_ref[:, 2048:2560]).astype(BF16)
    v_ref[...] = _dot(u, w_ref[:, 2560:3072]).astype(BF16)
    dt_ref[...] = _dot(u, wdt_ref[...])


def _in_proj(x2, norm_w, w_main, w_dt, to_bf16):
    T, D = x2.shape
    tm = min(TM_PROJ, T)
    steps = T // tm
    row = lambda n: pl.BlockSpec((tm, n), lambda i: (i, 0))
    full = lambda a: pl.BlockSpec(a.shape, lambda i: (0,) * a.ndim)
    piece = lambda a: pl.BlockSpec((a.shape[0] // steps, a.shape[1]), lambda i: (i, 0))
    return pl.pallas_call(
        functools.partial(_inproj_kernel, n_cast=len(to_bf16)),
        grid=(steps,),
        in_specs=[row(D), full(norm_w), full(w_main), full(w_dt)] + [piece(a) for a in to_bf16],
        out_specs=[row(512), row(1024), row(LANES), row(512), row(512), row(512)] + [piece(a) for a in to_bf16],
        out_shape=[
            jax.ShapeDtypeStruct((T, 512), BF16),
            jax.ShapeDtypeStruct((T, 1024), BF16),
            jax.ShapeDtypeStruct((T, LANES), F32),
            jax.ShapeDtypeStruct((T, 512), BF16),
            jax.ShapeDtypeStruct((T, 512), BF16),
            jax.ShapeDtypeStruct((T, 512), BF16),
        ] + [jax.ShapeDtypeStruct(a.shape, BF16) for a in to_bf16],
        compiler_params=_cparams("arbitrary"),
        name="in_proj",
    )(x2, norm_w, w_main, w_dt, *to_bf16)


def _ssd_kernel(*refs, reverse):
    c = pl.program_id(1)
    nc = pl.num_programs(1)
    pos = (nc - 1 - c) if reverse else c
    ht_ref = refs[-1]

    @pl.when(c == 0)
    def _():
        ht_ref[...] = jnp.zeros_like(ht_ref)

    for s in range(ht_ref.shape[0]):
        _ssd_sequence(s, pos, nc, refs, reverse)


def _ssd_sequence(s, pos, nc, refs, reverse):
    if reverse:
        (xp_ref, xc_ref, xn_ref, dt_ref, cw_ref, cb_ref, dtb_ref, alog_ref, tri_ref, e_ref, sh_ref,
         z_ref, yf_ref, dsk_ref, nw_ref, out_ref, ht_ref) = refs
    else:
        (xp_ref, xc_ref, xn_ref, dt_ref, cw_ref, cb_ref, dtb_ref, alog_ref, tri_ref, e_ref, sh_ref,
         out_ref, ht_ref) = refs
    Q = SSD_CHUNK

    halo_zero = jnp.zeros((HALO, SSD_CONV_CH), BF16)
    ext = jnp.concatenate([jnp.where(pos > 0, xp_ref[s], halo_zero), xc_ref[s],
                           jnp.where(pos < nc - 1, xn_ref[s], halo_zero)], axis=0)
    shifted = _dot(sh_ref[...], ext)
    pad = (SSD_CONV - 1) // 2
    conv = cb_ref[...] + cw_ref[pad:pad + 1, :] * xc_ref[s].astype(F32)
    for n, k in enumerate(k for k in range(SSD_CONV) if k != pad):
        conv = conv + cw_ref[k:k + 1, :] * shifted[n * Q:(n + 1) * Q, :]
    act = _silu(conv)
    xs = act[:, :SSD_WIDTH]

    dtraw = dt_ref[s] + dtb_ref[...]
    dt = jnp.maximum(dtraw, 0.0) + jnp.log(1.0 + jnp.exp(-jnp.abs(dtraw)))
    a = -jnp.exp(alog_ref[...])
    adt = dt * a
    tri = tri_ref[...]
    cvec = _select_rows(tri, adt)
    cvt = cvec.T
    e01 = e_ref[...]
    dt_e = _select_lanes(dt, e01)
    cv_e = _select_lanes(cvec, e01)
    edge = 0 if reverse else Q - 1
    tot_e = cv_e[edge:edge + 1, :]

    li = lax.broadcasted_iota(jnp.int32, (Q, Q), 0)
    si = lax.broadcasted_iota(jnp.int32, (Q, Q), 1)
    mask = (li <= si) if reverse else (li >= si)
    lane = lax.broadcasted_iota(jnp.int32, (Q, LANES), 1)
    head0 = SSD_HEADS if reverse else 0

    xdt = xs * dt_e
    ys = []
    for g in range(SSD_NGROUPS):
        bg = act[:, SSD_WIDTH + g * SSD_DSTATE:SSD_WIDTH + (g + 1) * SSD_DSTATE]
        cg = act[:, SSD_WIDTH + (SSD_NGROUPS + g) * SSD_DSTATE:SSD_WIDTH + (SSD_NGROUPS + g + 1) * SSD_DSTATE]
        bgb = bg.astype(BF16)
        cgb = cg.astype(BF16)
        gm = _dot_nt(cgb, bgb)
        gs = slice(g * 256, (g + 1) * 256)
        xdt_g = xdt[:, gs]
        xdtb = xdt_g.astype(BF16)
        ydiag = []
        for pr in range(2):
            xpair = xdtb[:, pr * LANES:(pr + 1) * LANES]
            yh = []
            for j in range(2):
                hl = head0 + g * SSD_HPG + pr * 2 + j
                seg = cvec[:, hl:hl + 1] - cvt[hl:hl + 1, :]
                decay = jnp.exp(jnp.where(mask, seg, -jnp.inf))
                yh.append(_dot((gm * decay).astype(BF16), xpair))
            ydiag.append(jnp.where(lane < SSD_HEADDIM, yh[0], yh[1]))
        ydiag = jnp.concatenate(ydiag, axis=1)
        ht = ht_ref[s, g]
        yoff = _dot(cgb, ht.astype(BF16)) * jnp.exp(cv_e[:, gs])
        ys.append(ydiag + yoff)
        wdec = jnp.exp(tot_e[:, gs] - cv_e[:, gs])
        xw = (xdt_g * wdec).astype(BF16)
        ht_ref[s, g] = jnp.exp(tot_e[:, gs]) * ht + _dot(bg.T.astype(BF16), xw)
    y = jnp.concatenate(ys, axis=1)

    if not reverse:
        out_ref[s] = y
    else:
        y = y + yf_ref[s] + dsk_ref[...] * xs
        gated = y * _silu(z_ref[s].astype(F32))
        ms = jnp.mean(gated * gated, axis=-1, keepdims=True)
        out_ref[s] = (gated * lax.rsqrt(ms + EPS) * nw_ref[...]).astype(out_ref.dtype)


def _ssd(xbc, dt, z, conv_w, conv_b, dtb, alog, dskip_e, norm_w, B, L):
    Q = SSD_CHUNK
    nc = L // Q
    ns = SSD_SEQS if B % SSD_SEQS == 0 else 1
    hb = Q // HALO
    nhb = L // HALO
    r = jnp.arange(Q)
    pad = (SSD_CONV - 1) // 2
    col = jnp.arange(Q + 2 * HALO)[None, :]
    shift = jnp.concatenate([(col == HALO + r[:, None] + k - pad) for k in range(SSD_CONV) if k != pad],
                            axis=0).astype(BF16)
    twice = lambda m, axis: jnp.concatenate([m, m], axis=axis).astype(BF16)
    tril = twice(r[:, None] >= r[None, :], 1)
    triu = twice(r[:, None] <= r[None, :], 1)
    lane = jnp.arange(LANES)[:, None]
    ch = jnp.arange(SSD_WIDTH)[None, :] // SSD_HEADDIM
    e_f = twice(lane == ch, 0)
    e_b = twice(lane == ch + SSD_HEADS, 0)

    def call(reverse, extra_in, extra_specs, out_dtype):
        pos = (lambda c: nc - 1 - c) if reverse else (lambda c: c)
        full = lambda a: pl.BlockSpec(a.shape, lambda b, c: (0,) * a.ndim)
        chunk = lambda n: pl.BlockSpec((ns, Q, n), lambda b, c: (b, pos(c), 0))
        in_specs = [
            pl.BlockSpec((ns, HALO, 1024), lambda b, c: (b, jnp.maximum(pos(c) * hb - 1, 0), 0)),
            chunk(1024),
            pl.BlockSpec((ns, HALO, 1024), lambda b, c: (b, jnp.minimum((pos(c) + 1) * hb, nhb - 1), 0)),
            chunk(LANES),
        ]
        consts = [conv_w, conv_b, dtb, alog, triu if reverse else tril, e_b if reverse else e_f, shift]
        in_specs += [full(a) for a in consts]
        in_specs += [chunk(512) if s == "chunk" else full(a) for a, s in zip(extra_in, extra_specs)]
        return pl.pallas_call(
            functools.partial(_ssd_kernel, reverse=reverse),
            grid=(B // ns, nc),
            in_specs=in_specs,
            out_specs=chunk(512),
            out_shape=jax.ShapeDtypeStruct((B, L, 512), out_dtype),
            scratch_shapes=[pltpu.VMEM((ns, SSD_NGROUPS, SSD_DSTATE, 256), F32)],
            compiler_params=_cparams("arbitrary", "arbitrary"),
            name="ssd_bwd" if reverse else "ssd_fwd",
        )(xbc, xbc, xbc, dt, *consts, *extra_in)

    yf = call(False, [], [], F32)
    return call(True, [z, yf, dskip_e, norm_w], ["chunk", "chunk", "full", "full"], BF16)


N_POS = 3
N_FEAT = 2 * N_POS + 1
MAX_SHIFT_LOG2 = 50.0
VT_ROWS = DA_VDIM + 16
ZERO_PROB_LOG2 = 160.0
DEAD_SHIFT = 1.0e4


def _max_block_dist(head, tq, nb):
    slope_log2 = 2.0 ** (-8.0 * (head + 1) / DA_HEADS) * LOG2E
    d = 0
    while d < nb - 1 and slope_log2 * (d * tq + 1) < ZERO_PROB_LOG2:
        d += 1
    return d


def _attn_kernel(slope_ref, lamv_ref, q_ref, qall_ref, k_ref, v_ref, sw_ref, o_ref,
                 kaug_ref, kfeat_ref, qfeat_ref, corr_ref, vt_ref, vtb_ref, pt_ref, shift_ref, worst_ref,
                 *, tq, tk, L, nsub):
    h = pl.program_id(0)
    b = pl.program_id(1)
    gi = pl.program_id(2)
    nb = L // tq
    sl2 = slope_ref[h] * LOG2E
    lv = lamv_ref[...]
    lam = (jnp.exp(jnp.sum(lv[0:1] * lv[1:2], axis=-1, keepdims=True))
           - jnp.exp(jnp.sum(lv[2:3] * lv[3:4], axis=-1, keepdims=True)) + LAMBDA_INIT)

    def feat0(c):
        return DA_HEADDIM if c == 0 else 0

    def own_lanes(lane, c):
        return (lane < DA_HEADDIM) if c == 0 else (lane >= DA_HEADDIM)

    ha = lax.broadcasted_iota(jnp.int32, (LANES, LANES), 0) < DA_HEADDIM
    hb = lax.broadcasted_iota(jnp.int32, (LANES, LANES), 1) < DA_HEADDIM
    other_half = jnp.where(ha != hb, 1.0, 0.0).astype(BF16)

    @pl.when((b == 0) & (gi == 0))
    def _tables():
        lane = lax.broadcasted_iota(jnp.int32, (L, LANES), 1)
        kpos = lax.broadcasted_iota(jnp.int32, (L, LANES), 0).astype(F32) * sl2
        pieces = [t.astype(F32) for t in _split3(kpos)]
        for c in range(2):
            rel = lane - feat0(c)
            feat = jnp.where((rel >= 0) & (rel <= N_POS), 1.0, 0.0)
            for n in range(N_POS):
                feat = jnp.where(rel == N_POS + 1 + n, pieces[n], feat)
            kfeat_ref[c] = feat.astype(BF16)
            qfeat = jnp.where((rel > N_POS) & (rel < N_FEAT), -1.0, 0.0)
            for n in range(N_POS):
                qfeat = jnp.where(rel == n, pieces[n], qfeat)
            qfeat_ref[c] = qfeat.astype(BF16)
        kr = lax.broadcasted_iota(jnp.int32, (tq, tq), 0)
        qc = lax.broadcasted_iota(jnp.int32, (tq, tq), 1)
        corr_ref[...] = jnp.maximum(qc - kr, 0).astype(F32) * (2.0 * sl2)

    @pl.when(gi == 0)
    def _prep():
        kraw = k_ref[0]
        lane = lax.broadcasted_iota(jnp.int32, kraw.shape, 1)
        for c in range(2):
            kaug_ref[c] = jnp.where(own_lanes(lane, c), kraw, kfeat_ref[c])
        kf = kraw.astype(F32)
        kn2 =_dot((kf * kf).astype(BF16), other_half)
        kmax2 = jnp.max(kn2, axis=0, keepdims=True)
        qa = qall_ref[0].astype(F32)
        prod2 = _dot((qa * qa).astype(BF16), other_half) * kmax2
        bound = prod2 * lax.rsqrt(jnp.maximum(prod2, 1e-30))
        shift_all = (bound * 1.02).astype(BF16).astype(F32)
        shift_ref[...] = shift_all
        worst_ref[0] = jnp.max(shift_all)
        ones_row = lax.broadcasted_iota(jnp.int32, (VT_ROWS - DA_VDIM, tq), 0) == 0
        ones_rows = jnp.where(ones_row, 1.0, 0.0).astype(BF16)
        for jb in range(nb):
            vt = v_ref[0, jb * tq:(jb + 1) * tq, :].astype(F32).T.astype(BF16)
            vt_ref[0:DA_VDIM, jb * tq:(jb + 1) * tq] = vt
            vt_ref[DA_VDIM:, jb * tq:(jb + 1) * tq] = ones_rows
            vtb_ref[jb, 0:DA_VDIM, :] = vt
            vtb_ref[jb, DA_VDIM:, :] = ones_rows

    worst = worst_ref[0]
    lane = lax.broadcasted_iota(jnp.int32, (tq, LANES), 1)

    def sub_rows(sub):
        return slice(sub * tq, (sub + 1) * tq)

    def load_q(sub):
        return q_ref[0, sub_rows(sub), :].astype(F32)

    def finish_t(o1t, o2t, sub):
        finish((o1t - lam * o2t).T, None, sub)

    def finish(o1, o2, sub):
        o = o1 if o2 is None else o1 - lam * o2
        ms = jnp.mean(o * o, axis=-1, keepdims=True)
        o = o * lax.rsqrt(ms + EPS) * sw_ref[...] * (1.0 - LAMBDA_INIT)
        o_ref[0, sub_rows(sub), :] = o.astype(o_ref.dtype)

    def fast_path(max_dist, sub):
        qi = gi * nsub + sub
        rows = pl.ds(pl.multiple_of(qi * tq, tq), tq)
        qraw = q_ref[0, sub_rows(sub), :]
        neg_shift = (-shift_ref[rows, :]).astype(BF16)
        zero = jnp.zeros_like(qraw)
        outs = []
        for c in range(2):
            rel = lane - feat0(c)
            own = own_lanes(lane, c)
            base = jnp.where(own, qraw, jnp.where(rel == N_POS, neg_shift, zero))
            posf = qfeat_ref[c, rows, :]
            q_after = base + posf
            q_before = base - posf
            if max_dist is None:
                blocks = []
                for d in range(nb):
                    wrapped = qi + d >= nb
                    jb = jnp.where(wrapped, qi + d - nb, qi + d)
                    blocks.append((jb, q_after if d == 0 else jnp.where(wrapped, q_before, q_after), d == 0))
            else:
                q_dead = jnp.where(own, qraw, jnp.where(rel == N_POS, -DEAD_SHIFT, 0.0).astype(BF16))
                blocks = []
                for r in range(-max_dist, max_dist + 1):
                    jb = qi + r
                    side = q_before if r < 0 else q_after
                    inside = (jb >= 0) & (jb < nb)
                    blocks.append((jnp.clip(jb, 0, nb - 1), side if r == 0 else jnp.where(inside, side, q_dead),
                                   r == 0))
            acc = None
            for jb, qsel, diagonal in blocks:
                dk = pl.ds(pl.multiple_of(jb * tq, tq), tq)
                st = _dot_nt(kaug_ref[c, dk, :], qsel)
                if diagonal:
                    st = st - corr_ref[...]
                p = jnp.exp2(st).astype(BF16)
                if max_dist is None:
                    pt_ref[sub, c, dk, :] = p
                else:
                    part = _dot(vtb_ref[jb], p)
                    acc = part if acc is None else acc + part
            if max_dist is None:
                acc = _dot(vt_ref[...], pt_ref[sub, c])
            outs.append(acc[:DA_VDIM] * (1.0 / acc[DA_VDIM:DA_VDIM + 1]))
        finish_t(outs[0], outs[1], sub)

    fast = worst <= MAX_SHIFT_LOG2
    dists = [_max_block_dist(hh, tq, nb) for hh in range(DA_HEADS)]
    dists = [d if 2 * d + 1 < nb else None for d in dists]
    for dist in set(dists):
        heads = [hh for hh in range(DA_HEADS) if dists[hh] == dist]
        in_heads = functools.reduce(lambda a, b: a | b, [h == hh for hh in heads])

        @pl.when(fast & in_heads)
        def _(dist=dist):
            for sub in range(nsub):
                fast_path(dist, sub)

    def online_path(sub):
        qi = gi * nsub + sub
        qf = load_q(sub)
        zero = jnp.zeros_like(qf)
        qm = [jnp.where(own_lanes(lane, c), qf, zero).astype(BF16) for c in range(2)]
        qk0 =(qi * tq + lax.broadcasted_iota(jnp.int32, (tq, tk), 0)
               - lax.broadcasted_iota(jnp.int32, (tq, tk), 1)).astype(F32)

        def body(j, carry):
            kb = k_ref[0, pl.ds(pl.multiple_of(j * tk, tk), tk), :]
            vb = v_ref[0, pl.ds(pl.multiple_of(j * tk, tk), tk), :]
            bias = sl2 * jnp.abs(qk0 - lax.convert_element_type(j * tk, F32))
            out = []
            for c in range(2):
                m, l, acc = carry[c]
                s = _dot_nt(qm[c], kb) - bias
                m_new = jnp.maximum(m, jnp.max(s, axis=-1, keepdims=True))
                p = jnp.exp2(s - m_new)
                alpha = jnp.exp2(m - m_new)
                l_new = alpha * l + jnp.sum(p, axis=-1, keepdims=True)
                acc_new = alpha * acc + _dot(p.astype(BF16), vb)
                out.append((m_new, l_new, acc_new))
            return tuple(out)

        init = tuple((jnp.full((tq, 1), -jnp.inf, F32), jnp.zeros((tq, 1), F32),
                      jnp.zeros((tq, DA_VDIM), F32)) for _ in range(2))
        (m1, l1, a1), (m2, l2, a2) = lax.fori_loop(0, L // tk, body, init)
        finish(a1 * (1.0 / l1), a2 * (1.0 / l2), sub)

    @pl.when(jnp.logical_not(fast))
    def _():
        for sub in range(nsub):
            online_path(sub)


def _diff_attention(q, k, v, slopes, lamv, subln_w, B, L):
    tq, tk = min(TQ, L), min(TK, L)
    nsub = TQ_SUBS if L % (tq * TQ_SUBS) == 0 else 1
    return pl.pallas_call(
        functools.partial(_attn_kernel, tq=tq, tk=tk, L=L, nsub=nsub),
        grid=(DA_HEADS, B, L // (tq * nsub)),
        in_specs=[
            pl.BlockSpec(memory_space=pltpu.SMEM),
            pl.BlockSpec(lamv.shape, lambda h, b, i: (0, 0)),
            pl.BlockSpec((1, tq * nsub, LANES), lambda h, b, i: (b, i, h)),
            pl.BlockSpec((1, L, LANES), lambda h, b, i: (b, 0, h)),
            pl.BlockSpec((1, L, LANES), lambda h, b, i: (b, 0, h)),
            pl.BlockSpec((1, L, LANES), lambda h, b, i: (b, 0, h)),
            pl.BlockSpec(subln_w.shape, lambda h, b, i: (0, 0)),
        ],
        out_specs=pl.BlockSpec((1, tq * nsub, LANES), lambda h, b, i: (b, i, h)),
        out_shape=jax.ShapeDtypeStruct((B, L, 512), BF16),
        scratch_shapes=[pltpu.VMEM((2, L, LANES), BF16),
                        pltpu.VMEM((2, L, LANES), BF16),
                        pltpu.VMEM((2, L, LANES), BF16),
                        pltpu.VMEM((tq, tq), F32),
                        pltpu.VMEM((VT_ROWS, L), BF16),
                        pltpu.VMEM((L // tq, VT_ROWS, tq), BF16),
                        pltpu.VMEM((nsub, 2, L, tq), BF16),
                        pltpu.VMEM((L, LANES), F32),
                        pltpu.SMEM((1,), F32)],
        compiler_params=_cparams("arbitrary", "arbitrary", "arbitrary"),
        name="diff_attn",
    )(slopes, lamv, q, q, k, v, subln_w)


def _pair_tables():
    lo, hi = [], []
    for g in range(N_GROUPS):
        for a in range(EPG):
            for b in range(a + 1, EPG):
                lo.append(g * EPG + a)
                hi.append(g * EPG + b)
    return lo, hi


PAIRS_PER_GROUP = EPG * (EPG - 1) // 2
N_BUCKETS = N_GROUPS * PAIRS_PER_GROUP
GATE_LANE = 0
ROW_TAIL = LANES


def _router_kernel(x_ref, ys_ref, ya_ref, wo_ref, nw_ref, wr_ref, br_ref, tri_ref,
                   h_ref, u_ref, meta_ref, cnt_ref, carry_ref):
    i = pl.program_id(0)

    @pl.when(i == 0)
    def _():
        carry_ref[...] = jnp.zeros_like(carry_ref)

    sub = tri_ref.shape[0]
    for s in range(x_ref.shape[0] // sub):
        rows = pl.ds(s * sub, sub)
        _route_tile(x_ref.at[rows], ys_ref.at[rows], ya_ref.at[rows], wo_ref, nw_ref, wr_ref, br_ref, tri_ref,
                    h_ref.at[rows], u_ref.at[rows], meta_ref.at[rows], cnt_ref, carry_ref)


def _route_tile(x_ref, ys_ref, ya_ref, wo_ref, nw_ref, wr_ref, br_ref, tri_ref,
                h_ref, u_ref, meta_ref, cnt_ref, carry_ref):
    h = x_ref[...] + _dot(ys_ref[...], wo_ref[0:512, :]) + _dot(ya_ref[...], wo_ref[512:1024, :])
    h_ref[...] = h
    D = h.shape[1]
    ms = jnp.mean(h * h, axis=-1, keepdims=True)
    u = h * lax.rsqrt(ms + EPS) * nw_ref[...]
    u_ref[:, 0:D // 2] = _pack_bf16_pairs(u)
    logits = _dot(u.astype(BF16), wr_ref[...]) + br_ref[...]

    tm = logits.shape[0]
    li = lax.broadcasted_iota(jnp.int32, (tm, LANES), 1)
    lf = li.astype(F32)
    ninf = jnp.float32(-jnp.inf)
    big = jnp.float32(1e9)
    gl = jnp.where(li < N_GROUPS, logits, ninf)
    gmax = jnp.max(gl, axis=-1, keepdims=True)
    gidx = jnp.min(jnp.where(gl == gmax, lf, big), axis=-1, keepdims=True)
    pg = 1.0 / jnp.sum(jnp.exp(gl - gmax), axis=-1, keepdims=True)
    lane_grp = ((li - EXP_LANE0) >> 3).astype(F32)
    in_grp = (li >= EXP_LANE0) & (li < EXP_LANE0 + N_EXPERTS) & (lane_grp == gidx)
    sl = jnp.where(in_grp, logits, ninf)
    m1 = jnp.max(sl, axis=-1, keepdims=True)
    i1 = jnp.min(jnp.where(sl == m1, lf, big), axis=-1, keepdims=True)
    sl2 = jnp.where(lf == i1, ninf, sl)
    m2 = jnp.max(sl2, axis=-1, keepdims=True)
    i2 = jnp.min(jnp.where(sl2 == m2, lf, big), axis=-1, keepdims=True)
    t = jnp.exp(m2 - m1)
    w1 = 1.0 / (1.0 + t)
    g1 = pg * w1
    g2 = pg * (t * w1)
    a1 = i1 - EXP_LANE0 - gidx * EPG
    a2 = i2 - EXP_LANE0 - gidx * EPG
    lo = jnp.minimum(a1, a2)
    hi = jnp.maximum(a1, a2)
    bucket = gidx * PAIRS_PER_GROUP + (lo * (EPG - 1) - lo * (lo - 1.0) * 0.5) + (hi - lo - 1.0)
    first_is_lo = a1 < a2
    g_lo = jnp.where(first_is_lo, g1, g2)
    g_hi = jnp.where(first_is_lo, g2, g1)
    ohb = lf == bucket
    oh = jnp.where(ohb, 1.0, 0.0)
    prefix = _dot(tri_ref[...], oh.astype(BF16)) + carry_ref[0:1, :]
    rank = jnp.sum(jnp.where(ohb, prefix, 0.0), axis=-1, keepdims=True)
    new_carry = carry_ref[0:1, :] + jnp.sum(oh, axis=0, keepdims=True)
    carry_ref[...] = jnp.broadcast_to(new_carry, carry_ref.shape)
    cnt_ref[...] = jnp.broadcast_to(new_carry, cnt_ref.shape)

    meta_ref[...] = jnp.where(li == 0, bucket, jnp.where(li == 1, rank, 0.0))
    gates = jnp.where(li == GATE_LANE, g_lo, jnp.where(li == GATE_LANE + 1, g_hi, 0.0))
    u_ref[:, D // 2:] = pltpu.bitcast(gates, jnp.uint32)


def _out_router(x2, y_ssd, y_da, w_out, norm_w, w_r, b_r):
    T, D = x2.shape
    tm = min(TM_ROUTE, T)
    r = jnp.arange(min(ROUTE_SUB, tm))
    tri = (r[:, None] > r[None, :]).astype(BF16)
    row = lambda n: pl.BlockSpec((tm, n), lambda i: (i, 0))
    full = lambda a: pl.BlockSpec(a.shape, lambda i: (0,) * a.ndim)
    return pl.pallas_call(
        _router_kernel,
        grid=(T // tm,),
        in_specs=[row(D), row(512), row(512), full(w_out), full(norm_w), full(w_r), full(b_r), full(tri)],
        out_specs=[row(D), row(D // 2 + ROW_TAIL), row(LANES), pl.BlockSpec((SUBLANES, LANES), lambda i: (0, 0))],
        out_shape=[
            jax.ShapeDtypeStruct((T, D), F32),
            jax.ShapeDtypeStruct((T, D // 2 + ROW_TAIL), jnp.uint32),
            jax.ShapeDtypeStruct((T, LANES), F32),
            jax.ShapeDtypeStruct((SUBLANES, LANES), F32),
        ],
        scratch_shapes=[pltpu.VMEM((SUBLANES, LANES), F32)],
        compiler_params=_cparams("arbitrary"),
        name="out_router",
    )(x2, y_ssd, y_da, w_out, norm_w, w_r, b_r, tri)


def _dispatch_kernel(cnt_ref, idx_ref, u_ref, xs_hbm, slots_ref, ea_ref, eb_ref, nused_ref,
                     pstart_ref, zero_ref, sem, zsem, *, tm, n_blocks):
    i = pl.program_id(0)

    @pl.when(i == 0)
    def _():
        zero_ref[...] = jnp.zeros_like(zero_ref)
        pair_lo, pair_hi = _pair_tables()

        def zero_block(first_row):
            return pltpu.make_async_copy(
                zero_ref, xs_hbm.at[pl.ds(pl.multiple_of(first_row, MOE_ROWS), MOE_ROWS)], zsem)

        acc = jnp.int32(0)
        nfill = jnp.int32(0)
        for q in range(N_BUCKETS):
            padded = ((cnt_ref[q] + (MOE_ROWS - 1)) // MOE_ROWS) * MOE_ROWS
            pstart_ref[q] = acc
            end = acc + padded

            def fill(j, _, q=q):
                ea_ref[j] = jnp.int32(pair_lo[q])
                eb_ref[j] = jnp.int32(pair_hi[q])
                return 0

            lax.fori_loop(acc // MOE_ROWS, end // MOE_ROWS, fill, 0)

            @pl.when(padded > 0)
            def _(end=end):
                zero_block(end - MOE_ROWS).start()

            nfill = nfill + jnp.where(padded > 0, 1, 0)
            acc = end
        nused = acc // MOE_ROWS
        nused_ref[0] = nused

        def fill_tail(j, _):
            ea_ref[j] = jnp.int32(0)
            eb_ref[j] = jnp.int32(0)
            zero_block(j * MOE_ROWS).start()
            return 0

        lax.fori_loop(nused, n_blocks, fill_tail, 0)

        def wait_fill(_, carry):
            zero_block(0).wait()
            return carry

        lax.fori_loop(0, nfill + (n_blocks - nused), wait_fill, 0)

    def start(i8, _):
        for r in range(SUBLANES):
            t = i8 * SUBLANES + r
            slot = pstart_ref[idx_ref[0, t]] + idx_ref[1, t]
            slots_ref[0, t] = slot
            pltpu.make_async_copy(u_ref.at[i8, pl.ds(r, 1)], xs_hbm.at[pl.ds(slot, 1)], sem).start(priority=r % 2)
        return 0

    lax.fori_loop(0, tm // SUBLANES, start, 0)
    pltpu.make_async_copy(u_ref, u_ref, sem).wait()


def _dispatch(counts, idx, u, n_blocks):
    T, W = u.shape
    tm = min(TM_DISP, T)
    n_slots = n_blocks * MOE_ROWS
    grid_spec = pltpu.PrefetchScalarGridSpec(
        num_scalar_prefetch=1,
        grid=(T // tm,),
        in_specs=[
            pl.BlockSpec((2, tm), lambda i, cnt: (0, i), memory_space=pltpu.SMEM),
            pl.BlockSpec((tm // SUBLANES, SUBLANES, W), lambda i, cnt: (i, 0, 0)),
        ],
        out_specs=[
            pl.BlockSpec(memory_space=pl.ANY),
            pl.BlockSpec((1, tm), lambda i, cnt: (0, i), memory_space=pltpu.SMEM),
            pl.BlockSpec(memory_space=pltpu.SMEM),
            pl.BlockSpec(memory_space=pltpu.SMEM),
            pl.BlockSpec(memory_space=pltpu.SMEM),
        ],
        scratch_shapes=[pltpu.SMEM((LANES,), jnp.int32),
                        pltpu.VMEM((MOE_ROWS, W), u.dtype),
                        pltpu.SemaphoreType.DMA(()),
                        pltpu.SemaphoreType.DMA(())],
    )
    return pl.pallas_call(
        functools.partial(_dispatch_kernel, tm=tm, n_blocks=n_blocks),
        grid_spec=grid_spec,
        out_shape=[
            jax.ShapeDtypeStruct((n_slots, W), u.dtype),
            jax.ShapeDtypeStruct((1, T), jnp.int32),
            jax.ShapeDtypeStruct((n_blocks,), jnp.int32),
            jax.ShapeDtypeStruct((n_blocks,), jnp.int32),
            jax.ShapeDtypeStruct((1,), jnp.int32),
        ],
        compiler_params=_cparams("arbitrary"),
        name="dispatch",
    )(counts, idx, u.reshape(T // SUBLANES, SUBLANES, W))


def _moe_kernel(ea_ref, eb_ref, nused_ref, xs_ref, wga_ref, wua_ref, wda_ref, wgb_ref, wub_ref, wdb_ref, o_ref):
    j = pl.program_id(0)
    H = o_ref.shape[1]

    @pl.when(j < nused_ref[0])
    def _():
        x = _unpack_bf16_pairs(xs_ref[:, 0:H]).astype(BF16)
        tail = pltpu.bitcast(xs_ref[:, H:], F32)
        out = None
        for (wg, wu, wd), lane in (((wga_ref, wua_ref, wda_ref), GATE_LANE),
                                   ((wgb_ref, wub_ref, wdb_ref), GATE_LANE + 1)):
            hid = _silu(_dot(x, wg[0])) * _dot(x, wu[0])
            y = _dot(hid.astype(BF16), wd[0]) * tail[:, lane:lane + 1]
            out = y if out is None else out + y
        o_ref[...] = _pack_bf16_pairs(out)

    @pl.when(j >= nused_ref[0])
    def _():
        o_ref[...] = jnp.zeros_like(o_ref)


def _moe(block_ea, block_eb, nused, xs, wg, wu, wd, n_blocks):
    W = xs.shape[1]
    H = W - ROW_TAIL
    used = lambda j, nu: jnp.minimum(j, nu[0] - 1)
    wspec_a = lambda a: pl.BlockSpec((1,) + a.shape[1:], lambda j, ea, eb, nu: (ea[used(j, nu)], 0, 0))
    wspec_b = lambda a: pl.BlockSpec((1,) + a.shape[1:], lambda j, ea, eb, nu: (eb[used(j, nu)], 0, 0))
    grid_spec = pltpu.PrefetchScalarGridSpec(
        num_scalar_prefetch=3,
        grid=(n_blocks,),
        in_specs=[
            pl.BlockSpec((MOE_ROWS, W), lambda j, ea, eb, nu: (used(j, nu), 0)),
            wspec_a(wg), wspec_a(wu), wspec_a(wd), wspec_b(wg), wspec_b(wu), wspec_b(wd),
        ],
        out_specs=pl.BlockSpec((MOE_ROWS, H), lambda j, ea, eb, nu: (j, 0)),
    )
    return pl.pallas_call(
        _moe_kernel,
        grid_spec=grid_spec,
        out_shape=jax.ShapeDtypeStruct((n_blocks * MOE_ROWS, H), jnp.uint32),
        compiler_params=_cparams("arbitrary"),
        name="moe_mlp",
    )(block_ea, block_eb, nused, xs, wg, wu, wd, wg, wu, wd)


def _combine_kernel(slots_ref, next_slots_ref, h_ref, nw_ref, eo_hbm, o_ref, g_ref, sems, *, tm):
    i = pl.program_id(0)
    n = pl.num_programs(0)
    cur = i % 2

    def gather_tile(sref, buf):
        def start(i8, _):
            for r in range(SUBLANES):
                pltpu.make_async_copy(eo_hbm.at[pl.ds(sref[0, i8 * SUBLANES + r], 1)],
                                      g_ref.at[buf, i8, pl.ds(r, 1)], sems.at[buf]).start(priority=r % 2)
            return 0

        lax.fori_loop(0, tm // SUBLANES, start, 0)

    @pl.when(i == 0)
    def _():
        gather_tile(slots_ref, 0)

    @pl.when(i + 1 < n)
    def _():
        gather_tile(next_slots_ref, 1 - cur)

    pltpu.make_async_copy(g_ref.at[cur], g_ref.at[cur], sems.at[cur]).wait()

    y = h_ref[...] + _unpack_bf16_pairs(g_ref[cur].reshape(tm, -1))
    ms = jnp.mean(y * y, axis=-1, keepdims=True)
    o_ref[...] = y * lax.rsqrt(ms + EPS) * nw_ref[...]


def _combine(slots, h, norm_w, eo):
    T, D = h.shape
    tm = min(TM_COMB, T)
    last = T // tm - 1
    return pl.pallas_call(
        functools.partial(_combine_kernel, tm=tm),
        grid=(T // tm,),
        in_specs=[
            pl.BlockSpec((1, tm), lambda i: (0, i), memory_space=pltpu.SMEM),
            pl.BlockSpec((1, tm), lambda i: (0, jnp.minimum(i + 1, last)), memory_space=pltpu.SMEM),
            pl.BlockSpec((tm, D), lambda i: (i, 0)),
            pl.BlockSpec(norm_w.shape, lambda i: (0, 0)),
            pl.BlockSpec(memory_space=pl.ANY),
        ],
        out_specs=pl.BlockSpec((tm, D), lambda i: (i, 0)),
        out_shape=jax.ShapeDtypeStruct((T, D), F32),
        scratch_shapes=[pltpu.VMEM((2, tm // SUBLANES, SUBLANES, D // 2), jnp.uint32),
                        pltpu.SemaphoreType.DMA((2,))],
        compiler_params=_cparams("arbitrary"),
        name="combine",
    )(slots, slots, h, norm_w, eo)


def _pad_lanes(v, offset=0):
    v = v.astype(F32).reshape(-1)
    return jnp.zeros((1, LANES), F32).at[0, offset:offset + v.shape[0]].set(v)


def kernel(x, norm_mix_w, w_in, conv_w, conv_b, dt_bias_fwd, dt_bias_bwd, a_log_fwd, a_log_bwd, ssd_d,
           ssd_norm_w, lambda_q1, lambda_k1, lambda_q2, lambda_k2, subln_w, w_out, norm_ffn_w,
           w_router_group, b_router_group, w_router_exp, b_router_exp, w_exp_gate, w_exp_up, w_exp_down,
           norm_final_w):
    B, L, D = x.shape
    T = B * L
    x2 = x.reshape(T, D)
    l = 0

    w = w_in[l]
    o_z, o_xbc, o_dt = 512, 512 + 1024, 512 + 1024 + 16
    o_q, o_k = o_dt + 512, o_dt + 1024
    w_main = jnp.concatenate(
        [w[:, :o_xbc], w[:, o_dt:o_q] * (DA_HEADDIM ** -0.5 * LOG2E), w[:, o_q:]], axis=1).astype(BF16)
    w_dt = jnp.zeros((D, LANES), F32).at[:, :16].set(w[:, o_xbc:o_dt]).astype(BF16)

    experts = [w_exp_gate[l], w_exp_up[l], w_exp_down[l]]
    z, xbc, dt, q, k, v, *experts_bf16 = _in_proj(
        x2, norm_mix_w[l].reshape(1, D), w_main, w_dt, [a.reshape(-1, a.shape[-1]) for a in experts])
    wg_b, wu_b, wd_b = [c.reshape(a.shape) for c, a in zip(experts_bf16, experts)]

    dtb = _pad_lanes(jnp.concatenate([dt_bias_fwd[l], dt_bias_bwd[l]]))
    alog = _pad_lanes(jnp.concatenate([a_log_fwd[l], a_log_bwd[l]]))
    dskip_e = jnp.repeat(ssd_d[l].astype(F32), SSD_HEADDIM).reshape(1, SSD_WIDTH)
    y_ssd = _ssd(xbc.reshape(B, L, 1024), dt.reshape(B, L, LANES), z.reshape(B, L, 512),
                 conv_w[l].astype(F32), conv_b[l].reshape(1, -1).astype(F32), dtb, alog, dskip_e,
                 ssd_norm_w[l].reshape(1, -1).astype(F32), B, L)

    slopes = jnp.power(2.0, -8.0 * jnp.arange(1, DA_HEADS + 1, dtype=F32) / DA_HEADS)
    lamv = jnp.concatenate([_pad_lanes(lambda_q1[l]), _pad_lanes(lambda_k1[l]),
                            _pad_lanes(lambda_q2[l]), _pad_lanes(lambda_k2[l]),
                            jnp.zeros((4, LANES), F32)], axis=0)
    y_da = _diff_attention(q.reshape(B, L, 512), k.reshape(B, L, 512), v.reshape(B, L, 512),
                           slopes, lamv, subln_w[l].reshape(1, -1).astype(F32), B, L)

    w_r = (jnp.zeros((D, LANES), F32).at[:, :N_GROUPS].set(w_router_group[l])
           .at[:, EXP_LANE0:EXP_LANE0 + N_EXPERTS].set(w_router_exp[l])).astype(BF16)
    b_r = _pad_lanes(jnp.concatenate([b_router_group[l], b_router_exp[l]]))
    h, u, meta, cnt = _out_router(x2, y_ssd.reshape(T, 512), y_da.reshape(T, 512), w_out[l].astype(BF16),
                                  norm_ffn_w[l].reshape(1, D), w_r, b_r)

    n_blocks = T // MOE_ROWS + N_BUCKETS
    counts = cnt[0].astype(jnp.int32)
    idx = meta[:, :2].astype(jnp.int32).T
    xs, slots, block_ea, block_eb, nused = _dispatch(counts, idx, u, n_blocks)
    eo = _moe(block_ea, block_eb, nused, xs, wg_b, wu_b, wd_b, n_blocks)
    out = _combine(slots, h, norm_final_w.reshape(1, D), eo)
    return out.reshape(B, L, D)
```

```python
import functools
import math

import jax
import jax.numpy as jnp
from jax import lax
from jax.experimental import pallas as pl
from jax.experimental.pallas import tpu as pltpu

F32 = jnp.float32
BF16 = jnp.bfloat16

EPS = 1e-6
LOG2E = math.log2(math.e)
LANES = 128
SUBLANES = 8
HALO = 16
VMEM_LIMIT = 48 * 1024 * 1024

SSD_WIDTH = 512
SSD_HEADDIM = 64
SSD_HEADS = 8
SSD_NGROUPS = 2
SSD_HPG = 4
SSD_DSTATE = 128
SSD_CONV = 5
SSD_CHUNK = 128
SSD_CONV_CH = 1024
DA_HEADDIM = 64
DA_VDIM = 128
DA_HEADS = 4
N_GROUPS = 4
EPG = 8
N_EXPERTS = 32
D_EXPERT = 512
LAMBDA_INIT = 0.8 - 0.6 * math.exp(-0.3 * 0)

TM_PROJ = 1024
PROJ_SUB = 512
SSD_SEQS = 4
TQ = 512
TQ_SUBS = 2
TK = 512
TM_ROUTE = 1024
ROUTE_SUB = 512
TM_DISP = 1024
MOE_ROWS = 256
TM_COMB = 512
EXP_LANE0 = 4


def _cparams(*sem):
    return pltpu.CompilerParams(dimension_semantics=sem, vmem_limit_bytes=VMEM_LIMIT)


def _dot(a, b):
    return jnp.dot(a, b, preferred_element_type=F32)


def _dot_nt(a, b):
    return lax.dot_general(a, b, (((1,), (1,)), ((), ())), preferred_element_type=F32)


def _split3(v):
    hi = v.astype(BF16)
    r1 = v - hi.astype(F32)
    mid = r1.astype(BF16)
    lo = (r1 - mid.astype(F32)).astype(BF16)
    return hi, mid, lo


def _split2(v):
    hi = v.astype(BF16)
    return hi, (v - hi.astype(F32)).astype(BF16)


def _select_rows(a01x2, v):
    hi, lo = _split2(v)
    return _dot(a01x2, jnp.concatenate([hi, lo], axis=0))


def _select_lanes(v, b01x2):
    hi, lo = _split2(v)
    return _dot(jnp.concatenate([hi, lo], axis=1), b01x2)


def _silu(x):
    return x * (1.0 / (1.0 + jnp.exp(-x)))


def _pack_bf16_pairs(x):
    n = x.shape[1] // 2
    bits = pltpu.bitcast(x.astype(BF16).astype(F32), jnp.uint32)
    return (bits[:, :n] >> 16) | bits[:, n:]


def _unpack_bf16_pairs(w):
    lo = pltpu.bitcast(w << 16, F32)
    hi = pltpu.bitcast(w & jnp.uint32(0xFFFF0000), F32)
    return jnp.concatenate([lo, hi], axis=1)


def _inproj_kernel(x_ref, nw_ref, w_ref, wdt_ref, *refs, n_cast):
    cast_in, out_refs, cast_out = refs[:n_cast], refs[n_cast:len(refs) - n_cast], refs[len(refs) - n_cast:]
    for s in range(x_ref.shape[0] // PROJ_SUB):
        rows = pl.ds(s * PROJ_SUB, PROJ_SUB)
        _inproj_rows(x_ref.at[rows], nw_ref, w_ref, wdt_ref, *[r.at[rows] for r in out_refs])
    for src, dst in zip(cast_in, cast_out):
        dst[...] = src[...].astype(BF16)


def _inproj_rows(x_ref, nw_ref, w_ref, wdt_ref, z_ref, xbc_ref, dt_ref, q_ref, k_ref, v_ref):
    x = x_ref[...]
    ms = jnp.mean(x * x, axis=-1, keepdims=True)
    u = (x * lax.rsqrt(ms + EPS) * nw_ref[...]).astype(BF16)
    z_ref[...] = _dot(u, w_ref[:, 0:512]).astype(BF16)
    xbc_ref[...] = _dot(u, w_ref[:, 512:1536]).astype(BF16)
    q_ref[...] = _dot(u, w_ref[:, 1536:2048]).astype(BF16)
    k_ref[...] = _dot(u, w_ref[:, 2048:2560]).astype(BF16)
    v_ref[...] = _dot(u, w_ref[:, 2560:3072]).astype(BF16)
    dt_ref[...] = _dot(u, wdt_ref[...])


def _in_proj(x2, norm_w, w_main, w_dt, to_bf16):
    T, D = x2.shape
    tm = min(TM_PROJ, T)
    steps = T // tm
    row = lambda n: pl.BlockSpec((tm, n), lambda i: (i, 0))
    full = lambda a: pl.BlockSpec(a.shape, lambda i: (0,) * a.ndim)
    piece = lambda a: pl.BlockSpec((a.shape[0] // steps, a.shape[1]), lambda i: (i, 0))
    return pl.pallas_call(
        functools.partial(_inproj_kernel, n_cast=len(to_bf16)),
        grid=(steps,),
        in_specs=[row(D), full(norm_w), full(w_main), full(w_dt)] + [piece(a) for a in to_bf16],
        out_specs=[row(512), row(1024), row(LANES), row(512), row(512), row(512)] + [piece(a) for a in to_bf16],
        out_shape=[
            jax.ShapeDtypeStruct((T, 512), BF16),
            jax.ShapeDtypeStruct((T, 1024), BF16),
            jax.ShapeDtypeStruct((T, LANES), F32),
            jax.ShapeDtypeStruct((T, 512), BF16),
            jax.ShapeDtypeStruct((T, 512), BF16),
            jax.ShapeDtypeStruct((T, 512), BF16),
        ] + [jax.ShapeDtypeStruct(a.shape, BF16) for a in to_bf16],
        compiler_params=_cparams("arbitrary"),
        name="in_proj",
    )(x2, norm_w, w_main, w_dt, *to_bf16)


def _ssd_kernel(*refs, reverse):
    c = pl.program_id(1)
    nc = pl.num_programs(1)
    pos = (nc - 1 - c) if reverse else c
    ht_ref = refs[-1]

    @pl.when(c == 0)
    def _():
        ht_ref[...] = jnp.zeros_like(ht_ref)

    for s in range(ht_ref.shape[0]):
        _ssd_sequence(s, pos, nc, refs, reverse)


def _ssd_sequence(s, pos, nc, refs, reverse):
    if reverse:
        (xp_ref, xc_ref, xn_ref, dt_ref, cw_ref, cb_ref, dtb_ref, alog_ref, tri_ref, e_ref, sh_ref,
         z_ref, yf_ref, dsk_ref, nw_ref, out_ref, ht_ref) = refs
    else:
        (xp_ref, xc_ref, xn_ref, dt_ref, cw_ref, cb_ref, dtb_ref, alog_ref, tri_ref, e_ref, sh_ref,
         out_ref, ht_ref) = refs
    Q = SSD_CHUNK

    halo_zero = jnp.zeros((HALO, SSD_CONV_CH), BF16)
    ext = jnp.concatenate([jnp.where(pos > 0, xp_ref[s], halo_zero), xc_ref[s],
                           jnp.where(pos < nc - 1, xn_ref[s], halo_zero)], axis=0)
    shifted = _dot(sh_ref[...], ext)
    pad = (SSD_CONV - 1) // 2
    conv = cb_ref[...] + cw_ref[pad:pad + 1, :] * xc_ref[s].astype(F32)
    for n, k in enumerate(k for k in range(SSD_CONV) if k != pad):
        conv = conv + cw_ref[k:k + 1, :] * shifted[n * Q:(n + 1) * Q, :]
    act = _silu(conv)
    xs = act[:, :SSD_WIDTH]

    dtraw = dt_ref[s] + dtb_ref[...]
    dt = jnp.maximum(dtraw, 0.0) + jnp.log(1.0 + jnp.exp(-jnp.abs(dtraw)))
    a = -jnp.exp(alog_ref[...])
    adt = dt * a
    tri = tri_ref[...]
    cvec = _select_rows(tri, adt)
    cvt = cvec.T
    e01 = e_ref[...]
    dt_e = _select_lanes(dt, e01)
    cv_e = _select_lanes(cvec, e01)
    edge = 0 if reverse else Q - 1
    tot_e = cv_e[edge:edge + 1, :]

    li = lax.broadcasted_iota(jnp.int32, (Q, Q), 0)
    si = lax.broadcasted_iota(jnp.int32, (Q, Q), 1)
    mask = (li <= si) if reverse else (li >= si)
    lane = lax.broadcasted_iota(jnp.int32, (Q, LANES), 1)
    head0 = SSD_HEADS if reverse else 0

    xdt = xs * dt_e
    ys = []
    for g in range(SSD_NGROUPS):
        bg = act[:, SSD_WIDTH + g * SSD_DSTATE:SSD_WIDTH + (g + 1) * SSD_DSTATE]
        cg = act[:, SSD_WIDTH + (SSD_NGROUPS + g) * SSD_DSTATE:SSD_WIDTH + (SSD_NGROUPS + g + 1) * SSD_DSTATE]
        bgb = bg.astype(BF16)
        cgb = cg.astype(BF16)
        gm = _dot_nt(cgb, bgb)
        gs = slice(g * 256, (g + 1) * 256)
        xdt_g = xdt[:, gs]
        xdtb = xdt_g.astype(BF16)
        ydiag = []
        for pr in range(2):
            xpair = xdtb[:, pr * LANES:(pr + 1) * LANES]
            yh = []
            for j in range(2):
                hl = head0 + g * SSD_HPG + pr * 2 + j
                seg = cvec[:, hl:hl + 1] - cvt[hl:hl + 1, :]
                decay = jnp.exp(jnp.where(mask, seg, -jnp.inf))
                yh.append(_dot((gm * decay).astype(BF16), xpair))
            ydiag.append(jnp.where(lane < SSD_HEADDIM, yh[0], yh[1]))
        ydiag = jnp.concatenate(ydiag, axis=1)
        ht = ht_ref[s, g]
        yoff = _dot(cgb, ht.astype(BF16)) * jnp.exp(cv_e[:, gs])
        ys.append(ydiag + yoff)
        wdec = jnp.exp(tot_e[:, gs] - cv_e[:, gs])
        xw = (xdt_g * wdec).astype(BF16)
        ht_ref[s, g] = jnp.exp(tot_e[:, gs]) * ht + _dot(bg.T.astype(BF16), xw)
    y = jnp.concatenate(ys, axis=1)

    if not reverse:
        out_ref[s] = y
    else:
        y = y + yf_ref[s] + dsk_ref[...] * xs
        gated = y * _silu(z_ref[s].astype(F32))
        ms = jnp.mean(gated * gated, axis=-1, keepdims=True)
        out_ref[s] = (gated * lax.rsqrt(ms + EPS) * nw_ref[...]).astype(out_ref.dtype)


def _ssd(xbc, dt, z, conv_w, conv_b, dtb, alog, dskip_e, norm_w, B, L):
    Q = SSD_CHUNK
    nc = L // Q
    ns = SSD_SEQS if B % SSD_SEQS == 0 else 1
    hb = Q // HALO
    nhb = L // HALO
    r = jnp.arange(Q)
    pad = (SSD_CONV - 1) // 2
    col = jnp.arange(Q + 2 * HALO)[None, :]
    shift = jnp.concatenate([(col == HALO + r[:, None] + k - pad) for k in range(SSD_CONV) if k != pad],
                            axis=0).astype(BF16)
    twice = lambda m, axis: jnp.concatenate([m, m], axis=axis).astype(BF16)
    tril = twice(r[:, None] >= r[None, :], 1)
    triu = twice(r[:, None] <= r[None, :], 1)
    lane = jnp.arange(LANES)[:, None]
    ch = jnp.arange(SSD_WIDTH)[None, :] // SSD_HEADDIM
    e_f = twice(lane == ch, 0)
    e_b = twice(lane == ch + SSD_HEADS, 0)

    def call(reverse, extra_in, extra_specs, out_dtype):
        pos = (lambda c: nc - 1 - c) if reverse else (lambda c: c)
        full = lambda a: pl.BlockSpec(a.shape, lambda b, c: (0,) * a.ndim)
        chunk = lambda n: pl.BlockSpec((ns, Q, n), lambda b, c: (b, pos(c), 0))
        in_specs = [
            pl.BlockSpec((ns, HALO, 1024), lambda b, c: (b, jnp.maximum(pos(c) * hb - 1, 0), 0)),
            chunk(1024),
            pl.BlockSpec((ns, HALO, 1024), lambda b, c: (b, jnp.minimum((pos(c) + 1) * hb, nhb - 1), 0)),
            chunk(LANES),
        ]
        consts = [conv_w, conv_b, dtb, alog, triu if reverse else tril, e_b if reverse else e_f, shift]
        in_specs += [full(a) for a in consts]
        in_specs += [chunk(512) if s == "chunk" else full(a) for a, s in zip(extra_in, extra_specs)]
        return pl.pallas_call(
            functools.partial(_ssd_kernel, reverse=reverse),
            grid=(B // ns, nc),
            in_specs=in_specs,
            out_specs=chunk(512),
            out_shape=jax.ShapeDtypeStruct((B, L, 512), out_dtype),
            scratch_shapes=[pltpu.VMEM((ns, SSD_NGROUPS, SSD_DSTATE, 256), F32)],
            compiler_params=_cparams("arbitrary", "arbitrary"),
            name="ssd_bwd" if reverse else "ssd_fwd",
        )(xbc, xbc, xbc, dt, *consts, *extra_in)

    yf = call(False, [], [], F32)
    return call(True, [z, yf, dskip_e, norm_w], ["chunk", "chunk", "full", "full"], BF16)


N_POS = 3
N_FEAT = 2 * N_POS + 1
MAX_SHIFT_LOG2 = 50.0
VT_ROWS = DA_VDIM + 16
ZERO_PROB_LOG2 = 160.0
DEAD_SHIFT = 1.0e4


def _max_block_dist(head, tq, nb):
    slope_log2 = 2.0 ** (-8.0 * (head + 1) / DA_HEADS) * LOG2E
    d = 0
    while d < nb - 1 and slope_log2 * (d * tq + 1) < ZERO_PROB_LOG2:
        d += 1
    return d


def _attn_kernel(slope_ref, lamv_ref, q_ref, qall_ref, k_ref, v_ref, sw_ref, o_ref,
                 kaug_ref, kfeat_ref, qfeat_ref, corr_ref, vt_ref, vtb_ref, pt_ref, shift_ref, worst_ref,
                 *, tq, tk, L, nsub):
    h = pl.program_id(0)
    b = pl.program_id(1)
    gi = pl.program_id(2)
    nb = L // tq
    sl2 = slope_ref[h] * LOG2E
    lv = lamv_ref[...]
    lam = (jnp.exp(jnp.sum(lv[0:1] * lv[1:2], axis=-1, keepdims=True))
           - jnp.exp(jnp.sum(lv[2:3] * lv[3:4], axis=-1, keepdims=True)) + LAMBDA_INIT)

    def feat0(c):
        return DA_HEADDIM if c == 0 else 0

    def own_lanes(lane, c):
        return (lane < DA_HEADDIM) if c == 0 else (lane >= DA_HEADDIM)

    ha = lax.broadcasted_iota(jnp.int32, (LANES, LANES), 0) < DA_HEADDIM
    hb = lax.broadcasted_iota(jnp.int32, (LANES, LANES), 1) < DA_HEADDIM
    other_half = jnp.where(ha != hb, 1.0, 0.0).astype(BF16)

    @pl.when((b == 0) & (gi == 0))
    def _tables():
        lane = lax.broadcasted_iota(jnp.int32, (L, LANES), 1)
        kpos = lax.broadcasted_iota(jnp.int32, (L, LANES), 0).astype(F32) * sl2
        pieces = [t.astype(F32) for t in _split3(kpos)]
        for c in range(2):
            rel = lane - feat0(c)
            feat = jnp.where((rel >= 0) & (rel <= N_POS), 1.0, 0.0)
            for n in range(N_POS):
                feat = jnp.where(rel == N_POS + 1 + n, pieces[n], feat)
            kfeat_ref[c] = feat.astype(BF16)
            qfeat = jnp.where((rel > N_POS) & (rel < N_FEAT), -1.0, 0.0)
            for n in range(N_POS):
                qfeat = jnp.where(rel == n, pieces[n], qfeat)
            qfeat_ref[c] = qfeat.astype(BF16)
        kr = lax.broadcasted_iota(jnp.int32, (tq, tq), 0)
        qc = lax.broadcasted_iota(jnp.int32, (tq, tq), 1)
        corr_ref[...] = jnp.maximum(qc - kr, 0).astype(F32) * (2.0 * sl2)

    @pl.when(gi == 0)
    def _prep():
        kraw = k_ref[0]
        lane = lax.broadcasted_iota(jnp.int32, kraw.shape, 1)
        for c in range(2):
            kaug_ref[c] = jnp.where(own_lanes(lane, c), kraw, kfeat_ref[c])
        kf = kraw.astype(F32)
        kn2 =_dot((kf * kf).astype(BF16), other_half)
        kmax2 = jnp.max(kn2, axis=0, keepdims=True)
        qa = qall_ref[0].astype(F32)
        prod2 = _dot((qa * qa).astype(BF16), other_half) * kmax2
        bound = prod2 * lax.rsqrt(jnp.maximum(prod2, 1e-30))
        shift_all = (bound * 1.02).astype(BF16).astype(F32)
        shift_ref[...] = shift_all
        worst_ref[0] = jnp.max(shift_all)
        ones_row = lax.broadcasted_iota(jnp.int32, (VT_ROWS - DA_VDIM, tq), 0) == 0
        ones_rows = jnp.where(ones_row, 1.0, 0.0).astype(BF16)
        for jb in range(nb):
            vt = v_ref[0, jb * tq:(jb + 1) * tq, :].astype(F32).T.astype(BF16)
            vt_ref[0:DA_VDIM, jb * tq:(jb + 1) * tq] = vt
            vt_ref[DA_VDIM:, jb * tq:(jb + 1) * tq] = ones_rows
            vtb_ref[jb, 0:DA_VDIM, :] = vt
            vtb_ref[jb, DA_VDIM:, :] = ones_rows

    worst = worst_ref[0]
    lane = lax.broadcasted_iota(jnp.int32, (tq, LANES), 1)

    def sub_rows(sub):
        return slice(sub * tq, (sub + 1) * tq)

    def load_q(sub):
        return q_ref[0, sub_rows(sub), :].astype(F32)

    def finish_t(o1t, o2t, sub):
        finish((o1t - lam * o2t).T, None, sub)

    def finish(o1, o2, sub):
        o = o1 if o2 is None else o1 - lam * o2
        ms = jnp.mean(o * o, axis=-1, keepdims=True)
        o = o * lax.rsqrt(ms + EPS) * sw_ref[...] * (1.0 - LAMBDA_INIT)
        o_ref[0, sub_rows(sub), :] = o.astype(o_ref.dtype)

    def fast_path(max_dist, sub):
        qi = gi * nsub + sub
        rows = pl.ds(pl.multiple_of(qi * tq, tq), tq)
        qraw = q_ref[0, sub_rows(sub), :]
        neg_shift = (-shift_ref[rows, :]).astype(BF16)
        zero = jnp.zeros_like(qraw)
        outs = []
        for c in range(2):
            rel = lane - feat0(c)
            own = own_lanes(lane, c)
            base = jnp.where(own, qraw, jnp.where(rel == N_POS, neg_shift, zero))
            posf = qfeat_ref[c, rows, :]
            q_after = base + posf
            q_before = base - posf
            if max_dist is None:
                blocks = []
                for d in range(nb):
                    wrapped = qi + d >= nb
                    jb = jnp.where(wrapped, qi + d - nb, qi + d)
                    blocks.append((jb, q_after if d == 0 else jnp.where(wrapped, q_before, q_after), d == 0))
            else:
                q_dead = jnp.where(own, qraw, jnp.where(rel == N_POS, -DEAD_SHIFT, 0.0).astype(BF16))
                blocks = []
                for r in range(-max_dist, max_dist + 1):
                    jb = qi + r
                    side = q_before if r < 0 else q_after
                    inside = (jb >= 0) & (jb < nb)
                    blocks.append((jnp.clip(jb, 0, nb - 1), side if r == 0 else jnp.where(inside, side, q_dead),
                                   r == 0))
            acc = None
            for jb, qsel, diagonal in blocks:
                dk = pl.ds(pl.multiple_of(jb * tq, tq), tq)
                st = _dot_nt(kaug_ref[c, dk, :], qsel)
                if diagonal:
                    st = st - corr_ref[...]
                p = jnp.exp2(st).astype(BF16)
                if max_dist is None:
                    pt_ref[sub, c, dk, :] = p
                else:
                    part = _dot(vtb_ref[jb], p)
                    acc = part if acc is None else acc + part
            if max_dist is None:
                acc = _dot(vt_ref[...], pt_ref[sub, c])
            outs.append(acc[:DA_VDIM] * (1.0 / acc[DA_VDIM:DA_VDIM + 1]))
        finish_t(outs[0], outs[1], sub)

    fast = worst <= MAX_SHIFT_LOG2
    dists = [_max_block_dist(hh, tq, nb) for hh in range(DA_HEADS)]
    dists = [d if 2 * d + 1 < nb else None for d in dists]
    for dist in set(dists):
        heads = [hh for hh in range(DA_HEADS) if dists[hh] == dist]
        in_heads = functools.reduce(lambda a, b: a | b, [h == hh for hh in heads])

        @pl.when(fast & in_heads)
        def _(dist=dist):
            for sub in range(nsub):
                fast_path(dist, sub)

    def online_path(sub):
        qi = gi * nsub + sub
        qf = load_q(sub)
        zero = jnp.zeros_like(qf)
        qm = [jnp.where(own_lanes(lane, c), qf, zero).astype(BF16) for c in range(2)]
        qk0 =(qi * tq + lax.broadcasted_iota(jnp.int32, (tq, tk), 0)
               - lax.broadcasted_iota(jnp.int32, (tq, tk), 1)).astype(F32)

        def body(j, carry):
            kb = k_ref[0, pl.ds(pl.multiple_of(j * tk, tk), tk), :]
            vb = v_ref[0, pl.ds(pl.multiple_of(j * tk, tk), tk), :]
            bias = sl2 * jnp.abs(qk0 - lax.convert_element_type(j * tk, F32))
            out = []
            for c in range(2):
                m, l, acc = carry[c]
                s = _dot_nt(qm[c], kb) - bias
                m_new = jnp.maximum(m, jnp.max(s, axis=-1, keepdims=True))
                p = jnp.exp2(s - m_new)
                alpha = jnp.exp2(m - m_new)
                l_new = alpha * l + jnp.sum(p, axis=-1, keepdims=True)
                acc_new = alpha * acc + _dot(p.astype(BF16), vb)
                out.append((m_new, l_new, acc_new))
            return tuple(out)

        init = tuple((jnp.full((tq, 1), -jnp.inf, F32), jnp.zeros((tq, 1), F32),
                      jnp.zeros((tq, DA_VDIM), F32)) for _ in range(2))
        (m1, l1, a1), (m2, l2, a2) = lax.fori_loop(0, L // tk, body, init)
        finish(a1 * (1.0 / l1), a2 * (1.0 / l2), sub)

    @pl.when(jnp.logical_not(fast))
    def _():
        for sub in range(nsub):
            online_path(sub)


def _diff_attention(q, k, v, slopes, lamv, subln_w, B, L):
    tq, tk = min(TQ, L), min(TK, L)
    nsub = TQ_SUBS if L % (tq * TQ_SUBS) == 0 else 1
    return pl.pallas_call(
        functools.partial(_attn_kernel, tq=tq, tk=tk, L=L, nsub=nsub),
        grid=(DA_HEADS, B, L // (tq * nsub)),
        in_specs=[
            pl.BlockSpec(memory_space=pltpu.SMEM),
            pl.BlockSpec(lamv.shape, lambda h, b, i: (0, 0)),
            pl.BlockSpec((1, tq * nsub, LANES), lambda h, b, i: (b, i, h)),
            pl.BlockSpec((1, L, LANES), lambda h, b, i: (b, 0, h)),
            pl.BlockSpec((1, L, LANES), lambda h, b, i: (b, 0, h)),
            pl.BlockSpec((1, L, LANES), lambda h, b, i: (b, 0, h)),
            pl.BlockSpec(subln_w.shape, lambda h, b, i: (0, 0)),
        ],
        out_specs=pl.BlockSpec((1, tq * nsub, LANES), lambda h, b, i: (b, i, h)),
        out_shape=jax.ShapeDtypeStruct((B, L, 512), BF16),
        scratch_shapes=[pltpu.VMEM((2, L, LANES), BF16),
                        pltpu.VMEM((2, L, LANES), BF16),
                        pltpu.VMEM((2, L, LANES), BF16),
                        pltpu.VMEM((tq, tq), F32),
                        pltpu.VMEM((VT_ROWS, L), BF16),
                        pltpu.VMEM((L // tq, VT_ROWS, tq), BF16),
                        pltpu.VMEM((nsub, 2, L, tq), BF16),
                        pltpu.VMEM((L, LANES), F32),
                        pltpu.SMEM((1,), F32)],
        compiler_params=_cparams("arbitrary", "arbitrary", "arbitrary"),
        name="diff_attn",
    )(slopes, lamv, q, q, k, v, subln_w)


def _pair_tables():
    lo, hi = [], []
    for g in range(N_GROUPS):
        for a in range(EPG):
            for b in range(a + 1, EPG):
                lo.append(g * EPG + a)
                hi.append(g * EPG + b)
    return lo, hi


PAIRS_PER_GROUP = EPG * (EPG - 1) // 2
N_BUCKETS = N_GROUPS * PAIRS_PER_GROUP
GATE_LANE = 0
ROW_TAIL = LANES


def _router_kernel(x_ref, ys_ref, ya_ref, wo_ref, nw_ref, wr_ref, br_ref, tri_ref,
                   h_ref, u_ref, meta_ref, cnt_ref, carry_ref):
    i = pl.program_id(0)

    @pl.when(i == 0)
    def _():
        carry_ref[...] = jnp.zeros_like(carry_ref)

    sub = tri_ref.shape[0]
    for s in range(x_ref.shape[0] // sub):
        rows = pl.ds(s * sub, sub)
        _route_tile(x_ref.at[rows], ys_ref.at[rows], ya_ref.at[rows], wo_ref, nw_ref, wr_ref, br_ref, tri_ref,
                    h_ref.at[rows], u_ref.at[rows], meta_ref.at[rows], cnt_ref, carry_ref)


def _route_tile(x_ref, ys_ref, ya_ref, wo_ref, nw_ref, wr_ref, br_ref, tri_ref,
                h_ref, u_ref, meta_ref, cnt_ref, carry_ref):
    h = x_ref[...] + _dot(ys_ref[...], wo_ref[0:512, :]) + _dot(ya_ref[...], wo_ref[512:1024, :])
    h_ref[...] = h
    D = h.shape[1]
    ms = jnp.mean(h * h, axis=-1, keepdims=True)
    u = h * lax.rsqrt(ms + EPS) * nw_ref[...]
    u_ref[:, 0:D // 2] = _pack_bf16_pairs(u)
    logits = _dot(u.astype(BF16), wr_ref[...]) + br_ref[...]

    tm = logits.shape[0]
    li = lax.broadcasted_iota(jnp.int32, (tm, LANES), 1)
    lf = li.astype(F32)
    ninf = jnp.float32(-jnp.inf)
    big = jnp.float32(1e9)
    gl = jnp.where(li < N_GROUPS, logits, ninf)
    gmax = jnp.max(gl, axis=-1, keepdims=True)
    gidx = jnp.min(jnp.where(gl == gmax, lf, big), axis=-1, keepdims=True)
    pg = 1.0 / jnp.sum(jnp.exp(gl - gmax), axis=-1, keepdims=True)
    lane_grp = ((li - EXP_LANE0) >> 3).astype(F32)
    in_grp = (li >= EXP_LANE0) & (li < EXP_LANE0 + N_EXPERTS) & (lane_grp == gidx)
    sl = jnp.where(in_grp, logits, ninf)
    m1 = jnp.max(sl, axis=-1, keepdims=True)
    i1 = jnp.min(jnp.where(sl == m1, lf, big), axis=-1, keepdims=True)
    sl2 = jnp.where(lf == i1, ninf, sl)
    m2 = jnp.max(sl2, axis=-1, keepdims=True)
    i2 = jnp.min(jnp.where(sl2 == m2, lf, big), axis=-1, keepdims=True)
    t = jnp.exp(m2 - m1)
    w1 = 1.0 / (1.0 + t)
    g1 = pg * w1
    g2 = pg * (t * w1)
    a1 = i1 - EXP_LANE0 - gidx * EPG
    a2 = i2 - EXP_LANE0 - gidx * EPG
    lo = jnp.minimum(a1, a2)
    hi = jnp.maximum(a1, a2)
    bucket = gidx * PAIRS_PER_GROUP + (lo * (EPG - 1) - lo * (lo - 1.0) * 0.5) + (hi - lo - 1.0)
    first_is_lo = a1 < a2
    g_lo = jnp.where(first_is_lo, g1, g2)
    g_hi = jnp.where(first_is_lo, g2, g1)
    ohb = lf == bucket
    oh = jnp.where(ohb, 1.0, 0.0)
    prefix = _dot(tri_ref[...], oh.astype(BF16)) + carry_ref[0:1, :]
    rank = jnp.sum(jnp.where(ohb, prefix, 0.0), axis=-1, keepdims=True)
    new_carry = carry_ref[0:1, :] + jnp.sum(oh, axis=0, keepdims=True)
    carry_ref[...] = jnp.broadcast_to(new_carry, carry_ref.shape)
    cnt_ref[...] = jnp.broadcast_to(new_carry, cnt_ref.shape)

    meta_ref[...] = jnp.where(li == 0, bucket, jnp.where(li == 1, rank, 0.0))
    gates = jnp.where(li == GATE_LANE, g_lo, jnp.where(li == GATE_LANE + 1, g_hi, 0.0))
    u_ref[:, D // 2:] = pltpu.bitcast(gates, jnp.uint32)


def _out_router(x2, y_ssd, y_da, w_out, norm_w, w_r, b_r):
    T, D = x2.shape
    tm = min(TM_ROUTE, T)
    r = jnp.arange(min(ROUTE_SUB, tm))
    tri = (r[:, None] > r[None, :]).astype(BF16)
    row = lambda n: pl.BlockSpec((tm, n), lambda i: (i, 0))
    full = lambda a: pl.BlockSpec(a.shape, lambda i: (0,) * a.ndim)
    return pl.pallas_call(
        _router_kernel,
        grid=(T // tm,),
        in_specs=[row(D), row(512), row(512), full(w_out), full(norm_w), full(w_r), full(b_r), full(tri)],
        out_specs=[row(D), row(D // 2 + ROW_TAIL), row(LANES), pl.BlockSpec((SUBLANES, LANES), lambda i: (0, 0))],
        out_shape=[
            jax.ShapeDtypeStruct((T, D), F32),
            jax.ShapeDtypeStruct((T, D // 2 + ROW_TAIL), jnp.uint32),
            jax.ShapeDtypeStruct((T, LANES), F32),
            jax.ShapeDtypeStruct((SUBLANES, LANES), F32),
        ],
        scratch_shapes=[pltpu.VMEM((SUBLANES, LANES), F32)],
        compiler_params=_cparams("arbitrary"),
        name="out_router",
    )(x2, y_ssd, y_da, w_out, norm_w, w_r, b_r, tri)


def _slotmap_kernel(cnt_ref, idx_ref, slots_ref, inv_ref, ea_ref, eb_ref, valid_ref, nused_ref,
                    pstart_ref, *, tm, n_blocks):
    i = pl.program_id(0)

    @pl.when(i == 0)
    def _():
        pair_lo, pair_hi = _pair_tables()
        acc = jnp.int32(0)
        for q in range(N_BUCKETS):
            cnt = cnt_ref[q]
            padded = ((cnt + (MOE_ROWS - 1)) // MOE_ROWS) * MOE_ROWS
            pstart_ref[q] = acc
            end = acc + padded

            def fill(j, _, q=q, acc=acc, cnt=cnt):
                ea_ref[j] = jnp.int32(pair_lo[q])
                eb_ref[j] = jnp.int32(pair_hi[q])
                valid_ref[j] = jnp.minimum(cnt - (j * MOE_ROWS - acc), MOE_ROWS)
                return 0

            lax.fori_loop(acc // MOE_ROWS, end // MOE_ROWS, fill, 0)

            def pad(s, _):
                inv_ref[s] = jnp.int32(0)
                return 0

            lax.fori_loop(acc + cnt, end, pad, 0)
            acc = end
        nused = acc // MOE_ROWS
        nused_ref[0] = nused

        def fill_tail(j, _):
            ea_ref[j] = jnp.int32(0)
            eb_ref[j] = jnp.int32(0)
            valid_ref[j] = jnp.int32(0)
            return 0

        lax.fori_loop(nused, n_blocks, fill_tail, 0)
        lax.fori_loop(acc, n_blocks * MOE_ROWS, pad, 0)

    def place(t8, _):
        for r in range(SUBLANES):
            t = t8 * SUBLANES + r
            slot = pstart_ref[idx_ref[0, t]] + idx_ref[1, t]
            slots_ref[0, t] = slot
            inv_ref[slot] = i * tm + t
        return 0

    lax.fori_loop(0, tm // SUBLANES, place, 0)


def _slot_map(counts, idx, n_blocks):
    T = idx.shape[1]
    tm = min(TM_DISP, T)
    n_slots = n_blocks * MOE_ROWS
    grid_spec = pltpu.PrefetchScalarGridSpec(
        num_scalar_prefetch=1,
        grid=(T // tm,),
        in_specs=[pl.BlockSpec((2, tm), lambda i, cnt: (0, i), memory_space=pltpu.SMEM)],
        out_specs=[
            pl.BlockSpec((1, tm), lambda i, cnt: (0, i), memory_space=pltpu.SMEM),
            pl.BlockSpec(memory_space=pltpu.SMEM),
            pl.BlockSpec(memory_space=pltpu.SMEM),
            pl.BlockSpec(memory_space=pltpu.SMEM),
            pl.BlockSpec(memory_space=pltpu.SMEM),
            pl.BlockSpec(memory_space=pltpu.SMEM),
        ],
        scratch_shapes=[pltpu.SMEM((LANES,), jnp.int32)],
    )
    return pl.pallas_call(
        functools.partial(_slotmap_kernel, tm=tm, n_blocks=n_blocks),
        grid_spec=grid_spec,
        out_shape=[
            jax.ShapeDtypeStruct((1, T), jnp.int32),
            jax.ShapeDtypeStruct((n_slots,), jnp.int32),
            jax.ShapeDtypeStruct((n_blocks,), jnp.int32),
            jax.ShapeDtypeStruct((n_blocks,), jnp.int32),
            jax.ShapeDtypeStruct((n_blocks,), jnp.int32),
            jax.ShapeDtypeStruct((1,), jnp.int32),
        ],
        compiler_params=_cparams("arbitrary"),
        name="slot_map",
    )(counts, idx)


def _moe_kernel(ea_ref, eb_ref, nused_ref, valid_ref, inv_ref, u_hbm,
                wga_ref, wua_ref, wda_ref, wgb_ref, wub_ref, wdb_ref, o_ref, x_ref, sems):
    j = pl.program_id(0)
    nused = nused_ref[0]
    H = o_ref.shape[1]
    cur = j % 2

    def whole(buf):
        return pltpu.make_async_copy(x_ref.at[buf], x_ref.at[buf], sems.at[buf])

    def gather(blk, buf):
        nvalid = valid_ref[blk]
        for r in range(MOE_ROWS):
            tok = jnp.where(r < nvalid, inv_ref[blk * MOE_ROWS + r], 0)
            pltpu.make_async_copy(u_hbm.at[pl.ds(tok, 1)],
                                  x_ref.at[buf, r // SUBLANES, pl.ds(r % SUBLANES, 1)], sems.at[buf]).start()

    @pl.when(j == 0)
    def _():
        gather(0, 0)

    @pl.when(j < nused)
    def _():
        whole(cur).wait()
        gather(jnp.minimum(j + 1, nused - 1), 1 - cur)
        blk = x_ref[cur].reshape(MOE_ROWS, -1)
        row = lax.broadcasted_iota(jnp.int32, blk.shape, 0)
        blk = jnp.where(row < valid_ref[j], blk, jnp.zeros_like(blk))
        x = _unpack_bf16_pairs(blk[:, 0:H]).astype(BF16)
        tail = pltpu.bitcast(blk[:, H:], F32)
        out = None
        for (wg, wu, wd), lane in (((wga_ref, wua_ref, wda_ref), GATE_LANE),
                                   ((wgb_ref, wub_ref, wdb_ref), GATE_LANE + 1)):
            hid = _silu(_dot(x, wg[0])) * _dot(x, wu[0])
            y = _dot(hid.astype(BF16), wd[0]) * tail[:, lane:lane + 1]
            out = y if out is None else out + y
        o_ref[...] = _pack_bf16_pairs(out)

    @pl.when(j == nused - 1)
    def _():
        whole(1 - cur).wait()

    @pl.when(j >= nused)
    def _():
        o_ref[...] = jnp.zeros_like(o_ref)


def _moe(block_ea, block_eb, nused, valid, inv, u, wg, wu, wd, n_blocks):
    W = u.shape[1]
    H = W - ROW_TAIL
    used = lambda j, nu: jnp.minimum(j, nu[0] - 1)
    wspec_a = lambda a: pl.BlockSpec((1,) + a.shape[1:], lambda j, ea, eb, nu, va, iv: (ea[used(j, nu)], 0, 0))
    wspec_b = lambda a: pl.BlockSpec((1,) + a.shape[1:], lambda j, ea, eb, nu, va, iv: (eb[used(j, nu)], 0, 0))
    grid_spec = pltpu.PrefetchScalarGridSpec(
        num_scalar_prefetch=5,
        grid=(n_blocks,),
        in_specs=[
            pl.BlockSpec(memory_space=pl.ANY),
            wspec_a(wg), wspec_a(wu), wspec_a(wd), wspec_b(wg), wspec_b(wu), wspec_b(wd),
        ],
        out_specs=pl.BlockSpec((MOE_ROWS, H), lambda j, ea, eb, nu, va, iv: (j, 0)),
        scratch_shapes=[pltpu.VMEM((2, MOE_ROWS // SUBLANES, SUBLANES, W), u.dtype),
                        pltpu.SemaphoreType.DMA((2,))],
    )
    return pl.pallas_call(
        _moe_kernel,
        grid_spec=grid_spec,
        out_shape=jax.ShapeDtypeStruct((n_blocks * MOE_ROWS, H), jnp.uint32),
        compiler_params=_cparams("arbitrary"),
        name="moe_mlp",
    )(block_ea, block_eb, nused, valid, inv, u, wg, wu, wd, wg, wu, wd)


def _combine_kernel(slots_ref, next_slots_ref, h_ref, nw_ref, eo_hbm, o_ref, g_ref, sems, *, tm):
    i = pl.program_id(0)
    n = pl.num_programs(0)
    cur = i % 2

    def gather_tile(sref, buf):
        def start(i8, _):
            for r in range(SUBLANES):
                pltpu.make_async_copy(eo_hbm.at[pl.ds(sref[0, i8 * SUBLANES + r], 1)],
                                      g_ref.at[buf, i8, pl.ds(r, 1)], sems.at[buf]).start()
            return 0

        lax.fori_loop(0, tm // SUBLANES, start, 0)

    @pl.when(i == 0)
    def _():
        gather_tile(slots_ref, 0)

    @pl.when(i + 1 < n)
    def _():
        gather_tile(next_slots_ref, 1 - cur)

    pltpu.make_async_copy(g_ref.at[cur], g_ref.at[cur], sems.at[cur]).wait()

    y = h_ref[...] + _unpack_bf16_pairs(g_ref[cur].reshape(tm, -1))
    ms = jnp.mean(y * y, axis=-1, keepdims=True)
    o_ref[...] = y * lax.rsqrt(ms + EPS) * nw_ref[...]


def _combine(slots, h, norm_w, eo):
    T, D = h.shape
    tm = min(TM_COMB, T)
    last = T // tm - 1
    return pl.pallas_call(
        functools.partial(_combine_kernel, tm=tm),
        grid=(T // tm,),
        in_specs=[
            pl.BlockSpec((1, tm), lambda i: (0, i), memory_space=pltpu.SMEM),
            pl.BlockSpec((1, tm), lambda i: (0, jnp.minimum(i + 1, last)), memory_space=pltpu.SMEM),
            pl.BlockSpec((tm, D), lambda i: (i, 0)),
            pl.BlockSpec(norm_w.shape, lambda i: (0, 0)),
            pl.BlockSpec(memory_space=pl.ANY),
        ],
        out_specs=pl.BlockSpec((tm, D), lambda i: (i, 0)),
        out_shape=jax.ShapeDtypeStruct((T, D), F32),
        scratch_shapes=[pltpu.VMEM((2, tm // SUBLANES, SUBLANES, D // 2), jnp.uint32),
                        pltpu.SemaphoreType.DMA((2,))],
        compiler_params=_cparams("arbitrary"),
        name="combine",
    )(slots, slots, h, norm_w, eo)


def _pad_lanes(v, offset=0):
    v = v.astype(F32).reshape(-1)
    return jnp.zeros((1, LANES), F32).at[0, offset:offset + v.shape[0]].set(v)


def kernel(x, norm_mix_w, w_in, conv_w, conv_b, dt_bias_fwd, dt_bias_bwd, a_log_fwd, a_log_bwd, ssd_d,
           ssd_norm_w, lambda_q1, lambda_k1, lambda_q2, lambda_k2, subln_w, w_out, norm_ffn_w,
           w_router_group, b_router_group, w_router_exp, b_router_exp, w_exp_gate, w_exp_up, w_exp_down,
           norm_final_w):
    B, L, D = x.shape
    T = B * L
    x2 = x.reshape(T, D)
    l = 0

    w = w_in[l]
    o_z, o_xbc, o_dt = 512, 512 + 1024, 512 + 1024 + 16
    o_q, o_k = o_dt + 512, o_dt + 1024
    w_main = jnp.concatenate(
        [w[:, :o_xbc], w[:, o_dt:o_q] * (DA_HEADDIM ** -0.5 * LOG2E), w[:, o_q:]], axis=1).astype(BF16)
    w_dt = jnp.zeros((D, LANES), F32).at[:, :16].set(w[:, o_xbc:o_dt]).astype(BF16)

    experts = [w_exp_gate[l], w_exp_up[l], w_exp_down[l]]
    z, xbc, dt, q, k, v, *experts_bf16 = _in_proj(
        x2, norm_mix_w[l].reshape(1, D), w_main, w_dt, [a.reshape(-1, a.shape[-1]) for a in experts])
    wg_b, wu_b, wd_b = [c.reshape(a.shape) for c, a in zip(experts_bf16, experts)]

    dtb = _pad_lanes(jnp.concatenate([dt_bias_fwd[l], dt_bias_bwd[l]]))
    alog = _pad_lanes(jnp.concatenate([a_log_fwd[l], a_log_bwd[l]]))
    dskip_e = jnp.repeat(ssd_d[l].astype(F32), SSD_HEADDIM).reshape(1, SSD_WIDTH)
    y_ssd = _ssd(xbc.reshape(B, L, 1024), dt.reshape(B, L, LANES), z.reshape(B, L, 512),
                 conv_w[l].astype(F32), conv_b[l].reshape(1, -1).astype(F32), dtb, alog, dskip_e,
                 ssd_norm_w[l].reshape(1, -1).astype(F32), B, L)

    slopes = jnp.power(2.0, -8.0 * jnp.arange(1, DA_HEADS + 1, dtype=F32) / DA_HEADS)
    lamv = jnp.concatenate([_pad_lanes(lambda_q1[l]), _pad_lanes(lambda_k1[l]),
                            _pad_lanes(lambda_q2[l]), _pad_lanes(lambda_k2[l]),
                            jnp.zeros((4, LANES), F32)], axis=0)
    y_da = _diff_attention(q.reshape(B, L, 512), k.reshape(B, L, 512), v.reshape(B, L, 512),
                           slopes, lamv, subln_w[l].reshape(1, -1).astype(F32), B, L)

    w_r = (jnp.zeros((D, LANES), F32).at[:, :N_GROUPS].set(w_router_group[l])
           .at[:, EXP_LANE0:EXP_LANE0 + N_EXPERTS].set(w_router_exp[l])).astype(BF16)
    b_r = _pad_lanes(jnp.concatenate([b_router_group[l], b_router_exp[l]]))
    h, u, meta, cnt = _out_router(x2, y_ssd.reshape(T, 512), y_da.reshape(T, 512), w_out[l].astype(BF16),
                                  norm_ffn_w[l].reshape(1, D), w_r, b_r)

    n_blocks = T // MOE_ROWS + N_BUCKETS
    counts = cnt[0].astype(jnp.int32)
    idx = meta[:, :2].astype(jnp.int32).T
    slots, inv, block_ea, block_eb, valid, nused = _slot_map(counts, idx, n_blocks)
    eo = _moe(block_ea, block_eb, nused, valid, inv, u, wg_b, wu_b, wd_b, n_blocks)
    out = _combine(slots, h, norm_final_w.reshape(1, D), eo)
    return out.reshape(B, L, D)
```

```python
import functools
import math

import jax
import jax.numpy as jnp
from jax import lax
from jax.experimental import pallas as pl
from jax.experimental.pallas import tpu as pltpu

F32 = jnp.float32
BF16 = jnp.bfloat16

EPS = 1e-6
LOG2E = math.log2(math.e)
LANES = 128
SUBLANES = 8
HALO = 16
VMEM_LIMIT = 48 * 1024 * 1024

SSD_WIDTH = 512
SSD_HEADDIM = 64
SSD_HEADS = 8
SSD_NGROUPS = 2
SSD_HPG = 4
SSD_DSTATE = 128
SSD_CONV = 5
SSD_CHUNK = 128
SSD_CONV_CH = 1024
DA_HEADDIM = 64
DA_VDIM = 128
DA_HEADS = 4
N_GROUPS = 4
EPG = 8
N_EXPERTS = 32
D_EXPERT = 512
LAMBDA_INIT = 0.8 - 0.6 * math.exp(-0.3 * 0)

TM_PROJ = 1024
PROJ_SUB = 512
SSD_SEQS = 4
TQ = 512
TQ_SUBS = 2
TK = 512
TM_ROUTE = 1024
ROUTE_SUB = 512
TM_DISP = 1024
MOE_ROWS = 256
TM_COMB = 512
EXP_LANE0 = 4


def _cparams(*sem):
    return pltpu.CompilerParams(dimension_semantics=sem, vmem_limit_bytes=VMEM_LIMIT)


def _dot(a, b):
    return jnp.dot(a, b, preferred_element_type=F32)


def _dot_nt(a, b):
    return lax.dot_general(a, b, (((1,), (1,)), ((), ())), preferred_element_type=F32)


def _split3(v):
    hi = v.astype(BF16)
    r1 = v - hi.astype(F32)
    mid = r1.astype(BF16)
    lo = (r1 - mid.astype(F32)).astype(BF16)
    return hi, mid, lo


def _split2(v):
    hi = v.astype(BF16)
    return hi, (v - hi.astype(F32)).astype(BF16)


def _select_rows(a01x2, v):
    hi, lo = _split2(v)
    return _dot(a01x2, jnp.concatenate([hi, lo], axis=0))


def _select_lanes(v, b01x2):
    hi, lo = _split2(v)
    return _dot(jnp.concatenate([hi, lo], axis=1), b01x2)


def _silu(x):
    return x * (1.0 / (1.0 + jnp.exp(-x)))


def _pack_bf16_pairs(x):
    n = x.shape[1] // 2
    bits = pltpu.bitcast(x.astype(BF16).astype(F32), jnp.uint32)
    return (bits[:, :n] >> 16) | bits[:, n:]


def _unpack_bf16_pairs(w):
    lo = pltpu.bitcast(w << 16, F32)
    hi = pltpu.bitcast(w & jnp.uint32(0xFFFF0000), F32)
    return jnp.concatenate([lo, hi], axis=1)


def _inproj_kernel(x_ref, nw_ref, w_ref, wdt_ref, *refs, n_cast):
    cast_in, out_refs, cast_out = refs[:n_cast], refs[n_cast:len(refs) - n_cast], refs[len(refs) - n_cast:]
    for s in range(x_ref.shape[0] // PROJ_SUB):
        rows = pl.ds(s * PROJ_SUB, PROJ_SUB)
        _inproj_rows(x_ref.at[rows], nw_ref, w_ref, wdt_ref, *[r.at[rows] for r in out_refs])
    for src, dst in zip(cast_in, cast_out):
        dst[...] = src[...].astype(BF16)


def _inproj_rows(x_ref, nw_ref, w_ref, wdt_ref, z_ref, xbc_ref, dt_ref, q_ref, k_ref, v_ref):
    x = x_ref[...]
    ms = jnp.mean(x * x, axis=-1, keepdims=True)
    u = (x * lax.rsqrt(ms + EPS) * nw_ref[...]).astype(BF16)
    z_ref[...] = _dot(u, w_ref[:, 0:512]).astype(BF16)
    xbc_ref[...] = _dot(u, w_ref[:, 512:1536]).astype(BF16)
    q_ref[...] = _dot(u, w_ref[:, 1536:2048]).astype(BF16)
    k_ref[...] = _dot(u, w_ref[:, 2048:2560]).astype(BF16)
    v_ref[...] = _dot(u, w_ref[:, 2560:3072]).astype(BF16)
    dt_ref[...] = _dot(u, wdt_ref[...])


def _in_proj(x2, norm_w, w_main, w_dt, to_bf16):
    T, D = x2.shape
    tm = min(TM_PROJ, T)
    steps = T // tm
    row = lambda n: pl.BlockSpec((tm, n), lambda i: (i, 0))
    full = lambda a: pl.BlockSpec(a.shape, lambda i: (0,) * a.ndim)
    piece = lambda a: pl.BlockSpec((a.shape[0] // steps, a.shape[1]), lambda i: (i, 0))
    return pl.pallas_call(
        functools.partial(_inproj_kernel, n_cast=len(to_bf16)),
        grid=(steps,),
        in_specs=[row(D), full(norm_w), full(w_main), full(w_dt)] + [piece(a) for a in to_bf16],
        out_specs=[row(512), row(1024), row(LANES), row(512), row(512), row(512)] + [piece(a) for a in to_bf16],
        out_shape=[
            jax.ShapeDtypeStruct((T, 512), BF16),
            jax.ShapeDtypeStruct((T, 1024), BF16),
            jax.ShapeDtypeStruct((T, LANES), F32),
            jax.ShapeDtypeStruct((T, 512), BF16),
            jax.ShapeDtypeStruct((T, 512), BF16),
            jax.ShapeDtypeStruct((T, 512), BF16),
        ] + [jax.ShapeDtypeStruct(a.shape, BF16) for a in to_bf16],
        compiler_params=_cparams("arbitrary"),
        name="in_proj",
    )(x2, norm_w, w_main, w_dt, *to_bf16)


def _ssd_kernel(*refs, reverse):
    c = pl.program_id(1)
    nc = pl.num_programs(1)
    pos = (nc - 1 - c) if reverse else c
    ht_ref = refs[-1]

    @pl.when(c == 0)
    def _():
        ht_ref[...] = jnp.zeros_like(ht_ref)

    for s in range(ht_ref.shape[0]):
        _ssd_sequence(s, pos, nc, refs, reverse)


def _ssd_sequence(s, pos, nc, refs, reverse):
    if reverse:
        (xp_ref, xc_ref, xn_ref, dt_ref, cw_ref, cb_ref, dtb_ref, alog_ref, tri_ref, e_ref, sh_ref,
         z_ref, yf_ref, dsk_ref, nw_ref, out_ref, ht_ref) = refs
    else:
        (xp_ref, xc_ref, xn_ref, dt_ref, cw_ref, cb_ref, dtb_ref, alog_ref, tri_ref, e_ref, sh_ref,
         out_ref, ht_ref) = refs
    Q = SSD_CHUNK

    halo_zero = jnp.zeros((HALO, SSD_CONV_CH), BF16)
    ext = jnp.concatenate([jnp.where(pos > 0, xp_ref[s], halo_zero), xc_ref[s],
                           jnp.where(pos < nc - 1, xn_ref[s], halo_zero)], axis=0)
    shifted = _dot(sh_ref[...], ext)
    pad = (SSD_CONV - 1) // 2
    conv = cb_ref[...] + cw_ref[pad:pad + 1, :] * xc_ref[s].astype(F32)
    for n, k in enumerate(k for k in range(SSD_CONV) if k != pad):
        conv = conv + cw_ref[k:k + 1, :] * shifted[n * Q:(n + 1) * Q, :]
    act = _silu(conv)
    xs = act[:, :SSD_WIDTH]

    dtraw = dt_ref[s] + dtb_ref[...]
    dt = jnp.maximum(dtraw, 0.0) + jnp.log(1.0 + jnp.exp(-jnp.abs(dtraw)))
    a = -jnp.exp(alog_ref[...])
    adt = dt * a
    tri = tri_ref[...]
    cvec = _select_rows(tri, adt)
    cvt = cvec.T
    e01 = e_ref[...]
    dt_e = _select_lanes(dt, e01)
    cv_e = _select_lanes(cvec, e01)
    edge = 0 if reverse else Q - 1
    tot_e = cv_e[edge:edge + 1, :]

    li = lax.broadcasted_iota(jnp.int32, (Q, Q), 0)
    si = lax.broadcasted_iota(jnp.int32, (Q, Q), 1)
    mask = (li <= si) if reverse else (li >= si)
    lane = lax.broadcasted_iota(jnp.int32, (Q, LANES), 1)
    head0 = SSD_HEADS if reverse else 0

    xdt = xs * dt_e
    ys = []
    for g in range(SSD_NGROUPS):
        bg = act[:, SSD_WIDTH + g * SSD_DSTATE:SSD_WIDTH + (g + 1) * SSD_DSTATE]
        cg = act[:, SSD_WIDTH + (SSD_NGROUPS + g) * SSD_DSTATE:SSD_WIDTH + (SSD_NGROUPS + g + 1) * SSD_DSTATE]
        bgb = bg.astype(BF16)
        cgb = cg.astype(BF16)
        gm = _dot_nt(cgb, bgb)
        gs = slice(g * 256, (g + 1) * 256)
        xdt_g = xdt[:, gs]
        xdtb = xdt_g.astype(BF16)
        ydiag = []
        for pr in range(2):
            xpair = xdtb[:, pr * LANES:(pr + 1) * LANES]
            yh = []
            for j in range(2):
                hl = head0 + g * SSD_HPG + pr * 2 + j
                seg = cvec[:, hl:hl + 1] - cvt[hl:hl + 1, :]
                decay = jnp.exp(jnp.where(mask, seg, -jnp.inf))
                yh.append(_dot((gm * decay).astype(BF16), xpair))
            ydiag.append(jnp.where(lane < SSD_HEADDIM, yh[0], yh[1]))
        ydiag = jnp.concatenate(ydiag, axis=1)
        ht = ht_ref[s, g]
        yoff = _dot(cgb, ht.astype(BF16)) * jnp.exp(cv_e[:, gs])
        ys.append(ydiag + yoff)
        wdec = jnp.exp(tot_e[:, gs] - cv_e[:, gs])
        xw = (xdt_g * wdec).astype(BF16)
        ht_ref[s, g] = jnp.exp(tot_e[:, gs]) * ht + _dot(bg.T.astype(BF16), xw)
    y = jnp.concatenate(ys, axis=1)

    if not reverse:
        out_ref[s] = y
    else:
        y = y + yf_ref[s] + dsk_ref[...] * xs
        gated = y * _silu(z_ref[s].astype(F32))
        ms = jnp.mean(gated * gated, axis=-1, keepdims=True)
        out_ref[s] = (gated * lax.rsqrt(ms + EPS) * nw_ref[...]).astype(out_ref.dtype)


def _ssd(xbc, dt, z, conv_w, conv_b, dtb, alog, dskip_e, norm_w, B, L):
    Q = SSD_CHUNK
    nc = L // Q
    ns = SSD_SEQS if B % SSD_SEQS == 0 else 1
    hb = Q // HALO
    nhb = L // HALO
    r = jnp.arange(Q)
    pad = (SSD_CONV - 1) // 2
    col = jnp.arange(Q + 2 * HALO)[None, :]
    shift = jnp.concatenate([(col == HALO + r[:, None] + k - pad) for k in range(SSD_CONV) if k != pad],
                            axis=0).astype(BF16)
    twice = lambda m, axis: jnp.concatenate([m, m], axis=axis).astype(BF16)
    tril = twice(r[:, None] >= r[None, :], 1)
    triu = twice(r[:, None] <= r[None, :], 1)
    lane = jnp.arange(LANES)[:, None]
    ch = jnp.arange(SSD_WIDTH)[None, :] // SSD_HEADDIM
    e_f = twice(lane == ch, 0)
    e_b = twice(lane == ch + SSD_HEADS, 0)

    def call(reverse, extra_in, extra_specs, out_dtype):
        pos = (lambda c: nc - 1 - c) if reverse else (lambda c: c)
        full = lambda a: pl.BlockSpec(a.shape, lambda b, c: (0,) * a.ndim)
        chunk = lambda n: pl.BlockSpec((ns, Q, n), lambda b, c: (b, pos(c), 0))
        in_specs = [
            pl.BlockSpec((ns, HALO, 1024), lambda b, c: (b, jnp.maximum(pos(c) * hb - 1, 0), 0)),
            chunk(1024),
            pl.BlockSpec((ns, HALO, 1024), lambda b, c: (b, jnp.minimum((pos(c) + 1) * hb, nhb - 1), 0)),
            chunk(LANES),
        ]
        consts = [conv_w, conv_b, dtb, alog, triu if reverse else tril, e_b if reverse else e_f, shift]
        in_specs += [full(a) for a in consts]
        in_specs += [chunk(512) if s == "chunk" else full(a) for a, s in zip(extra_in, extra_specs)]
        return pl.pallas_call(
            functools.partial(_ssd_kernel, reverse=reverse),
            grid=(B // ns, nc),
            in_specs=in_specs,
            out_specs=chunk(512),
            out_shape=jax.ShapeDtypeStruct((B, L, 512), out_dtype),
            scratch_shapes=[pltpu.VMEM((ns, SSD_NGROUPS, SSD_DSTATE, 256), F32)],
            compiler_params=_cparams("arbitrary", "arbitrary"),
            name="ssd_bwd" if reverse else "ssd_fwd",
        )(xbc, xbc, xbc, dt, *consts, *extra_in)

    yf = call(False, [], [], F32)
    return call(True, [z, yf, dskip_e, norm_w], ["chunk", "chunk", "full", "full"], BF16)


N_POS = 3
N_FEAT = 2 * N_POS + 1
MAX_SHIFT_LOG2 = 50.0
VT_ROWS = DA_VDIM + 16
ZERO_PROB_LOG2 = 160.0
DEAD_SHIFT = 1.0e4


def _max_block_dist(head, tq, nb):
    slope_log2 = 2.0 ** (-8.0 * (head + 1) / DA_HEADS) * LOG2E
    d = 0
    while d < nb - 1 and slope_log2 * (d * tq + 1) < ZERO_PROB_LOG2:
        d += 1
    return d


def _attn_kernel(slope_ref, lamv_ref, q_ref, qall_ref, k_ref, v_ref, sw_ref, o_ref,
                 kaug_ref, kfeat_ref, qfeat_ref, corr_ref, vt_ref, vtb_ref, pt_ref, shift_ref, worst_ref,
                 *, tq, tk, L, nsub):
    h = pl.program_id(0)
    b = pl.program_id(1)
    gi = pl.program_id(2)
    nb = L // tq
    sl2 = slope_ref[h] * LOG2E
    lv = lamv_ref[...]
    lam = (jnp.exp(jnp.sum(lv[0:1] * lv[1:2], axis=-1, keepdims=True))
           - jnp.exp(jnp.sum(lv[2:3] * lv[3:4], axis=-1, keepdims=True)) + LAMBDA_INIT)

    def feat0(c):
        return DA_HEADDIM if c == 0 else 0

    def own_lanes(lane, c):
        return (lane < DA_HEADDIM) if c == 0 else (lane >= DA_HEADDIM)

    ha = lax.broadcasted_iota(jnp.int32, (LANES, LANES), 0) < DA_HEADDIM
    hb = lax.broadcasted_iota(jnp.int32, (LANES, LANES), 1) < DA_HEADDIM
    other_half = jnp.where(ha != hb, 1.0, 0.0).astype(BF16)

    @pl.when((b == 0) & (gi == 0))
    def _tables():
        lane = lax.broadcasted_iota(jnp.int32, (L, LANES), 1)
        kpos = lax.broadcasted_iota(jnp.int32, (L, LANES), 0).astype(F32) * sl2
        pieces = [t.astype(F32) for t in _split3(kpos)]
        for c in range(2):
            rel = lane - feat0(c)
            feat = jnp.where((rel >= 0) & (rel <= N_POS), 1.0, 0.0)
            for n in range(N_POS):
                feat = jnp.where(rel == N_POS + 1 + n, pieces[n], feat)
            kfeat_ref[c] = feat.astype(BF16)
            qfeat = jnp.where((rel > N_POS) & (rel < N_FEAT), -1.0, 0.0)
            for n in range(N_POS):
                qfeat = jnp.where(rel == n, pieces[n], qfeat)
            qfeat_ref[c] = qfeat.astype(BF16)
        kr = lax.broadcasted_iota(jnp.int32, (tq, tq), 0)
        qc = lax.broadcasted_iota(jnp.int32, (tq, tq), 1)
        corr_ref[...] = jnp.maximum(qc - kr, 0).astype(F32) * (2.0 * sl2)

    @pl.when(gi == 0)
    def _prep():
        kraw = k_ref[0]
        lane = lax.broadcasted_iota(jnp.int32, kraw.shape, 1)
        for c in range(2):
            kaug_ref[c] = jnp.where(own_lanes(lane, c), kraw, kfeat_ref[c])
        kf = kraw.astype(F32)
        kn2 =_dot((kf * kf).astype(BF16), other_half)
        kmax2 = jnp.max(kn2, axis=0, keepdims=True)
        qa = qall_ref[0].astype(F32)
        prod2 = _dot((qa * qa).astype(BF16), other_half) * kmax2
        bound = prod2 * lax.rsqrt(jnp.maximum(prod2, 1e-30))
        shift_all = (bound * 1.02).astype(BF16).astype(F32)
        shift_ref[...] = shift_all
        worst_ref[0] = jnp.max(shift_all)
        ones_row = lax.broadcasted_iota(jnp.int32, (VT_ROWS - DA_VDIM, tq), 0) == 0
        ones_rows = jnp.where(ones_row, 1.0, 0.0).astype(BF16)
        for jb in range(nb):
            vt = v_ref[0, jb * tq:(jb + 1) * tq, :].astype(F32).T.astype(BF16)
            vt_ref[0:DA_VDIM, jb * tq:(jb + 1) * tq] = vt
            vt_ref[DA_VDIM:, jb * tq:(jb + 1) * tq] = ones_rows
            vtb_ref[jb, 0:DA_VDIM, :] = vt
            vtb_ref[jb, DA_VDIM:, :] = ones_rows

    worst = worst_ref[0]
    lane = lax.broadcasted_iota(jnp.int32, (tq, LANES), 1)

    def sub_rows(sub):
        return slice(sub * tq, (sub + 1) * tq)

    def load_q(sub):
        return q_ref[0, sub_rows(sub), :].astype(F32)

    def finish_t(o1t, o2t, sub):
        finish((o1t - lam * o2t).T, None, sub)

    def finish(o1, o2, sub):
        o = o1 if o2 is None else o1 - lam * o2
        ms = jnp.mean(o * o, axis=-1, keepdims=True)
        o = o * lax.rsqrt(ms + EPS) * sw_ref[...] * (1.0 - LAMBDA_INIT)
        o_ref[0, sub_rows(sub), :] = o.astype(o_ref.dtype)

    def fast_path(max_dist, sub):
        qi = gi * nsub + sub
        rows = pl.ds(pl.multiple_of(qi * tq, tq), tq)
        qraw = q_ref[0, sub_rows(sub), :]
        neg_shift = (-shift_ref[rows, :]).astype(BF16)
        zero = jnp.zeros_like(qraw)
        outs = []
        for c in range(2):
            rel = lane - feat0(c)
            own = own_lanes(lane, c)
            base = jnp.where(own, qraw, jnp.where(rel == N_POS, neg_shift, zero))
            posf = qfeat_ref[c, rows, :]
            q_after = base + posf
            q_before = base - posf
            if max_dist is None:
                blocks = []
                for d in range(nb):
                    wrapped = qi + d >= nb
                    jb = jnp.where(wrapped, qi + d - nb, qi + d)
                    blocks.append((jb, q_after if d == 0 else jnp.where(wrapped, q_before, q_after), d == 0))
            else:
                q_dead = jnp.where(own, qraw, jnp.where(rel == N_POS, -DEAD_SHIFT, 0.0).astype(BF16))
                blocks = []
                for r in range(-max_dist, max_dist + 1):
                    jb = qi + r
                    side = q_before if r < 0 else q_after
                    inside = (jb >= 0) & (jb < nb)
                    blocks.append((jnp.clip(jb, 0, nb - 1), side if r == 0 else jnp.where(inside, side, q_dead),
                                   r == 0))
            acc = None
            for jb, qsel, diagonal in blocks:
                dk = pl.ds(pl.multiple_of(jb * tq, tq), tq)
                st = _dot_nt(kaug_ref[c, dk, :], qsel)
                if diagonal:
                    st = st - corr_ref[...]
                p = jnp.exp2(st).astype(BF16)
                if max_dist is None:
                    pt_ref[sub, c, dk, :] = p
                else:
                    part = _dot(vtb_ref[jb], p)
                    acc = part if acc is None else acc + part
            if max_dist is None:
                acc = _dot(vt_ref[...], pt_ref[sub, c])
            outs.append(acc[:DA_VDIM] * (1.0 / acc[DA_VDIM:DA_VDIM + 1]))
        finish_t(outs[0], outs[1], sub)

    fast = worst <= MAX_SHIFT_LOG2
    dists = [_max_block_dist(hh, tq, nb) for hh in range(DA_HEADS)]
    dists = [d if 2 * d + 1 < nb else None for d in dists]
    for dist in set(dists):
        heads = [hh for hh in range(DA_HEADS) if dists[hh] == dist]
        in_heads = functools.reduce(lambda a, b: a | b, [h == hh for hh in heads])

        @pl.when(fast & in_heads)
        def _(dist=dist):
            for sub in range(nsub):
                fast_path(dist, sub)

    def online_path(sub):
        qi = gi * nsub + sub
        qf = load_q(sub)
        zero = jnp.zeros_like(qf)
        qm = [jnp.where(own_lanes(lane, c), qf, zero).astype(BF16) for c in range(2)]
        qk0 =(qi * tq + lax.broadcasted_iota(jnp.int32, (tq, tk), 0)
               - lax.broadcasted_iota(jnp.int32, (tq, tk), 1)).astype(F32)

        def body(j, carry):
            kb = k_ref[0, pl.ds(pl.multiple_of(j * tk, tk), tk), :]
            vb = v_ref[0, pl.ds(pl.multiple_of(j * tk, tk), tk), :]
            bias = sl2 * jnp.abs(qk0 - lax.convert_element_type(j * tk, F32))
            out = []
            for c in range(2):
                m, l, acc = carry[c]
                s = _dot_nt(qm[c], kb) - bias
                m_new = jnp.maximum(m, jnp.max(s, axis=-1, keepdims=True))
                p = jnp.exp2(s - m_new)
                alpha = jnp.exp2(m - m_new)
                l_new = alpha * l + jnp.sum(p, axis=-1, keepdims=True)
                acc_new = alpha * acc + _dot(p.astype(BF16), vb)
                out.append((m_new, l_new, acc_new))
            return tuple(out)

        init = tuple((jnp.full((tq, 1), -jnp.inf, F32), jnp.zeros((tq, 1), F32),
                      jnp.zeros((tq, DA_VDIM), F32)) for _ in range(2))
        (m1, l1, a1), (m2, l2, a2) = lax.fori_loop(0, L // tk, body, init)
        finish(a1 * (1.0 / l1), a2 * (1.0 / l2), sub)

    @pl.when(jnp.logical_not(fast))
    def _():
        for sub in range(nsub):
            online_path(sub)


def _diff_attention(q, k, v, slopes, lamv, subln_w, B, L):
    tq, tk = min(TQ, L), min(TK, L)
    nsub = TQ_SUBS if L % (tq * TQ_SUBS) == 0 else 1
    return pl.pallas_call(
        functools.partial(_attn_kernel, tq=tq, tk=tk, L=L, nsub=nsub),
        grid=(DA_HEADS, B, L // (tq * nsub)),
        in_specs=[
            pl.BlockSpec(memory_space=pltpu.SMEM),
            pl.BlockSpec(lamv.shape, lambda h, b, i: (0, 0)),
            pl.BlockSpec((1, tq * nsub, LANES), lambda h, b, i: (b, i, h)),
            pl.BlockSpec((1, L, LANES), lambda h, b, i: (b, 0, h)),
            pl.BlockSpec((1, L, LANES), lambda h, b, i: (b, 0, h)),
            pl.BlockSpec((1, L, LANES), lambda h, b, i: (b, 0, h)),
            pl.BlockSpec(subln_w.shape, lambda h, b, i: (0, 0)),
        ],
        out_specs=pl.BlockSpec((1, tq * nsub, LANES), lambda h, b, i: (b, i, h)),
        out_shape=jax.ShapeDtypeStruct((B, L, 512), BF16),
        scratch_shapes=[pltpu.VMEM((2, L, LANES), BF16),
                        pltpu.VMEM((2, L, LANES), BF16),
                        pltpu.VMEM((2, L, LANES), BF16),
                        pltpu.VMEM((tq, tq), F32),
                        pltpu.VMEM((VT_ROWS, L), BF16),
                        pltpu.VMEM((L // tq, VT_ROWS, tq), BF16),
                        pltpu.VMEM((nsub, 2, L, tq), BF16),
                        pltpu.VMEM((L, LANES), F32),
                        pltpu.SMEM((1,), F32)],
        compiler_params=_cparams("arbitrary", "arbitrary", "arbitrary"),
        name="diff_attn",
    )(slopes, lamv, q, q, k, v, subln_w)


def _pair_tables():
    lo, hi = [], []
    for g in range(N_GROUPS):
        for a in range(EPG):
            for b in range(a + 1, EPG):
                lo.append(g * EPG + a)
                hi.append(g * EPG + b)
    return lo, hi


PAIRS_PER_GROUP = EPG * (EPG - 1) // 2
N_BUCKETS = N_GROUPS * PAIRS_PER_GROUP
GATE_LANE = 0
ROW_TAIL = LANES


def _router_kernel(x_ref, ys_ref, ya_ref, wo_ref, nw_ref, wr_ref, br_ref, tri_ref,
                   h_ref, u_ref, idx_ref, cnt_ref, carry_ref):
    i = pl.program_id(0)

    @pl.when(i == 0)
    def _():
        carry_ref[...] = jnp.zeros_like(carry_ref)

    sub = tri_ref.shape[0]
    for s in range(x_ref.shape[0] // sub):
        rows = pl.ds(s * sub, sub)
        _route_tile(x_ref.at[rows], ys_ref.at[rows], ya_ref.at[rows], wo_ref, nw_ref, wr_ref, br_ref, tri_ref,
                    h_ref.at[rows], u_ref.at[rows], idx_ref.at[:, rows], cnt_ref, carry_ref)


def _route_tile(x_ref, ys_ref, ya_ref, wo_ref, nw_ref, wr_ref, br_ref, tri_ref,
                h_ref, u_ref, idx_ref, cnt_ref, carry_ref):
    h = x_ref[...] + _dot(ys_ref[...], wo_ref[0:512, :]) + _dot(ya_ref[...], wo_ref[512:1024, :])
    h_ref[...] = h
    D = h.shape[1]
    ms = jnp.mean(h * h, axis=-1, keepdims=True)
    u = h * lax.rsqrt(ms + EPS) * nw_ref[...]
    u_ref[:, 0:D // 2] = _pack_bf16_pairs(u)
    logits = _dot(u.astype(BF16), wr_ref[...]) + br_ref[...]

    tm = logits.shape[0]
    li = lax.broadcasted_iota(jnp.int32, (tm, LANES), 1)
    lf = li.astype(F32)
    ninf = jnp.float32(-jnp.inf)
    big = jnp.float32(1e9)
    gl = jnp.where(li < N_GROUPS, logits, ninf)
    gmax = jnp.max(gl, axis=-1, keepdims=True)
    gidx = jnp.min(jnp.where(gl == gmax, lf, big), axis=-1, keepdims=True)
    pg = 1.0 / jnp.sum(jnp.exp(gl - gmax), axis=-1, keepdims=True)
    lane_grp = ((li - EXP_LANE0) >> 3).astype(F32)
    in_grp = (li >= EXP_LANE0) & (li < EXP_LANE0 + N_EXPERTS) & (lane_grp == gidx)
    sl = jnp.where(in_grp, logits, ninf)
    m1 = jnp.max(sl, axis=-1, keepdims=True)
    i1 = jnp.min(jnp.where(sl == m1, lf, big), axis=-1, keepdims=True)
    sl2 = jnp.where(lf == i1, ninf, sl)
    m2 = jnp.max(sl2, axis=-1, keepdims=True)
    i2 = jnp.min(jnp.where(sl2 == m2, lf, big), axis=-1, keepdims=True)
    t = jnp.exp(m2 - m1)
    w1 = 1.0 / (1.0 + t)
    g1 = pg * w1
    g2 = pg * (t * w1)
    a1 = i1 - EXP_LANE0 - gidx * EPG
    a2 = i2 - EXP_LANE0 - gidx * EPG
    lo = jnp.minimum(a1, a2)
    hi = jnp.maximum(a1, a2)
    bucket = gidx * PAIRS_PER_GROUP + (lo * (EPG - 1) - lo * (lo - 1.0) * 0.5) + (hi - lo - 1.0)
    first_is_lo = a1 < a2
    g_lo = jnp.where(first_is_lo, g1, g2)
    g_hi = jnp.where(first_is_lo, g2, g1)
    ohb = lf == bucket
    oh = jnp.where(ohb, 1.0, 0.0)
    prefix = _dot(tri_ref[...], oh.astype(BF16)) + carry_ref[0:1, :]
    rank = jnp.sum(jnp.where(ohb, prefix, 0.0), axis=-1, keepdims=True)
    new_carry = carry_ref[0:1, :] + jnp.sum(oh, axis=0, keepdims=True)
    carry_ref[...] = jnp.broadcast_to(new_carry, carry_ref.shape)
    cnt_ref[...] = jnp.broadcast_to(new_carry, cnt_ref.shape)

    meta = jnp.where(li == 0, bucket, jnp.where(li == 1, rank, 0.0))
    idx_ref[...] = meta.T[0:SUBLANES, :].astype(jnp.int32)
    gates = jnp.where(li == GATE_LANE, g_lo, jnp.where(li == GATE_LANE + 1, g_hi, 0.0))
    u_ref[:, D // 2:] = pltpu.bitcast(gates, jnp.uint32)


def _out_router(x2, y_ssd, y_da, w_out, norm_w, w_r, b_r):
    T, D = x2.shape
    tm = min(TM_ROUTE, T)
    r = jnp.arange(min(ROUTE_SUB, tm))
    tri = (r[:, None] > r[None, :]).astype(BF16)
    row = lambda n: pl.BlockSpec((tm, n), lambda i: (i, 0))
    full = lambda a: pl.BlockSpec(a.shape, lambda i: (0,) * a.ndim)
    return pl.pallas_call(
        _router_kernel,
        grid=(T // tm,),
        in_specs=[row(D), row(512), row(512), full(w_out), full(norm_w), full(w_r), full(b_r), full(tri)],
        out_specs=[row(D), row(D // 2 + ROW_TAIL), pl.BlockSpec((SUBLANES, tm), lambda i: (0, i)),
                   pl.BlockSpec((SUBLANES, LANES), lambda i: (0, 0))],
        out_shape=[
            jax.ShapeDtypeStruct((T, D), F32),
            jax.ShapeDtypeStruct((T, D // 2 + ROW_TAIL), jnp.uint32),
            jax.ShapeDtypeStruct((SUBLANES, T), jnp.int32),
            jax.ShapeDtypeStruct((SUBLANES, LANES), F32),
        ],
        scratch_shapes=[pltpu.VMEM((SUBLANES, LANES), F32)],
        compiler_params=_cparams("arbitrary"),
        name="out_router",
    )(x2, y_ssd, y_da, w_out, norm_w, w_r, b_r, tri)


def _dispatch_kernel(cnt_ref, idx_ref, u_ref, xs_hbm, slots_ref, ea_ref, eb_ref, nused_ref,
                     pstart_ref, zero_ref, sem, zsem, *, tm, n_blocks):
    i = pl.program_id(0)

    @pl.when(i == 0)
    def _():
        zero_ref[...] = jnp.zeros_like(zero_ref)
        pair_lo, pair_hi = _pair_tables()

        def zero_block(first_row):
            return pltpu.make_async_copy(
                zero_ref, xs_hbm.at[pl.ds(pl.multiple_of(first_row, MOE_ROWS), MOE_ROWS)], zsem)

        acc = jnp.int32(0)
        nfill = jnp.int32(0)
        for q in range(N_BUCKETS):
            padded = ((cnt_ref[q] + (MOE_ROWS - 1)) // MOE_ROWS) * MOE_ROWS
            pstart_ref[q] = acc
            end = acc + padded

            def fill(j, _, q=q):
                ea_ref[j] = jnp.int32(pair_lo[q])
                eb_ref[j] = jnp.int32(pair_hi[q])
                return 0

            lax.fori_loop(acc // MOE_ROWS, end // MOE_ROWS, fill, 0)

            @pl.when(padded > 0)
            def _(end=end):
                zero_block(end - MOE_ROWS).start()

            nfill = nfill + jnp.where(padded > 0, 1, 0)
            acc = end
        nused = acc // MOE_ROWS
        nused_ref[0] = nused

        def fill_tail(j, _):
            ea_ref[j] = jnp.int32(0)
            eb_ref[j] = jnp.int32(0)
            zero_block(j * MOE_ROWS).start()
            return 0

        lax.fori_loop(nused, n_blocks, fill_tail, 0)

        def wait_fill(_, carry):
            zero_block(0).wait()
            return carry

        lax.fori_loop(0, nfill + (n_blocks - nused), wait_fill, 0)

    def start(i8, _):
        for r in range(SUBLANES):
            t = i8 * SUBLANES + r
            slot = pstart_ref[idx_ref[0, t]] + idx_ref[1, t]
            slots_ref[0, t] = slot
            pltpu.make_async_copy(u_ref.at[i8, pl.ds(r, 1)], xs_hbm.at[pl.ds(slot, 1)], sem).start()
        return 0

    lax.fori_loop(0, tm // SUBLANES, start, 0)
    pltpu.make_async_copy(u_ref, u_ref, sem).wait()


def _dispatch(counts, idx, u, n_blocks):
    T, W = u.shape
    tm = min(TM_DISP, T)
    n_slots = n_blocks * MOE_ROWS
    grid_spec = pltpu.PrefetchScalarGridSpec(
        num_scalar_prefetch=1,
        grid=(T // tm,),
        in_specs=[
            pl.BlockSpec((SUBLANES, tm), lambda i, cnt: (0, i), memory_space=pltpu.SMEM),
            pl.BlockSpec((tm // SUBLANES, SUBLANES, W), lambda i, cnt: (i, 0, 0)),
        ],
        out_specs=[
            pl.BlockSpec(memory_space=pl.ANY),
            pl.BlockSpec((1, tm), lambda i, cnt: (0, i), memory_space=pltpu.SMEM),
            pl.BlockSpec(memory_space=pltpu.SMEM),
            pl.BlockSpec(memory_space=pltpu.SMEM),
            pl.BlockSpec(memory_space=pltpu.SMEM),
        ],
        scratch_shapes=[pltpu.SMEM((LANES,), jnp.int32),
                        pltpu.VMEM((MOE_ROWS, W), u.dtype),
                        pltpu.SemaphoreType.DMA(()),
                        pltpu.SemaphoreType.DMA(())],
    )
    return pl.pallas_call(
        functools.partial(_dispatch_kernel, tm=tm, n_blocks=n_blocks),
        grid_spec=grid_spec,
        out_shape=[
            jax.ShapeDtypeStruct((n_slots, W), u.dtype),
            jax.ShapeDtypeStruct((1, T), jnp.int32),
            jax.ShapeDtypeStruct((n_blocks,), jnp.int32),
            jax.ShapeDtypeStruct((n_blocks,), jnp.int32),
            jax.ShapeDtypeStruct((1,), jnp.int32),
        ],
        compiler_params=_cparams("arbitrary"),
        name="dispatch",
    )(counts, idx, u.reshape(T // SUBLANES, SUBLANES, W))


def _moe_kernel(ea_ref, eb_ref, nused_ref, xs_ref, wga_ref, wua_ref, wda_ref, wgb_ref, wub_ref, wdb_ref, o_ref):
    j = pl.program_id(0)
    H = o_ref.shape[1]

    @pl.when(j < nused_ref[0])
    def _():
        x = _unpack_bf16_pairs(xs_ref[:, 0:H]).astype(BF16)
        tail = pltpu.bitcast(xs_ref[:, H:], F32)
        out = None
        for (wg, wu, wd), lane in (((wga_ref, wua_ref, wda_ref), GATE_LANE),
                                   ((wgb_ref, wub_ref, wdb_ref), GATE_LANE + 1)):
            hid = _silu(_dot(x, wg[0])) * _dot(x, wu[0])
            y = _dot(hid.astype(BF16), wd[0]) * tail[:, lane:lane + 1]
            out = y if out is None else out + y
        o_ref[...] = _pack_bf16_pairs(out)

    @pl.when(j >= nused_ref[0])
    def _():
        o_ref[...] = jnp.zeros_like(o_ref)


def _moe(block_ea, block_eb, nused, xs, wg, wu, wd, n_blocks):
    W = xs.shape[1]
    H = W - ROW_TAIL
    used = lambda j, nu: jnp.minimum(j, nu[0] - 1)
    wspec_a = lambda a: pl.BlockSpec((1,) + a.shape[1:], lambda j, ea, eb, nu: (ea[used(j, nu)], 0, 0))
    wspec_b = lambda a: pl.BlockSpec((1,) + a.shape[1:], lambda j, ea, eb, nu: (eb[used(j, nu)], 0, 0))
    grid_spec = pltpu.PrefetchScalarGridSpec(
        num_scalar_prefetch=3,
        grid=(n_blocks,),
        in_specs=[
            pl.BlockSpec((MOE_ROWS, W), lambda j, ea, eb, nu: (used(j, nu), 0)),
            wspec_a(wg), wspec_a(wu), wspec_a(wd), wspec_b(wg), wspec_b(wu), wspec_b(wd),
        ],
        out_specs=pl.BlockSpec((MOE_ROWS, H), lambda j, ea, eb, nu: (j, 0)),
    )
    return pl.pallas_call(
        _moe_kernel,
        grid_spec=grid_spec,
        out_shape=jax.ShapeDtypeStruct((n_blocks * MOE_ROWS, H), jnp.uint32),
        compiler_params=_cparams("arbitrary"),
        name="moe_mlp",
    )(block_ea, block_eb, nused, xs, wg, wu, wd, wg, wu, wd)


def _combine_kernel(slots_ref, next_slots_ref, h_ref, nw_ref, eo_hbm, o_ref, g_ref, sems, *, tm):
    i = pl.program_id(0)
    n = pl.num_programs(0)
    cur = i % 2

    def gather_tile(sref, buf):
        def start(i8, _):
            for r in range(SUBLANES):
                pltpu.make_async_copy(eo_hbm.at[pl.ds(sref[0, i8 * SUBLANES + r], 1)],
                                      g_ref.at[buf, i8, pl.ds(r, 1)], sems.at[buf]).start()
            return 0

        lax.fori_loop(0, tm // SUBLANES, start, 0)

    @pl.when(i == 0)
    def _():
        gather_tile(slots_ref, 0)

    @pl.when(i + 1 < n)
    def _():
        gather_tile(next_slots_ref, 1 - cur)

    pltpu.make_async_copy(g_ref.at[cur], g_ref.at[cur], sems.at[cur]).wait()

    y = h_ref[...] + _unpack_bf16_pairs(g_ref[cur].reshape(tm, -1))
    ms = jnp.mean(y * y, axis=-1, keepdims=True)
    o_ref[...] = y * lax.rsqrt(ms + EPS) * nw_ref[...]


def _combine(slots, h, norm_w, eo):
    T, D = h.shape
    tm = min(TM_COMB, T)
    last = T // tm - 1
    return pl.pallas_call(
        functools.partial(_combine_kernel, tm=tm),
        grid=(T // tm,),
        in_specs=[
            pl.BlockSpec((1, tm), lambda i: (0, i), memory_space=pltpu.SMEM),
            pl.BlockSpec((1, tm), lambda i: (0, jnp.minimum(i + 1, last)), memory_space=pltpu.SMEM),
            pl.BlockSpec((tm, D), lambda i: (i, 0)),
            pl.BlockSpec(norm_w.shape, lambda i: (0, 0)),
            pl.BlockSpec(memory_space=pl.ANY),
        ],
        out_specs=pl.BlockSpec((tm, D), lambda i: (i, 0)),
        out_shape=jax.ShapeDtypeStruct((T, D), F32),
        scratch_shapes=[pltpu.VMEM((2, tm // SUBLANES, SUBLANES, D // 2), jnp.uint32),
                        pltpu.SemaphoreType.DMA((2,))],
        compiler_params=_cparams("arbitrary"),
        name="combine",
    )(slots, slots, h, norm_w, eo)


def _pad_lanes(v, offset=0):
    v = v.astype(F32).reshape(-1)
    return jnp.zeros((1, LANES), F32).at[0, offset:offset + v.shape[0]].set(v)


def kernel(x, norm_mix_w, w_in, conv_w, conv_b, dt_bias_fwd, dt_bias_bwd, a_log_fwd, a_log_bwd, ssd_d,
           ssd_norm_w, lambda_q1, lambda_k1, lambda_q2, lambda_k2, subln_w, w_out, norm_ffn_w,
           w_router_group, b_router_group, w_router_exp, b_router_exp, w_exp_gate, w_exp_up, w_exp_down,
           norm_final_w):
    B, L, D = x.shape
    T = B * L
    x2 = x.reshape(T, D)
    l = 0

    w = w_in[l]
    o_z, o_xbc, o_dt = 512, 512 + 1024, 512 + 1024 + 16
    o_q, o_k = o_dt + 512, o_dt + 1024
    w_main = jnp.concatenate(
        [w[:, :o_xbc], w[:, o_dt:o_q] * (DA_HEADDIM ** -0.5 * LOG2E), w[:, o_q:]], axis=1).astype(BF16)
    w_dt = jnp.zeros((D, LANES), F32).at[:, :16].set(w[:, o_xbc:o_dt]).astype(BF16)

    experts = [w_exp_gate[l], w_exp_up[l], w_exp_down[l]]
    z, xbc, dt, q, k, v, *experts_bf16 = _in_proj(
        x2, norm_mix_w[l].reshape(1, D), w_main, w_dt, [a.reshape(-1, a.shape[-1]) for a in experts])
    wg_b, wu_b, wd_b = [c.reshape(a.shape) for c, a in zip(experts_bf16, experts)]

    dtb = _pad_lanes(jnp.concatenate([dt_bias_fwd[l], dt_bias_bwd[l]]))
    alog = _pad_lanes(jnp.concatenate([a_log_fwd[l], a_log_bwd[l]]))
    dskip_e = jnp.repeat(ssd_d[l].astype(F32), SSD_HEADDIM).reshape(1, SSD_WIDTH)
    y_ssd = _ssd(xbc.reshape(B, L, 1024), dt.reshape(B, L, LANES), z.reshape(B, L, 512),
                 conv_w[l].astype(F32), conv_b[l].reshape(1, -1).astype(F32), dtb, alog, dskip_e,
                 ssd_norm_w[l].reshape(1, -1).astype(F32), B, L)

    slopes = jnp.power(2.0, -8.0 * jnp.arange(1, DA_HEADS + 1, dtype=F32) / DA_HEADS)
    lamv = jnp.concatenate([_pad_lanes(lambda_q1[l]), _pad_lanes(lambda_k1[l]),
                            _pad_lanes(lambda_q2[l]), _pad_lanes(lambda_k2[l]),
                            jnp.zeros((4, LANES), F32)], axis=0)
    y_da = _diff_attention(q.reshape(B, L, 512), k.reshape(B, L, 512), v.reshape(B, L, 512),
                           slopes, lamv, subln_w[l].reshape(1, -1).astype(F32), B, L)

    w_r = (jnp.zeros((D, LANES), F32).at[:, :N_GROUPS].set(w_router_group[l])
           .at[:, EXP_LANE0:EXP_LANE0 + N_EXPERTS].set(w_router_exp[l])).astype(BF16)
    b_r = _pad_lanes(jnp.concatenate([b_router_group[l], b_router_exp[l]]))
    h, u, idx, cnt = _out_router(x2, y_ssd.reshape(T, 512), y_da.reshape(T, 512), w_out[l].astype(BF16),
                                  norm_ffn_w[l].reshape(1, D), w_r, b_r)

    n_blocks = T // MOE_ROWS + N_BUCKETS
    counts = cnt[0].astype(jnp.int32)
    xs, slots, block_ea, block_eb, nused = _dispatch(counts, idx, u, n_blocks)
    eo = _moe(block_ea, block_eb, nused, xs, wg_b, wu_b, wd_b, n_blocks)
    out = _combine(slots, h, norm_final_w.reshape(1, D), eo)
    return out.reshape(B, L, D)
```

```python
import functools
import math

import jax
import jax.numpy as jnp
from jax import lax
from jax.experimental import pallas as pl
from jax.experimental.pallas import tpu as pltpu

F32 = jnp.float32
BF16 = jnp.bfloat16

EPS = 1e-6
LOG2E = math.log2(math.e)
LANES = 128
SUBLANES = 8
HALO = 16
VMEM_LIMIT = 48 * 1024 * 1024

SSD_WIDTH = 512
SSD_HEADDIM = 64
SSD_HEADS = 8
SSD_NGROUPS = 2
SSD_HPG = 4
SSD_DSTATE = 128
SSD_CONV = 5
SSD_CHUNK = 128
SSD_CONV_CH = 1024
DA_HEADDIM = 64
DA_VDIM = 128
DA_HEADS = 4
N_GROUPS = 4
EPG = 8
N_EXPERTS = 32
D_EXPERT = 512
LAMBDA_INIT = 0.8 - 0.6 * math.exp(-0.3 * 0)

TM_PROJ = 1024
PROJ_SUB = 512
SSD_SEQS = 8
TQ = 512
TQ_SUBS = 2
TK = 512
TM_ROUTE = 1024
ROUTE_SUB = 512
TM_DISP = 2048
MOE_ROWS = 256
TM_COMB = 1024
EXP_LANE0 = 4


def _cparams(*sem):
    return pltpu.CompilerParams(dimension_semantics=sem, vmem_limit_bytes=VMEM_LIMIT)


def _dot(a, b):
    return jnp.dot(a, b, preferred_element_type=F32)


def _dot_nt(a, b):
    return lax.dot_general(a, b, (((1,), (1,)), ((), ())), preferred_element_type=F32)


def _split3(v):
    hi = v.astype(BF16)
    r1 = v - hi.astype(F32)
    mid = r1.astype(BF16)
    lo = (r1 - mid.astype(F32)).astype(BF16)
    return hi, mid, lo


def _split2(v):
    hi = v.astype(BF16)
    return hi, (v - hi.astype(F32)).astype(BF16)


def _select_rows(a01x2, v):
    hi, lo = _split2(v)
    return _dot(a01x2, jnp.concatenate([hi, lo], axis=0))


def _select_lanes(v, b01x2):
    hi, lo = _split2(v)
    return _dot(jnp.concatenate([hi, lo], axis=1), b01x2)


def _silu(x):
    return x * (1.0 / (1.0 + jnp.exp(-x)))


def _pack_bf16_pairs(x):
    n = x.shape[1] // 2
    bits = pltpu.bitcast(x.astype(BF16).astype(F32), jnp.uint32)
    return (bits[:, :n] >> 16) | bits[:, n:]


def _unpack_bf16_pairs(w):
    lo = pltpu.bitcast(w << 16, F32)
    hi = pltpu.bitcast(w & jnp.uint32(0xFFFF0000), F32)
    return jnp.concatenate([lo, hi], axis=1)


def _inproj_kernel(x_ref, nw_ref, w_ref, wdt_ref, *refs, n_cast):
    cast_in, out_refs, cast_out = refs[:n_cast], refs[n_cast:len(refs) - n_cast], refs[len(refs) - n_cast:]
    for s in range(x_ref.shape[0] // PROJ_SUB):
        rows = pl.ds(s * PROJ_SUB, PROJ_SUB)
        _inproj_rows(x_ref.at[rows], nw_ref, w_ref, wdt_ref, *[r.at[rows] for r in out_refs])
    for src, dst in zip(cast_in, cast_out):
        dst[...] = src[...].astype(BF16)


def _inproj_rows(x_ref, nw_ref, w_ref, wdt_ref, z_ref, xbc_ref, dt_ref, q_ref, k_ref, v_ref):
    x = x_ref[...]
    ms = jnp.mean(x * x, axis=-1, keepdims=True)
    u = (x * lax.rsqrt(ms + EPS) * nw_ref[...]).astype(BF16)
    z_ref[...] = _dot(u, w_ref[:, 0:512]).astype(BF16)
    xbc_ref[...] = _dot(u, w_ref[:, 512:1536]).astype(BF16)
    q_ref[...] = _dot(u, w_ref[:, 1536:2048]).astype(BF16)
    k_ref[...] = _dot(u, w_ref[:, 2048:2560]).astype(BF16)
    v_ref[...] = _dot(u, w_ref[:, 2560:3072]).astype(BF16)
    dt_ref[...] = _dot(u, wdt_ref[...])


def _in_proj(x2, norm_w, w_main, w_dt, to_bf16):
    T, D = x2.shape
    tm = min(TM_PROJ, T)
    steps = T // tm
    row = lambda n: pl.BlockSpec((tm, n), lambda i: (i, 0))
    full = lambda a: pl.BlockSpec(a.shape, lambda i: (0,) * a.ndim)
    piece = lambda a: pl.BlockSpec((a.shape[0] // steps, a.shape[1]), lambda i: (i, 0))
    return pl.pallas_call(
        functools.partial(_inproj_kernel, n_cast=len(to_bf16)),
        grid=(steps,),
        in_specs=[row(D), full(norm_w), full(w_main), full(w_dt)] + [piece(a) for a in to_bf16],
        out_specs=[row(512), row(1024), row(LANES), row(512), row(512), row(512)] + [piece(a) for a in to_bf16],
        out_shape=[
            jax.ShapeDtypeStruct((T, 512), BF16),
            jax.ShapeDtypeStruct((T, 1024), BF16),
            jax.ShapeDtypeStruct((T, LANES), F32),
            jax.ShapeDtypeStruct((T, 512), BF16),
            jax.ShapeDtypeStruct((T, 512), BF16),
            jax.ShapeDtypeStruct((T, 512), BF16),
        ] + [jax.ShapeDtypeStruct(a.shape, BF16) for a in to_bf16],
        compiler_params=_cparams("arbitrary"),
        name="in_proj",
    )(x2, norm_w, w_main, w_dt, *to_bf16)


def _ssd_kernel(*refs, reverse):
    c = pl.program_id(1)
    nc = pl.num_programs(1)
    pos = (nc - 1 - c) if reverse else c
    ht_ref = refs[-1]

    @pl.when(c == 0)
    def _():
        ht_ref[...] = jnp.zeros_like(ht_ref)

    for s in range(ht_ref.shape[0]):
        _ssd_sequence(s, pos, nc, refs, reverse)


def _ssd_sequence(s, pos, nc, refs, reverse):
    if reverse:
        (xp_ref, xc_ref, xn_ref, dt_ref, cw_ref, cb_ref, dtb_ref, alog_ref, tri_ref, e_ref, sh_ref,
         z_ref, yf_ref, dsk_ref, nw_ref, out_ref, ht_ref) = refs
    else:
        (xp_ref, xc_ref, xn_ref, dt_ref, cw_ref, cb_ref, dtb_ref, alog_ref, tri_ref, e_ref, sh_ref,
         out_ref, ht_ref) = refs
    Q = SSD_CHUNK

    halo_zero = jnp.zeros((HALO, SSD_CONV_CH), BF16)
    ext = jnp.concatenate([jnp.where(pos > 0, xp_ref[s], halo_zero), xc_ref[s],
                           jnp.where(pos < nc - 1, xn_ref[s], halo_zero)], axis=0)
    shifted = _dot(sh_ref[...], ext)
    pad = (SSD_CONV - 1) // 2
    conv = cb_ref[...] + cw_ref[pad:pad + 1, :] * xc_ref[s].astype(F32)
    for n, k in enumerate(k for k in range(SSD_CONV) if k != pad):
        conv = conv + cw_ref[k:k + 1, :] * shifted[n * Q:(n + 1) * Q, :]
    act = _silu(conv)
    xs = act[:, :SSD_WIDTH]

    dtraw = dt_ref[s] + dtb_ref[...]
    dt = jnp.maximum(dtraw, 0.0) + jnp.log(1.0 + jnp.exp(-jnp.abs(dtraw)))
    a = -jnp.exp(alog_ref[...])
    adt = dt * a
    tri = tri_ref[...]
    cvec = _select_rows(tri, adt)
    cvt = cvec.T
    e01 = e_ref[...]
    dt_e = _select_lanes(dt, e01)
    cv_e = _select_lanes(cvec, e01)
    edge = 0 if reverse else Q - 1
    tot_e = cv_e[edge:edge + 1, :]

    li = lax.broadcasted_iota(jnp.int32, (Q, Q), 0)
    si = lax.broadcasted_iota(jnp.int32, (Q, Q), 1)
    mask = (li <= si) if reverse else (li >= si)
    lane = lax.broadcasted_iota(jnp.int32, (Q, LANES), 1)
    head0 = SSD_HEADS if reverse else 0

    xdt = xs * dt_e
    ys = []
    for g in range(SSD_NGROUPS):
        bg = act[:, SSD_WIDTH + g * SSD_DSTATE:SSD_WIDTH + (g + 1) * SSD_DSTATE]
        cg = act[:, SSD_WIDTH + (SSD_NGROUPS + g) * SSD_DSTATE:SSD_WIDTH + (SSD_NGROUPS + g + 1) * SSD_DSTATE]
        bgb = bg.astype(BF16)
        cgb = cg.astype(BF16)
        gm = _dot_nt(cgb, bgb)
        gs = slice(g * 256, (g + 1) * 256)
        xdt_g = xdt[:, gs]
        xdtb = xdt_g.astype(BF16)
        ydiag = []
        for pr in range(2):
            xpair = xdtb[:, pr * LANES:(pr + 1) * LANES]
            yh = []
            for j in range(2):
                hl = head0 + g * SSD_HPG + pr * 2 + j
                seg = cvec[:, hl:hl + 1] - cvt[hl:hl + 1, :]
                decay = jnp.exp(jnp.where(mask, seg, -jnp.inf))
                yh.append(_dot((gm * decay).astype(BF16), xpair))
            ydiag.append(jnp.where(lane < SSD_HEADDIM, yh[0], yh[1]))
        ydiag = jnp.concatenate(ydiag, axis=1)
        ht = ht_ref[s, g]
        yoff = _dot(cgb, ht.astype(BF16)) * jnp.exp(cv_e[:, gs])
        ys.append(ydiag + yoff)
        wdec = jnp.exp(tot_e[:, gs] - cv_e[:, gs])
        xw = (xdt_g * wdec).astype(BF16)
        ht_ref[s, g] = jnp.exp(tot_e[:, gs]) * ht + _dot(bg.T.astype(BF16), xw)
    y = jnp.concatenate(ys, axis=1)

    if not reverse:
        out_ref[s] = y
    else:
        y = y + yf_ref[s] + dsk_ref[...] * xs
        gated = y * _silu(z_ref[s].astype(F32))
        ms = jnp.mean(gated * gated, axis=-1, keepdims=True)
        out_ref[s] = (gated * lax.rsqrt(ms + EPS) * nw_ref[...]).astype(out_ref.dtype)


def _ssd(xbc, dt, z, conv_w, conv_b, dtb, alog, dskip_e, norm_w, B, L):
    Q = SSD_CHUNK
    nc = L // Q
    ns = SSD_SEQS if B % SSD_SEQS == 0 else 1
    hb = Q // HALO
    nhb = L // HALO
    r = jnp.arange(Q)
    pad = (SSD_CONV - 1) // 2
    col = jnp.arange(Q + 2 * HALO)[None, :]
    shift = jnp.concatenate([(col == HALO + r[:, None] + k - pad) for k in range(SSD_CONV) if k != pad],
                            axis=0).astype(BF16)
    twice = lambda m, axis: jnp.concatenate([m, m], axis=axis).astype(BF16)
    tril = twice(r[:, None] >= r[None, :], 1)
    triu = twice(r[:, None] <= r[None, :], 1)
    lane = jnp.arange(LANES)[:, None]
    ch = jnp.arange(SSD_WIDTH)[None, :] // SSD_HEADDIM
    e_f = twice(lane == ch, 0)
    e_b = twice(lane == ch + SSD_HEADS, 0)

    def call(reverse, extra_in, extra_specs, out_dtype):
        pos = (lambda c: nc - 1 - c) if reverse else (lambda c: c)
        full = lambda a: pl.BlockSpec(a.shape, lambda b, c: (0,) * a.ndim)
        chunk = lambda n: pl.BlockSpec((ns, Q, n), lambda b, c: (b, pos(c), 0))
        in_specs = [
            pl.BlockSpec((ns, HALO, 1024), lambda b, c: (b, jnp.maximum(pos(c) * hb - 1, 0), 0)),
            chunk(1024),
            pl.BlockSpec((ns, HALO, 1024), lambda b, c: (b, jnp.minimum((pos(c) + 1) * hb, nhb - 1), 0)),
            chunk(LANES),
        ]
        consts = [conv_w, conv_b, dtb, alog, triu if reverse else tril, e_b if reverse else e_f, shift]
        in_specs += [full(a) for a in consts]
        in_specs += [chunk(512) if s == "chunk" else full(a) for a, s in zip(extra_in, extra_specs)]
        return pl.pallas_call(
            functools.partial(_ssd_kernel, reverse=reverse),
            grid=(B // ns, nc),
            in_specs=in_specs,
            out_specs=chunk(512),
            out_shape=jax.ShapeDtypeStruct((B, L, 512), out_dtype),
            scratch_shapes=[pltpu.VMEM((ns, SSD_NGROUPS, SSD_DSTATE, 256), F32)],
            compiler_params=_cparams("arbitrary", "arbitrary"),
            name="ssd_bwd" if reverse else "ssd_fwd",
        )(xbc, xbc, xbc, dt, *consts, *extra_in)

    yf = call(False, [], [], F32)
    return call(True, [z, yf, dskip_e, norm_w], ["chunk", "chunk", "full", "full"], BF16)


N_POS = 3
N_FEAT = 2 * N_POS + 1
MAX_SHIFT_LOG2 = 50.0
VT_ROWS = DA_VDIM + 16
ZERO_PROB_LOG2 = 160.0
DEAD_SHIFT = 1.0e4


def _max_block_dist(head, tq, nb):
    slope_log2 = 2.0 ** (-8.0 * (head + 1) / DA_HEADS) * LOG2E
    d = 0
    while d < nb - 1 and slope_log2 * (d * tq + 1) < ZERO_PROB_LOG2:
        d += 1
    return d


def _attn_kernel(slope_ref, lamv_ref, q_ref, qall_ref, k_ref, v_ref, sw_ref, o_ref,
                 kaug_ref, kfeat_ref, qfeat_ref, corr_ref, vt_ref, vtb_ref, pt_ref, shift_ref, worst_ref,
                 *, tq, tk, L, nsub):
    h = pl.program_id(0)
    b = pl.program_id(1)
    gi = pl.program_id(2)
    nb = L // tq
    sl2 = slope_ref[h] * LOG2E
    lv = lamv_ref[...]
    lam = (jnp.exp(jnp.sum(lv[0:1] * lv[1:2], axis=-1, keepdims=True))
           - jnp.exp(jnp.sum(lv[2:3] * lv[3:4], axis=-1, keepdims=True)) + LAMBDA_INIT)

    def feat0(c):
        return DA_HEADDIM if c == 0 else 0

    def own_lanes(lane, c):
        return (lane < DA_HEADDIM) if c == 0 else (lane >= DA_HEADDIM)

    ha = lax.broadcasted_iota(jnp.int32, (LANES, LANES), 0) < DA_HEADDIM
    hb = lax.broadcasted_iota(jnp.int32, (LANES, LANES), 1) < DA_HEADDIM
    other_half = jnp.where(ha != hb, 1.0, 0.0).astype(BF16)

    @pl.when((b == 0) & (gi == 0))
    def _tables():
        lane = lax.broadcasted_iota(jnp.int32, (L, LANES), 1)
        kpos = lax.broadcasted_iota(jnp.int32, (L, LANES), 0).astype(F32) * sl2
        pieces = [t.astype(F32) for t in _split3(kpos)]
        for c in range(2):
            rel = lane - feat0(c)
            feat = jnp.where((rel >= 0) & (rel <= N_POS), 1.0, 0.0)
            for n in range(N_POS):
                feat = jnp.where(rel == N_POS + 1 + n, pieces[n], feat)
            kfeat_ref[c] = feat.astype(BF16)
            qfeat = jnp.where((rel > N_POS) & (rel < N_FEAT), -1.0, 0.0)
            for n in range(N_POS):
                qfeat = jnp.where(rel == n, pieces[n], qfeat)
            qfeat_ref[c] = qfeat.astype(BF16)
        kr = lax.broadcasted_iota(jnp.int32, (tq, tq), 0)
        qc = lax.broadcasted_iota(jnp.int32, (tq, tq), 1)
        corr_ref[...] = jnp.maximum(qc - kr, 0).astype(F32) * (2.0 * sl2)

    @pl.when(gi == 0)
    def _prep():
        kraw = k_ref[0]
        lane = lax.broadcasted_iota(jnp.int32, kraw.shape, 1)
        for c in range(2):
            kaug_ref[c] = jnp.where(own_lanes(lane, c), kraw, kfeat_ref[c])
        kf = kraw.astype(F32)
        kn2 =_dot((kf * kf).astype(BF16), other_half)
        kmax2 = jnp.max(kn2, axis=0, keepdims=True)
        qa = qall_ref[0].astype(F32)
        prod2 = _dot((qa * qa).astype(BF16), other_half) * kmax2
        bound = prod2 * lax.rsqrt(jnp.maximum(prod2, 1e-30))
        shift_all = (bound * 1.02).astype(BF16).astype(F32)
        shift_ref[...] = shift_all
        worst_ref[0] = jnp.max(shift_all)
        ones_row = lax.broadcasted_iota(jnp.int32, (VT_ROWS - DA_VDIM, tq), 0) == 0
        ones_rows = jnp.where(ones_row, 1.0, 0.0).astype(BF16)
        for jb in range(nb):
            vt = v_ref[0, jb * tq:(jb + 1) * tq, :].astype(F32).T.astype(BF16)
            vt_ref[0:DA_VDIM, jb * tq:(jb + 1) * tq] = vt
            vt_ref[DA_VDIM:, jb * tq:(jb + 1) * tq] = ones_rows
            vtb_ref[jb, 0:DA_VDIM, :] = vt
            vtb_ref[jb, DA_VDIM:, :] = ones_rows

    worst = worst_ref[0]
    lane = lax.broadcasted_iota(jnp.int32, (tq, LANES), 1)

    def sub_rows(sub):
        return slice(sub * tq, (sub + 1) * tq)

    def load_q(sub):
        return q_ref[0, sub_rows(sub), :].astype(F32)

    def finish_t(o1t, o2t, sub):
        finish((o1t - lam * o2t).T, None, sub)

    def finish(o1, o2, sub):
        o = o1 if o2 is None else o1 - lam * o2
        ms = jnp.mean(o * o, axis=-1, keepdims=True)
        o = o * lax.rsqrt(ms + EPS) * sw_ref[...] * (1.0 - LAMBDA_INIT)
        o_ref[0, sub_rows(sub), :] = o.astype(o_ref.dtype)

    def fast_path(max_dist, sub):
        qi = gi * nsub + sub
        rows = pl.ds(pl.multiple_of(qi * tq, tq), tq)
        qraw = q_ref[0, sub_rows(sub), :]
        neg_shift = (-shift_ref[rows, :]).astype(BF16)
        zero = jnp.zeros_like(qraw)
        outs = []
        for c in range(2):
            rel = lane - feat0(c)
            own = own_lanes(lane, c)
            base = jnp.where(own, qraw, jnp.where(rel == N_POS, neg_shift, zero))
            posf = qfeat_ref[c, rows, :]
            q_after = base + posf
            q_before = base - posf
            if max_dist is None:
                blocks = []
                for d in range(nb):
                    wrapped = qi + d >= nb
                    jb = jnp.where(wrapped, qi + d - nb, qi + d)
                    blocks.append((jb, q_after if d == 0 else jnp.where(wrapped, q_before, q_after), d == 0))
            else:
                q_dead = jnp.where(own, qraw, jnp.where(rel == N_POS, -DEAD_SHIFT, 0.0).astype(BF16))
                blocks = []
                for r in range(-max_dist, max_dist + 1):
                    jb = qi + r
                    side = q_before if r < 0 else q_after
                    inside = (jb >= 0) & (jb < nb)
                    blocks.append((jnp.clip(jb, 0, nb - 1), side if r == 0 else jnp.where(inside, side, q_dead),
                                   r == 0))
            acc = None
            for jb, qsel, diagonal in blocks:
                dk = pl.ds(pl.multiple_of(jb * tq, tq), tq)
                st = _dot_nt(kaug_ref[c, dk, :], qsel)
                if diagonal:
                    st = st - corr_ref[...]
                p = jnp.exp2(st).astype(BF16)
                if max_dist is None:
                    pt_ref[sub, c, dk, :] = p
                else:
                    part = _dot(vtb_ref[jb], p)
                    acc = part if acc is None else acc + part
            if max_dist is None:
                acc = _dot(vt_ref[...], pt_ref[sub, c])
            outs.append(acc[:DA_VDIM] * (1.0 / acc[DA_VDIM:DA_VDIM + 1]))
        finish_t(outs[0], outs[1], sub)

    fast = worst <= MAX_SHIFT_LOG2
    dists = [_max_block_dist(hh, tq, nb) for hh in range(DA_HEADS)]
    dists = [d if 2 * d + 1 < nb else None for d in dists]
    for dist in set(dists):
        heads = [hh for hh in range(DA_HEADS) if dists[hh] == dist]
        in_heads = functools.reduce(lambda a, b: a | b, [h == hh for hh in heads])

        @pl.when(fast & in_heads)
        def _(dist=dist):
            for sub in range(nsub):
                fast_path(dist, sub)

    def online_path(sub):
        qi = gi * nsub + sub
        qf = load_q(sub)
        zero = jnp.zeros_like(qf)
        qm = [jnp.where(own_lanes(lane, c), qf, zero).astype(BF16) for c in range(2)]
        qk0 =(qi * tq + lax.broadcasted_iota(jnp.int32, (tq, tk), 0)
               - lax.broadcasted_iota(jnp.int32, (tq, tk), 1)).astype(F32)

        def body(j, carry):
            kb = k_ref[0, pl.ds(pl.multiple_of(j * tk, tk), tk), :]
            vb = v_ref[0, pl.ds(pl.multiple_of(j * tk, tk), tk), :]
            bias = sl2 * jnp.abs(qk0 - lax.convert_element_type(j * tk, F32))
            out = []
            for c in range(2):
                m, l, acc = carry[c]
                s = _dot_nt(qm[c], kb) - bias
                m_new = jnp.maximum(m, jnp.max(s, axis=-1, keepdims=True))
                p = jnp.exp2(s - m_new)
                alpha = jnp.exp2(m - m_new)
                l_new = alpha * l + jnp.sum(p, axis=-1, keepdims=True)
                acc_new = alpha * acc + _dot(p.astype(BF16), vb)
                out.append((m_new, l_new, acc_new))
            return tuple(out)

        init = tuple((jnp.full((tq, 1), -jnp.inf, F32), jnp.zeros((tq, 1), F32),
                      jnp.zeros((tq, DA_VDIM), F32)) for _ in range(2))
        (m1, l1, a1), (m2, l2, a2) = lax.fori_loop(0, L // tk, body, init)
        finish(a1 * (1.0 / l1), a2 * (1.0 / l2), sub)

    @pl.when(jnp.logical_not(fast))
    def _():
        for sub in range(nsub):
            online_path(sub)


def _diff_attention(q, k, v, slopes, lamv, subln_w, B, L):
    tq, tk = min(TQ, L), min(TK, L)
    nsub = TQ_SUBS if L % (tq * TQ_SUBS) == 0 else 1
    return pl.pallas_call(
        functools.partial(_attn_kernel, tq=tq, tk=tk, L=L, nsub=nsub),
        grid=(DA_HEADS, B, L // (tq * nsub)),
        in_specs=[
            pl.BlockSpec(memory_space=pltpu.SMEM),
            pl.BlockSpec(lamv.shape, lambda h, b, i: (0, 0)),
            pl.BlockSpec((1, tq * nsub, LANES), lambda h, b, i: (b, i, h)),
            pl.BlockSpec((1, L, LANES), lambda h, b, i: (b, 0, h)),
            pl.BlockSpec((1, L, LANES), lambda h, b, i: (b, 0, h)),
            pl.BlockSpec((1, L, LANES), lambda h, b, i: (b, 0, h)),
            pl.BlockSpec(subln_w.shape, lambda h, b, i: (0, 0)),
        ],
        out_specs=pl.BlockSpec((1, tq * nsub, LANES), lambda h, b, i: (b, i, h)),
        out_shape=jax.ShapeDtypeStruct((B, L, 512), BF16),
        scratch_shapes=[pltpu.VMEM((2, L, LANES), BF16),
                        pltpu.VMEM((2, L, LANES), BF16),
                        pltpu.VMEM((2, L, LANES), BF16),
                        pltpu.VMEM((tq, tq), F32),
                        pltpu.VMEM((VT_ROWS, L), BF16),
                        pltpu.VMEM((L // tq, VT_ROWS, tq), BF16),
                        pltpu.VMEM((nsub, 2, L, tq), BF16),
                        pltpu.VMEM((L, LANES), F32),
                        pltpu.SMEM((1,), F32)],
        compiler_params=_cparams("arbitrary", "arbitrary", "arbitrary"),
        name="diff_attn",
    )(slopes, lamv, q, q, k, v, subln_w)


def _pair_tables():
    lo, hi = [], []
    for g in range(N_GROUPS):
        for a in range(EPG):
            for b in range(a + 1, EPG):
                lo.append(g * EPG + a)
                hi.append(g * EPG + b)
    return lo, hi


PAIRS_PER_GROUP = EPG * (EPG - 1) // 2
N_BUCKETS = N_GROUPS * PAIRS_PER_GROUP
GATE_LANE = 0
ROW_TAIL = LANES


def _router_kernel(x_ref, ys_ref, ya_ref, wo_ref, nw_ref, wr_ref, br_ref, tri_ref,
                   h_ref, u_ref, idx_ref, cnt_ref, carry_ref):
    i = pl.program_id(0)

    @pl.when(i == 0)
    def _():
        carry_ref[...] = jnp.zeros_like(carry_ref)

    sub = tri_ref.shape[0]
    for s in range(x_ref.shape[0] // sub):
        rows = pl.ds(s * sub, sub)
        _route_tile(x_ref.at[rows], ys_ref.at[rows], ya_ref.at[rows], wo_ref, nw_ref, wr_ref, br_ref, tri_ref,
                    h_ref.at[rows], u_ref.at[rows], idx_ref.at[:, rows], cnt_ref, carry_ref)


def _route_tile(x_ref, ys_ref, ya_ref, wo_ref, nw_ref, wr_ref, br_ref, tri_ref,
                h_ref, u_ref, idx_ref, cnt_ref, carry_ref):
    h = x_ref[...] + _dot(ys_ref[...], wo_ref[0:512, :]) + _dot(ya_ref[...], wo_ref[512:1024, :])
    h_ref[...] = h
    D = h.shape[1]
    ms = jnp.mean(h * h, axis=-1, keepdims=True)
    u = h * lax.rsqrt(ms + EPS) * nw_ref[...]
    u_ref[:, 0:D // 2] = _pack_bf16_pairs(u)
    logits = _dot(u.astype(BF16), wr_ref[...]) + br_ref[...]

    tm = logits.shape[0]
    li = lax.broadcasted_iota(jnp.int32, (tm, LANES), 1)
    lf = li.astype(F32)
    ninf = jnp.float32(-jnp.inf)
    big = jnp.float32(1e9)
    gl = jnp.where(li < N_GROUPS, logits, ninf)
    gmax = jnp.max(gl, axis=-1, keepdims=True)
    gidx = jnp.min(jnp.where(gl == gmax, lf, big), axis=-1, keepdims=True)
    pg = 1.0 / jnp.sum(jnp.exp(gl - gmax), axis=-1, keepdims=True)
    lane_grp = ((li - EXP_LANE0) >> 3).astype(F32)
    in_grp = (li >= EXP_LANE0) & (li < EXP_LANE0 + N_EXPERTS) & (lane_grp == gidx)
    sl = jnp.where(in_grp, logits, ninf)
    m1 = jnp.max(sl, axis=-1, keepdims=True)
    i1 = jnp.min(jnp.where(sl == m1, lf, big), axis=-1, keepdims=True)
    sl2 = jnp.where(lf == i1, ninf, sl)
    m2 = jnp.max(sl2, axis=-1, keepdims=True)
    i2 = jnp.min(jnp.where(sl2 == m2, lf, big), axis=-1, keepdims=True)
    t = jnp.exp(m2 - m1)
    w1 = 1.0 / (1.0 + t)
    g1 = pg * w1
    g2 = pg * (t * w1)
    a1 = i1 - EXP_LANE0 - gidx * EPG
    a2 = i2 - EXP_LANE0 - gidx * EPG
    lo = jnp.minimum(a1, a2)
    hi = jnp.maximum(a1, a2)
    bucket = gidx * PAIRS_PER_GROUP + (lo * (EPG - 1) - lo * (lo - 1.0) * 0.5) + (hi - lo - 1.0)
    first_is_lo = a1 < a2
    g_lo = jnp.where(first_is_lo, g1, g2)
    g_hi = jnp.where(first_is_lo, g2, g1)
    ohb = lf == bucket
    oh = jnp.where(ohb, 1.0, 0.0)
    prefix = _dot(tri_ref[...], oh.astype(BF16)) + carry_ref[0:1, :]
    rank = jnp.sum(jnp.where(ohb, prefix, 0.0), axis=-1, keepdims=True)
    new_carry = carry_ref[0:1, :] + jnp.sum(oh, axis=0, keepdims=True)
    carry_ref[...] = jnp.broadcast_to(new_carry, carry_ref.shape)
    cnt_ref[...] = jnp.broadcast_to(new_carry, cnt_ref.shape)

    meta = jnp.where(li == 0, bucket, jnp.where(li == 1, rank, 0.0))
    idx_ref[...] = meta.T[0:SUBLANES, :].astype(jnp.int32)
    gates = jnp.where(li == GATE_LANE, g_lo, jnp.where(li == GATE_LANE + 1, g_hi, 0.0))
    u_ref[:, D // 2:] = pltpu.bitcast(gates, jnp.uint32)


def _out_router(x2, y_ssd, y_da, w_out, norm_w, w_r, b_r):
    T, D = x2.shape
    tm = min(TM_ROUTE, T)
    r = jnp.arange(min(ROUTE_SUB, tm))
    tri = (r[:, None] > r[None, :]).astype(BF16)
    row = lambda n: pl.BlockSpec((tm, n), lambda i: (i, 0))
    full = lambda a: pl.BlockSpec(a.shape, lambda i: (0,) * a.ndim)
    return pl.pallas_call(
        _router_kernel,
        grid=(T // tm,),
        in_specs=[row(D), row(512), row(512), full(w_out), full(norm_w), full(w_r), full(b_r), full(tri)],
        out_specs=[row(D), row(D // 2 + ROW_TAIL), pl.BlockSpec((SUBLANES, tm), lambda i: (0, i)),
                   pl.BlockSpec((SUBLANES, LANES), lambda i: (0, 0))],
        out_shape=[
            jax.ShapeDtypeStruct((T, D), F32),
            jax.ShapeDtypeStruct((T, D // 2 + ROW_TAIL), jnp.uint32),
            jax.ShapeDtypeStruct((SUBLANES, T), jnp.int32),
            jax.ShapeDtypeStruct((SUBLANES, LANES), F32),
        ],
        scratch_shapes=[pltpu.VMEM((SUBLANES, LANES), F32)],
        compiler_params=_cparams("arbitrary"),
        name="out_router",
    )(x2, y_ssd, y_da, w_out, norm_w, w_r, b_r, tri)


def _dispatch_kernel(cnt_ref, idx_ref, u_ref, xs_hbm, slots_ref, ea_ref, eb_ref, nused_ref,
                     pstart_ref, zero_ref, sem, zsem, *, tm, n_blocks):
    i = pl.program_id(0)

    @pl.when(i == 0)
    def _():
        zero_ref[...] = jnp.zeros_like(zero_ref)
        pair_lo, pair_hi = _pair_tables()

        def zero_block(first_row):
            return pltpu.make_async_copy(
                zero_ref, xs_hbm.at[pl.ds(pl.multiple_of(first_row, MOE_ROWS), MOE_ROWS)], zsem)

        acc = jnp.int32(0)
        nfill = jnp.int32(0)
        for q in range(N_BUCKETS):
            padded = ((cnt_ref[q] + (MOE_ROWS - 1)) // MOE_ROWS) * MOE_ROWS
            pstart_ref[q] = acc
            end = acc + padded

            def fill(j, _, q=q):
                ea_ref[j] = jnp.int32(pair_lo[q])
                eb_ref[j] = jnp.int32(pair_hi[q])
                return 0

            lax.fori_loop(acc // MOE_ROWS, end // MOE_ROWS, fill, 0)

            @pl.when(padded > 0)
            def _(end=end):
                zero_block(end - MOE_ROWS).start()

            nfill = nfill + jnp.where(padded > 0, 1, 0)
            acc = end
        nused = acc // MOE_ROWS
        nused_ref[0] = nused

        def fill_tail(j, _):
            ea_ref[j] = jnp.int32(0)
            eb_ref[j] = jnp.int32(0)
            zero_block(j * MOE_ROWS).start()
            return 0

        lax.fori_loop(nused, n_blocks, fill_tail, 0)

        def wait_fill(_, carry):
            zero_block(0).wait()
            return carry

        lax.fori_loop(0, nfill + (n_blocks - nused), wait_fill, 0)

    def start(i8, _):
        for r in range(SUBLANES):
            t = i8 * SUBLANES + r
            slot = pstart_ref[idx_ref[0, t]] + idx_ref[1, t]
            slots_ref[0, t] = slot
            pltpu.make_async_copy(u_ref.at[i8, pl.ds(r, 1)], xs_hbm.at[pl.ds(slot, 1)], sem).start()
        return 0

    lax.fori_loop(0, tm // SUBLANES, start, 0)
    pltpu.make_async_copy(u_ref, u_ref, sem).wait()


def _dispatch(counts, idx, u, n_blocks):
    T, W = u.shape
    tm = min(TM_DISP, T)
    n_slots = n_blocks * MOE_ROWS
    grid_spec = pltpu.PrefetchScalarGridSpec(
        num_scalar_prefetch=1,
        grid=(T // tm,),
        in_specs=[
            pl.BlockSpec((SUBLANES, tm), lambda i, cnt: (0, i), memory_space=pltpu.SMEM),
            pl.BlockSpec((tm // SUBLANES, SUBLANES, W), lambda i, cnt: (i, 0, 0)),
        ],
        out_specs=[
            pl.BlockSpec(memory_space=pl.ANY),
            pl.BlockSpec((1, tm), lambda i, cnt: (0, i), memory_space=pltpu.SMEM),
            pl.BlockSpec(memory_space=pltpu.SMEM),
            pl.BlockSpec(memory_space=pltpu.SMEM),
            pl.BlockSpec(memory_space=pltpu.SMEM),
        ],
        scratch_shapes=[pltpu.SMEM((LANES,), jnp.int32),
                        pltpu.VMEM((MOE_ROWS, W), u.dtype),
                        pltpu.SemaphoreType.DMA(()),
                        pltpu.SemaphoreType.DMA(())],
    )
    return pl.pallas_call(
        functools.partial(_dispatch_kernel, tm=tm, n_blocks=n_blocks),
        grid_spec=grid_spec,
        out_shape=[
            jax.ShapeDtypeStruct((n_slots, W), u.dtype),
            jax.ShapeDtypeStruct((1, T), jnp.int32),
            jax.ShapeDtypeStruct((n_blocks,), jnp.int32),
            jax.ShapeDtypeStruct((n_blocks,), jnp.int32),
            jax.ShapeDtypeStruct((1,), jnp.int32),
        ],
        compiler_params=_cparams("arbitrary"),
        name="dispatch",
    )(counts, idx, u.reshape(T // SUBLANES, SUBLANES, W))


def _moe_kernel(ea_ref, eb_ref, nused_ref, xs_ref, wga_ref, wua_ref, wda_ref, wgb_ref, wub_ref, wdb_ref, o_ref):
    j = pl.program_id(0)
    H = o_ref.shape[1]

    @pl.when(j < nused_ref[0])
    def _():
        x = _unpack_bf16_pairs(xs_ref[:, 0:H]).astype(BF16)
        tail = pltpu.bitcast(xs_ref[:, H:], F32)
        out = None
        for (wg, wu, wd), lane in (((wga_ref, wua_ref, wda_ref), GATE_LANE),
                                   ((wgb_ref, wub_ref, wdb_ref), GATE_LANE + 1)):
            hid = _silu(_dot(x, wg[0])) * _dot(x, wu[0])
            y = _dot(hid.astype(BF16), wd[0]) * tail[:, lane:lane + 1]
            out = y if out is None else out + y
        o_ref[...] = _pack_bf16_pairs(out)

    @pl.when(j >= nused_ref[0])
    def _():
        o_ref[...] = jnp.zeros_like(o_ref)


def _moe(block_ea, block_eb, nused, xs, wg, wu, wd, n_blocks):
    W = xs.shape[1]
    H = W - ROW_TAIL
    used = lambda j, nu: jnp.minimum(j, nu[0] - 1)
    wspec_a = lambda a: pl.BlockSpec((1,) + a.shape[1:], lambda j, ea, eb, nu: (ea[used(j, nu)], 0, 0))
    wspec_b = lambda a: pl.BlockSpec((1,) + a.shape[1:], lambda j, ea, eb, nu: (eb[used(j, nu)], 0, 0))
    grid_spec = pltpu.PrefetchScalarGridSpec(
        num_scalar_prefetch=3,
        grid=(n_blocks,),
        in_specs=[
            pl.BlockSpec((MOE_ROWS, W), lambda j, ea, eb, nu: (used(j, nu), 0)),
            wspec_a(wg), wspec_a(wu), wspec_a(wd), wspec_b(wg), wspec_b(wu), wspec_b(wd),
        ],
        out_specs=pl.BlockSpec((MOE_ROWS, H), lambda j, ea, eb, nu: (j, 0)),
    )
    return pl.pallas_call(
        _moe_kernel,
        grid_spec=grid_spec,
        out_shape=jax.ShapeDtypeStruct((n_blocks * MOE_ROWS, H), jnp.uint32),
        compiler_params=_cparams("arbitrary"),
        name="moe_mlp",
    )(block_ea, block_eb, nused, xs, wg, wu, wd, wg, wu, wd)


def _combine_kernel(slots_ref, next_slots_ref, h_ref, nw_ref, eo_hbm, o_ref, g_ref, sems, *, tm):
    i = pl.program_id(0)
    n = pl.num_programs(0)
    cur = i % 2

    def gather_tile(sref, buf):
        def start(i8, _):
            for r in range(SUBLANES):
                pltpu.make_async_copy(eo_hbm.at[pl.ds(sref[0, i8 * SUBLANES + r], 1)],
                                      g_ref.at[buf, i8, pl.ds(r, 1)], sems.at[buf]).start()
            return 0

        lax.fori_loop(0, tm // SUBLANES, start, 0)

    @pl.when(i == 0)
    def _():
        gather_tile(slots_ref, 0)

    @pl.when(i + 1 < n)
    def _():
        gather_tile(next_slots_ref, 1 - cur)

    pltpu.make_async_copy(g_ref.at[cur], g_ref.at[cur], sems.at[cur]).wait()

    y = h_ref[...] + _unpack_bf16_pairs(g_ref[cur].reshape(tm, -1))
    ms = jnp.mean(y * y, axis=-1, keepdims=True)
    o_ref[...] = y * lax.rsqrt(ms + EPS) * nw_ref[...]


def _combine(slots, h, norm_w, eo):
    T, D = h.shape
    tm = min(TM_COMB, T)
    last = T // tm - 1
    return pl.pallas_call(
        functools.partial(_combine_kernel, tm=tm),
        grid=(T // tm,),
        in_specs=[
            pl.BlockSpec((1, tm), lambda i: (0, i), memory_space=pltpu.SMEM),
            pl.BlockSpec((1, tm), lambda i: (0, jnp.minimum(i + 1, last)), memory_space=pltpu.SMEM),
            pl.BlockSpec((tm, D), lambda i: (i, 0)),
            pl.BlockSpec(norm_w.shape, lambda i: (0, 0)),
            pl.BlockSpec(memory_space=pl.ANY),
        ],
        out_specs=pl.BlockSpec((tm, D), lambda i: (i, 0)),
        out_shape=jax.ShapeDtypeStruct((T, D), F32),
        scratch_shapes=[pltpu.VMEM((2, tm // SUBLANES, SUBLANES, D // 2), jnp.uint32),
                        pltpu.SemaphoreType.DMA((2,))],
        compiler_params=_cparams("arbitrary"),
        name="combine",
    )(slots, slots, h, norm_w, eo)


def _pad_lanes(v, offset=0):
    v = v.astype(F32).reshape(-1)
    return jnp.zeros((1, LANES), F32).at[0, offset:offset + v.shape[0]].set(v)


def kernel(x, norm_mix_w, w_in, conv_w, conv_b, dt_bias_fwd, dt_bias_bwd, a_log_fwd, a_log_bwd, ssd_d,
           ssd_norm_w, lambda_q1, lambda_k1, lambda_q2, lambda_k2, subln_w, w_out, norm_ffn_w,
           w_router_group, b_router_group, w_router_exp, b_router_exp, w_exp_gate, w_exp_up, w_exp_down,
           norm_final_w):
    B, L, D = x.shape
    T = B * L
    x2 = x.reshape(T, D)
    l = 0

    w = w_in[l]
    o_z, o_xbc, o_dt = 512, 512 + 1024, 512 + 1024 + 16
    o_q, o_k = o_dt + 512, o_dt + 1024
    w_main = jnp.concatenate(
        [w[:, :o_xbc], w[:, o_dt:o_q] * (DA_HEADDIM ** -0.5 * LOG2E), w[:, o_q:]], axis=1).astype(BF16)
    w_dt = jnp.zeros((D, LANES), F32).at[:, :16].set(w[:, o_xbc:o_dt]).astype(BF16)

    experts = [w_exp_gate[l], w_exp_up[l], w_exp_down[l]]
    z, xbc, dt, q, k, v, *experts_bf16 = _in_proj(
        x2, norm_mix_w[l].reshape(1, D), w_main, w_dt, [a.reshape(-1, a.shape[-1]) for a in experts])
    wg_b, wu_b, wd_b = [c.reshape(a.shape) for c, a in zip(experts_bf16, experts)]

    dtb = _pad_lanes(jnp.concatenate([dt_bias_fwd[l], dt_bias_bwd[l]]))
    alog = _pad_lanes(jnp.concatenate([a_log_fwd[l], a_log_bwd[l]]))
    dskip_e = jnp.repeat(ssd_d[l].astype(F32), SSD_HEADDIM).reshape(1, SSD_WIDTH)
    y_ssd = _ssd(xbc.reshape(B, L, 1024), dt.reshape(B, L, LANES), z.reshape(B, L, 512),
                 conv_w[l].astype(F32), conv_b[l].reshape(1, -1).astype(F32), dtb, alog, dskip_e,
                 ssd_norm_w[l].reshape(1, -1).astype(F32), B, L)

    slopes = jnp.power(2.0, -8.0 * jnp.arange(1, DA_HEADS + 1, dtype=F32) / DA_HEADS)
    lamv = jnp.concatenate([_pad_lanes(lambda_q1[l]), _pad_lanes(lambda_k1[l]),
                            _pad_lanes(lambda_q2[l]), _pad_lanes(lambda_k2[l]),
                            jnp.zeros((4, LANES), F32)], axis=0)
    y_da = _diff_attention(q.reshape(B, L, 512), k.reshape(B, L, 512), v.reshape(B, L, 512),
                           slopes, lamv, subln_w[l].reshape(1, -1).astype(F32), B, L)

    w_r = (jnp.zeros((D, LANES), F32).at[:, :N_GROUPS].set(w_router_group[l])
           .at[:, EXP_LANE0:EXP_LANE0 + N_EXPERTS].set(w_router_exp[l])).astype(BF16)
    b_r = _pad_lanes(jnp.concatenate([b_router_group[l], b_router_exp[l]]))
    h, u, idx, cnt = _out_router(x2, y_ssd.reshape(T, 512), y_da.reshape(T, 512), w_out[l].astype(BF16),
                                  norm_ffn_w[l].reshape(1, D), w_r, b_r)

    n_blocks = T // MOE_ROWS + N_BUCKETS
    counts = cnt[0].astype(jnp.int32)
    xs, slots, block_ea, block_eb, nused = _dispatch(counts, idx, u, n_blocks)
    eo = _moe(block_ea, block_eb, nused, xs, wg_b, wu_b, wd_b, n_blocks)
    out = _combine(slots, h, norm_final_w.reshape(1, D), eo)
    return out.reshape(B, L, D)
```
